```python
import math
import jax, jax.numpy as jnp
from jax import lax
import numpy as np

D_MODEL = 1024
BATCH = 16
SEQ = 256
DEPTH = 2
DEC_BATCH = 8
DEC_SEQ = 1024
PAST_LEN = 256

GRID_W = 64
GROUP_W = D_MODEL // 4
D_MIX = 4 * GROUP_W
HG_HEADS = 4
HG_DK = GROUP_W // HG_HEADS
HG_DV = GROUP_W // HG_HEADS
ML_HEADS = 4
ML_DK = GROUP_W // ML_HEADS
ML_DV = GROUP_W // ML_HEADS
S5_CH = 16
S5_GROUPS = GROUP_W // S5_CH
S5_P = 64
HY_ORDER = 2
HY_BANDS = 16
HY_EMB = 1 + 2 * HY_BANDS
HY_HIDDEN = 64
HY_SHIFT = 0.05
N_EXPERT_GROUPS = 4
EXPERTS_PER_GROUP = 4
N_EXPERTS = N_EXPERT_GROUPS * EXPERTS_PER_GROUP
D_EXPERT = D_MODEL // 2
TOP_K = 2
CHUNK = 64
EPS = 1e-6
IN_SIZES = (GROUP_W,) * 9 + (2 * ML_HEADS, 2 * ML_HEADS) + (GROUP_W,) * 4
D_IN = 13 * GROUP_W + 4 * ML_HEADS

kernel_name = 'hybrid_diffusion_prefix_step'


def rmsnorm(x, g):
    xf = x.astype(jnp.float32)
    y = xf * lax.rsqrt(jnp.mean(xf * xf, axis=-1, keepdims=True) + EPS)
    return (y * g.astype(jnp.float32)).astype(x.dtype)


def head_rmsnorm(o, g):
    y = o * lax.rsqrt(jnp.mean(o * o, axis=-1, keepdims=True) + EPS)
    return y * g.reshape(o.shape[2:])


def short_conv(z, w, on_grid):
    bsz, n, ch = z.shape
    if on_grid:
        rows = n // GRID_W
        zp = jnp.pad(z.reshape(bsz, rows, GRID_W, ch), ((0, 0), (0, 0), (1, 1), (0, 0)))
        out = w[0] * zp[:, :, :-2] + w[1] * zp[:, :, 1:-1] + w[2] * zp[:, :, 2:]
        return out.reshape(bsz, n, ch)
    zp = jnp.pad(z, ((0, 0), (1, 1), (0, 0)))
    return w[0] * zp[:, :-2] + w[1] * zp[:, 1:-1] + w[2] * zp[:, 2:]


def to_chunks(a):
    b, n = a.shape[0], a.shape[1]
    a = a.reshape((b, n // CHUNK, CHUNK) + a.shape[2:])
    return jnp.moveaxis(a, 1, 0)


def from_chunks(a):
    a = jnp.moveaxis(a, 0, 1)
    return a.reshape((a.shape[0], a.shape[1] * a.shape[2]) + a.shape[3:])


def chunk_mask():
    idx = jnp.arange(CHUNK)
    return idx[:, None] >= idx[None, :]


def gla_chunk_scan(q, k, v, log_f, s0):
    mask = chunk_mask()

    def step(s, inp):
        qc, kc, vc, lfc = inp
        bcum = jnp.cumsum(lfc, axis=1)
        blast = bcum[:, -1]
        diff = bcum[:, :, None] - bcum[:, None, :]
        decay = jnp.exp(jnp.where(mask[None, :, :, None, None], diff, -jnp.inf))
        scores = jnp.einsum('bthk,bshk,btshk->btsh', qc, kc, decay)
        o = (jnp.einsum('bthk,bhkv->bthv', qc * jnp.exp(bcum), s)
             + jnp.einsum('btsh,bshv->bthv', scores, vc))
        s_new = (jnp.exp(blast)[..., None] * s
                 + jnp.einsum('bshk,bshv->bhkv', kc * jnp.exp(blast[:, None] - bcum), vc))
        return s_new, o

    s_fin, o = lax.scan(step, s0, (to_chunks(q), to_chunks(k), to_chunks(v), to_chunks(log_f)))
    return from_chunks(o), s_fin


def hgrn2_mixer(pq, pf_fwd, pf_bwd, pi, pg, lb, norm_g, s0):
    bsz, n, _ = pq.shape
    hd = (bsz, n, HG_HEADS, HG_DK)
    q = jax.nn.silu(pq).reshape(hd)
    v = pi.reshape(bsz, n, HG_HEADS, HG_DV)
    log_lb = jnp.log(lb).reshape(HG_HEADS, HG_DK)
    log_1mlb = jnp.log1p(-lb).reshape(HG_HEADS, HG_DK)

    def gates(pf):
        log_f = jnp.logaddexp(log_lb, log_1mlb + jax.nn.log_sigmoid(pf.reshape(hd)))
        return log_f, -jnp.expm1(log_f)

    lf_f, k_f = gates(pf_fwd)
    lf_b, k_b = gates(pf_bwd)
    s0 = s0.astype(jnp.float32)
    o_f, s_f = gla_chunk_scan(q, k_f, v, lf_f, s0[:, 0])
    o_b, s_b = gla_chunk_scan(q[:, ::-1], k_b[:, ::-1], v[:, ::-1], lf_b[:, ::-1], s0[:, 1])
    o = head_rmsnorm(o_f + o_b[:, ::-1], norm_g) * jax.nn.silu(pg).reshape(bsz, n, HG_HEADS, HG_DV)
    return o.reshape(bsz, n, GROUP_W), jnp.stack([s_f, s_b], axis=1)


def mlstm_chunk_scan(q, k, v, log_i, log_f, state):
    mask = chunk_mask()

    def step(carry, inp):
        c, nn, m = carry
        qc, kc, vc, lic, lfc = inp
        bcum = jnp.cumsum(lfc, axis=1)
        dmat = jnp.where(mask[None, :, :, None],
                         bcum[:, :, None] - bcum[:, None, :] + lic[:, None, :], -jnp.inf)
        prev = bcum + m[:, None]
        m_t = jnp.maximum(prev, jnp.max(dmat, axis=2))
        w = jnp.exp(dmat - m_t[:, :, None]) * jnp.einsum('bthk,bshk->btsh', qc, kc)
        wp = jnp.exp(prev - m_t)
        num = (wp[..., None] * jnp.einsum('bthk,bhkv->bthv', qc, c)
               + jnp.einsum('btsh,bshv->bthv', w, vc))
        den = wp * jnp.einsum('bthk,bhk->bth', qc, nn) + jnp.sum(w, axis=2)
        h = num / jnp.maximum(jnp.abs(den), jnp.exp(-m_t))[..., None]
        blast = bcum[:, -1]
        g = blast[:, None] - bcum + lic
        m_new = jnp.maximum(blast + m, jnp.max(g, axis=1))
        dec = jnp.exp(blast + m - m_new)
        wg = jnp.exp(g - m_new[:, None])
        c_new = dec[..., None, None] * c + jnp.einsum('bsh,bshk,bshv->bhkv', wg, kc, vc)
        n_new = dec[..., None] * nn + jnp.einsum('bsh,bshk->bhk', wg, kc)
        return (c_new, n_new, m_new), h

    fin, h = lax.scan(step, state, (to_chunks(q), to_chunks(k), to_chunks(v),
                                    to_chunks(log_i), to_chunks(log_f)))
    return from_chunks(h), fin


def mlstm_mixer(pq, pk, pv, po, pi, pf, conv_w, i_bias, f_bias, norm_g, c0, n0, m0, on_grid):
    bsz, n, _ = pq.shape
    qk = jax.nn.silu(short_conv(jnp.concatenate([pq, pk], axis=-1), conv_w, on_grid))
    q = qk[..., :GROUP_W].reshape(bsz, n, ML_HEADS, ML_DK)
    k = qk[..., GROUP_W:].reshape(bsz, n, ML_HEADS, ML_DK) * (ML_DK ** -0.5)
    v = pv.reshape(bsz, n, ML_HEADS, ML_DV)
    log_i = (pi + i_bias).reshape(bsz, n, 2, ML_HEADS)
    log_f = jax.nn.log_sigmoid(pf + f_bias).reshape(bsz, n, 2, ML_HEADS)

    def init(d):
        return (c0[:, d].astype(jnp.float32), n0[:, d].astype(jnp.float32), m0[:, d].astype(jnp.float32))

    h_f, (cf, nf, mf) = mlstm_chunk_scan(q, k, v, log_i[:, :, 0], log_f[:, :, 0], init(0))
    h_b, (cb, nb, mb) = mlstm_chunk_scan(q[:, ::-1], k[:, ::-1], v[:, ::-1],
                                         log_i[:, ::-1, 1], log_f[:, ::-1, 1], init(1))
    h = head_rmsnorm(h_f + h_b[:, ::-1], norm_g) * jax.nn.sigmoid(po).reshape(bsz, n, ML_HEADS, ML_DV)
    return (h.reshape(bsz, n, GROUP_W), jnp.stack([cf, cb], axis=1),
            jnp.stack([nf, nb], axis=1), jnp.stack([mf, mb], axis=1))


def cplx_combine(e1, e2):
    a1r, a1i, b1r, b1i = e1
    a2r, a2i, b2r, b2i = e2
    return (a2r * a1r - a2i * a1i, a2r * a1i + a2i * a1r,
            a2r * b1r - a2i * b1i + b2r, a2r * b1i + a2i * b1r + b2i)


def s5_scan(u, a_re, a_im, log_dt, b_re, b_im, h0_re, h0_im):
    dt = jnp.exp(log_dt)[:, None]
    mag = jnp.exp(a_re * dt)
    ab_re, ab_im = mag * jnp.cos(a_im * dt), mag * jnp.sin(a_im * dt)
    den = a_re * a_re + a_im * a_im
    g_re = ((ab_re - 1.0) * a_re + ab_im * a_im) / den
    g_im = (ab_im * a_re - (ab_re - 1.0) * a_im) / den
    bb_re = g_re[..., None] * b_re - g_im[..., None] * b_im
    bb_im = g_re[..., None] * b_im + g_im[..., None] * b_re
    bu_re = jnp.einsum('gpc,blgc->blgp', bb_re, u)
    bu_im = jnp.einsum('gpc,blgc->blgp', bb_im, u)
    bu_re = bu_re.at[:, 0].add(ab_re * h0_re - ab_im * h0_im)
    bu_im = bu_im.at[:, 0].add(ab_re * h0_im + ab_im * h0_re)
    shape = bu_re.shape
    _, _, h_re, h_im = lax.associative_scan(
        cplx_combine, (jnp.broadcast_to(ab_re, shape), jnp.broadcast_to(ab_im, shape), bu_re, bu_im), axis=1)
    return h_re, h_im


def s5_mixer(u, a_re, a_im, log_dt, b_re, b_im, c_re, c_im, d, glu_w, glu_b, h0_re, h0_im):
    bsz, n, _ = u.shape
    ug = u.reshape(bsz, n, S5_GROUPS, S5_CH)
    h0_re = h0_re.astype(jnp.float32)
    h0_im = h0_im.astype(jnp.float32)
    hf_re, hf_im = s5_scan(ug, a_re[0], a_im[0], log_dt[0], b_re, b_im, h0_re[:, 0], h0_im[:, 0])
    hb_re, hb_im = s5_scan(ug[:, ::-1], a_re[1], a_im[1], log_dt[1], b_re, b_im, h0_re[:, 1], h0_im[:, 1])
    h_re = hf_re + hb_re[:, ::-1]
    h_im = hf_im + hb_im[:, ::-1]
    y = jnp.einsum('gcp,blgp->blgc', c_re, h_re) - jnp.einsum('gcp,blgp->blgc', c_im, h_im)
    y = y.reshape(bsz, n, GROUP_W) + d * u
    z = jax.nn.gelu(y)
    out = z * jax.nn.sigmoid(z @ glu_w + glu_b)
    return (out, jnp.stack([hf_re[:, -1], hb_re[:, -1]], axis=1),
            jnp.stack([hf_im[:, -1], hb_im[:, -1]], axis=1))


def hyena_filters(n, w1, b1, w2, b2, w3, freq, decay):
    t = jnp.linspace(0.0, 1.0, n, dtype=jnp.float32)[:, None]
    w = (2.0 * math.pi / n) * jnp.arange(n, dtype=jnp.float32)[:, None]
    bands = jnp.linspace(1e-4, HY_BANDS - 1, HY_BANDS, dtype=jnp.float32)[None, :]
    feats = jnp.concatenate([t, jnp.cos(w * bands), -jnp.sin(w * bands)], axis=-1)
    hdn = jnp.sin(freq * (feats @ w1 + b1))
    hdn = jnp.sin(freq * (hdn @ w2 + b2))
    h = (hdn @ w3) * (jnp.exp(-t * jnp.abs(decay)) + HY_SHIFT)
    h = h.reshape(n, 2, HY_ORDER, GROUP_W)
    h_f = h[:, 0]
    h_b = h[1:, 1][::-1]
    norm = jnp.sum(jnp.abs(h_f), axis=0) + jnp.sum(jnp.abs(h_b), axis=0)
    k = jnp.concatenate([h_f, jnp.zeros((1, HY_ORDER, GROUP_W), jnp.float32), h_b], axis=0)
    return k / norm


def hyena_mixer(pv, px1, px2, conv_w, w1, b1, w2, b2, w3, freq, decay, bias, on_grid):
    n = pv.shape[1]
    z = short_conv(jnp.concatenate([pv, px1, px2], axis=-1), conv_w, on_grid)
    v, x1, x2 = jnp.split(z, 3, axis=-1)
    kf = jnp.fft.rfft(hyena_filters(n, w1, b1, w2, b2, w3, freq, decay), axis=0)
    y = v
    for o, gate in enumerate((x1, x2)):
        conv = jnp.fft.irfft(jnp.fft.rfft(y, n=2 * n, axis=1) * kf[None, :, o], n=2 * n, axis=1)[:, :n]
        y = gate * (conv + bias[o] * y)
    return y


def hier_moe(h, w_rg, b_rg, w_re, b_re, w_gate, w_up, w_down):
    bsz, n, d = h.shape
    t = h.reshape(bsz * n, d)
    g_logits = jnp.dot(t, w_rg).astype(jnp.float32) + b_rg.astype(jnp.float32)
    p_group = jax.nn.softmax(g_logits, axis=-1)
    g_sel = jnp.argmax(g_logits, axis=-1)
    p_sel = jnp.take_along_axis(p_group, g_sel[:, None], axis=-1)
    e_logits = (jnp.dot(t, w_re).astype(jnp.float32) + b_re.astype(jnp.float32)).reshape(
        -1, N_EXPERT_GROUPS, EXPERTS_PER_GROUP)
    e_in = jnp.take_along_axis(e_logits, g_sel[:, None, None], axis=1)[:, 0]
    top_v, top_i = lax.top_k(e_in, TOP_K)
    weights = jax.nn.softmax(top_v, axis=-1) * p_sel
    e_idx = g_sel[:, None] * EXPERTS_PER_GROUP + top_i
    gates = jnp.einsum('tk,tke->te', weights,
                       jax.nn.one_hot(e_idx, N_EXPERTS, dtype=jnp.float32)).astype(h.dtype)
    hid = jax.nn.silu(jnp.einsum('td,edf->tef', t, w_gate)) * jnp.einsum('td,edf->tef', t, w_up)
    y = jnp.einsum('tef,efd->td', hid * gates[:, :, None], w_down)
    return y.reshape(bsz, n, d)


def zero_states(b):
    return dict(
        hgrn=jnp.zeros((b, 2, HG_HEADS, HG_DK, HG_DV), jnp.float32),
        ml_c=jnp.zeros((b, 2, ML_HEADS, ML_DK, ML_DV), jnp.float32),
        ml_n=jnp.zeros((b, 2, ML_HEADS, ML_DK), jnp.float32),
        ml_m=jnp.zeros((b, 2, ML_HEADS), jnp.float32),
        s5_re=jnp.zeros((b, 2, S5_GROUPS, S5_P), jnp.float32),
        s5_im=jnp.zeros((b, 2, S5_GROUPS, S5_P), jnp.float32))


def trunk_layer(x, cond, st, l, p, on_grid):
    def f(name):
        return p[name][l].astype(jnp.float32)

    mod = jnp.dot(jax.nn.silu(cond), p['w_ada'][l]) + p['b_ada'][l]
    sh1, sc1, g1, sh2, sc2, g2 = jnp.split(mod[:, None, :], 6, axis=-1)
    h = rmsnorm(x, p['norm1'][l]) * (1 + sc1) + sh1
    proj = jnp.dot(h, p['w_in'][l]).astype(jnp.float32)
    splits = [int(s) for s in np.cumsum(IN_SIZES)[:-1]]
    (hq, hff, hfb, hi, hg, mq, mk, mv, mo, mi, mf, su, yv, yx1, yx2) = jnp.split(proj, splits, axis=-1)

    lb_all = jnp.cumsum(jax.nn.softmax(p['hg_lb'].astype(jnp.float32), axis=0), axis=0)
    lb = lb_all[l] - lb_all[0]

    a_out, hg_st = hgrn2_mixer(hq, hff, hfb, hi, hg, lb, f('hg_norm'), st['hgrn'])
    b_out, mlc, mln, mlm = mlstm_mixer(mq, mk, mv, mo, mi, mf, f('ml_conv'), f('ml_i_bias'), f('ml_f_bias'),
                                       f('ml_norm'), st['ml_c'], st['ml_n'], st['ml_m'], on_grid)
    c_out, s5re, s5im = s5_mixer(su, f('s5_a_re'), f('s5_a_im'), f('s5_log_dt'), f('s5_b_re'), f('s5_b_im'),
                                 f('s5_c_re'), f('s5_c_im'), f('s5_d'), f('s5_glu_w'), f('s5_glu_b'),
                                 st['s5_re'], st['s5_im'])
    d_out = hyena_mixer(yv, yx1, yx2, f('hy_conv'), f('hy_w1'), f('hy_b1'), f('hy_w2'), f('hy_b2'),
                        f('hy_w3'), f('hy_freq'), f('hy_decay'), f('hy_bias'), on_grid)
    mix = jnp.concatenate([a_out, b_out, c_out, d_out], axis=-1).astype(x.dtype)
    x = x + g1 * jnp.dot(mix, p['w_out'][l])
    h2 = rmsnorm(x, p['norm2'][l]) * (1 + sc2) + sh2
    x = x + g2 * hier_moe(h2, p['moe_wg'][l], p['moe_bg'][l], p['moe_we'][l], p['moe_be'][l],
                          p['moe_w_gate'][l], p['moe_w_up'][l], p['moe_w_down'][l])
    new_st = dict(hgrn=hg_st, ml_c=mlc, ml_n=mln, ml_m=mlm, s5_re=s5re, s5_im=s5im)
    return x, new_st


def setup_inputs(seed: int = 0) -> dict:
    key = jax.random.key(seed)
    ks = iter(jax.random.split(key, 64))
    D = D_MODEL

    def nrm(shape, scale=1.0):
        return scale * jax.random.normal(next(ks), shape, jnp.float32)

    def unif(shape, lo, hi):
        return jax.random.uniform(next(ks), shape, jnp.float32, minval=lo, maxval=hi)

    pi_n = math.pi * jnp.arange(S5_P, dtype=jnp.float32)
    return {
        'x_prompt': nrm((BATCH, SEQ, D)),
        'x_sample': nrm((DEC_BATCH, DEC_SEQ, D)),
        'c': nrm((DEC_BATCH, D)),
        'state_hgrn': nrm((DEC_BATCH, DEPTH, 2, HG_HEADS, HG_DK, HG_DV), 0.3),
        'state_mlstm_c': nrm((DEC_BATCH, DEPTH, 2, ML_HEADS, ML_DK, ML_DV), 0.3),
        'state_mlstm_n': nrm((DEC_BATCH, DEPTH, 2, ML_HEADS, ML_DK), 0.3),
        'state_mlstm_m': nrm((DEC_BATCH, DEPTH, 2, ML_HEADS), 0.5),
        'state_s5_re': nrm((DEC_BATCH, DEPTH, 2, S5_GROUPS, S5_P), 0.1),
        'state_s5_im': nrm((DEC_BATCH, DEPTH, 2, S5_GROUPS, S5_P), 0.1),
        'c_ctx': nrm((D,)),
        'w_ada': nrm((DEPTH, D, 6 * D), 0.5 * D ** -0.5),
        'b_ada': nrm((DEPTH, 6 * D), 0.02),
        'norm1': 1.0 + nrm((DEPTH, D), 0.02),
        'norm2': 1.0 + nrm((DEPTH, D), 0.02),
        'w_in': nrm((DEPTH, D, D_IN), D ** -0.5),
        'w_out': nrm((DEPTH, D_MIX, D), D_MIX ** -0.5),
        'hg_lb': nrm((DEPTH, GROUP_W)),
        'hg_norm': 1.0 + nrm((DEPTH, GROUP_W), 0.02),
        'ml_conv': nrm((DEPTH, 3, 2 * GROUP_W), 0.5),
        'ml_i_bias': nrm((DEPTH, 2 * ML_HEADS), 0.1),
        'ml_f_bias': 3.0 + nrm((DEPTH, 2 * ML_HEADS), 0.5),
        'ml_norm': 1.0 + nrm((DEPTH, GROUP_W), 0.02),
        's5_a_re': -0.5 + nrm((DEPTH, 2, S5_GROUPS, S5_P), 0.01),
        's5_a_im': pi_n + nrm((DEPTH, 2, S5_GROUPS, S5_P), 0.01),
        's5_log_dt': unif((DEPTH, 2, S5_GROUPS), math.log(1e-3), math.log(1e-1)),
        's5_b_re': nrm((DEPTH, S5_GROUPS, S5_P, S5_CH), (2 * S5_CH) ** -0.5),
        's5_b_im': nrm((DEPTH, S5_GROUPS, S5_P, S5_CH), (2 * S5_CH) ** -0.5),
        's5_c_re': nrm((DEPTH, S5_GROUPS, S5_CH, S5_P), S5_P ** -0.5),
        's5_c_im': nrm((DEPTH, S5_GROUPS, S5_CH, S5_P), S5_P ** -0.5),
        's5_d': nrm((DEPTH, GROUP_W)),
        's5_glu_w': nrm((DEPTH, GROUP_W, GROUP_W), GROUP_W ** -0.5),
        's5_glu_b': nrm((DEPTH, GROUP_W), 0.02),
        'hy_conv': nrm((DEPTH, 3, 3 * GROUP_W), 0.5),
        'hy_w1': nrm((DEPTH, HY_EMB, HY_HIDDEN), HY_EMB ** -0.5),
        'hy_b1': nrm((DEPTH, HY_HIDDEN), 0.1),
        'hy_w2': nrm((DEPTH, HY_HIDDEN, HY_HIDDEN), HY_HIDDEN ** -0.5),
        'hy_b2': nrm((DEPTH, HY_HIDDEN), 0.1),
        'hy_w3': nrm((DEPTH, HY_HIDDEN, 2 * HY_ORDER * GROUP_W), HY_HIDDEN ** -0.5),
        'hy_freq': 1.0 + nrm((DEPTH, HY_HIDDEN), 0.1),
        'hy_decay': unif((DEPTH, 2 * HY_ORDER * GROUP_W), 3.0, 15.0),
        'hy_bias': nrm((DEPTH, HY_ORDER, GROUP_W)),
        'moe_wg': nrm((DEPTH, D, N_EXPERT_GROUPS), D ** -0.5),
        'moe_bg': nrm((DEPTH, N_EXPERT_GROUPS), 0.01),
        'moe_we': nrm((DEPTH, D, N_EXPERTS), D ** -0.5),
        'moe_be': nrm((DEPTH, N_EXPERTS), 0.01),
        'moe_w_gate': nrm((DEPTH, N_EXPERTS, D, D_EXPERT), D ** -0.5),
        'moe_w_up': nrm((DEPTH, N_EXPERTS, D, D_EXPERT), D ** -0.5),
        'moe_w_down': nrm((DEPTH, N_EXPERTS, D_EXPERT, D), D_EXPERT ** -0.5),
        'final_norm': 1.0 + nrm((D,), 0.02),
    }


def reference(x_prompt, x_sample, c, state_hgrn, state_mlstm_c, state_mlstm_n, state_mlstm_m,
              state_s5_re, state_s5_im, c_ctx, w_ada, b_ada, norm1, norm2, w_in, w_out, hg_lb, hg_norm,
              ml_conv, ml_i_bias, ml_f_bias, ml_norm, s5_a_re, s5_a_im, s5_log_dt, s5_b_re, s5_b_im,
              s5_c_re, s5_c_im, s5_d, s5_glu_w, s5_glu_b, hy_conv, hy_w1, hy_b1, hy_w2, hy_b2, hy_w3,
              hy_freq, hy_decay, hy_bias, moe_wg, moe_bg, moe_we, moe_be, moe_w_gate, moe_w_up,
              moe_w_down, final_norm):
    p = dict(w_ada=w_ada, b_ada=b_ada, norm1=norm1, norm2=norm2, w_in=w_in, w_out=w_out,
             hg_lb=hg_lb, hg_norm=hg_norm, ml_conv=ml_conv, ml_i_bias=ml_i_bias, ml_f_bias=ml_f_bias,
             ml_norm=ml_norm, s5_a_re=s5_a_re, s5_a_im=s5_a_im, s5_log_dt=s5_log_dt, s5_b_re=s5_b_re,
             s5_b_im=s5_b_im, s5_c_re=s5_c_re, s5_c_im=s5_c_im, s5_d=s5_d, s5_glu_w=s5_glu_w,
             s5_glu_b=s5_glu_b, hy_conv=hy_conv, hy_w1=hy_w1, hy_b1=hy_b1, hy_w2=hy_w2, hy_b2=hy_b2,
             hy_w3=hy_w3, hy_freq=hy_freq, hy_decay=hy_decay, hy_bias=hy_bias, moe_wg=moe_wg,
             moe_bg=moe_bg, moe_we=moe_we, moe_be=moe_be, moe_w_gate=moe_w_gate, moe_w_up=moe_w_up,
             moe_w_down=moe_w_down)

    xp = x_prompt
    cond_ctx = c_ctx[None, :]
    st0 = zero_states(x_prompt.shape[0])
    per_layer = []
    for l in range(DEPTH):
        xp, st = trunk_layer(xp, cond_ctx, st0, l, p, False)
        per_layer.append(st)
    y_prompt = rmsnorm(xp, final_norm)
    new_hgrn = jnp.stack([s['hgrn'] for s in per_layer], axis=1)
    new_ml_c = jnp.stack([s['ml_c'] for s in per_layer], axis=1)
    new_ml_n = jnp.stack([s['ml_n'] for s in per_layer], axis=1)
    new_ml_m = jnp.stack([s['ml_m'] for s in per_layer], axis=1)
    new_s5_re = jnp.stack([s['s5_re'] for s in per_layer], axis=1)
    new_s5_im = jnp.stack([s['s5_im'] for s in per_layer], axis=1)

    xs = x_sample
    for l in range(DEPTH):
        st_in = dict(hgrn=state_hgrn[:, l], ml_c=state_mlstm_c[:, l], ml_n=state_mlstm_n[:, l],
                     ml_m=state_mlstm_m[:, l], s5_re=state_s5_re[:, l], s5_im=state_s5_im[:, l])
        xs, _ = trunk_layer(xs, c, st_in, l, p, True)
    y_sample = rmsnorm(xs, final_norm)
    return (y_prompt, y_sample, new_hgrn, new_ml_c, new_ml_n, new_ml_m, new_s5_re, new_s5_im)
```

```python
import functools
import math

import numpy as np
import jax
import jax.numpy as jnp
from jax import lax
from jax.experimental import pallas as pl
from jax.experimental.pallas import tpu as pltpu

F32 = jnp.float32
BF16 = jnp.bfloat16

D_MODEL = 1024
DEPTH = 2
GRID_W = 64
GROUP_W = D_MODEL // 4
HEADS = 4
HEAD_D = GROUP_W // HEADS
S5_CH = 16
S5_GROUPS = GROUP_W // S5_CH
S5_P = 64
S5_STATE = S5_GROUPS * S5_P
HY_ORDER = 2
HY_BANDS = 16
HY_EMB = 1 + 2 * HY_BANDS
HY_HIDDEN = 64
HY_SHIFT = 0.05
N_EXPERT_GROUPS = 4
EXPERTS_PER_GROUP = 4
N_EXPERTS = N_EXPERT_GROUPS * EXPERTS_PER_GROUP
D_EXPERT = D_MODEL // 2
EPS = 1e-6
N_GATES = 4 * HEADS

LANES = 128
MOD_ROWS = 16
TOK_TILE = 256
MOE_TILE = 1024
HG_BLK = 16
HG_SB = 128
ML_CHUNK = 64
S5_TC = 64
NEG_INF = float("-inf")


def _dot(a, b):
    return jnp.dot(a, b, preferred_element_type=F32)


def _dot_nt(a, b):
    return lax.dot_general(a, b, (((1,), (1,)), ((), ())), preferred_element_type=F32)


def _dot_tn(a, b):
    return lax.dot_general(a, b, (((0,), (0,)), ((), ())), preferred_element_type=F32)


def _split3(x):
    hi = x.astype(BF16)
    r1 = x - hi.astype(F32)
    mid = r1.astype(BF16)
    lo = (r1 - mid.astype(F32)).astype(BF16)
    return hi, mid, lo


def _dot_sel_l(m, x):
    hi, mid, lo = _split3(x)
    return _dot(m, hi) + _dot(m, mid) + _dot(m, lo)


def _dot_sel_r(x, m):
    hi, mid, lo = _split3(x)
    return _dot(hi, m) + _dot(mid, m) + _dot(lo, m)


def _dot3(a, b):
    ah = a.astype(BF16)
    al = (a - ah.astype(F32)).astype(BF16)
    bh = b.astype(BF16)
    bl = (b - bh.astype(F32)).astype(BF16)
    return _dot(ah, bh) + _dot(al, bh) + _dot(ah, bl)


def _sigmoid(x):
    return 1.0 / (1.0 + jnp.exp(-x))


def _silu(x):
    return x * _sigmoid(x)


def _log_sigmoid(x):
    return jnp.minimum(x, 0.0) - jnp.log1p(jnp.exp(-jnp.abs(x)))


def _rmsnorm_rows(x, g):
    return x * lax.rsqrt(jnp.mean(x * x, axis=-1, keepdims=True) + EPS) * g


def _ada_kernel(c_ref, w_ref, b_ref, o_ref):
    s = _silu(c_ref[...]).astype(BF16)
    o_ref[0] = _dot(s, w_ref[0].astype(BF16)) + b_ref[0]


def _ada_mod(cond, w_ada, b_ada):
    tn = 1536
    n = w_ada.shape[-1]
    return pl.pallas_call(
        _ada_kernel,
        grid=(DEPTH, n // tn),
        in_specs=[
            pl.BlockSpec((MOD_ROWS, D_MODEL), lambda l, j: (0, 0)),
            pl.BlockSpec((1, D_MODEL, tn), lambda l, j: (l, 0, j)),
            pl.BlockSpec((1, 1, tn), lambda l, j: (l, 0, j)),
        ],
        out_specs=pl.BlockSpec((1, MOD_ROWS, tn), lambda l, j: (l, 0, j)),
        out_shape=jax.ShapeDtypeStruct((DEPTH, MOD_ROWS, n), F32),
        name="ada_mod",
    )(cond, w_ada, b_ada.reshape(DEPTH, 1, n))


def _inproj_kernel(x_ref, sh_ref, sc_ref, g_ref, wm_ref, wg_ref, wgt_ref,
                   hg_ref, ml_ref, su_ref, hy_ref, gt_ref, gtt_ref):
    h = _rmsnorm_rows(x_ref[...], g_ref[...])
    h = h * (1.0 + sc_ref[0]) + sh_ref[0]
    hb = h.astype(BF16)
    hg_ref[...] = _dot(hb, wm_ref[:, 0:5 * GROUP_W])
    ml_ref[...] = _dot(hb, wm_ref[:, 5 * GROUP_W:9 * GROUP_W])
    su_ref[...] = _dot(hb, wm_ref[:, 9 * GROUP_W:10 * GROUP_W])
    hy_ref[...] = _dot(hb, wm_ref[:, 10 * GROUP_W:13 * GROUP_W])
    gt_ref[...] = _dot(hb, wg_ref[...])
    gtt_ref[...] = _dot_nt(wgt_ref[...], hb)


def _mod_spec(k, row_of_tile):
    return pl.BlockSpec((1, 1, D_MODEL), lambda i: (row_of_tile(i), 0, k))


def _inproj(x2d, mod3, row_of_tile, norm_g, w_main, w_g, w_gt):
    t = x2d.shape[0]
    tm = TOK_TILE
    const = lambda i: (0, 0)
    tile = lambda i: (i, 0)
    widths = (5 * GROUP_W, 4 * GROUP_W, GROUP_W, 3 * GROUP_W)
    return pl.pallas_call(
        _inproj_kernel,
        grid=(t // tm,),
        in_specs=[
            pl.BlockSpec((tm, D_MODEL), tile),
            _mod_spec(0, row_of_tile),
            _mod_spec(1, row_of_tile),
            pl.BlockSpec((1, D_MODEL), const),
            pl.BlockSpec(w_main.shape, const),
            pl.BlockSpec(w_g.shape, const),
            pl.BlockSpec(w_gt.shape, const),
        ],
        out_specs=[pl.BlockSpec((tm, w), tile) for w in widths]
        + [pl.BlockSpec((tm, N_GATES), tile), pl.BlockSpec((N_GATES, tm), lambda i: (0, i))],
        out_shape=[jax.ShapeDtypeStruct((t, w), F32) for w in widths]
        + [jax.ShapeDtypeStruct((t, N_GATES), F32), jax.ShapeDtypeStruct((N_GATES, t), F32)],
        name="inproj",
    )(x2d, mod3, mod3, norm_g, w_main, w_g, w_gt)


@functools.lru_cache(maxsize=None)
def _hgrn_consts():
    r = np.arange(HG_SB)
    same = (r[:, None] // HG_BLK) == (r[None, :] // HG_BLK)
    tri_f = same & (r[:, None] >= r[None, :])
    tri_b = same & (r[:, None] <= r[None, :])
    c = np.arange(GROUP_W)
    head = (c[:, None] // HEAD_D) == (c[None, :] // HEAD_D)
    return (np.stack([tri_f, tri_b]).astype(np.float32), same.astype(np.float32),
            head.astype(np.float32))


def _hgrn_kernel(p_ref, s0_ref, lb_ref, ng_ref, tri_ref, bones_ref, e_ref, hmask_ref,
                 out_ref, *rest, seq, emit_state):
    if emit_state:
        st_ref, of_scr, st_scr, q_scr, k_scr, bc_scr, gq_scr, gk_scr, dec_scr = rest
    else:
        of_scr, st_scr, q_scr, k_scr, bc_scr, gq_scr, gk_scr, dec_scr = rest
    nsb = seq // HG_SB
    nblk = HG_SB // HG_BLK
    log_lb = lb_ref[0:1, :]
    log_1mlb = lb_ref[1:2, :]
    one_m_lb = lb_ref[2:3, :]
    row = lax.broadcasted_iota(jnp.int32, (HG_BLK, 1), 0)

    def superblock(sb, d, finalize):
        r0 = pl.multiple_of(sb * HG_SB, HG_SB)
        rows = pl.ds(r0, HG_SB)
        pq = p_ref[rows, 0:GROUP_W]
        pf = p_ref[rows, (1 + d) * GROUP_W:(2 + d) * GROUP_W]
        q_scr[...] = _silu(pq)
        b2 = log_1mlb + _log_sigmoid(pf)
        mx = jnp.maximum(log_lb, b2)
        mn = jnp.minimum(log_lb, b2)
        delta = jnp.where(mn == NEG_INF, NEG_INF, mn - mx)
        lf = mx + jnp.log1p(jnp.exp(delta))
        k_scr[...] = one_m_lb * _sigmoid(-pf)
        bc = _dot_sel_l(tri_ref[d], lf)
        bt = _dot_sel_l(bones_ref[...], lf)
        bc_scr[...] = bc
        gq_scr[...] = (q_scr[...] * jnp.exp(bc)).astype(BF16)
        gk_scr[...] = (k_scr[...] * jnp.exp(bt - bc)).astype(BF16)
        dec_scr[...] = jnp.exp(bt)
        o_blocks = [None] * nblk
        order = range(nblk) if d == 0 else range(nblk - 1, -1, -1)
        for j in order:
            sl = slice(HG_BLK * j, HG_BLK * (j + 1))
            qj = q_scr[sl, :]
            kj = k_scr[sl, :]
            vj = p_ref[pl.ds(r0 + HG_BLK * j, HG_BLK), 3 * GROUP_W:4 * GROUP_W]
            bcj = bc_scr[sl, :]
            parts = []
            for s in range(HG_BLK):
                diff = bcj - bcj[s:s + 1, :]
                keep = (row >= s) if d == 0 else (row <= s)
                dd = jnp.exp(jnp.where(keep, diff, NEG_INF))
                parts.append((qj * dd * kj[s:s + 1, :]).astype(BF16))
            r_all = _dot(jnp.concatenate(parts, axis=0), e_ref[...])
            o = _dot_nt(gq_scr[sl, :], st_scr[...].astype(BF16))
            for s in range(HG_BLK):
                o = o + r_all[HG_BLK * s:HG_BLK * (s + 1), :] * vj[s:s + 1, :]
            upd = _dot_tn(vj.astype(BF16), gk_scr[sl, :])
            st_scr[...] = st_scr[...] * dec_scr[HG_BLK * j:HG_BLK * j + 1, :] + upd * hmask_ref[...]
            o_blocks[j] = o
        o_sb = jnp.concatenate(o_blocks, axis=0)
        if finalize:
            o_sb = o_sb + of_scr[rows, :]
            ms = _dot_sel_r(o_sb * o_sb, e_ref[...]) * (1.0 / HEAD_D)
            pg = p_ref[rows, 4 * GROUP_W:5 * GROUP_W]
            y = o_sb * lax.rsqrt(ms + EPS) * ng_ref[...] * _silu(pg)
            out_ref[rows, :] = y.astype(out_ref.dtype)
        else:
            of_scr[rows, :] = o_sb

    st_scr[...] = s0_ref[0]

    def fwd(i, carry):
        superblock(i, 0, False)
        return carry

    lax.fori_loop(0, nsb, fwd, 0)
    if emit_state:
        st_ref[0] = st_scr[...]
    st_scr[...] = s0_ref[1]

    def bwd(i, carry):
        superblock(nsb - 1 - i, 1, True)
        return carry

    lax.fori_loop(0, nsb, bwd, 0)
    if emit_state:
        st_ref[1] = st_scr[...]


def _hgrn(p3, s0_bd, lb_rows, norm_g, emit_state):
    b, seq, _ = p3.shape
    tri, bones, head = _hgrn_consts()
    const2 = lambda i: (0, 0)
    out_shape = [jax.ShapeDtypeStruct((b, seq, GROUP_W), BF16)]
    out_specs = [pl.BlockSpec((None, seq, GROUP_W), lambda i: (i, 0, 0))]
    if emit_state:
        out_shape.append(jax.ShapeDtypeStruct((b, 2, GROUP_W, GROUP_W), F32))
        out_specs.append(pl.BlockSpec((None, 2, GROUP_W, GROUP_W), lambda i: (i, 0, 0, 0)))
    res = pl.pallas_call(
        functools.partial(_hgrn_kernel, seq=seq, emit_state=emit_state),
        grid=(b,),
        in_specs=[
            pl.BlockSpec((None, seq, 5 * GROUP_W), lambda i: (i, 0, 0)),
            pl.BlockSpec((None, 2, GROUP_W, GROUP_W), lambda i: (i, 0, 0, 0)),
            pl.BlockSpec((8, GROUP_W), const2),
            pl.BlockSpec((1, GROUP_W), const2),
            pl.BlockSpec((2, HG_SB, HG_SB), lambda i: (0, 0, 0)),
            pl.BlockSpec((HG_SB, HG_SB), const2),
            pl.BlockSpec((GROUP_W, GROUP_W), const2),
            pl.BlockSpec((GROUP_W, GROUP_W), const2),
        ],
        out_specs=out_specs,
        out_shape=out_shape,
        scratch_shapes=[
            pltpu.VMEM((seq, GROUP_W), F32),
            pltpu.VMEM((GROUP_W, GROUP_W), F32),
            pltpu.VMEM((HG_SB, GROUP_W), F32),
            pltpu.VMEM((HG_SB, GROUP_W), F32),
            pltpu.VMEM((HG_SB, GROUP_W), F32),
            pltpu.VMEM((HG_SB, GROUP_W), BF16),
            pltpu.VMEM((HG_SB, GROUP_W), BF16),
            pltpu.VMEM((HG_SB, GROUP_W), F32),
        ],
        name="hgrn2",
    )(p3, s0_bd, lb_rows, norm_g, jnp.asarray(tri, BF16), jnp.asarray(bones, BF16),
      jnp.asarray(head, BF16), jnp.asarray(head, F32))
    return res if emit_state else (res[0], None)


@functools.lru_cache(maxsize=None)
def _mlstm_consts():
    r = np.arange(ML_CHUNK)
    tri_f = (r[:, None] >= r[None, :]).astype(np.float32)
    return np.stack([tri_f, tri_f.T])


def _mlstm_kernel(p_ref, g_ref, gt_ref, conv_ref, brow_ref, bcol_ref, ng_ref,
                  c0_ref, n0_ref, m0_ref, tri_ref, out_ref, *rest, seq, conv_w, emit_state):
    if emit_state:
        cst_ref, nst_ref, mst_ref, qk_scr, hf_scr, c_scr, n_scr, m_scr = rest
    else:
        qk_scr, hf_scr, c_scr, n_scr, m_scr = rest
    nch = seq // ML_CHUNK
    rowid = lax.broadcasted_iota(jnp.int32, (ML_CHUNK, 1), 0)
    ti = lax.broadcasted_iota(jnp.int32, (ML_CHUNK, ML_CHUNK), 0)
    si = lax.broadcasted_iota(jnp.int32, (ML_CHUNK, ML_CHUNK), 1)

    def conv_chunk(ci, carry):
        r0 = pl.multiple_of(ci * ML_CHUNK, ML_CHUNK)
        zc = p_ref[pl.ds(r0, ML_CHUNK), 0:2 * GROUP_W]
        zp = pltpu.roll(zc, 1, axis=0)
        zn = pltpu.roll(zc, ML_CHUNK - 1, axis=0)
        if conv_w == ML_CHUNK:
            prev_row = jnp.zeros((1, 2 * GROUP_W), F32)
            next_row = prev_row
        else:
            prev_row = p_ref[pl.ds(jnp.maximum(r0 - 1, 0), 1), 0:2 * GROUP_W]
            prev_row = jnp.where(r0 % conv_w == 0, 0.0, prev_row)
            next_row = p_ref[pl.ds(jnp.minimum(r0 + ML_CHUNK, seq - 1), 1), 0:2 * GROUP_W]
            next_row = jnp.where((r0 + ML_CHUNK) % conv_w == 0, 0.0, next_row)
        zp = jnp.where(rowid == 0, prev_row, zp)
        zn = jnp.where(rowid == ML_CHUNK - 1, next_row, zn)
        out = conv_ref[0:1, :] * zp + conv_ref[1:2, :] * zc + conv_ref[2:3, :] * zn
        qk_scr[pl.ds(r0, ML_CHUNK), :] = _silu(out)
        return carry

    lax.fori_loop(0, nch, conv_chunk, 0)

    def chunk(ci, d, finalize):
        r0 = pl.multiple_of(ci * ML_CHUNK, ML_CHUNK)
        rows = pl.ds(r0, ML_CHUNK)
        g = g_ref[rows, :] + brow_ref[...]
        gt = gt_ref[ci] + bcol_ref[...]
        li = g[:, 0:2 * HEADS]
        lf = _log_sigmoid(g[:, 2 * HEADS:4 * HEADS])
        lit = gt[0:2 * HEADS, :]
        lft = _log_sigmoid(gt[2 * HEADS:4 * HEADS, :])
        bc_all = _dot_sel_l(tri_ref[d], lf)
        bct_all = _dot_sel_r(lft, tri_ref[1 - d])
        keep = (ti >= si) if d == 0 else (ti <= si)
        edge = ML_CHUNK - 1 if d == 0 else 0
        outs = []
        for h in range(HEADS):
            c = HEADS * d + h
            lo = HEAD_D * h
            q = qk_scr[rows, lo:lo + HEAD_D]
            k = qk_scr[rows, GROUP_W + lo:GROUP_W + lo + HEAD_D] * (HEAD_D ** -0.5)
            v = p_ref[rows, 2 * GROUP_W + lo:2 * GROUP_W + lo + HEAD_D]
            qb = q.astype(BF16)
            vb = v.astype(BF16)
            bcol = bc_all[:, c:c + 1]
            brow = bct_all[c:c + 1, :]
            m = m_scr[h]
            dmat = jnp.where(keep, bcol - brow + lit[c:c + 1, :], NEG_INF)
            prev = bcol + m
            m_t = jnp.maximum(prev, jnp.max(dmat, axis=1, keepdims=True))
            w = jnp.exp(dmat - m_t) * _dot_nt(qb, k.astype(BF16))
            wp = jnp.exp(prev - m_t)
            cst = c_scr[h]
            nrow = n_scr[h]
            num = wp * _dot(qb, cst.astype(BF16)) + _dot(w.astype(BF16), vb)
            den = wp * jnp.sum(q * nrow, axis=1, keepdims=True) + jnp.sum(w, axis=1, keepdims=True)
            hh = num / jnp.maximum(jnp.abs(den), jnp.exp(-m_t))
            blast = bcol[edge:edge + 1, :]
            gc = blast - bcol + li[:, c:c + 1]
            m_new = jnp.maximum(blast + m, jnp.max(gc, axis=0, keepdims=True))
            dec = jnp.exp(blast + m - m_new)
            kw = k * jnp.exp(gc - m_new)
            c_scr[h] = dec * cst + _dot_tn(kw.astype(BF16), vb)
            n_scr[h] = dec * nrow + jnp.sum(kw, axis=0, keepdims=True)
            m_scr[h] = m_new
            if finalize:
                hs = hh + hf_scr[rows, lo:lo + HEAD_D]
                hs = hs * lax.rsqrt(jnp.mean(hs * hs, axis=-1, keepdims=True) + EPS)
                outs.append(hs)
            else:
                outs.append(hh)
        o = jnp.concatenate(outs, axis=1)
        if finalize:
            po = p_ref[rows, 3 * GROUP_W:4 * GROUP_W]
            out_ref[rows, :] = (o * ng_ref[...] * _sigmoid(po)).astype(out_ref.dtype)
        else:
            hf_scr[rows, :] = o

    def run(d):
        c_scr[...] = c0_ref[d]
        n_scr[...] = n0_ref[d]
        m_scr[...] = m0_ref[d]

        def body(i, carry):
            chunk(i if d == 0 else nch - 1 - i, d, d == 1)
            return carry

        lax.fori_loop(0, nch, body, 0)
        if emit_state:
            cst_ref[d] = c_scr[...]
            nst_ref[d] = n_scr[...]
            mst_ref[d] = m_scr[...]

    run(0)
    run(1)


def _mlstm(p3, g3, gt4, conv_w3, brow, bcol, norm_g, c0, n0, m0, conv_width, emit_state):
    b, seq, _ = p3.shape
    nch = seq // ML_CHUNK
    const2 = lambda i: (0, 0)
    b5 = lambda i: (i, 0, 0, 0, 0)
    out_shape = [jax.ShapeDtypeStruct((b, seq, GROUP_W), BF16)]
    out_specs = [pl.BlockSpec((None, seq, GROUP_W), lambda i: (i, 0, 0))]
    st_shapes = [(2, HEADS, HEAD_D, HEAD_D), (2, HEADS, 1, HEAD_D), (2, HEADS, 1, 1)]
    if emit_state:
        for s in st_shapes:
            out_shape.append(jax.ShapeDtypeStruct((b,) + s, F32))
            out_specs.append(pl.BlockSpec((None,) + s, b5))
    res = pl.pallas_call(
        functools.partial(_mlstm_kernel, seq=seq, conv_w=conv_width, emit_state=emit_state),
        grid=(b,),
        in_specs=[
            pl.BlockSpec((None, seq, 4 * GROUP_W), lambda i: (i, 0, 0)),
            pl.BlockSpec((None, seq, N_GATES), lambda i: (i, 0, 0)),
            pl.BlockSpec((None, nch, N_GATES, ML_CHUNK), lambda i: (i, 0, 0, 0)),
            pl.BlockSpec((3, 2 * GROUP_W), const2),
            pl.BlockSpec((1, N_GATES), const2),
            pl.BlockSpec((N_GATES, 1), const2),
            pl.BlockSpec((1, GROUP_W), const2),
        ] + [pl.BlockSpec((None,) + s, b5) for s in st_shapes] + [
            pl.BlockSpec((2, ML_CHUNK, ML_CHUNK), lambda i: (0, 0, 0)),
        ],
        out_specs=out_specs,
        out_shape=out_shape,
        scratch_shapes=[
            pltpu.VMEM((seq, 2 * GROUP_W), F32),
            pltpu.VMEM((seq, GROUP_W), F32),
            pltpu.VMEM((HEADS, HEAD_D, HEAD_D), F32),
            pltpu.VMEM((HEADS, 1, HEAD_D), F32),
            pltpu.VMEM((HEADS, 1, 1), F32),
        ],
        name="mlstm",
    )(p3, g3, gt4, conv_w3, brow, bcol, norm_g, c0, n0, m0, jnp.asarray(_mlstm_consts(), BF16))
    if emit_state:
        return res
    return res[0], None, None, None


def _s5_kernel(*refs, nb, reverse, final):
    if final:
        (u_ref, hre0_ref, him0_ref, are_ref, aim_ref, wb_ref, wc_ref, yin_ref, d_ref, gw_ref, gb_ref,
         y_ref, sre_ref, sim_ref, utm_scr, bu_scr, ytm_scr, hre_scr, him_scr) = refs
    else:
        (u_ref, hre0_ref, him0_ref, are_ref, aim_ref, wb_ref, wc_ref,
         y_ref, sre_ref, sim_ref, utm_scr, bu_scr, ytm_scr, hre_scr, him_scr) = refs
    tc = S5_TC
    step_id = pl.program_id(0)

    @pl.when(step_id == 0)
    def _():
        hre_scr[...] = hre0_ref[...]
        him_scr[...] = him0_ref[...]

    halves = GROUP_W // LANES
    for b in range(nb):
        for hv in range(halves):
            utm_scr[hv, pl.ds(b, tc, stride=nb), :] = u_ref[b, :, hv * LANES:(hv + 1) * LANES]
    u_tm = jnp.concatenate([utm_scr[hv] for hv in range(halves)], axis=1)
    bu_scr[...] = _dot(u_tm.astype(BF16), wb_ref[...])
    are = jnp.broadcast_to(are_ref[...], (nb, S5_STATE))
    aim = jnp.broadcast_to(aim_ref[...], (nb, S5_STATE))

    def step(i, carry):
        hre, him = carry
        t = (tc - 1 - i) if reverse else i
        r0 = pl.multiple_of(t * nb, nb)
        bu = bu_scr[pl.ds(r0, nb), :]
        nre = are * hre - aim * him + bu[:, 0:S5_STATE]
        nim = are * him + aim * hre + bu[:, S5_STATE:2 * S5_STATE]
        bu_scr[pl.ds(r0, nb), 0:S5_STATE] = nre
        bu_scr[pl.ds(r0, nb), S5_STATE:2 * S5_STATE] = nim
        return nre, nim

    hre, him = lax.fori_loop(0, tc, step, (hre_scr[...], him_scr[...]))
    hre_scr[...] = hre
    him_scr[...] = him
    y_tm = _dot(bu_scr[...].astype(BF16), wc_ref[...])
    for hv in range(halves):
        ytm_scr[hv] = y_tm[:, hv * LANES:(hv + 1) * LANES]
    for b in range(nb):
        yb = jnp.concatenate([ytm_scr[hv, pl.ds(b, tc, stride=nb), :] for hv in range(halves)], axis=1)
        if final:
            y = yb + yin_ref[b] + d_ref[...] * u_ref[b]
            z = 0.5 * y * (1.0 + jnp.tanh(math.sqrt(2.0 / math.pi) * (y + 0.044715 * (y * y * y))))
            gate = _sigmoid(_dot(z.astype(BF16), gw_ref[...]) + gb_ref[...])
            y_ref[b] = (z * gate).astype(y_ref.dtype)
        else:
            y_ref[b] = yb

    @pl.when(step_id == pl.num_programs(0) - 1)
    def _():
        sre_ref[...] = hre
        sim_ref[...] = him


def _s5_pass(u3, hre0, him0, are, aim, wb, wc, reverse, extra):
    nb, seq, _ = u3.shape
    nch = seq // S5_TC
    final = extra is not None
    chunk = (lambda i: (0, nch - 1 - i, 0)) if reverse else (lambda i: (0, i, 0))
    const2 = lambda i: (0, 0)
    in_specs = [
        pl.BlockSpec((nb, S5_TC, GROUP_W), chunk),
        pl.BlockSpec((nb, S5_STATE), const2),
        pl.BlockSpec((nb, S5_STATE), const2),
        pl.BlockSpec((1, S5_STATE), const2),
        pl.BlockSpec((1, S5_STATE), const2),
        pl.BlockSpec((GROUP_W, 2 * S5_STATE), const2),
        pl.BlockSpec((2 * S5_STATE, GROUP_W), const2),
    ]
    args = [u3, hre0, him0, are, aim, wb, wc]
    if final:
        yin, dvec, gw, gb = extra
        in_specs += [pl.BlockSpec((nb, S5_TC, GROUP_W), chunk), pl.BlockSpec((1, GROUP_W), const2),
                     pl.BlockSpec((GROUP_W, GROUP_W), const2), pl.BlockSpec((1, GROUP_W), const2)]
        args += [yin, dvec, gw, gb]
    return pl.pallas_call(
        functools.partial(_s5_kernel, nb=nb, reverse=reverse, final=final),
        grid=(nch,),
        in_specs=in_specs,
        out_specs=[pl.BlockSpec((nb, S5_TC, GROUP_W), chunk),
                   pl.BlockSpec((nb, S5_STATE), const2), pl.BlockSpec((nb, S5_STATE), const2)],
        out_shape=[jax.ShapeDtypeStruct((nb, seq, GROUP_W), BF16 if final else F32),
                   jax.ShapeDtypeStruct((nb, S5_STATE), F32), jax.ShapeDtypeStruct((nb, S5_STATE), F32)],
        scratch_shapes=[
            pltpu.VMEM((GROUP_W // LANES, nb * S5_TC, LANES), F32),
            pltpu.VMEM((nb * S5_TC, 2 * S5_STATE), F32),
            pltpu.VMEM((GROUP_W // LANES, nb * S5_TC, LANES), F32),
            pltpu.VMEM((nb, S5_STATE), F32),
            pltpu.VMEM((nb, S5_STATE), F32),
        ],
        name="s5_bwd" if reverse else "s5_fwd",
    )(*args)


def _s5_params(a_re, a_im, log_dt, b_re, b_im, c_re, c_im):
    eye = jnp.eye(S5_GROUPS, dtype=F32)
    dt = jnp.exp(log_dt)[..., None]
    mag = jnp.exp(a_re * dt)
    ab_re, ab_im = mag * jnp.cos(a_im * dt), mag * jnp.sin(a_im * dt)
    den = a_re * a_re + a_im * a_im
    g_re = ((ab_re - 1.0) * a_re + ab_im * a_im) / den
    g_im = (ab_im * a_re - (ab_re - 1.0) * a_im) / den
    bb_re = g_re[..., None] * b_re - g_im[..., None] * b_im
    bb_im = g_re[..., None] * b_im + g_im[..., None] * b_re

    def in_mat(bb):
        return jnp.einsum('dgpc,gh->dgchp', bb, eye).reshape(2, GROUP_W, S5_STATE)

    wb = jnp.concatenate([in_mat(bb_re), in_mat(bb_im)], axis=-1).astype(BF16)

    def out_mat(cc):
        return jnp.einsum('gcp,gh->gphc', cc, eye).reshape(S5_STATE, GROUP_W)

    wc = jnp.concatenate([out_mat(c_re), -out_mat(c_im)], axis=0).astype(BF16)
    return ab_re.reshape(2, 1, S5_STATE), ab_im.reshape(2, 1, S5_STATE), wb, wc


def _s5(u3, h0_re, h0_im, prm, d_vec, glu_w, glu_b):
    ab_re, ab_im, wb, wc = prm
    yf, fre, fim = _s5_pass(u3, h0_re[:, 0], h0_im[:, 0], ab_re[0], ab_im[0], wb[0], wc, False, None)
    out, bre, bim = _s5_pass(u3, h0_re[:, 1], h0_im[:, 1], ab_re[1], ab_im[1], wb[1], wc, True,
                             (yf, d_vec, glu_w, glu_b))
    return out, jnp.stack([fre, bre], axis=1), jnp.stack([fim, bim], axis=1)


@functools.lru_cache(maxsize=None)
def _dft_consts(n):
    idx = np.arange(n, dtype=np.int64)
    ang = (np.pi / n) * ((idx[:, None] * idx[None, :]) % (2 * n)).astype(np.float64)
    cos, sin = np.cos(ang), np.sin(ang)
    sign = np.where(idx % 2 == 0, 1.0, -1.0)
    sin_n = sin.copy()
    sin_n[0, :] = sign
    fwd = np.concatenate([cos, sin_n], axis=0)
    inv_c = cos.T / n
    inv_c[:, 0] = 1.0 / (2 * n)
    inv_s = sin.T / n
    inv_s[:, 0] = sign / (2 * n)
    inv = np.concatenate([inv_c, inv_s], axis=1)
    return fwd.astype(np.float32), inv.astype(np.float32), sign.astype(np.float32)[:, None]


@functools.lru_cache(maxsize=None)
def _hyena_feats(n):
    t = np.linspace(0.0, 1.0, n, dtype=np.float64)[:, None]
    w = (2.0 * np.pi / n) * np.arange(n, dtype=np.float64)[:, None]
    bands = np.linspace(1e-4, HY_BANDS - 1, HY_BANDS, dtype=np.float64)[None, :]
    feats = np.concatenate([t, np.cos(w * bands), -np.sin(w * bands)], axis=-1)
    pad = np.zeros((n, LANES - HY_EMB))
    return np.concatenate([feats, pad], axis=-1).astype(np.float32), t.astype(np.float32)


def _hyfilt_kernel(feat_ref, t_ref, sign_ref, w1_ref, b1_ref, w2_ref, b2_ref, w3_ref, fr_ref, dec_ref,
                   fh_ref, fl_ref, ka_ref, kb_ref, ka2_ref, *, n):
    freq = fr_ref[...]
    hdn = jnp.sin(freq * (_dot3(feat_ref[...], w1_ref[...]) + b1_ref[...]))
    hdn = jnp.sin(freq * (_dot3(hdn, w2_ref[...]) + b2_ref[...]))
    h = _dot3(hdn, w3_ref[...]) * (jnp.exp(-t_ref[...] * jnp.abs(dec_ref[...])) + HY_SHIFT)
    half = HY_ORDER * GROUP_W
    row = lax.broadcasted_iota(jnp.int32, (n, 1), 0)
    h0 = h[:, 0:half]
    h1 = jnp.where(row == 0, 0.0, h[:, half:2 * half])
    norm = jnp.sum(jnp.abs(h0), axis=0, keepdims=True) + jnp.sum(jnp.abs(h1), axis=0, keepdims=True)
    hp = (h0 + h1) / norm
    hm = (h0 - h1) / norm

    def dft(lo, x):
        xh = x.astype(BF16)
        xl = (x - xh.astype(F32)).astype(BF16)
        fh = fh_ref[lo:lo + n, :]
        return _dot(fh, xh) + _dot(fl_ref[lo:lo + n, :], xh) + _dot(fh, xl)

    kc = dft(0, hp)
    ks = dft(n, hm)
    kn = jnp.sum(hp * sign_ref[...], axis=0, keepdims=True)
    ka_ref[...] = kc
    kb_ref[...] = jnp.where(row == 0, 0.0, ks)
    ka2_ref[...] = jnp.where(row == 0, kn, kc)


def _hyena_filter(n, w1p, b1, w2, b2, w3, freq, decay):
    fwd, _, sign = _dft_consts(n)
    feats, t = _hyena_feats(n)
    fwd = jnp.asarray(fwd)
    fh = fwd.astype(BF16)
    fl = (fwd - fh.astype(F32)).astype(BF16)
    half = HY_ORDER * GROUP_W
    return pl.pallas_call(
        functools.partial(_hyfilt_kernel, n=n),
        out_shape=[jax.ShapeDtypeStruct((n, half), F32)] * 3,
        name="hyena_filter",
    )(jnp.asarray(feats), jnp.asarray(t), jnp.asarray(sign), w1p, b1, w2, b2, w3, freq, decay, fh, fl)


def _hyena_kernel(p_ref, cw_ref, ka_ref, kb_ref, ka2_ref, bias_ref, f_ref, g_ref, out_ref,
                  z_scr, x_scr, zf_scr, *, seq, conv_w):
    nch = seq // ML_CHUNK
    rowid = lax.broadcasted_iota(jnp.int32, (ML_CHUNK, 1), 0)

    def conv_chunk(ci, carry):
        r0 = pl.multiple_of(ci * ML_CHUNK, ML_CHUNK)
        zc = p_ref[pl.ds(r0, ML_CHUNK), :]
        zp = pltpu.roll(zc, 1, axis=0)
        zn = pltpu.roll(zc, ML_CHUNK - 1, axis=0)
        if conv_w == ML_CHUNK:
            prev_row = jnp.zeros((1, 3 * GROUP_W), F32)
            next_row = prev_row
        else:
            prev_row = p_ref[pl.ds(jnp.maximum(r0 - 1, 0), 1), :]
            prev_row = jnp.where(r0 % conv_w == 0, 0.0, prev_row)
            next_row = p_ref[pl.ds(jnp.minimum(r0 + ML_CHUNK, seq - 1), 1), :]
            next_row = jnp.where((r0 + ML_CHUNK) % conv_w == 0, 0.0, next_row)
        zp = jnp.where(rowid == 0, prev_row, zp)
        zn = jnp.where(rowid == ML_CHUNK - 1, next_row, zn)
        z_scr[pl.ds(r0, ML_CHUNK), :] = cw_ref[0:1, :] * zp + cw_ref[1:2, :] * zc + cw_ref[2:3, :] * zn
        return carry

    lax.fori_loop(0, nch, conv_chunk, 0)
    y = z_scr[:, 0:GROUP_W]
    for o in range(HY_ORDER):
        cols = slice(o * GROUP_W, (o + 1) * GROUP_W)
        x_scr[...] = _dot(f_ref[...], y.astype(BF16))
        xc = x_scr[0:seq, :]
        xs = x_scr[seq:2 * seq, :]
        kb = kb_ref[:, cols]
        zf_scr[0:seq, :] = (xc * ka_ref[:, cols] - xs * kb).astype(BF16)
        zf_scr[seq:2 * seq, :] = (xc * kb + xs * ka2_ref[:, cols]).astype(BF16)
        conv = _dot(g_ref[...], zf_scr[...])
        gate = z_scr[:, (o + 1) * GROUP_W:(o + 2) * GROUP_W]
        y = gate * (conv + bias_ref[o:o + 1, :] * y)
    out_ref[...] = y.astype(out_ref.dtype)


def _hyena(p3, conv_w3, filt, bias, conv_width):
    b, seq, _ = p3.shape
    ka, kb, ka2 = filt
    fwd, inv, _ = _dft_consts(seq)
    const2 = lambda i: (0, 0)
    half = HY_ORDER * GROUP_W
    return pl.pallas_call(
        functools.partial(_hyena_kernel, seq=seq, conv_w=conv_width),
        grid=(b,),
        in_specs=[
            pl.BlockSpec((None, seq, 3 * GROUP_W), lambda i: (i, 0, 0)),
            pl.BlockSpec((3, 3 * GROUP_W), const2),
            pl.BlockSpec((seq, half), const2),
            pl.BlockSpec((seq, half), const2),
            pl.BlockSpec((seq, half), const2),
            pl.BlockSpec((HY_ORDER, GROUP_W), const2),
            pl.BlockSpec((2 * seq, seq), const2),
            pl.BlockSpec((seq, 2 * seq), const2),
        ],
        out_specs=pl.BlockSpec((None, seq, GROUP_W), lambda i: (i, 0, 0)),
        out_shape=jax.ShapeDtypeStruct((b, seq, GROUP_W), BF16),
        scratch_shapes=[
            pltpu.VMEM((seq, 3 * GROUP_W), F32),
            pltpu.VMEM((2 * seq, GROUP_W), F32),
            pltpu.VMEM((2 * seq, GROUP_W), BF16),
        ],
        name="hyena",
    )(p3, conv_w3, ka, kb, ka2, bias, jnp.asarray(fwd, BF16), jnp.asarray(inv, BF16))


def _outproj_kernel(a_ref, b_ref, c_ref, d_ref, x_ref, g1_ref, sh2_ref, sc2_ref, n2_ref, wo_ref,
                    wr_ref, br_ref, x1_ref, h2_ref, gate_ref):
    acc = _dot(a_ref[...], wo_ref[0:GROUP_W, :])
    acc += _dot(b_ref[...], wo_ref[GROUP_W:2 * GROUP_W, :])
    acc += _dot(c_ref[...], wo_ref[2 * GROUP_W:3 * GROUP_W, :])
    acc += _dot(d_ref[...], wo_ref[3 * GROUP_W:4 * GROUP_W, :])
    x1 = x_ref[...] + g1_ref[0] * acc
    x1_ref[...] = x1
    h2 = _rmsnorm_rows(x1, n2_ref[...]) * (1.0 + sc2_ref[0]) + sh2_ref[0]
    h2_ref[...] = h2.astype(BF16)
    logits = _dot3(h2, wr_ref[...]) + br_ref[...]
    lane = lax.broadcasted_iota(jnp.int32, logits.shape, 1).astype(F32)
    big = float(LANES)
    gl = jnp.where(lane < N_EXPERT_GROUPS, logits, NEG_INF)
    gmax = jnp.max(gl, axis=1, keepdims=True)
    gsel = jnp.min(jnp.where(gl == gmax, lane, big), axis=1, keepdims=True)
    psel = 1.0 / jnp.sum(jnp.exp(gl - gmax), axis=1, keepdims=True)
    lo = N_EXPERT_GROUPS + EXPERTS_PER_GROUP * gsel
    el = jnp.where((lane >= lo) & (lane < lo + EXPERTS_PER_GROUP), logits, NEG_INF)
    v1 = jnp.max(el, axis=1, keepdims=True)
    i1 = jnp.min(jnp.where(el == v1, lane, big), axis=1, keepdims=True)
    el2 = jnp.where(lane == i1, NEG_INF, el)
    v2 = jnp.max(el2, axis=1, keepdims=True)
    i2 = jnp.min(jnp.where(el2 == v2, lane, big), axis=1, keepdims=True)
    e2 = jnp.exp(v2 - v1)
    w1 = psel / (1.0 + e2)
    w2 = psel * e2 / (1.0 + e2)
    gate_ref[...] = jnp.where(lane == i1, w1, 0.0) + jnp.where(lane == i2, w2, 0.0)


def _outproj(mix, x2d, mod3, row_of_tile, norm_g, w_out, w_r, b_r):
    t = x2d.shape[0]
    tm = TOK_TILE
    const = lambda i: (0, 0)
    tile = lambda i: (i, 0)
    return pl.pallas_call(
        _outproj_kernel,
        grid=(t // tm,),
        in_specs=[pl.BlockSpec((tm, GROUP_W), tile)] * 4 + [
            pl.BlockSpec((tm, D_MODEL), tile),
            _mod_spec(2, row_of_tile),
            _mod_spec(3, row_of_tile),
            _mod_spec(4, row_of_tile),
            pl.BlockSpec((1, D_MODEL), const),
            pl.BlockSpec((D_MODEL, D_MODEL), const),
            pl.BlockSpec((D_MODEL, LANES), const),
            pl.BlockSpec((1, LANES), const),
        ],
        out_specs=[pl.BlockSpec((tm, D_MODEL), tile), pl.BlockSpec((tm, D_MODEL), tile),
                   pl.BlockSpec((tm, LANES), tile)],
        out_shape=[jax.ShapeDtypeStruct((t, D_MODEL), F32), jax.ShapeDtypeStruct((t, D_MODEL), BF16),
                   jax.ShapeDtypeStruct((t, LANES), F32)],
        name="outproj_router",
    )(*mix, x2d, mod3, mod3, mod3, norm_g, w_out, w_r, b_r)


def _moe_kernel(h_ref, gate_ref, wg_ref, wu_ref, wd_ref, x1_ref, g2_ref, fn_ref, out_ref, acc_scr, *, final):
    e = pl.program_id(1)

    @pl.when(e == 0)
    def _():
        acc_scr[...] = jnp.zeros_like(acc_scr)

    h = h_ref[...]
    a = _dot(h, wg_ref[0])
    u = _dot(h, wu_ref[0])
    hid = (_silu(a) * u * gate_ref[0]).astype(BF16)
    acc_scr[...] += _dot(hid, wd_ref[0])

    @pl.when(e == pl.num_programs(1) - 1)
    def _():
        x2 = x1_ref[...] + g2_ref[0] * acc_scr[...]
        if final:
            x2 = _rmsnorm_rows(x2, fn_ref[...])
        out_ref[...] = x2


def _moe(h2, gates_e, w_gate, w_up, w_down, x1, mod3, tm, row_of_tile, final_g, final):
    t = h2.shape[0]
    tile = lambda i, e: (i, 0)
    return pl.pallas_call(
        functools.partial(_moe_kernel, final=final),
        grid=(t // tm, N_EXPERTS),
        in_specs=[
            pl.BlockSpec((tm, D_MODEL), tile),
            pl.BlockSpec((1, tm, 1), lambda i, e: (e, i, 0)),
            pl.BlockSpec((1, D_MODEL, D_EXPERT), lambda i, e: (e, 0, 0)),
            pl.BlockSpec((1, D_MODEL, D_EXPERT), lambda i, e: (e, 0, 0)),
            pl.BlockSpec((1, D_EXPERT, D_MODEL), lambda i, e: (e, 0, 0)),
            pl.BlockSpec((tm, D_MODEL), tile),
            pl.BlockSpec((1, 1, D_MODEL), lambda i, e: (row_of_tile(i), 0, 5)),
            pl.BlockSpec((1, D_MODEL), lambda i, e: (0, 0)),
        ],
        out_specs=pl.BlockSpec((tm, D_MODEL), tile),
        out_shape=jax.ShapeDtypeStruct((t, D_MODEL), F32),
        scratch_shapes=[pltpu.VMEM((tm, D_MODEL), F32)],
        name="moe",
    )(h2, gates_e, w_gate, w_up, w_down, x1, mod3, final_g)


def _block_diag_t(s):
    eye = jnp.eye(HEADS, dtype=s.dtype)
    b = s.shape[0]
    return jnp.einsum('bdhkv,hg->bdhvgk', s, eye).reshape(b, 2, GROUP_W, GROUP_W)


def _block_diag_t_inv(st):
    b = st.shape[0]
    s6 = st.reshape(b, 2, HEADS, HEAD_D, HEADS, HEAD_D)
    idx = jnp.arange(HEADS)
    diag = s6[:, :, idx, :, idx, :]
    return jnp.transpose(diag, (1, 2, 0, 4, 3))


def _prep_layer(l, w):
    gw = GROUP_W
    w_in = w['w_in'][l]
    w_g = w_in[:, 9 * gw:9 * gw + N_GATES]
    lb_all = jnp.cumsum(jax.nn.softmax(w['hg_lb'].astype(F32), axis=0), axis=0)
    lb = lb_all[l] - lb_all[0]
    lb_rows = jnp.zeros((8, gw), F32).at[0].set(jnp.log(lb)).at[1].set(jnp.log1p(-lb)).at[2].set(1.0 - lb)
    w_r = jnp.zeros((D_MODEL, LANES), F32)
    w_r = w_r.at[:, 0:N_EXPERT_GROUPS].set(w['moe_wg'][l])
    w_r = w_r.at[:, N_EXPERT_GROUPS:N_EXPERT_GROUPS + N_EXPERTS].set(w['moe_we'][l])
    b_r = jnp.zeros((1, LANES), F32)
    b_r = b_r.at[0, 0:N_EXPERT_GROUPS].set(w['moe_bg'][l])
    b_r = b_r.at[0, N_EXPERT_GROUPS:N_EXPERT_GROUPS + N_EXPERTS].set(w['moe_be'][l])
    gate_bias = jnp.concatenate([w['ml_i_bias'][l], w['ml_f_bias'][l]])
    w1p = jnp.zeros((LANES, HY_HIDDEN), F32).at[0:HY_EMB].set(w['hy_w1'][l])
    return dict(
        norm1=w['norm1'][l][None, :], norm2=w['norm2'][l][None, :],
        w_main=jnp.concatenate([w_in[:, :9 * gw], w_in[:, 9 * gw + N_GATES:]], axis=1).astype(BF16),
        w_g=w_g.astype(BF16), w_gt=w_g.T.astype(BF16),
        w_out=w['w_out'][l].astype(BF16),
        lb_rows=lb_rows, hg_norm=w['hg_norm'][l][None, :],
        ml_conv=w['ml_conv'][l], ml_brow=gate_bias[None, :], ml_bcol=gate_bias[:, None],
        ml_norm=w['ml_norm'][l][None, :],
        s5=_s5_params(w['s5_a_re'][l], w['s5_a_im'][l], w['s5_log_dt'][l], w['s5_b_re'][l], w['s5_b_im'][l],
                      w['s5_c_re'][l], w['s5_c_im'][l]),
        s5_d=w['s5_d'][l][None, :], s5_glu_w=w['s5_glu_w'][l].astype(BF16), s5_glu_b=w['s5_glu_b'][l][None, :],
        hy_conv=w['hy_conv'][l], hy_bias=w['hy_bias'][l],
        hy_mlp=(w1p, w['hy_b1'][l][None, :], w['hy_w2'][l], w['hy_b2'][l][None, :], w['hy_w3'][l],
                w['hy_freq'][l][None, :], w['hy_decay'][l][None, :]),
        w_r=w_r, b_r=b_r,
        moe_gate=w['moe_w_gate'][l].astype(BF16), moe_up=w['moe_w_up'][l].astype(BF16),
        moe_down=w['moe_w_down'][l].astype(BF16),
    )


def _trunk_layer(x2d, nb, seq, mod3, cond_row, st, lw, filt, conv_width, emit_state, final_g, final):
    if cond_row is None:
        moe_tile = min(MOE_TILE, seq)
        row_of_tile = lambda i: i // (seq // TOK_TILE)
        moe_row_of_tile = lambda i: i // (seq // moe_tile)
    else:
        moe_tile = min(MOE_TILE, nb * seq)
        row_of_tile = moe_row_of_tile = lambda i: cond_row
    hg, ml, su, hy, gt, gtt = _inproj(x2d, mod3, row_of_tile, lw['norm1'], lw['w_main'], lw['w_g'], lw['w_gt'])
    nch = seq // ML_CHUNK
    a_out, hg_st = _hgrn(hg.reshape(nb, seq, -1), st['hgrn'], lw['lb_rows'], lw['hg_norm'], emit_state)
    gt4 = jnp.transpose(gtt.reshape(N_GATES, nb, nch, ML_CHUNK), (1, 2, 0, 3))
    b_out, mlc, mln, mlm = _mlstm(ml.reshape(nb, seq, -1), gt.reshape(nb, seq, N_GATES), gt4, lw['ml_conv'],
                                  lw['ml_brow'], lw['ml_bcol'], lw['ml_norm'], st['ml_c'], st['ml_n'],
                                  st['ml_m'], conv_width, emit_state)
    c_out, s5re, s5im = _s5(su.reshape(nb, seq, -1), st['s5_re'], st['s5_im'], lw['s5'], lw['s5_d'],
                            lw['s5_glu_w'], lw['s5_glu_b'])
    d_out = _hyena(hy.reshape(nb, seq, -1), lw['hy_conv'], filt, lw['hy_bias'], conv_width)
    mix = [m.reshape(nb * seq, GROUP_W) for m in (a_out, b_out, c_out, d_out)]
    x1, h2, gates = _outproj(mix, x2d, mod3, row_of_tile, lw['norm2'], lw['w_out'], lw['w_r'], lw['b_r'])
    gates_e = jnp.transpose(gates[:, N_EXPERT_GROUPS:N_EXPERT_GROUPS + N_EXPERTS])[:, :, None]
    x2 = _moe(h2, gates_e, lw['moe_gate'], lw['moe_up'], lw['moe_down'], x1, mod3, moe_tile,
              moe_row_of_tile, final_g, final)
    new_st = dict(hgrn=hg_st, ml_c=mlc, ml_n=mln, ml_m=mlm, s5_re=s5re, s5_im=s5im)
    return x2, new_st


def kernel(x_prompt, x_sample, c, state_hgrn, state_mlstm_c, state_mlstm_n, state_mlstm_m, state_s5_re, state_s5_im, c_ctx, w_ada, b_ada, norm1, norm2, w_in, w_out, hg_lb, hg_norm, ml_conv, ml_i_bias, ml_f_bias, ml_norm, s5_a_re, s5_a_im, s5_log_dt, s5_b_re, s5_b_im, s5_c_re, s5_c_im, s5_d, s5_glu_w, s5_glu_b, hy_conv, hy_w1, hy_b1, hy_w2, hy_b2, hy_w3, hy_freq, hy_decay, hy_bias, moe_wg, moe_bg, moe_we, moe_be, moe_w_gate, moe_w_up, moe_w_down, final_norm):
    w = dict(w_in=w_in, w_out=w_out, norm1=norm1, norm2=norm2, hg_lb=hg_lb, hg_norm=hg_norm, ml_conv=ml_conv,
             ml_i_bias=ml_i_bias, ml_f_bias=ml_f_bias, ml_norm=ml_norm, s5_a_re=s5_a_re, s5_a_im=s5_a_im,
             s5_log_dt=s5_log_dt, s5_b_re=s5_b_re, s5_b_im=s5_b_im, s5_c_re=s5_c_re, s5_c_im=s5_c_im,
             s5_d=s5_d, s5_glu_w=s5_glu_w, s5_glu_b=s5_glu_b, hy_conv=hy_conv, hy_w1=hy_w1, hy_b1=hy_b1,
             hy_w2=hy_w2, hy_b2=hy_b2, hy_w3=hy_w3, hy_freq=hy_freq, hy_decay=hy_decay, hy_bias=hy_bias,
             moe_wg=moe_wg, moe_bg=moe_bg, moe_we=moe_we, moe_be=moe_be, moe_w_gate=moe_w_gate,
             moe_w_up=moe_w_up, moe_w_down=moe_w_down)
    bp, lp, _ = x_prompt.shape
    bs, ls, _ = x_sample.shape
    ctx_row = bs
    cond = jnp.zeros((MOD_ROWS, D_MODEL), F32).at[0:bs].set(c).at[ctx_row].set(c_ctx)
    mod = _ada_mod(cond, w_ada, b_ada)
    final_g = final_norm[None, :]

    zero_st = dict(
        hgrn=jnp.zeros((bp, 2, GROUP_W, GROUP_W), F32),
        ml_c=jnp.zeros((bp, 2, HEADS, HEAD_D, HEAD_D), F32),
        ml_n=jnp.zeros((bp, 2, HEADS, 1, HEAD_D), F32),
        ml_m=jnp.zeros((bp, 2, HEADS, 1, 1), F32),
        s5_re=jnp.zeros((bp, 2, S5_STATE), F32),
        s5_im=jnp.zeros((bp, 2, S5_STATE), F32))

    xp = x_prompt.reshape(bp * lp, D_MODEL)
    xs = x_sample.reshape(bs * ls, D_MODEL)
    per_layer = []
    for l in range(DEPTH):
        lw = _prep_layer(l, w)
        mod3 = mod[l].reshape(MOD_ROWS, 1, 6 * D_MODEL)
        last = l == DEPTH - 1
        filt_p = _hyena_filter(lp, *lw['hy_mlp'])
        filt_s = filt_p if ls == lp else _hyena_filter(ls, *lw['hy_mlp'])
        xp, st = _trunk_layer(xp, bp, lp, mod3, ctx_row, zero_st, lw, filt_p, lp, True, final_g, last)
        per_layer.append(st)
        st_in = dict(
            hgrn=_block_diag_t(state_hgrn[:, l].astype(F32)),
            ml_c=state_mlstm_c[:, l].astype(F32),
            ml_n=state_mlstm_n[:, l].astype(F32)[:, :, :, None, :],
            ml_m=state_mlstm_m[:, l].astype(F32)[:, :, :, None, None],
            s5_re=state_s5_re[:, l].astype(F32).reshape(bs, 2, S5_STATE),
            s5_im=state_s5_im[:, l].astype(F32).reshape(bs, 2, S5_STATE))
        xs, _ = _trunk_layer(xs, bs, ls, mod3, None, st_in, lw, filt_s, GRID_W, False, final_g, last)

    new_hgrn = jnp.stack([_block_diag_t_inv(s['hgrn']) for s in per_layer], axis=1)
    new_ml_c = jnp.stack([s['ml_c'] for s in per_layer], axis=1)
    new_ml_n = jnp.stack([s['ml_n'][:, :, :, 0, :] for s in per_layer], axis=1)
    new_ml_m = jnp.stack([s['ml_m'][:, :, :, 0, 0] for s in per_layer], axis=1)
    new_s5_re = jnp.stack([s['s5_re'].reshape(bp, 2, S5_GROUPS, S5_P) for s in per_layer], axis=1)
    new_s5_im = jnp.stack([s['s5_im'].reshape(bp, 2, S5_GROUPS, S5_P) for s in per_layer], axis=1)
    return (xp.reshape(bp, lp, D_MODEL), xs.reshape(bs, ls, D_MODEL),
            new_hgrn, new_ml_c, new_ml_n, new_ml_m, new_s5_re, new_s5_im)
```

```python
import functools
import math

import numpy as np
import jax
import jax.numpy as jnp
from jax import lax
from jax.experimental import pallas as pl
from jax.experimental.pallas import tpu as pltpu

F32 = jnp.float32
BF16 = jnp.bfloat16

D_MODEL = 1024
DEPTH = 2
GRID_W = 64
GROUP_W = D_MODEL // 4
HEADS = 4
HEAD_D = GROUP_W // HEADS
S5_CH = 16
S5_GROUPS = GROUP_W // S5_CH
S5_P = 64
S5_STATE = S5_GROUPS * S5_P
HY_ORDER = 2
HY_BANDS = 16
HY_EMB = 1 + 2 * HY_BANDS
HY_HIDDEN = 64
HY_SHIFT = 0.05
N_EXPERT_GROUPS = 4
EXPERTS_PER_GROUP = 4
N_EXPERTS = N_EXPERT_GROUPS * EXPERTS_PER_GROUP
D_EXPERT = D_MODEL // 2
EPS = 1e-6
N_GATES = 4 * HEADS

LANES = 128
MOD_ROWS = 16
TOK_TILE = 256
MOE_TILE = 1024
HG_BLK = 16
HG_SB = 128
ML_CHUNK = 64
S5_TC = 64
NEG_INF = float("-inf")


def _dot(a, b):
    return jnp.dot(a, b, preferred_element_type=F32)


def _dot_nt(a, b):
    return lax.dot_general(a, b, (((1,), (1,)), ((), ())), preferred_element_type=F32)


def _dot_tn(a, b):
    return lax.dot_general(a, b, (((0,), (0,)), ((), ())), preferred_element_type=F32)


def _split3(x):
    hi = x.astype(BF16)
    r1 = x - hi.astype(F32)
    mid = r1.astype(BF16)
    lo = (r1 - mid.astype(F32)).astype(BF16)
    return hi, mid, lo


def _dot_sel_l(m, x):
    hi, mid, lo = _split3(x)
    return _dot(m, hi) + _dot(m, mid) + _dot(m, lo)


def _dot_sel_r(x, m):
    hi, mid, lo = _split3(x)
    return _dot(hi, m) + _dot(mid, m) + _dot(lo, m)


def _dot3(a, b):
    ah = a.astype(BF16)
    al = (a - ah.astype(F32)).astype(BF16)
    bh = b.astype(BF16)
    bl = (b - bh.astype(F32)).astype(BF16)
    return _dot(ah, bh) + _dot(al, bh) + _dot(ah, bl)


def _sigmoid(x):
    return 1.0 / (1.0 + jnp.exp(-x))


def _silu(x):
    return x * _sigmoid(x)


def _log_sigmoid(x):
    return jnp.minimum(x, 0.0) - jnp.log1p(jnp.exp(-jnp.abs(x)))


def _rmsnorm_rows(x, g):
    return x * lax.rsqrt(jnp.mean(x * x, axis=-1, keepdims=True) + EPS) * g


def _ada_kernel(c_ref, w_ref, b_ref, o_ref):
    s = _silu(c_ref[...]).astype(BF16)
    o_ref[0] = _dot(s, w_ref[0].astype(BF16)) + b_ref[0]


def _ada_mod(cond, w_ada, b_ada):
    tn = 1536
    n = w_ada.shape[-1]
    return pl.pallas_call(
        _ada_kernel,
        grid=(DEPTH, n // tn),
        in_specs=[
            pl.BlockSpec((MOD_ROWS, D_MODEL), lambda l, j: (0, 0)),
            pl.BlockSpec((1, D_MODEL, tn), lambda l, j: (l, 0, j)),
            pl.BlockSpec((1, 1, tn), lambda l, j: (l, 0, j)),
        ],
        out_specs=pl.BlockSpec((1, MOD_ROWS, tn), lambda l, j: (l, 0, j)),
        out_shape=jax.ShapeDtypeStruct((DEPTH, MOD_ROWS, n), F32),
        name="ada_mod",
    )(cond, w_ada, b_ada.reshape(DEPTH, 1, n))


def _inproj_kernel(x_ref, sh_ref, sc_ref, g_ref, wa_ref, wb_ref, wg_ref, wgt_ref,
                   hg_ref, ml_ref, su_ref, hy_ref, gt_ref, gtt_ref):
    h = _rmsnorm_rows(x_ref[...], g_ref[...])
    h = h * (1.0 + sc_ref[0]) + sh_ref[0]
    hb = h.astype(BF16)
    hg_ref[...] = _dot(hb, wa_ref[:, 0:5 * GROUP_W])
    ml_ref[...] = _dot(hb, wa_ref[:, 5 * GROUP_W:9 * GROUP_W])
    su_ref[...] = _dot(hb, wb_ref[:, 0:GROUP_W])
    hy_ref[...] = _dot(hb, wb_ref[:, GROUP_W:4 * GROUP_W])
    gt_ref[...] = _dot(hb, wg_ref[...])
    gtt_ref[...] = _dot_nt(wgt_ref[...], hb)


def _mod_spec(k, row_of_tile):
    return pl.BlockSpec((1, 1, D_MODEL), lambda i: (row_of_tile(i), 0, k))


def _inproj(x2d, mod3, row_of_tile, norm_g, w_a, w_b, w_g, w_gt):
    t = x2d.shape[0]
    tm = TOK_TILE
    const = lambda i: (0, 0)
    tile = lambda i: (i, 0)
    widths = (5 * GROUP_W, 4 * GROUP_W, GROUP_W, 3 * GROUP_W)
    return pl.pallas_call(
        _inproj_kernel,
        grid=(t // tm,),
        in_specs=[
            pl.BlockSpec((tm, D_MODEL), tile),
            _mod_spec(0, row_of_tile),
            _mod_spec(1, row_of_tile),
            pl.BlockSpec((1, D_MODEL), const),
            pl.BlockSpec(w_a.shape, const),
            pl.BlockSpec(w_b.shape, const),
            pl.BlockSpec(w_g.shape, const),
            pl.BlockSpec(w_gt.shape, const),
        ],
        out_specs=[pl.BlockSpec((tm, w), tile) for w in widths]
        + [pl.BlockSpec((tm, N_GATES), tile), pl.BlockSpec((N_GATES, tm), lambda i: (0, i))],
        out_shape=[jax.ShapeDtypeStruct((t, w), F32) for w in widths]
        + [jax.ShapeDtypeStruct((t, N_GATES), F32), jax.ShapeDtypeStruct((N_GATES, t), F32)],
        name="inproj",
    )(x2d, mod3, mod3, norm_g, w_a, w_b, w_g, w_gt)


@functools.lru_cache(maxsize=None)
def _hgrn_consts():
    r = np.arange(HG_SB)
    same = (r[:, None] // HG_BLK) == (r[None, :] // HG_BLK)
    tri_f = same & (r[:, None] >= r[None, :])
    tri_b = same & (r[:, None] <= r[None, :])
    c = np.arange(GROUP_W)
    head = (c[:, None] // HEAD_D) == (c[None, :] // HEAD_D)
    return (np.stack([tri_f, tri_b]).astype(np.float32), same.astype(np.float32),
            head.astype(np.float32))


def _hgrn_kernel(p_ref, s0_ref, lb_ref, ng_ref, tri_ref, bones_ref, e_ref, hmask_ref,
                 out_ref, *rest, seq, emit_state):
    if emit_state:
        st_ref, of_scr, st_scr, q_scr, k_scr, bc_scr, gq_scr, gk_scr, dec_scr = rest
    else:
        of_scr, st_scr, q_scr, k_scr, bc_scr, gq_scr, gk_scr, dec_scr = rest
    nsb = seq // HG_SB
    nblk = HG_SB // HG_BLK
    log_lb = lb_ref[0:1, :]
    log_1mlb = lb_ref[1:2, :]
    one_m_lb = lb_ref[2:3, :]
    row = lax.broadcasted_iota(jnp.int32, (HG_BLK, 1), 0)

    def superblock(sb, d, finalize):
        r0 = pl.multiple_of(sb * HG_SB, HG_SB)
        rows = pl.ds(r0, HG_SB)
        pq = p_ref[rows, 0:GROUP_W]
        pf = p_ref[rows, (1 + d) * GROUP_W:(2 + d) * GROUP_W]
        q_scr[...] = _silu(pq)
        b2 = log_1mlb + _log_sigmoid(pf)
        mx = jnp.maximum(log_lb, b2)
        mn = jnp.minimum(log_lb, b2)
        delta = jnp.where(mn == NEG_INF, NEG_INF, mn - mx)
        lf = mx + jnp.log1p(jnp.exp(delta))
        k_scr[...] = one_m_lb * _sigmoid(-pf)
        bc = _dot_sel_l(tri_ref[d], lf)
        bt = _dot_sel_l(bones_ref[...], lf)
        bc_scr[...] = bc
        gq_scr[...] = (q_scr[...] * jnp.exp(bc)).astype(BF16)
        gk_scr[...] = (k_scr[...] * jnp.exp(bt - bc)).astype(BF16)
        dec_scr[...] = jnp.exp(bt)
        o_blocks = [None] * nblk
        order = range(nblk) if d == 0 else range(nblk - 1, -1, -1)
        for j in order:
            sl = slice(HG_BLK * j, HG_BLK * (j + 1))
            qj = q_scr[sl, :]
            kj = k_scr[sl, :]
            vj = p_ref[pl.ds(r0 + HG_BLK * j, HG_BLK), 3 * GROUP_W:4 * GROUP_W]
            bcj = bc_scr[sl, :]
            parts = []
            for s in range(HG_BLK):
                diff = bcj - bcj[s:s + 1, :]
                keep = (row >= s) if d == 0 else (row <= s)
                dd = jnp.exp(jnp.where(keep, diff, NEG_INF))
                parts.append((qj * dd * kj[s:s + 1, :]).astype(BF16))
            r_all = _dot(jnp.concatenate(parts, axis=0), e_ref[...])
            o = _dot_nt(gq_scr[sl, :], st_scr[...].astype(BF16))
            for s in range(HG_BLK):
                o = o + r_all[HG_BLK * s:HG_BLK * (s + 1), :] * vj[s:s + 1, :]
            upd = _dot_tn(vj.astype(BF16), gk_scr[sl, :])
            st_scr[...] = st_scr[...] * dec_scr[HG_BLK * j:HG_BLK * j + 1, :] + upd * hmask_ref[...]
            o_blocks[j] = o
        o_sb = jnp.concatenate(o_blocks, axis=0)
        if finalize:
            o_sb = o_sb + of_scr[rows, :]
            ms = _dot_sel_r(o_sb * o_sb, e_ref[...]) * (1.0 / HEAD_D)
            pg = p_ref[rows, 4 * GROUP_W:5 * GROUP_W]
            y = o_sb * lax.rsqrt(ms + EPS) * ng_ref[...] * _silu(pg)
            out_ref[rows, :] = y.astype(out_ref.dtype)
        else:
            of_scr[rows, :] = o_sb

    st_scr[...] = s0_ref[0]

    def fwd(i, carry):
        superblock(i, 0, False)
        return carry

    lax.fori_loop(0, nsb, fwd, 0)
    if emit_state:
        st_ref[0] = st_scr[...]
    st_scr[...] = s0_ref[1]

    def bwd(i, carry):
        superblock(nsb - 1 - i, 1, True)
        return carry

    lax.fori_loop(0, nsb, bwd, 0)
    if emit_state:
        st_ref[1] = st_scr[...]


def _hgrn(p3, s0_bd, lb_rows, norm_g, emit_state):
    b, seq, _ = p3.shape
    tri, bones, head = _hgrn_consts()
    const2 = lambda i: (0, 0)
    out_shape = [jax.ShapeDtypeStruct((b, seq, GROUP_W), BF16)]
    out_specs = [pl.BlockSpec((None, seq, GROUP_W), lambda i: (i, 0, 0))]
    if emit_state:
        out_shape.append(jax.ShapeDtypeStruct((b, 2, GROUP_W, GROUP_W), F32))
        out_specs.append(pl.BlockSpec((None, 2, GROUP_W, GROUP_W), lambda i: (i, 0, 0, 0)))
    res = pl.pallas_call(
        functools.partial(_hgrn_kernel, seq=seq, emit_state=emit_state),
        grid=(b,),
        in_specs=[
            pl.BlockSpec((None, seq, 5 * GROUP_W), lambda i: (i, 0, 0)),
            pl.BlockSpec((None, 2, GROUP_W, GROUP_W), lambda i: (i, 0, 0, 0)),
            pl.BlockSpec((8, GROUP_W), const2),
            pl.BlockSpec((1, GROUP_W), const2),
            pl.BlockSpec((2, HG_SB, HG_SB), lambda i: (0, 0, 0)),
            pl.BlockSpec((HG_SB, HG_SB), const2),
            pl.BlockSpec((GROUP_W, GROUP_W), const2),
            pl.BlockSpec((GROUP_W, GROUP_W), const2),
        ],
        out_specs=out_specs,
        out_shape=out_shape,
        scratch_shapes=[
            pltpu.VMEM((seq, GROUP_W), F32),
            pltpu.VMEM((GROUP_W, GROUP_W), F32),
            pltpu.VMEM((HG_SB, GROUP_W), F32),
            pltpu.VMEM((HG_SB, GROUP_W), F32),
            pltpu.VMEM((HG_SB, GROUP_W), F32),
            pltpu.VMEM((HG_SB, GROUP_W), BF16),
            pltpu.VMEM((HG_SB, GROUP_W), BF16),
            pltpu.VMEM((HG_SB, GROUP_W), F32),
        ],
        name="hgrn2",
    )(p3, s0_bd, lb_rows, norm_g, jnp.asarray(tri, BF16), jnp.asarray(bones, BF16),
      jnp.asarray(head, BF16), jnp.asarray(head, F32))
    return res if emit_state else (res[0], None)


@functools.lru_cache(maxsize=None)
def _mlstm_consts():
    r = np.arange(ML_CHUNK)
    tri_f = (r[:, None] >= r[None, :]).astype(np.float32)
    tri = np.stack([tri_f, tri_f.T])
    c = np.arange(GROUP_W)
    head = (c[:, None] // HEAD_D) == (c[None, :] // HEAD_D)
    pos = c % HEAD_D
    row_f = head & (pos[:, None] <= pos[None, :])
    row_b = head & (pos[:, None] >= pos[None, :])
    rowtri = np.stack([row_f, row_b]).astype(np.float32)
    expand = np.zeros((2, N_GATES, 2 * GROUP_W), np.float32)
    for d in range(2):
        for h in range(HEADS):
            expand[d, d * HEADS + h, h * HEAD_D:(h + 1) * HEAD_D] = 1.0
            expand[d, 2 * HEADS + d * HEADS + h, GROUP_W + h * HEAD_D:GROUP_W + (h + 1) * HEAD_D] = 1.0
    return tri, rowtri, expand, head.astype(np.float32)


def _dot_hilo(x, m):
    hi = x.astype(BF16)
    lo = (x - hi.astype(F32)).astype(BF16)
    return _dot(hi, m) + _dot(lo, m)


def _mlstm_kernel(p_ref, g_ref, gr_ref, conv_ref, brow_ref, brows_ref, ng_ref, c0_ref, n0_ref, m0_ref,
                  tri_ref, rowtri_ref, exp_ref, e_ref, out_ref, *rest, seq, conv_w, emit_state):
    if emit_state:
        cst_ref, nst_ref, mst_ref = rest[:3]
        rest = rest[3:]
    (q_scr, k_scr, col_scr, mloc_scr, gb_scr, h_scr, row_scr, blast_scr, gmax_scr,
     c_scr, n_scr, m_scr) = rest
    nch = seq // ML_CHUNK
    rowid = lax.broadcasted_iota(jnp.int32, (ML_CHUNK, 1), 0)
    ti = lax.broadcasted_iota(jnp.int32, (ML_CHUNK, GROUP_W), 0)
    si = lax.broadcasted_iota(jnp.int32, (ML_CHUNK, GROUP_W), 1) % HEAD_D

    for d in range(2):
        li_row = gr_ref[d] + brows_ref[d:d + 1, :]
        lf_row = _log_sigmoid(gr_ref[2 + d] + brows_ref[2 + d:3 + d, :])
        row_scr[d] = li_row - _dot_sel_r(lf_row, rowtri_ref[d])

    def pre_chunk(ci, carry):
        r0 = pl.multiple_of(ci * ML_CHUNK, ML_CHUNK)
        rows = pl.ds(r0, ML_CHUNK)
        zc = p_ref[rows, 0:2 * GROUP_W]
        zp = pltpu.roll(zc, 1, axis=0)
        zn = pltpu.roll(zc, ML_CHUNK - 1, axis=0)
        if conv_w == ML_CHUNK:
            prev_row = jnp.zeros((1, 2 * GROUP_W), F32)
            next_row = prev_row
        else:
            prev_row = p_ref[pl.ds(jnp.maximum(r0 - 1, 0), 1), 0:2 * GROUP_W]
            prev_row = jnp.where(r0 % conv_w == 0, 0.0, prev_row)
            next_row = p_ref[pl.ds(jnp.minimum(r0 + ML_CHUNK, seq - 1), 1), 0:2 * GROUP_W]
            next_row = jnp.where((r0 + ML_CHUNK) % conv_w == 0, 0.0, next_row)
        zp = jnp.where(rowid == 0, prev_row, zp)
        zn = jnp.where(rowid == ML_CHUNK - 1, next_row, zn)
        qk = _silu(conv_ref[0:1, :] * zp + conv_ref[1:2, :] * zc + conv_ref[2:3, :] * zn)
        q_scr[rows, :] = qk[:, 0:GROUP_W].astype(BF16)
        k_scr[rows, :] = qk[:, GROUP_W:2 * GROUP_W] * (HEAD_D ** -0.5)
        ge = g_ref[rows, :] + brow_ref[...]
        for d in range(2):
            ex = _dot_sel_r(ge, exp_ref[d])
            li_e = ex[:, 0:GROUP_W]
            colb = _dot_sel_l(tri_ref[d], _log_sigmoid(ex[:, GROUP_W:2 * GROUP_W]))
            x = li_e - colb
            for sh in (1, 2, 4, 8, 16, 32):
                if d == 0:
                    shifted = jnp.where(rowid >= sh, pltpu.roll(x, sh, axis=0), NEG_INF)
                else:
                    shifted = jnp.where(rowid < ML_CHUNK - sh, pltpu.roll(x, ML_CHUNK - sh, axis=0), NEG_INF)
                x = jnp.maximum(x, shifted)
            edge = ML_CHUNK - 1 if d == 0 else 0
            blast = colb[edge:edge + 1, :]
            gb = blast - colb + li_e
            col_scr[d, rows, :] = colb
            mloc_scr[d, rows, :] = colb + x
            gb_scr[d, rows, :] = gb
            blast_scr[d, pl.ds(ci, 1), :] = blast
            gmax_scr[d, pl.ds(ci, 1), :] = jnp.max(gb, axis=0, keepdims=True)
        return carry

    lax.fori_loop(0, nch, pre_chunk, 0)

    for d in range(2):
        c_scr[d] = jnp.zeros((GROUP_W, GROUP_W), F32)
        for h in range(HEADS):
            lo = HEAD_D * h
            c_scr[d, lo:lo + HEAD_D, lo:lo + HEAD_D] = c0_ref[d, h]
    n_scr[...] = n0_ref[...]
    m_scr[...] = m0_ref[...]

    def chunk(ci, d):
        r0 = pl.multiple_of(ci * ML_CHUNK, ML_CHUNK)
        rows = pl.ds(r0, ML_CHUNK)
        qb = q_scr[rows, :]
        kc = k_scr[rows, :]
        vb = p_ref[rows, 2 * GROUP_W:3 * GROUP_W].astype(BF16)
        hm = e_ref[...]
        kbd = jnp.concatenate([kc.astype(BF16)] * HEADS, axis=0) * hm
        vbd = jnp.concatenate([vb] * HEADS, axis=0) * hm
        s = _dot_nt(qb, kbd)
        colb = col_scr[d, rows, :]
        m_b = m_scr[d]
        prev = colb + m_b
        m_t = jnp.maximum(prev, mloc_scr[d, rows, :])
        keep = (ti >= si) if d == 0 else (ti <= si)
        w = jnp.exp(jnp.where(keep, colb + row_scr[d, pl.ds(ci, 1), :], NEG_INF) - m_t) * s
        wp = jnp.exp(prev - m_t)
        cst = c_scr[d]
        nrow = n_scr[d]
        num = wp * _dot(qb, cst.astype(BF16)) + _dot(w.astype(BF16), vbd)
        den = _dot_hilo(wp * qb.astype(F32) * nrow + w, e_ref[...])
        h_scr[d, rows, :] = num / jnp.maximum(jnp.abs(den), jnp.exp(-m_t))
        blast = blast_scr[d, pl.ds(ci, 1), :]
        m_new = jnp.maximum(blast + m_b, gmax_scr[d, pl.ds(ci, 1), :])
        dec = jnp.exp(blast + m_b - m_new)
        kw = kc * jnp.exp(gb_scr[d, rows, :] - m_new)
        c_scr[d] = cst * dec + _dot_tn(kw.astype(BF16), vb) * hm.astype(F32)
        n_scr[d] = nrow * dec + jnp.sum(kw, axis=0, keepdims=True)
        m_scr[d] = m_new

    def body(i, carry):
        chunk(i, 0)
        chunk(nch - 1 - i, 1)
        return carry

    lax.fori_loop(0, nch, body, 0)

    def fin_chunk(ci, carry):
        rows = pl.ds(pl.multiple_of(ci * ML_CHUNK, ML_CHUNK), ML_CHUNK)
        hs = h_scr[0, rows, :] + h_scr[1, rows, :]
        ms = _dot_hilo(hs * hs, e_ref[...]) * (1.0 / HEAD_D)
        po = p_ref[rows, 3 * GROUP_W:4 * GROUP_W]
        out_ref[rows, :] = (hs * lax.rsqrt(ms + EPS) * ng_ref[...] * _sigmoid(po)).astype(out_ref.dtype)
        return carry

    lax.fori_loop(0, nch, fin_chunk, 0)
    if emit_state:
        for d in range(2):
            for h in range(HEADS):
                lo = HEAD_D * h
                cst_ref[d, h] = c_scr[d, lo:lo + HEAD_D, lo:lo + HEAD_D]
        nst_ref[...] = n_scr[...]
        mst_ref[...] = m_scr[...]


def _mlstm_kernel_old(p_ref, g_ref, gt_ref, conv_ref, brow_ref, bcol_ref, ng_ref,
                  c0_ref, n0_ref, m0_ref, tri_ref, out_ref, *rest, seq, conv_w, emit_state):
    if emit_state:
        cst_ref, nst_ref, mst_ref, qk_scr, hf_scr, c_scr, n_scr, m_scr = rest
    else:
        qk_scr, hf_scr, c_scr, n_scr, m_scr = rest
    nch = seq // ML_CHUNK
    rowid = lax.broadcasted_iota(jnp.int32, (ML_CHUNK, 1), 0)
    ti = lax.broadcasted_iota(jnp.int32, (ML_CHUNK, ML_CHUNK), 0)
    si = lax.broadcasted_iota(jnp.int32, (ML_CHUNK, ML_CHUNK), 1)

    def conv_chunk(ci, carry):
        r0 = pl.multiple_of(ci * ML_CHUNK, ML_CHUNK)
        zc = p_ref[pl.ds(r0, ML_CHUNK), 0:2 * GROUP_W]
        zp = pltpu.roll(zc, 1, axis=0)
        zn = pltpu.roll(zc, ML_CHUNK - 1, axis=0)
        if conv_w == ML_CHUNK:
            prev_row = jnp.zeros((1, 2 * GROUP_W), F32)
            next_row = prev_row
        else:
            prev_row = p_ref[pl.ds(jnp.maximum(r0 - 1, 0), 1), 0:2 * GROUP_W]
            prev_row = jnp.where(r0 % conv_w == 0, 0.0, prev_row)
            next_row = p_ref[pl.ds(jnp.minimum(r0 + ML_CHUNK, seq - 1), 1), 0:2 * GROUP_W]
            next_row = jnp.where((r0 + ML_CHUNK) % conv_w == 0, 0.0, next_row)
        zp = jnp.where(rowid == 0, prev_row, zp)
        zn = jnp.where(rowid == ML_CHUNK - 1, next_row, zn)
        out = conv_ref[0:1, :] * zp + conv_ref[1:2, :] * zc + conv_ref[2:3, :] * zn
        qk_scr[pl.ds(r0, ML_CHUNK), :] = _silu(out)
        return carry

    lax.fori_loop(0, nch, conv_chunk, 0)

    def chunk(ci, d, finalize):
        r0 = pl.multiple_of(ci * ML_CHUNK, ML_CHUNK)
        rows = pl.ds(r0, ML_CHUNK)
        g = g_ref[rows, :] + brow_ref[...]
        gt = gt_ref[ci] + bcol_ref[...]
        li = g[:, 0:2 * HEADS]
        lf = _log_sigmoid(g[:, 2 * HEADS:4 * HEADS])
        lit = gt[0:2 * HEADS, :]
        lft = _log_sigmoid(gt[2 * HEADS:4 * HEADS, :])
        bc_all = _dot_sel_l(tri_ref[d], lf)
        bct_all = _dot_sel_r(lft, tri_ref[1 - d])
        keep = (ti >= si) if d == 0 else (ti <= si)
        edge = ML_CHUNK - 1 if d == 0 else 0
        outs = []
        for h in range(HEADS):
            c = HEADS * d + h
            lo = HEAD_D * h
            q = qk_scr[rows, lo:lo + HEAD_D]
            k = qk_scr[rows, GROUP_W + lo:GROUP_W + lo + HEAD_D] * (HEAD_D ** -0.5)
            v = p_ref[rows, 2 * GROUP_W + lo:2 * GROUP_W + lo + HEAD_D]
            qb = q.astype(BF16)
            vb = v.astype(BF16)
            bcol = bc_all[:, c:c + 1]
            brow = bct_all[c:c + 1, :]
            m = m_scr[h]
            dmat = jnp.where(keep, bcol - brow + lit[c:c + 1, :], NEG_INF)
            prev = bcol + m
            m_t = jnp.maximum(prev, jnp.max(dmat, axis=1, keepdims=True))
            w = jnp.exp(dmat - m_t) * _dot_nt(qb, k.astype(BF16))
            wp = jnp.exp(prev - m_t)
            cst = c_scr[h]
            nrow = n_scr[h]
            num = wp * _dot(qb, cst.astype(BF16)) + _dot(w.astype(BF16), vb)
            den = wp * jnp.sum(q * nrow, axis=1, keepdims=True) + jnp.sum(w, axis=1, keepdims=True)
            hh = num / jnp.maximum(jnp.abs(den), jnp.exp(-m_t))
            blast = bcol[edge:edge + 1, :]
            gc = blast - bcol + li[:, c:c + 1]
            m_new = jnp.maximum(blast + m, jnp.max(gc, axis=0, keepdims=True))
            dec = jnp.exp(blast + m - m_new)
            kw = k * jnp.exp(gc - m_new)
            c_scr[h] = dec * cst + _dot_tn(kw.astype(BF16), vb)
            n_scr[h] = dec * nrow + jnp.sum(kw, axis=0, keepdims=True)
            m_scr[h] = m_new
            if finalize:
                hs = hh + hf_scr[rows, lo:lo + HEAD_D]
                hs = hs * lax.rsqrt(jnp.mean(hs * hs, axis=-1, keepdims=True) + EPS)
                outs.append(hs)
            else:
                outs.append(hh)
        o = jnp.concatenate(outs, axis=1)
        if finalize:
            po = p_ref[rows, 3 * GROUP_W:4 * GROUP_W]
            out_ref[rows, :] = (o * ng_ref[...] * _sigmoid(po)).astype(out_ref.dtype)
        else:
            hf_scr[rows, :] = o

    def run(d):
        c_scr[...] = c0_ref[d]
        n_scr[...] = n0_ref[d]
        m_scr[...] = m0_ref[d]

        def body(i, carry):
            chunk(i if d == 0 else nch - 1 - i, d, d == 1)
            return carry

        lax.fori_loop(0, nch, body, 0)
        if emit_state:
            cst_ref[d] = c_scr[...]
            nst_ref[d] = n_scr[...]
            mst_ref[d] = m_scr[...]

    run(0)
    run(1)


def _mlstm(p3, g3, gr4, conv_w3, brow, brows, norm_g, c0, n0, m0, conv_width, emit_state):
    b, seq, _ = p3.shape
    nch = seq // ML_CHUNK
    tri, rowtri, expand, head = _mlstm_consts()
    const2 = lambda i: (0, 0)
    const3 = lambda i: (0, 0, 0)
    out_shape = [jax.ShapeDtypeStruct((b, seq, GROUP_W), BF16)]
    out_specs = [pl.BlockSpec((None, seq, GROUP_W), lambda i: (i, 0, 0))]
    c_spec = pl.BlockSpec((None, 2, HEADS, HEAD_D, HEAD_D), lambda i: (i, 0, 0, 0, 0))
    row_spec = pl.BlockSpec((None, 2, 1, GROUP_W), lambda i: (i, 0, 0, 0))
    if emit_state:
        out_shape += [jax.ShapeDtypeStruct((b, 2, HEADS, HEAD_D, HEAD_D), F32),
                      jax.ShapeDtypeStruct((b, 2, 1, GROUP_W), F32), jax.ShapeDtypeStruct((b, 2, 1, GROUP_W), F32)]
        out_specs += [c_spec, row_spec, row_spec]
    dir_seq = (2, seq, GROUP_W)
    dir_chunks = (2, nch, GROUP_W)
    res = pl.pallas_call(
        functools.partial(_mlstm_kernel, seq=seq, conv_w=conv_width, emit_state=emit_state),
        grid=(b,),
        in_specs=[
            pl.BlockSpec((None, seq, 4 * GROUP_W), lambda i: (i, 0, 0)),
            pl.BlockSpec((None, seq, N_GATES), lambda i: (i, 0, 0)),
            pl.BlockSpec((None, 4, nch, GROUP_W), lambda i: (i, 0, 0, 0)),
            pl.BlockSpec((3, 2 * GROUP_W), const2),
            pl.BlockSpec((1, N_GATES), const2),
            pl.BlockSpec((4, GROUP_W), const2),
            pl.BlockSpec((1, GROUP_W), const2),
            c_spec, row_spec, row_spec,
            pl.BlockSpec((2, ML_CHUNK, ML_CHUNK), const3),
            pl.BlockSpec((2, GROUP_W, GROUP_W), const3),
            pl.BlockSpec((2, N_GATES, 2 * GROUP_W), const3),
            pl.BlockSpec((GROUP_W, GROUP_W), const2),
        ],
        out_specs=out_specs,
        out_shape=out_shape,
        scratch_shapes=[
            pltpu.VMEM((seq, GROUP_W), BF16),
            pltpu.VMEM((seq, GROUP_W), F32),
            pltpu.VMEM(dir_seq, F32), pltpu.VMEM(dir_seq, F32), pltpu.VMEM(dir_seq, F32), pltpu.VMEM(dir_seq, F32),
            pltpu.VMEM(dir_chunks, F32), pltpu.VMEM(dir_chunks, F32), pltpu.VMEM(dir_chunks, F32),
            pltpu.VMEM((2, GROUP_W, GROUP_W), F32),
            pltpu.VMEM((2, 1, GROUP_W), F32),
            pltpu.VMEM((2, 1, GROUP_W), F32),
        ],
        name="mlstm",
    )(p3, g3, gr4, conv_w3, brow, brows, norm_g, c0, n0, m0, jnp.asarray(tri, BF16), jnp.asarray(rowtri, BF16),
      jnp.asarray(expand, BF16), jnp.asarray(head, BF16))
    if emit_state:
        return res
    return res[0], None, None, None


def _s5_kernel(*refs, nb, reverse, final):
    if final:
        (u_ref, hre0_ref, him0_ref, are_ref, aim_ref, wb_ref, wc_ref, yin_ref, d_ref, gw_ref, gb_ref,
         y_ref, sre_ref, sim_ref, utm_scr, bu_scr, ytm_scr, hre_scr, him_scr) = refs
    else:
        (u_ref, hre0_ref, him0_ref, are_ref, aim_ref, wb_ref, wc_ref,
         y_ref, sre_ref, sim_ref, utm_scr, bu_scr, ytm_scr, hre_scr, him_scr) = refs
    tc = S5_TC
    step_id = pl.program_id(0)

    @pl.when(step_id == 0)
    def _():
        hre_scr[...] = hre0_ref[...]
        him_scr[...] = him0_ref[...]

    halves = GROUP_W // LANES
    for b in range(nb):
        for hv in range(halves):
            utm_scr[hv, pl.ds(b, tc, stride=nb), :] = u_ref[b, :, hv * LANES:(hv + 1) * LANES]
    u_tm = jnp.concatenate([utm_scr[hv] for hv in range(halves)], axis=1)
    bu_scr[...] = _dot(u_tm.astype(BF16), wb_ref[...])
    are = jnp.broadcast_to(are_ref[...], (nb, S5_STATE))
    aim = jnp.broadcast_to(aim_ref[...], (nb, S5_STATE))

    def step(i, carry):
        hre, him = carry
        t = (tc - 1 - i) if reverse else i
        r0 = pl.multiple_of(t * nb, nb)
        bu = bu_scr[pl.ds(r0, nb), :]
        nre = are * hre - aim * him + bu[:, 0:S5_STATE]
        nim = are * him + aim * hre + bu[:, S5_STATE:2 * S5_STATE]
        bu_scr[pl.ds(r0, nb), 0:S5_STATE] = nre
        bu_scr[pl.ds(r0, nb), S5_STATE:2 * S5_STATE] = nim
        return nre, nim

    hre, him = lax.fori_loop(0, tc, step, (hre_scr[...], him_scr[...]))
    hre_scr[...] = hre
    him_scr[...] = him
    y_tm = _dot(bu_scr[...].astype(BF16), wc_ref[...])
    for hv in range(halves):
        ytm_scr[hv] = y_tm[:, hv * LANES:(hv + 1) * LANES]
    for b in range(nb):
        yb = jnp.concatenate([ytm_scr[hv, pl.ds(b, tc, stride=nb), :] for hv in range(halves)], axis=1)
        if final:
            y = yb + yin_ref[b] + d_ref[...] * u_ref[b]
            z = 0.5 * y * (1.0 + jnp.tanh(math.sqrt(2.0 / math.pi) * (y + 0.044715 * (y * y * y))))
            gate = _sigmoid(_dot(z.astype(BF16), gw_ref[...]) + gb_ref[...])
            y_ref[b] = (z * gate).astype(y_ref.dtype)
        else:
            y_ref[b] = yb

    @pl.when(step_id == pl.num_programs(0) - 1)
    def _():
        sre_ref[...] = hre
        sim_ref[...] = him


def _s5_pass(u3, hre0, him0, are, aim, wb, wc, reverse, extra):
    nb, seq, _ = u3.shape
    nch = seq // S5_TC
    final = extra is not None
    chunk = (lambda i: (0, nch - 1 - i, 0)) if reverse else (lambda i: (0, i, 0))
    const2 = lambda i: (0, 0)
    in_specs = [
        pl.BlockSpec((nb, S5_TC, GROUP_W), chunk),
        pl.BlockSpec((nb, S5_STATE), const2),
        pl.BlockSpec((nb, S5_STATE), const2),
        pl.BlockSpec((1, S5_STATE), const2),
        pl.BlockSpec((1, S5_STATE), const2),
        pl.BlockSpec((GROUP_W, 2 * S5_STATE), const2),
        pl.BlockSpec((2 * S5_STATE, GROUP_W), const2),
    ]
    args = [u3, hre0, him0, are, aim, wb, wc]
    if final:
        yin, dvec, gw, gb = extra
        in_specs += [pl.BlockSpec((nb, S5_TC, GROUP_W), chunk), pl.BlockSpec((1, GROUP_W), const2),
                     pl.BlockSpec((GROUP_W, GROUP_W), const2), pl.BlockSpec((1, GROUP_W), const2)]
        args += [yin, dvec, gw, gb]
    return pl.pallas_call(
        functools.partial(_s5_kernel, nb=nb, reverse=reverse, final=final),
        grid=(nch,),
        in_specs=in_specs,
        out_specs=[pl.BlockSpec((nb, S5_TC, GROUP_W), chunk),
                   pl.BlockSpec((nb, S5_STATE), const2), pl.BlockSpec((nb, S5_STATE), const2)],
        out_shape=[jax.ShapeDtypeStruct((nb, seq, GROUP_W), BF16 if final else F32),
                   jax.ShapeDtypeStruct((nb, S5_STATE), F32), jax.ShapeDtypeStruct((nb, S5_STATE), F32)],
        scratch_shapes=[
            pltpu.VMEM((GROUP_W // LANES, nb * S5_TC, LANES), F32),
            pltpu.VMEM((nb * S5_TC, 2 * S5_STATE), F32),
            pltpu.VMEM((GROUP_W // LANES, nb * S5_TC, LANES), F32),
            pltpu.VMEM((nb, S5_STATE), F32),
            pltpu.VMEM((nb, S5_STATE), F32),
        ],
        name="s5_bwd" if reverse else "s5_fwd",
    )(*args)


def _s5_params(a_re, a_im, log_dt, b_re, b_im, c_re, c_im):
    eye = jnp.eye(S5_GROUPS, dtype=F32)
    dt = jnp.exp(log_dt)[..., None]
    mag = jnp.exp(a_re * dt)
    ab_re, ab_im = mag * jnp.cos(a_im * dt), mag * jnp.sin(a_im * dt)
    den = a_re * a_re + a_im * a_im
    g_re = ((ab_re - 1.0) * a_re + ab_im * a_im) / den
    g_im = (ab_im * a_re - (ab_re - 1.0) * a_im) / den
    bb_re = g_re[..., None] * b_re - g_im[..., None] * b_im
    bb_im = g_re[..., None] * b_im + g_im[..., None] * b_re

    def in_mat(bb):
        return jnp.einsum('dgpc,gh->dgchp', bb, eye).reshape(2, GROUP_W, S5_STATE)

    wb = jnp.concatenate([in_mat(bb_re), in_mat(bb_im)], axis=-1).astype(BF16)

    def out_mat(cc):
        return jnp.einsum('gcp,gh->gphc', cc, eye).reshape(S5_STATE, GROUP_W)

    wc = jnp.concatenate([out_mat(c_re), -out_mat(c_im)], axis=0).astype(BF16)
    return ab_re.reshape(2, 1, S5_STATE), ab_im.reshape(2, 1, S5_STATE), wb, wc


def _s5(u3, h0_re, h0_im, prm, d_vec, glu_w, glu_b):
    ab_re, ab_im, wb, wc = prm
    yf, fre, fim = _s5_pass(u3, h0_re[:, 0], h0_im[:, 0], ab_re[0], ab_im[0], wb[0], wc, False, None)
    out, bre, bim = _s5_pass(u3, h0_re[:, 1], h0_im[:, 1], ab_re[1], ab_im[1], wb[1], wc, True,
                             (yf, d_vec, glu_w, glu_b))
    return out, jnp.stack([fre, bre], axis=1), jnp.stack([fim, bim], axis=1)


@functools.lru_cache(maxsize=None)
def _dft_consts(n):
    idx = np.arange(n, dtype=np.int64)
    ang = (np.pi / n) * ((idx[:, None] * idx[None, :]) % (2 * n)).astype(np.float64)
    cos, sin = np.cos(ang), np.sin(ang)
    sign = np.where(idx % 2 == 0, 1.0, -1.0)
    sin_n = sin.copy()
    sin_n[0, :] = sign
    fwd = np.concatenate([cos, sin_n], axis=0)
    inv_c = cos.T / n
    inv_c[:, 0] = 1.0 / (2 * n)
    inv_s = sin.T / n
    inv_s[:, 0] = sign / (2 * n)
    inv = np.concatenate([inv_c, inv_s], axis=1)
    return fwd.astype(np.float32), inv.astype(np.float32), sign.astype(np.float32)[:, None]


@functools.lru_cache(maxsize=None)
def _hyena_feats(n):
    t = np.linspace(0.0, 1.0, n, dtype=np.float64)[:, None]
    w = (2.0 * np.pi / n) * np.arange(n, dtype=np.float64)[:, None]
    bands = np.linspace(1e-4, HY_BANDS - 1, HY_BANDS, dtype=np.float64)[None, :]
    feats = np.concatenate([t, np.cos(w * bands), -np.sin(w * bands)], axis=-1)
    pad = np.zeros((n, LANES - HY_EMB))
    return np.concatenate([feats, pad], axis=-1).astype(np.float32), t.astype(np.float32)


def _hyfilt_kernel(feat_ref, t_ref, sign_ref, w1_ref, b1_ref, w2_ref, b2_ref, w3_ref, fr_ref, dec_ref,
                   fh_ref, fl_ref, ka_ref, kb_ref, ka2_ref, *, n):
    freq = fr_ref[...]
    hdn = jnp.sin(freq * (_dot3(feat_ref[...], w1_ref[...]) + b1_ref[...]))
    hdn = jnp.sin(freq * (_dot3(hdn, w2_ref[...]) + b2_ref[...]))
    h = _dot3(hdn, w3_ref[...]) * (jnp.exp(-t_ref[...] * jnp.abs(dec_ref[...])) + HY_SHIFT)
    half = HY_ORDER * GROUP_W
    row = lax.broadcasted_iota(jnp.int32, (n, 1), 0)
    h0 = h[:, 0:half]
    h1 = jnp.where(row == 0, 0.0, h[:, half:2 * half])
    norm = jnp.sum(jnp.abs(h0), axis=0, keepdims=True) + jnp.sum(jnp.abs(h1), axis=0, keepdims=True)
    hp = (h0 + h1) / norm
    hm = (h0 - h1) / norm

    def dft(lo, x):
        xh = x.astype(BF16)
        xl = (x - xh.astype(F32)).astype(BF16)
        fh = fh_ref[lo:lo + n, :]
        return _dot(fh, xh) + _dot(fl_ref[lo:lo + n, :], xh) + _dot(fh, xl)

    kc = dft(0, hp)
    ks = dft(n, hm)
    kn = jnp.sum(hp * sign_ref[...], axis=0, keepdims=True)
    ka_ref[...] = kc
    kb_ref[...] = jnp.where(row == 0, 0.0, ks)
    ka2_ref[...] = jnp.where(row == 0, kn, kc)


def _hyena_filter(n, w1p, b1, w2, b2, w3, freq, decay):
    fwd, _, sign = _dft_consts(n)
    feats, t = _hyena_feats(n)
    fwd = jnp.asarray(fwd)
    fh = fwd.astype(BF16)
    fl = (fwd - fh.astype(F32)).astype(BF16)
    half = HY_ORDER * GROUP_W
    return pl.pallas_call(
        functools.partial(_hyfilt_kernel, n=n),
        out_shape=[jax.ShapeDtypeStruct((n, half), F32)] * 3,
        name="hyena_filter",
    )(jnp.asarray(feats), jnp.asarray(t), jnp.asarray(sign), w1p, b1, w2, b2, w3, freq, decay, fh, fl)


def _hyena_kernel(p_ref, cw_ref, ka_ref, kb_ref, ka2_ref, bias_ref, f_ref, g_ref, out_ref,
                  z_scr, x_scr, zf_scr, *, seq, conv_w):
    nch = seq // ML_CHUNK
    rowid = lax.broadcasted_iota(jnp.int32, (ML_CHUNK, 1), 0)

    def conv_chunk(ci, carry):
        r0 = pl.multiple_of(ci * ML_CHUNK, ML_CHUNK)
        zc = p_ref[pl.ds(r0, ML_CHUNK), :]
        zp = pltpu.roll(zc, 1, axis=0)
        zn = pltpu.roll(zc, ML_CHUNK - 1, axis=0)
        if conv_w == ML_CHUNK:
            prev_row = jnp.zeros((1, 3 * GROUP_W), F32)
            next_row = prev_row
        else:
            prev_row = p_ref[pl.ds(jnp.maximum(r0 - 1, 0), 1), :]
            prev_row = jnp.where(r0 % conv_w == 0, 0.0, prev_row)
            next_row = p_ref[pl.ds(jnp.minimum(r0 + ML_CHUNK, seq - 1), 1), :]
            next_row = jnp.where((r0 + ML_CHUNK) % conv_w == 0, 0.0, next_row)
        zp = jnp.where(rowid == 0, prev_row, zp)
        zn = jnp.where(rowid == ML_CHUNK - 1, next_row, zn)
        z_scr[pl.ds(r0, ML_CHUNK), :] = cw_ref[0:1, :] * zp + cw_ref[1:2, :] * zc + cw_ref[2:3, :] * zn
        return carry

    lax.fori_loop(0, nch, conv_chunk, 0)
    y = z_scr[:, 0:GROUP_W]
    for o in range(HY_ORDER):
        cols = slice(o * GROUP_W, (o + 1) * GROUP_W)
        x_scr[...] = _dot(f_ref[...], y.astype(BF16))
        xc = x_scr[0:seq, :]
        xs = x_scr[seq:2 * seq, :]
        kb = kb_ref[:, cols]
        zf_scr[0:seq, :] = (xc * ka_ref[:, cols] - xs * kb).astype(BF16)
        zf_scr[seq:2 * seq, :] = (xc * kb + xs * ka2_ref[:, cols]).astype(BF16)
        conv = _dot(g_ref[...], zf_scr[...])
        gate = z_scr[:, (o + 1) * GROUP_W:(o + 2) * GROUP_W]
        y = gate * (conv + bias_ref[o:o + 1, :] * y)
    out_ref[...] = y.astype(out_ref.dtype)


def _hyena(p3, conv_w3, filt, bias, conv_width):
    b, seq, _ = p3.shape
    ka, kb, ka2 = filt
    fwd, inv, _ = _dft_consts(seq)
    const2 = lambda i: (0, 0)
    half = HY_ORDER * GROUP_W
    return pl.pallas_call(
        functools.partial(_hyena_kernel, seq=seq, conv_w=conv_width),
        grid=(b,),
        in_specs=[
            pl.BlockSpec((None, seq, 3 * GROUP_W), lambda i: (i, 0, 0)),
            pl.BlockSpec((3, 3 * GROUP_W), const2),
            pl.BlockSpec((seq, half), const2),
            pl.BlockSpec((seq, half), const2),
            pl.BlockSpec((seq, half), const2),
            pl.BlockSpec((HY_ORDER, GROUP_W), const2),
            pl.BlockSpec((2 * seq, seq), const2),
            pl.BlockSpec((seq, 2 * seq), const2),
        ],
        out_specs=pl.BlockSpec((None, seq, GROUP_W), lambda i: (i, 0, 0)),
        out_shape=jax.ShapeDtypeStruct((b, seq, GROUP_W), BF16),
        scratch_shapes=[
            pltpu.VMEM((seq, 3 * GROUP_W), F32),
            pltpu.VMEM((2 * seq, GROUP_W), F32),
            pltpu.VMEM((2 * seq, GROUP_W), BF16),
        ],
        name="hyena",
    )(p3, conv_w3, ka, kb, ka2, bias, jnp.asarray(fwd, BF16), jnp.asarray(inv, BF16))


def _outproj_kernel(a_ref, b_ref, c_ref, d_ref, x_ref, g1_ref, sh2_ref, sc2_ref, n2_ref, wo_ref,
                    wr_ref, br_ref, x1_ref, h2_ref, gate_ref):
    acc = _dot(a_ref[...], wo_ref[0:GROUP_W, :])
    acc += _dot(b_ref[...], wo_ref[GROUP_W:2 * GROUP_W, :])
    acc += _dot(c_ref[...], wo_ref[2 * GROUP_W:3 * GROUP_W, :])
    acc += _dot(d_ref[...], wo_ref[3 * GROUP_W:4 * GROUP_W, :])
    x1 = x_ref[...] + g1_ref[0] * acc
    x1_ref[...] = x1
    h2 = _rmsnorm_rows(x1, n2_ref[...]) * (1.0 + sc2_ref[0]) + sh2_ref[0]
    h2_ref[...] = h2.astype(BF16)
    logits = _dot3(h2, wr_ref[...]) + br_ref[...]
    lane = lax.broadcasted_iota(jnp.int32, logits.shape, 1).astype(F32)
    big = float(LANES)
    gl = jnp.where(lane < N_EXPERT_GROUPS, logits, NEG_INF)
    gmax = jnp.max(gl, axis=1, keepdims=True)
    gsel = jnp.min(jnp.where(gl == gmax, lane, big), axis=1, keepdims=True)
    psel = 1.0 / jnp.sum(jnp.exp(gl - gmax), axis=1, keepdims=True)
    lo = N_EXPERT_GROUPS + EXPERTS_PER_GROUP * gsel
    el = jnp.where((lane >= lo) & (lane < lo + EXPERTS_PER_GROUP), logits, NEG_INF)
    v1 = jnp.max(el, axis=1, keepdims=True)
    i1 = jnp.min(jnp.where(el == v1, lane, big), axis=1, keepdims=True)
    el2 = jnp.where(lane == i1, NEG_INF, el)
    v2 = jnp.max(el2, axis=1, keepdims=True)
    i2 = jnp.min(jnp.where(el2 == v2, lane, big), axis=1, keepdims=True)
    e2 = jnp.exp(v2 - v1)
    w1 = psel / (1.0 + e2)
    w2 = psel * e2 / (1.0 + e2)
    gate_ref[...] = jnp.where(lane == i1, w1, 0.0) + jnp.where(lane == i2, w2, 0.0)


def _outproj(mix, x2d, mod3, row_of_tile, norm_g, w_out, w_r, b_r):
    t = x2d.shape[0]
    tm = TOK_TILE
    const = lambda i: (0, 0)
    tile = lambda i: (i, 0)
    return pl.pallas_call(
        _outproj_kernel,
        grid=(t // tm,),
        in_specs=[pl.BlockSpec((tm, GROUP_W), tile)] * 4 + [
            pl.BlockSpec((tm, D_MODEL), tile),
            _mod_spec(2, row_of_tile),
            _mod_spec(3, row_of_tile),
            _mod_spec(4, row_of_tile),
            pl.BlockSpec((1, D_MODEL), const),
            pl.BlockSpec((D_MODEL, D_MODEL), const),
            pl.BlockSpec((D_MODEL, LANES), const),
            pl.BlockSpec((1, LANES), const),
        ],
        out_specs=[pl.BlockSpec((tm, D_MODEL), tile), pl.BlockSpec((tm, D_MODEL), tile),
                   pl.BlockSpec((tm, LANES), tile)],
        out_shape=[jax.ShapeDtypeStruct((t, D_MODEL), F32), jax.ShapeDtypeStruct((t, D_MODEL), BF16),
                   jax.ShapeDtypeStruct((t, LANES), F32)],
        name="outproj_router",
    )(*mix, x2d, mod3, mod3, mod3, norm_g, w_out, w_r, b_r)


def _moe_kernel(h_ref, gate_ref, wg_ref, wu_ref, wd_ref, x1_ref, g2_ref, fn_ref, out_ref, acc_scr, *, final):
    e = pl.program_id(1)

    @pl.when(e == 0)
    def _():
        acc_scr[...] = jnp.zeros_like(acc_scr)

    h = h_ref[...]
    a = _dot(h, wg_ref[0])
    u = _dot(h, wu_ref[0])
    gates = gate_ref[...]
    lane = lax.broadcasted_iota(jnp.int32, gates.shape, 1)
    gate = jnp.sum(jnp.where(lane == N_EXPERT_GROUPS + e, gates, 0.0), axis=1, keepdims=True)
    hid = (_silu(a) * u * gate).astype(BF16)
    acc_scr[...] += _dot(hid, wd_ref[0])

    @pl.when(e == pl.num_programs(1) - 1)
    def _():
        x2 = x1_ref[...] + g2_ref[0] * acc_scr[...]
        if final:
            x2 = _rmsnorm_rows(x2, fn_ref[...])
        out_ref[...] = x2


def _moe(h2, gates_e, w_gate, w_up, w_down, x1, mod3, tm, row_of_tile, final_g, final):
    t = h2.shape[0]
    tile = lambda i, e: (i, 0)
    return pl.pallas_call(
        functools.partial(_moe_kernel, final=final),
        grid=(t // tm, N_EXPERTS),
        in_specs=[
            pl.BlockSpec((tm, D_MODEL), tile),
            pl.BlockSpec((tm, LANES), tile),
            pl.BlockSpec((1, D_MODEL, D_EXPERT), lambda i, e: (e, 0, 0)),
            pl.BlockSpec((1, D_MODEL, D_EXPERT), lambda i, e: (e, 0, 0)),
            pl.BlockSpec((1, D_EXPERT, D_MODEL), lambda i, e: (e, 0, 0)),
            pl.BlockSpec((tm, D_MODEL), tile),
            pl.BlockSpec((1, 1, D_MODEL), lambda i, e: (row_of_tile(i), 0, 5)),
            pl.BlockSpec((1, D_MODEL), lambda i, e: (0, 0)),
        ],
        out_specs=pl.BlockSpec((tm, D_MODEL), tile),
        out_shape=jax.ShapeDtypeStruct((t, D_MODEL), F32),
        scratch_shapes=[pltpu.VMEM((tm, D_MODEL), F32)],
        name="moe",
    )(h2, gates_e, w_gate, w_up, w_down, x1, mod3, final_g)


def _block_diag_t(s):
    eye = jnp.eye(HEADS, dtype=s.dtype)
    b = s.shape[0]
    return jnp.einsum('bdhkv,hg->bdhvgk', s, eye).reshape(b, 2, GROUP_W, GROUP_W)


def _block_diag_t_inv(st):
    b = st.shape[0]
    s6 = st.reshape(b, 2, HEADS, HEAD_D, HEADS, HEAD_D)
    idx = jnp.arange(HEADS)
    diag = s6[:, :, idx, :, idx, :]
    return jnp.transpose(diag, (1, 2, 0, 4, 3))


def _prep_layer(l, w):
    gw = GROUP_W
    w_in = w['w_in'][l]
    w_g = w_in[:, 9 * gw:9 * gw + N_GATES]
    lb_all = jnp.cumsum(jax.nn.softmax(w['hg_lb'].astype(F32), axis=0), axis=0)
    lb = lb_all[l] - lb_all[0]
    lb_rows = jnp.zeros((8, gw), F32).at[0].set(jnp.log(lb)).at[1].set(jnp.log1p(-lb)).at[2].set(1.0 - lb)
    w_r = jnp.zeros((D_MODEL, LANES), F32)
    w_r = w_r.at[:, 0:N_EXPERT_GROUPS].set(w['moe_wg'][l])
    w_r = w_r.at[:, N_EXPERT_GROUPS:N_EXPERT_GROUPS + N_EXPERTS].set(w['moe_we'][l])
    b_r = jnp.zeros((1, LANES), F32)
    b_r = b_r.at[0, 0:N_EXPERT_GROUPS].set(w['moe_bg'][l])
    b_r = b_r.at[0, N_EXPERT_GROUPS:N_EXPERT_GROUPS + N_EXPERTS].set(w['moe_be'][l])
    gate_bias = jnp.concatenate([w['ml_i_bias'][l], w['ml_f_bias'][l]])
    w1p = jnp.zeros((LANES, HY_HIDDEN), F32).at[0:HY_EMB].set(w['hy_w1'][l])
    return dict(
        norm1=w['norm1'][l][None, :], norm2=w['norm2'][l][None, :],
        w_a=w_in[:, :9 * gw].astype(BF16), w_b=w_in[:, 9 * gw + N_GATES:].astype(BF16),
        w_g=w_g.astype(BF16), w_gt=w_g.T.astype(BF16),
        w_out=w['w_out'][l].astype(BF16),
        lb_rows=lb_rows, hg_norm=w['hg_norm'][l][None, :],
        ml_conv=w['ml_conv'][l], ml_brow=gate_bias[None, :],
        ml_brows=jnp.repeat(gate_bias.reshape(4, HEADS), HEAD_D, axis=1),
        ml_norm=w['ml_norm'][l][None, :],
        s5=_s5_params(w['s5_a_re'][l], w['s5_a_im'][l], w['s5_log_dt'][l], w['s5_b_re'][l], w['s5_b_im'][l],
                      w['s5_c_re'][l], w['s5_c_im'][l]),
        s5_d=w['s5_d'][l][None, :], s5_glu_w=w['s5_glu_w'][l].astype(BF16), s5_glu_b=w['s5_glu_b'][l][None, :],
        hy_conv=w['hy_conv'][l], hy_bias=w['hy_bias'][l],
        hy_mlp=(w1p, w['hy_b1'][l][None, :], w['hy_w2'][l], w['hy_b2'][l][None, :], w['hy_w3'][l],
                w['hy_freq'][l][None, :], w['hy_decay'][l][None, :]),
        w_r=w_r, b_r=b_r,
        moe_gate=w['moe_w_gate'][l].astype(BF16), moe_up=w['moe_w_up'][l].astype(BF16),
        moe_down=w['moe_w_down'][l].astype(BF16),
    )


def _trunk_layer(x2d, nb, seq, mod3, cond_row, st, lw, filt, conv_width, emit_state, final_g, final):
    if cond_row is None:
        moe_tile = min(MOE_TILE, seq)
        row_of_tile = lambda i: i // (seq // TOK_TILE)
        moe_row_of_tile = lambda i: i // (seq // moe_tile)
    else:
        moe_tile = min(MOE_TILE, nb * seq)
        row_of_tile = moe_row_of_tile = lambda i: cond_row
    hg, ml, su, hy, gt, gtt = _inproj(x2d, mod3, row_of_tile, lw['norm1'], lw['w_a'], lw['w_b'], lw['w_g'],
                                       lw['w_gt'])
    nch = seq // ML_CHUNK
    a_out, hg_st = _hgrn(hg.reshape(nb, seq, -1), st['hgrn'], lw['lb_rows'], lw['hg_norm'], emit_state)
    gr4 = jnp.transpose(gtt.reshape(4, HEADS, nb, nch, ML_CHUNK), (2, 0, 3, 1, 4)).reshape(nb, 4, nch, GROUP_W)
    b_out, mlc, mln, mlm = _mlstm(ml.reshape(nb, seq, -1), gt.reshape(nb, seq, N_GATES), gr4, lw['ml_conv'],
                                  lw['ml_brow'], lw['ml_brows'], lw['ml_norm'], st['ml_c'], st['ml_n'],
                                  st['ml_m'], conv_width, emit_state)
    c_out, s5re, s5im = _s5(su.reshape(nb, seq, -1), st['s5_re'], st['s5_im'], lw['s5'], lw['s5_d'],
                            lw['s5_glu_w'], lw['s5_glu_b'])
    d_out = _hyena(hy.reshape(nb, seq, -1), lw['hy_conv'], filt, lw['hy_bias'], conv_width)
    mix = [m.reshape(nb * seq, GROUP_W) for m in (a_out, b_out, c_out, d_out)]
    x1, h2, gates = _outproj(mix, x2d, mod3, row_of_tile, lw['norm2'], lw['w_out'], lw['w_r'], lw['b_r'])
    x2 = _moe(h2, gates, lw['moe_gate'], lw['moe_up'], lw['moe_down'], x1, mod3, moe_tile,
              moe_row_of_tile, final_g, final)
    new_st = dict(hgrn=hg_st, ml_c=mlc, ml_n=mln, ml_m=mlm, s5_re=s5re, s5_im=s5im)
    return x2, new_st


def kernel(x_prompt, x_sample, c, state_hgrn, state_mlstm_c, state_mlstm_n, state_mlstm_m, state_s5_re, state_s5_im, c_ctx, w_ada, b_ada, norm1, norm2, w_in, w_out, hg_lb, hg_norm, ml_conv, ml_i_bias, ml_f_bias, ml_norm, s5_a_re, s5_a_im, s5_log_dt, s5_b_re, s5_b_im, s5_c_re, s5_c_im, s5_d, s5_glu_w, s5_glu_b, hy_conv, hy_w1, hy_b1, hy_w2, hy_b2, hy_w3, hy_freq, hy_decay, hy_bias, moe_wg, moe_bg, moe_we, moe_be, moe_w_gate, moe_w_up, moe_w_down, final_norm):
    w = dict(w_in=w_in, w_out=w_out, norm1=norm1, norm2=norm2, hg_lb=hg_lb, hg_norm=hg_norm, ml_conv=ml_conv,
             ml_i_bias=ml_i_bias, ml_f_bias=ml_f_bias, ml_norm=ml_norm, s5_a_re=s5_a_re, s5_a_im=s5_a_im,
             s5_log_dt=s5_log_dt, s5_b_re=s5_b_re, s5_b_im=s5_b_im, s5_c_re=s5_c_re, s5_c_im=s5_c_im,
             s5_d=s5_d, s5_glu_w=s5_glu_w, s5_glu_b=s5_glu_b, hy_conv=hy_conv, hy_w1=hy_w1, hy_b1=hy_b1,
             hy_w2=hy_w2, hy_b2=hy_b2, hy_w3=hy_w3, hy_freq=hy_freq, hy_decay=hy_decay, hy_bias=hy_bias,
             moe_wg=moe_wg, moe_bg=moe_bg, moe_we=moe_we, moe_be=moe_be, moe_w_gate=moe_w_gate,
             moe_w_up=moe_w_up, moe_w_down=moe_w_down)
    bp, lp, _ = x_prompt.shape
    bs, ls, _ = x_sample.shape
    ctx_row = bs
    cond = jnp.zeros((MOD_ROWS, D_MODEL), F32).at[0:bs].set(c).at[ctx_row].set(c_ctx)
    mod = _ada_mod(cond, w_ada, b_ada)
    final_g = final_norm[None, :]

    zero_st = dict(
        hgrn=jnp.zeros((bp, 2, GROUP_W, GROUP_W), F32),
        ml_c=jnp.zeros((bp, 2, HEADS, HEAD_D, HEAD_D), F32),
        ml_n=jnp.zeros((bp, 2, 1, GROUP_W), F32),
        ml_m=jnp.zeros((bp, 2, 1, GROUP_W), F32),
        s5_re=jnp.zeros((bp, 2, S5_STATE), F32),
        s5_im=jnp.zeros((bp, 2, S5_STATE), F32))

    xp = x_prompt.reshape(bp * lp, D_MODEL)
    xs = x_sample.reshape(bs * ls, D_MODEL)
    per_layer = []
    for l in range(DEPTH):
        lw = _prep_layer(l, w)
        mod3 = mod[l].reshape(MOD_ROWS, 1, 6 * D_MODEL)
        last = l == DEPTH - 1
        filt_p = _hyena_filter(lp, *lw['hy_mlp'])
        filt_s = filt_p if ls == lp else _hyena_filter(ls, *lw['hy_mlp'])
        xp, st = _trunk_layer(xp, bp, lp, mod3, ctx_row, zero_st, lw, filt_p, lp, True, final_g, last)
        per_layer.append(st)
        st_in = dict(
            hgrn=_block_diag_t(state_hgrn[:, l].astype(F32)),
            ml_c=state_mlstm_c[:, l].astype(F32),
            ml_n=state_mlstm_n[:, l].astype(F32).reshape(bs, 2, 1, GROUP_W),
            ml_m=jnp.repeat(state_mlstm_m[:, l].astype(F32), HEAD_D, axis=-1).reshape(bs, 2, 1, GROUP_W),
            s5_re=state_s5_re[:, l].astype(F32).reshape(bs, 2, S5_STATE),
            s5_im=state_s5_im[:, l].astype(F32).reshape(bs, 2, S5_STATE))
        xs, _ = _trunk_layer(xs, bs, ls, mod3, None, st_in, lw, filt_s, GRID_W, False, final_g, last)

    new_hgrn = jnp.stack([_block_diag_t_inv(s['hgrn']) for s in per_layer], axis=1)
    new_ml_c = jnp.stack([s['ml_c'] for s in per_layer], axis=1)
    new_ml_n = jnp.stack([s['ml_n'].reshape(bp, 2, HEADS, HEAD_D) for s in per_layer], axis=1)
    new_ml_m = jnp.stack([s['ml_m'][:, :, 0, ::HEAD_D] for s in per_layer], axis=1)
    new_s5_re = jnp.stack([s['s5_re'].reshape(bp, 2, S5_GROUPS, S5_P) for s in per_layer], axis=1)
    new_s5_im = jnp.stack([s['s5_im'].reshape(bp, 2, S5_GROUPS, S5_P) for s in per_layer], axis=1)
    return (xp.reshape(bp, lp, D_MODEL), xs.reshape(bs, ls, D_MODEL),
            new_hgrn, new_ml_c, new_ml_n, new_ml_m, new_s5_re, new_s5_im)
```

```python
import functools
import math

import numpy as np
import jax
import jax.numpy as jnp
from jax import lax
from jax.experimental import pallas as pl
from jax.experimental.pallas import tpu as pltpu

F32 = jnp.float32
BF16 = jnp.bfloat16

D_MODEL = 1024
DEPTH = 2
GRID_W = 64
GROUP_W = D_MODEL // 4
HEADS = 4
HEAD_D = GROUP_W // HEADS
S5_CH = 16
S5_GROUPS = GROUP_W // S5_CH
S5_P = 64
S5_STATE = S5_GROUPS * S5_P
HY_ORDER = 2
HY_BANDS = 16
HY_EMB = 1 + 2 * HY_BANDS
HY_HIDDEN = 64
HY_SHIFT = 0.05
N_EXPERT_GROUPS = 4
EXPERTS_PER_GROUP = 4
N_EXPERTS = N_EXPERT_GROUPS * EXPERTS_PER_GROUP
D_EXPERT = D_MODEL // 2
EPS = 1e-6
N_GATES = 4 * HEADS

LANES = 128
MOD_ROWS = 16
TOK_TILE = 256
MOE_TOK = TOK_TILE
MOE_ROWS = 512
ROW_ALIGN = 16
HG_BLK = 16
HG_SB = 128
ML_CHUNK = 64
S5_TC = 64
NEG_INF = float("-inf")


def _dot(a, b):
    return jnp.dot(a, b, preferred_element_type=F32)


def _dot_nt(a, b):
    return lax.dot_general(a, b, (((1,), (1,)), ((), ())), preferred_element_type=F32)


def _dot_tn(a, b):
    return lax.dot_general(a, b, (((0,), (0,)), ((), ())), preferred_element_type=F32)


def _split3(x):
    hi = x.astype(BF16)
    r1 = x - hi.astype(F32)
    mid = r1.astype(BF16)
    lo = (r1 - mid.astype(F32)).astype(BF16)
    return hi, mid, lo


def _dot_sel_l(m, x):
    hi, mid, lo = _split3(x)
    return _dot(m, hi) + _dot(m, mid) + _dot(m, lo)


def _dot_sel_r(x, m):
    hi, mid, lo = _split3(x)
    return _dot(hi, m) + _dot(mid, m) + _dot(lo, m)


def _dot3(a, b):
    ah = a.astype(BF16)
    al = (a - ah.astype(F32)).astype(BF16)
    bh = b.astype(BF16)
    bl = (b - bh.astype(F32)).astype(BF16)
    return _dot(ah, bh) + _dot(al, bh) + _dot(ah, bl)


def _sigmoid(x):
    return 1.0 / (1.0 + jnp.exp(-x))


def _silu(x):
    return x * _sigmoid(x)


def _log_sigmoid(x):
    return jnp.minimum(x, 0.0) - jnp.log1p(jnp.exp(-jnp.abs(x)))


def _rmsnorm_rows(x, g):
    return x * lax.rsqrt(jnp.mean(x * x, axis=-1, keepdims=True) + EPS) * g


def _ada_kernel(c_ref, w_ref, b_ref, o_ref):
    s = _silu(c_ref[...]).astype(BF16)
    o_ref[0] = _dot(s, w_ref[0].astype(BF16)) + b_ref[0]


def _ada_mod(cond, w_ada, b_ada):
    tn = 1536
    n = w_ada.shape[-1]
    return pl.pallas_call(
        _ada_kernel,
        grid=(DEPTH, n // tn),
        in_specs=[
            pl.BlockSpec((MOD_ROWS, D_MODEL), lambda l, j: (0, 0)),
            pl.BlockSpec((1, D_MODEL, tn), lambda l, j: (l, 0, j)),
            pl.BlockSpec((1, 1, tn), lambda l, j: (l, 0, j)),
        ],
        out_specs=pl.BlockSpec((1, MOD_ROWS, tn), lambda l, j: (l, 0, j)),
        out_shape=jax.ShapeDtypeStruct((DEPTH, MOD_ROWS, n), F32),
        name="ada_mod",
    )(cond, w_ada, b_ada.reshape(DEPTH, 1, n))


def _inproj_kernel(x_ref, sh_ref, sc_ref, g_ref, wa_ref, wb_ref, wg_ref, wgt_ref,
                   hg_ref, ml_ref, su_ref, hy_ref, gt_ref, gtt_ref):
    h = _rmsnorm_rows(x_ref[...], g_ref[...])
    h = h * (1.0 + sc_ref[0]) + sh_ref[0]
    hb = h.astype(BF16)
    hg_ref[...] = _dot(hb, wa_ref[:, 0:5 * GROUP_W])
    ml_ref[...] = _dot(hb, wa_ref[:, 5 * GROUP_W:9 * GROUP_W])
    su_ref[...] = _dot(hb, wb_ref[:, 0:GROUP_W])
    hy_ref[...] = _dot(hb, wb_ref[:, GROUP_W:4 * GROUP_W])
    gt_ref[...] = _dot(hb, wg_ref[...])
    gtt_ref[...] = _dot_nt(wgt_ref[...], hb)


def _mod_spec(k, row_of_tile):
    return pl.BlockSpec((1, 1, D_MODEL), lambda i: (row_of_tile(i), 0, k))


def _inproj(x2d, mod3, row_of_tile, norm_g, w_a, w_b, w_g, w_gt):
    t = x2d.shape[0]
    tm = TOK_TILE
    const = lambda i: (0, 0)
    tile = lambda i: (i, 0)
    widths = (5 * GROUP_W, 4 * GROUP_W, GROUP_W, 3 * GROUP_W)
    return pl.pallas_call(
        _inproj_kernel,
        grid=(t // tm,),
        in_specs=[
            pl.BlockSpec((tm, D_MODEL), tile),
            _mod_spec(0, row_of_tile),
            _mod_spec(1, row_of_tile),
            pl.BlockSpec((1, D_MODEL), const),
            pl.BlockSpec(w_a.shape, const),
            pl.BlockSpec(w_b.shape, const),
            pl.BlockSpec(w_g.shape, const),
            pl.BlockSpec(w_gt.shape, const),
        ],
        out_specs=[pl.BlockSpec((tm, w), tile) for w in widths]
        + [pl.BlockSpec((tm, N_GATES), tile), pl.BlockSpec((N_GATES, tm), lambda i: (0, i))],
        out_shape=[jax.ShapeDtypeStruct((t, w), F32) for w in widths]
        + [jax.ShapeDtypeStruct((t, N_GATES), F32), jax.ShapeDtypeStruct((N_GATES, t), F32)],
        name="inproj",
    )(x2d, mod3, mod3, norm_g, w_a, w_b, w_g, w_gt)


@functools.lru_cache(maxsize=None)
def _hgrn_consts():
    r = np.arange(HG_SB)
    same = (r[:, None] // HG_BLK) == (r[None, :] // HG_BLK)
    tri_f = same & (r[:, None] >= r[None, :])
    tri_b = same & (r[:, None] <= r[None, :])
    c = np.arange(GROUP_W)
    head = (c[:, None] // HEAD_D) == (c[None, :] // HEAD_D)
    return (np.stack([tri_f, tri_b]).astype(np.float32), same.astype(np.float32),
            head.astype(np.float32))


def _hgrn_kernel(p_ref, s0_ref, lb_ref, ng_ref, tri_ref, bones_ref, e_ref, hmask_ref,
                 out_ref, *rest, seq, emit_state):
    if emit_state:
        st_ref, of_scr, st_scr, q_scr, k_scr, bc_scr, gq_scr, gk_scr, dec_scr = rest
    else:
        of_scr, st_scr, q_scr, k_scr, bc_scr, gq_scr, gk_scr, dec_scr = rest
    nsb = seq // HG_SB
    nblk = HG_SB // HG_BLK
    log_lb = lb_ref[0:1, :]
    log_1mlb = lb_ref[1:2, :]
    one_m_lb = lb_ref[2:3, :]
    row = lax.broadcasted_iota(jnp.int32, (HG_BLK, 1), 0)

    def superblock(sb, d, finalize):
        r0 = pl.multiple_of(sb * HG_SB, HG_SB)
        rows = pl.ds(r0, HG_SB)
        pq = p_ref[rows, 0:GROUP_W]
        pf = p_ref[rows, (1 + d) * GROUP_W:(2 + d) * GROUP_W]
        q_scr[...] = _silu(pq)
        b2 = log_1mlb + _log_sigmoid(pf)
        mx = jnp.maximum(log_lb, b2)
        mn = jnp.minimum(log_lb, b2)
        delta = jnp.where(mn == NEG_INF, NEG_INF, mn - mx)
        lf = mx + jnp.log1p(jnp.exp(delta))
        k_scr[...] = one_m_lb * _sigmoid(-pf)
        bc = _dot_sel_l(tri_ref[d], lf)
        bt = _dot_sel_l(bones_ref[...], lf)
        bc_scr[...] = bc
        gq_scr[...] = (q_scr[...] * jnp.exp(bc)).astype(BF16)
        gk_scr[...] = (k_scr[...] * jnp.exp(bt - bc)).astype(BF16)
        dec_scr[...] = jnp.exp(bt)
        o_blocks = [None] * nblk
        order = range(nblk) if d == 0 else range(nblk - 1, -1, -1)
        for j in order:
            sl = slice(HG_BLK * j, HG_BLK * (j + 1))
            qj = q_scr[sl, :]
            kj = k_scr[sl, :]
            vj = p_ref[pl.ds(r0 + HG_BLK * j, HG_BLK), 3 * GROUP_W:4 * GROUP_W]
            bcj = bc_scr[sl, :]
            parts = []
            for s in range(HG_BLK):
                diff = bcj - bcj[s:s + 1, :]
                keep = (row >= s) if d == 0 else (row <= s)
                dd = jnp.exp(jnp.where(keep, diff, NEG_INF))
                parts.append((qj * dd * kj[s:s + 1, :]).astype(BF16))
            r_all = _dot(jnp.concatenate(parts, axis=0), e_ref[...])
            o = _dot_nt(gq_scr[sl, :], st_scr[...].astype(BF16))
            for s in range(HG_BLK):
                o = o + r_all[HG_BLK * s:HG_BLK * (s + 1), :] * vj[s:s + 1, :]
            upd = _dot_tn(vj.astype(BF16), gk_scr[sl, :])
            st_scr[...] = st_scr[...] * dec_scr[HG_BLK * j:HG_BLK * j + 1, :] + upd * hmask_ref[...]
            o_blocks[j] = o
        o_sb = jnp.concatenate(o_blocks, axis=0)
        if finalize:
            o_sb = o_sb + of_scr[rows, :]
            ms = _dot_sel_r(o_sb * o_sb, e_ref[...]) * (1.0 / HEAD_D)
            pg = p_ref[rows, 4 * GROUP_W:5 * GROUP_W]
            y = o_sb * lax.rsqrt(ms + EPS) * ng_ref[...] * _silu(pg)
            out_ref[rows, :] = y.astype(out_ref.dtype)
        else:
            of_scr[rows, :] = o_sb

    st_scr[...] = s0_ref[0]

    def fwd(i, carry):
        superblock(i, 0, False)
        return carry

    lax.fori_loop(0, nsb, fwd, 0)
    if emit_state:
        st_ref[0] = st_scr[...]
    st_scr[...] = s0_ref[1]

    def bwd(i, carry):
        superblock(nsb - 1 - i, 1, True)
        return carry

    lax.fori_loop(0, nsb, bwd, 0)
    if emit_state:
        st_ref[1] = st_scr[...]


def _hgrn(p3, s0_bd, lb_rows, norm_g, emit_state):
    b, seq, _ = p3.shape
    tri, bones, head = _hgrn_consts()
    const2 = lambda i: (0, 0)
    out_shape = [jax.ShapeDtypeStruct((b, seq, GROUP_W), BF16)]
    out_specs = [pl.BlockSpec((None, seq, GROUP_W), lambda i: (i, 0, 0))]
    if emit_state:
        out_shape.append(jax.ShapeDtypeStruct((b, 2, GROUP_W, GROUP_W), F32))
        out_specs.append(pl.BlockSpec((None, 2, GROUP_W, GROUP_W), lambda i: (i, 0, 0, 0)))
    res = pl.pallas_call(
        functools.partial(_hgrn_kernel, seq=seq, emit_state=emit_state),
        grid=(b,),
        in_specs=[
            pl.BlockSpec((None, seq, 5 * GROUP_W), lambda i: (i, 0, 0)),
            pl.BlockSpec((None, 2, GROUP_W, GROUP_W), lambda i: (i, 0, 0, 0)),
            pl.BlockSpec((8, GROUP_W), const2),
            pl.BlockSpec((1, GROUP_W), const2),
            pl.BlockSpec((2, HG_SB, HG_SB), lambda i: (0, 0, 0)),
            pl.BlockSpec((HG_SB, HG_SB), const2),
            pl.BlockSpec((GROUP_W, GROUP_W), const2),
            pl.BlockSpec((GROUP_W, GROUP_W), const2),
        ],
        out_specs=out_specs,
        out_shape=out_shape,
        scratch_shapes=[
            pltpu.VMEM((seq, GROUP_W), F32),
            pltpu.VMEM((GROUP_W, GROUP_W), F32),
            pltpu.VMEM((HG_SB, GROUP_W), F32),
            pltpu.VMEM((HG_SB, GROUP_W), F32),
            pltpu.VMEM((HG_SB, GROUP_W), F32),
            pltpu.VMEM((HG_SB, GROUP_W), BF16),
            pltpu.VMEM((HG_SB, GROUP_W), BF16),
            pltpu.VMEM((HG_SB, GROUP_W), F32),
        ],
        name="hgrn2",
    )(p3, s0_bd, lb_rows, norm_g, jnp.asarray(tri, BF16), jnp.asarray(bones, BF16),
      jnp.asarray(head, BF16), jnp.asarray(head, F32))
    return res if emit_state else (res[0], None)


@functools.lru_cache(maxsize=None)
def _mlstm_consts():
    r = np.arange(ML_CHUNK)
    tri_f = (r[:, None] >= r[None, :]).astype(np.float32)
    tri = np.stack([tri_f, tri_f.T])
    c = np.arange(GROUP_W)
    head = (c[:, None] // HEAD_D) == (c[None, :] // HEAD_D)
    pos = c % HEAD_D
    row_f = head & (pos[:, None] <= pos[None, :])
    row_b = head & (pos[:, None] >= pos[None, :])
    rowtri = np.stack([row_f, row_b]).astype(np.float32)
    expand = np.zeros((2, N_GATES, 2 * GROUP_W), np.float32)
    for d in range(2):
        for h in range(HEADS):
            expand[d, d * HEADS + h, h * HEAD_D:(h + 1) * HEAD_D] = 1.0
            expand[d, 2 * HEADS + d * HEADS + h, GROUP_W + h * HEAD_D:GROUP_W + (h + 1) * HEAD_D] = 1.0
    return tri, rowtri, expand, head.astype(np.float32)


def _dot_hilo(x, m):
    hi = x.astype(BF16)
    lo = (x - hi.astype(F32)).astype(BF16)
    return _dot(hi, m) + _dot(lo, m)


def _mlstm_kernel(p_ref, g_ref, gr_ref, conv_ref, brow_ref, brows_ref, ng_ref, c0_ref, n0_ref, m0_ref,
                  tri_ref, rowtri_ref, exp_ref, e_ref, out_ref, *rest, seq, conv_w, emit_state):
    if emit_state:
        cst_ref, nst_ref, mst_ref = rest[:3]
        rest = rest[3:]
    (q_scr, k_scr, col_scr, mloc_scr, gb_scr, h_scr, row_scr, blast_scr, gmax_scr,
     c_scr, n_scr, m_scr) = rest
    nch = seq // ML_CHUNK
    rowid = lax.broadcasted_iota(jnp.int32, (ML_CHUNK, 1), 0)
    ti = lax.broadcasted_iota(jnp.int32, (ML_CHUNK, GROUP_W), 0)
    si = lax.broadcasted_iota(jnp.int32, (ML_CHUNK, GROUP_W), 1) % HEAD_D

    for d in range(2):
        li_row = gr_ref[d] + brows_ref[d:d + 1, :]
        lf_row = _log_sigmoid(gr_ref[2 + d] + brows_ref[2 + d:3 + d, :])
        row_scr[d] = li_row - _dot_sel_r(lf_row, rowtri_ref[d])

    def pre_chunk(ci, carry):
        r0 = pl.multiple_of(ci * ML_CHUNK, ML_CHUNK)
        rows = pl.ds(r0, ML_CHUNK)
        zc = p_ref[rows, 0:2 * GROUP_W]
        zp = pltpu.roll(zc, 1, axis=0)
        zn = pltpu.roll(zc, ML_CHUNK - 1, axis=0)
        if conv_w == ML_CHUNK:
            prev_row = jnp.zeros((1, 2 * GROUP_W), F32)
            next_row = prev_row
        else:
            prev_row = p_ref[pl.ds(jnp.maximum(r0 - 1, 0), 1), 0:2 * GROUP_W]
            prev_row = jnp.where(r0 % conv_w == 0, 0.0, prev_row)
            next_row = p_ref[pl.ds(jnp.minimum(r0 + ML_CHUNK, seq - 1), 1), 0:2 * GROUP_W]
            next_row = jnp.where((r0 + ML_CHUNK) % conv_w == 0, 0.0, next_row)
        zp = jnp.where(rowid == 0, prev_row, zp)
        zn = jnp.where(rowid == ML_CHUNK - 1, next_row, zn)
        qk = _silu(conv_ref[0:1, :] * zp + conv_ref[1:2, :] * zc + conv_ref[2:3, :] * zn)
        q_scr[rows, :] = qk[:, 0:GROUP_W].astype(BF16)
        k_scr[rows, :] = qk[:, GROUP_W:2 * GROUP_W] * (HEAD_D ** -0.5)
        ge = g_ref[rows, :] + brow_ref[...]
        for d in range(2):
            ex = _dot_sel_r(ge, exp_ref[d])
            li_e = ex[:, 0:GROUP_W]
            colb = _dot_sel_l(tri_ref[d], _log_sigmoid(ex[:, GROUP_W:2 * GROUP_W]))
            x = li_e - colb
            for sh in (1, 2, 4, 8, 16, 32):
                if d == 0:
                    shifted = jnp.where(rowid >= sh, pltpu.roll(x, sh, axis=0), NEG_INF)
                else:
                    shifted = jnp.where(rowid < ML_CHUNK - sh, pltpu.roll(x, ML_CHUNK - sh, axis=0), NEG_INF)
                x = jnp.maximum(x, shifted)
            edge = ML_CHUNK - 1 if d == 0 else 0
            blast = colb[edge:edge + 1, :]
            gb = blast - colb + li_e
            col_scr[d, rows, :] = colb
            mloc_scr[d, rows, :] = colb + x
            gb_scr[d, rows, :] = gb
            blast_scr[d, pl.ds(ci, 1), :] = blast
            gmax_scr[d, pl.ds(ci, 1), :] = jnp.max(gb, axis=0, keepdims=True)
        return carry

    lax.fori_loop(0, nch, pre_chunk, 0)

    for d in range(2):
        c_scr[d] = jnp.zeros((GROUP_W, GROUP_W), F32)
        for h in range(HEADS):
            lo = HEAD_D * h
            c_scr[d, lo:lo + HEAD_D, lo:lo + HEAD_D] = c0_ref[d, h]
    n_scr[...] = n0_ref[...]
    m_scr[...] = m0_ref[...]

    def chunk(ci, d):
        r0 = pl.multiple_of(ci * ML_CHUNK, ML_CHUNK)
        rows = pl.ds(r0, ML_CHUNK)
        qb = q_scr[rows, :]
        kc = k_scr[rows, :]
        vb = p_ref[rows, 2 * GROUP_W:3 * GROUP_W].astype(BF16)
        hm = e_ref[...]
        kbd = jnp.concatenate([kc.astype(BF16)] * HEADS, axis=0) * hm
        vbd = jnp.concatenate([vb] * HEADS, axis=0) * hm
        s = _dot_nt(qb, kbd)
        colb = col_scr[d, rows, :]
        m_b = m_scr[d]
        prev = colb + m_b
        m_t = jnp.maximum(prev, mloc_scr[d, rows, :])
        keep = (ti >= si) if d == 0 else (ti <= si)
        w = jnp.exp(jnp.where(keep, colb + row_scr[d, pl.ds(ci, 1), :], NEG_INF) - m_t) * s
        wp = jnp.exp(prev - m_t)
        cst = c_scr[d]
        nrow = n_scr[d]
        num = wp * _dot(qb, cst.astype(BF16)) + _dot(w.astype(BF16), vbd)
        den = _dot_hilo(wp * qb.astype(F32) * nrow + w, e_ref[...])
        h_scr[d, rows, :] = num / jnp.maximum(jnp.abs(den), jnp.exp(-m_t))
        blast = blast_scr[d, pl.ds(ci, 1), :]
        m_new = jnp.maximum(blast + m_b, gmax_scr[d, pl.ds(ci, 1), :])
        dec = jnp.exp(blast + m_b - m_new)
        kw = kc * jnp.exp(gb_scr[d, rows, :] - m_new)
        c_scr[d] = cst * dec + _dot_tn(kw.astype(BF16), vb) * hm.astype(F32)
        n_scr[d] = nrow * dec + jnp.sum(kw, axis=0, keepdims=True)
        m_scr[d] = m_new

    def body(i, carry):
        chunk(i, 0)
        chunk(nch - 1 - i, 1)
        return carry

    lax.fori_loop(0, nch, body, 0)

    def fin_chunk(ci, carry):
        rows = pl.ds(pl.multiple_of(ci * ML_CHUNK, ML_CHUNK), ML_CHUNK)
        hs = h_scr[0, rows, :] + h_scr[1, rows, :]
        ms = _dot_hilo(hs * hs, e_ref[...]) * (1.0 / HEAD_D)
        po = p_ref[rows, 3 * GROUP_W:4 * GROUP_W]
        out_ref[rows, :] = (hs * lax.rsqrt(ms + EPS) * ng_ref[...] * _sigmoid(po)).astype(out_ref.dtype)
        return carry

    lax.fori_loop(0, nch, fin_chunk, 0)
    if emit_state:
        for d in range(2):
            for h in range(HEADS):
                lo = HEAD_D * h
                cst_ref[d, h] = c_scr[d, lo:lo + HEAD_D, lo:lo + HEAD_D]
        nst_ref[...] = n_scr[...]
        mst_ref[...] = m_scr[...]


def _mlstm_kernel_old(p_ref, g_ref, gt_ref, conv_ref, brow_ref, bcol_ref, ng_ref,
                  c0_ref, n0_ref, m0_ref, tri_ref, out_ref, *rest, seq, conv_w, emit_state):
    if emit_state:
        cst_ref, nst_ref, mst_ref, qk_scr, hf_scr, c_scr, n_scr, m_scr = rest
    else:
        qk_scr, hf_scr, c_scr, n_scr, m_scr = rest
    nch = seq // ML_CHUNK
    rowid = lax.broadcasted_iota(jnp.int32, (ML_CHUNK, 1), 0)
    ti = lax.broadcasted_iota(jnp.int32, (ML_CHUNK, ML_CHUNK), 0)
    si = lax.broadcasted_iota(jnp.int32, (ML_CHUNK, ML_CHUNK), 1)

    def conv_chunk(ci, carry):
        r0 = pl.multiple_of(ci * ML_CHUNK, ML_CHUNK)
        zc = p_ref[pl.ds(r0, ML_CHUNK), 0:2 * GROUP_W]
        zp = pltpu.roll(zc, 1, axis=0)
        zn = pltpu.roll(zc, ML_CHUNK - 1, axis=0)
        if conv_w == ML_CHUNK:
            prev_row = jnp.zeros((1, 2 * GROUP_W), F32)
            next_row = prev_row
        else:
            prev_row = p_ref[pl.ds(jnp.maximum(r0 - 1, 0), 1), 0:2 * GROUP_W]
            prev_row = jnp.where(r0 % conv_w == 0, 0.0, prev_row)
            next_row = p_ref[pl.ds(jnp.minimum(r0 + ML_CHUNK, seq - 1), 1), 0:2 * GROUP_W]
            next_row = jnp.where((r0 + ML_CHUNK) % conv_w == 0, 0.0, next_row)
        zp = jnp.where(rowid == 0, prev_row, zp)
        zn = jnp.where(rowid == ML_CHUNK - 1, next_row, zn)
        out = conv_ref[0:1, :] * zp + conv_ref[1:2, :] * zc + conv_ref[2:3, :] * zn
        qk_scr[pl.ds(r0, ML_CHUNK), :] = _silu(out)
        return carry

    lax.fori_loop(0, nch, conv_chunk, 0)

    def chunk(ci, d, finalize):
        r0 = pl.multiple_of(ci * ML_CHUNK, ML_CHUNK)
        rows = pl.ds(r0, ML_CHUNK)
        g = g_ref[rows, :] + brow_ref[...]
        gt = gt_ref[ci] + bcol_ref[...]
        li = g[:, 0:2 * HEADS]
        lf = _log_sigmoid(g[:, 2 * HEADS:4 * HEADS])
        lit = gt[0:2 * HEADS, :]
        lft = _log_sigmoid(gt[2 * HEADS:4 * HEADS, :])
        bc_all = _dot_sel_l(tri_ref[d], lf)
        bct_all = _dot_sel_r(lft, tri_ref[1 - d])
        keep = (ti >= si) if d == 0 else (ti <= si)
        edge = ML_CHUNK - 1 if d == 0 else 0
        outs = []
        for h in range(HEADS):
            c = HEADS * d + h
            lo = HEAD_D * h
            q = qk_scr[rows, lo:lo + HEAD_D]
            k = qk_scr[rows, GROUP_W + lo:GROUP_W + lo + HEAD_D] * (HEAD_D ** -0.5)
            v = p_ref[rows, 2 * GROUP_W + lo:2 * GROUP_W + lo + HEAD_D]
            qb = q.astype(BF16)
            vb = v.astype(BF16)
            bcol = bc_all[:, c:c + 1]
            brow = bct_all[c:c + 1, :]
            m = m_scr[h]
            dmat = jnp.where(keep, bcol - brow + lit[c:c + 1, :], NEG_INF)
            prev = bcol + m
            m_t = jnp.maximum(prev, jnp.max(dmat, axis=1, keepdims=True))
            w = jnp.exp(dmat - m_t) * _dot_nt(qb, k.astype(BF16))
            wp = jnp.exp(prev - m_t)
            cst = c_scr[h]
            nrow = n_scr[h]
            num = wp * _dot(qb, cst.astype(BF16)) + _dot(w.astype(BF16), vb)
            den = wp * jnp.sum(q * nrow, axis=1, keepdims=True) + jnp.sum(w, axis=1, keepdims=True)
            hh = num / jnp.maximum(jnp.abs(den), jnp.exp(-m_t))
            blast = bcol[edge:edge + 1, :]
            gc = blast - bcol + li[:, c:c + 1]
            m_new = jnp.maximum(blast + m, jnp.max(gc, axis=0, keepdims=True))
            dec = jnp.exp(blast + m - m_new)
            kw = k * jnp.exp(gc - m_new)
            c_scr[h] = dec * cst + _dot_tn(kw.astype(BF16), vb)
            n_scr[h] = dec * nrow + jnp.sum(kw, axis=0, keepdims=True)
            m_scr[h] = m_new
            if finalize:
                hs = hh + hf_scr[rows, lo:lo + HEAD_D]
                hs = hs * lax.rsqrt(jnp.mean(hs * hs, axis=-1, keepdims=True) + EPS)
                outs.append(hs)
            else:
                outs.append(hh)
        o = jnp.concatenate(outs, axis=1)
        if finalize:
            po = p_ref[rows, 3 * GROUP_W:4 * GROUP_W]
            out_ref[rows, :] = (o * ng_ref[...] * _sigmoid(po)).astype(out_ref.dtype)
        else:
            hf_scr[rows, :] = o

    def run(d):
        c_scr[...] = c0_ref[d]
        n_scr[...] = n0_ref[d]
        m_scr[...] = m0_ref[d]

        def body(i, carry):
            chunk(i if d == 0 else nch - 1 - i, d, d == 1)
            return carry

        lax.fori_loop(0, nch, body, 0)
        if emit_state:
            cst_ref[d] = c_scr[...]
            nst_ref[d] = n_scr[...]
            mst_ref[d] = m_scr[...]

    run(0)
    run(1)


def _mlstm(p3, g3, gr4, conv_w3, brow, brows, norm_g, c0, n0, m0, conv_width, emit_state):
    b, seq, _ = p3.shape
    nch = seq // ML_CHUNK
    tri, rowtri, expand, head = _mlstm_consts()
    const2 = lambda i: (0, 0)
    const3 = lambda i: (0, 0, 0)
    out_shape = [jax.ShapeDtypeStruct((b, seq, GROUP_W), BF16)]
    out_specs = [pl.BlockSpec((None, seq, GROUP_W), lambda i: (i, 0, 0))]
    c_spec = pl.BlockSpec((None, 2, HEADS, HEAD_D, HEAD_D), lambda i: (i, 0, 0, 0, 0))
    row_spec = pl.BlockSpec((None, 2, 1, GROUP_W), lambda i: (i, 0, 0, 0))
    if emit_state:
        out_shape += [jax.ShapeDtypeStruct((b, 2, HEADS, HEAD_D, HEAD_D), F32),
                      jax.ShapeDtypeStruct((b, 2, 1, GROUP_W), F32), jax.ShapeDtypeStruct((b, 2, 1, GROUP_W), F32)]
        out_specs += [c_spec, row_spec, row_spec]
    dir_seq = (2, seq, GROUP_W)
    dir_chunks = (2, nch, GROUP_W)
    res = pl.pallas_call(
        functools.partial(_mlstm_kernel, seq=seq, conv_w=conv_width, emit_state=emit_state),
        grid=(b,),
        in_specs=[
            pl.BlockSpec((None, seq, 4 * GROUP_W), lambda i: (i, 0, 0)),
            pl.BlockSpec((None, seq, N_GATES), lambda i: (i, 0, 0)),
            pl.BlockSpec((None, 4, nch, GROUP_W), lambda i: (i, 0, 0, 0)),
            pl.BlockSpec((3, 2 * GROUP_W), const2),
            pl.BlockSpec((1, N_GATES), const2),
            pl.BlockSpec((4, GROUP_W), const2),
            pl.BlockSpec((1, GROUP_W), const2),
            c_spec, row_spec, row_spec,
            pl.BlockSpec((2, ML_CHUNK, ML_CHUNK), const3),
            pl.BlockSpec((2, GROUP_W, GROUP_W), const3),
            pl.BlockSpec((2, N_GATES, 2 * GROUP_W), const3),
            pl.BlockSpec((GROUP_W, GROUP_W), const2),
        ],
        out_specs=out_specs,
        out_shape=out_shape,
        scratch_shapes=[
            pltpu.VMEM((seq, GROUP_W), BF16),
            pltpu.VMEM((seq, GROUP_W), F32),
            pltpu.VMEM(dir_seq, F32), pltpu.VMEM(dir_seq, F32), pltpu.VMEM(dir_seq, F32), pltpu.VMEM(dir_seq, F32),
            pltpu.VMEM(dir_chunks, F32), pltpu.VMEM(dir_chunks, F32), pltpu.VMEM(dir_chunks, F32),
            pltpu.VMEM((2, GROUP_W, GROUP_W), F32),
            pltpu.VMEM((2, 1, GROUP_W), F32),
            pltpu.VMEM((2, 1, GROUP_W), F32),
        ],
        name="mlstm",
    )(p3, g3, gr4, conv_w3, brow, brows, norm_g, c0, n0, m0, jnp.asarray(tri, BF16), jnp.asarray(rowtri, BF16),
      jnp.asarray(expand, BF16), jnp.asarray(head, BF16))
    if emit_state:
        return res
    return res[0], None, None, None


def _s5_kernel(*refs, nb, reverse, final):
    if final:
        (u_ref, hre0_ref, him0_ref, are_ref, aim_ref, wb_ref, wc_ref, yin_ref, d_ref, gw_ref, gb_ref,
         y_ref, sre_ref, sim_ref, utm_scr, bu_scr, ytm_scr, hre_scr, him_scr) = refs
    else:
        (u_ref, hre0_ref, him0_ref, are_ref, aim_ref, wb_ref, wc_ref,
         y_ref, sre_ref, sim_ref, utm_scr, bu_scr, ytm_scr, hre_scr, him_scr) = refs
    tc = S5_TC
    step_id = pl.program_id(0)

    @pl.when(step_id == 0)
    def _():
        hre_scr[...] = hre0_ref[...]
        him_scr[...] = him0_ref[...]

    halves = GROUP_W // LANES
    for b in range(nb):
        for hv in range(halves):
            utm_scr[hv, pl.ds(b, tc, stride=nb), :] = u_ref[b, :, hv * LANES:(hv + 1) * LANES]
    u_tm = jnp.concatenate([utm_scr[hv] for hv in range(halves)], axis=1)
    bu_scr[...] = _dot(u_tm.astype(BF16), wb_ref[...])
    are = jnp.broadcast_to(are_ref[...], (nb, S5_STATE))
    aim = jnp.broadcast_to(aim_ref[...], (nb, S5_STATE))

    def step(i, carry):
        hre, him = carry
        t = (tc - 1 - i) if reverse else i
        r0 = pl.multiple_of(t * nb, nb)
        bu = bu_scr[pl.ds(r0, nb), :]
        nre = are * hre - aim * him + bu[:, 0:S5_STATE]
        nim = are * him + aim * hre + bu[:, S5_STATE:2 * S5_STATE]
        bu_scr[pl.ds(r0, nb), 0:S5_STATE] = nre
        bu_scr[pl.ds(r0, nb), S5_STATE:2 * S5_STATE] = nim
        return nre, nim

    hre, him = lax.fori_loop(0, tc, step, (hre_scr[...], him_scr[...]))
    hre_scr[...] = hre
    him_scr[...] = him
    y_tm = _dot(bu_scr[...].astype(BF16), wc_ref[...])
    for hv in range(halves):
        ytm_scr[hv] = y_tm[:, hv * LANES:(hv + 1) * LANES]
    for b in range(nb):
        yb = jnp.concatenate([ytm_scr[hv, pl.ds(b, tc, stride=nb), :] for hv in range(halves)], axis=1)
        if final:
            y = yb + yin_ref[b] + d_ref[...] * u_ref[b]
            z = 0.5 * y * (1.0 + jnp.tanh(math.sqrt(2.0 / math.pi) * (y + 0.044715 * (y * y * y))))
            gate = _sigmoid(_dot(z.astype(BF16), gw_ref[...]) + gb_ref[...])
            y_ref[b] = (z * gate).astype(y_ref.dtype)
        else:
            y_ref[b] = yb

    @pl.when(step_id == pl.num_programs(0) - 1)
    def _():
        sre_ref[...] = hre
        sim_ref[...] = him


def _s5_pass(u3, hre0, him0, are, aim, wb, wc, reverse, extra):
    nb, seq, _ = u3.shape
    nch = seq // S5_TC
    final = extra is not None
    chunk = (lambda i: (0, nch - 1 - i, 0)) if reverse else (lambda i: (0, i, 0))
    const2 = lambda i: (0, 0)
    in_specs = [
        pl.BlockSpec((nb, S5_TC, GROUP_W), chunk),
        pl.BlockSpec((nb, S5_STATE), const2),
        pl.BlockSpec((nb, S5_STATE), const2),
        pl.BlockSpec((1, S5_STATE), const2),
        pl.BlockSpec((1, S5_STATE), const2),
        pl.BlockSpec((GROUP_W, 2 * S5_STATE), const2),
        pl.BlockSpec((2 * S5_STATE, GROUP_W), const2),
    ]
    args = [u3, hre0, him0, are, aim, wb, wc]
    if final:
        yin, dvec, gw, gb = extra
        in_specs += [pl.BlockSpec((nb, S5_TC, GROUP_W), chunk), pl.BlockSpec((1, GROUP_W), const2),
                     pl.BlockSpec((GROUP_W, GROUP_W), const2), pl.BlockSpec((1, GROUP_W), const2)]
        args += [yin, dvec, gw, gb]
    return pl.pallas_call(
        functools.partial(_s5_kernel, nb=nb, reverse=reverse, final=final),
        grid=(nch,),
        in_specs=in_specs,
        out_specs=[pl.BlockSpec((nb, S5_TC, GROUP_W), chunk),
                   pl.BlockSpec((nb, S5_STATE), const2), pl.BlockSpec((nb, S5_STATE), const2)],
        out_shape=[jax.ShapeDtypeStruct((nb, seq, GROUP_W), BF16 if final else F32),
                   jax.ShapeDtypeStruct((nb, S5_STATE), F32), jax.ShapeDtypeStruct((nb, S5_STATE), F32)],
        scratch_shapes=[
            pltpu.VMEM((GROUP_W // LANES, nb * S5_TC, LANES), F32),
            pltpu.VMEM((nb * S5_TC, 2 * S5_STATE), F32),
            pltpu.VMEM((GROUP_W // LANES, nb * S5_TC, LANES), F32),
            pltpu.VMEM((nb, S5_STATE), F32),
            pltpu.VMEM((nb, S5_STATE), F32),
        ],
        name="s5_bwd" if reverse else "s5_fwd",
    )(*args)


def _s5_params(a_re, a_im, log_dt, b_re, b_im, c_re, c_im):
    eye = jnp.eye(S5_GROUPS, dtype=F32)
    dt = jnp.exp(log_dt)[..., None]
    mag = jnp.exp(a_re * dt)
    ab_re, ab_im = mag * jnp.cos(a_im * dt), mag * jnp.sin(a_im * dt)
    den = a_re * a_re + a_im * a_im
    g_re = ((ab_re - 1.0) * a_re + ab_im * a_im) / den
    g_im = (ab_im * a_re - (ab_re - 1.0) * a_im) / den
    bb_re = g_re[..., None] * b_re - g_im[..., None] * b_im
    bb_im = g_re[..., None] * b_im + g_im[..., None] * b_re

    def in_mat(bb):
        return jnp.einsum('dgpc,gh->dgchp', bb, eye).reshape(2, GROUP_W, S5_STATE)

    wb = jnp.concatenate([in_mat(bb_re), in_mat(bb_im)], axis=-1).astype(BF16)

    def out_mat(cc):
        return jnp.einsum('gcp,gh->gphc', cc, eye).reshape(S5_STATE, GROUP_W)

    wc = jnp.concatenate([out_mat(c_re), -out_mat(c_im)], axis=0).astype(BF16)
    return ab_re.reshape(2, 1, S5_STATE), ab_im.reshape(2, 1, S5_STATE), wb, wc


def _s5(u3, h0_re, h0_im, prm, d_vec, glu_w, glu_b):
    ab_re, ab_im, wb, wc = prm
    yf, fre, fim = _s5_pass(u3, h0_re[:, 0], h0_im[:, 0], ab_re[0], ab_im[0], wb[0], wc, False, None)
    out, bre, bim = _s5_pass(u3, h0_re[:, 1], h0_im[:, 1], ab_re[1], ab_im[1], wb[1], wc, True,
                             (yf, d_vec, glu_w, glu_b))
    return out, jnp.stack([fre, bre], axis=1), jnp.stack([fim, bim], axis=1)


@functools.lru_cache(maxsize=None)
def _dft_consts(n):
    idx = np.arange(n, dtype=np.int64)
    ang = (np.pi / n) * ((idx[:, None] * idx[None, :]) % (2 * n)).astype(np.float64)
    cos, sin = np.cos(ang), np.sin(ang)
    sign = np.where(idx % 2 == 0, 1.0, -1.0)
    sin_n = sin.copy()
    sin_n[0, :] = sign
    fwd = np.concatenate([cos, sin_n], axis=0)
    inv_c = cos.T / n
    inv_c[:, 0] = 1.0 / (2 * n)
    inv_s = sin.T / n
    inv_s[:, 0] = sign / (2 * n)
    inv = np.concatenate([inv_c, inv_s], axis=1)
    return fwd.astype(np.float32), inv.astype(np.float32), sign.astype(np.float32)[:, None]


@functools.lru_cache(maxsize=None)
def _hyena_feats(n):
    t = np.linspace(0.0, 1.0, n, dtype=np.float64)[:, None]
    w = (2.0 * np.pi / n) * np.arange(n, dtype=np.float64)[:, None]
    bands = np.linspace(1e-4, HY_BANDS - 1, HY_BANDS, dtype=np.float64)[None, :]
    feats = np.concatenate([t, np.cos(w * bands), -np.sin(w * bands)], axis=-1)
    pad = np.zeros((n, LANES - HY_EMB))
    return np.concatenate([feats, pad], axis=-1).astype(np.float32), t.astype(np.float32)


def _hyfilt_kernel(feat_ref, t_ref, sign_ref, w1_ref, b1_ref, w2_ref, b2_ref, w3_ref, fr_ref, dec_ref,
                   fh_ref, fl_ref, ka_ref, kb_ref, ka2_ref, *, n):
    freq = fr_ref[...]
    hdn = jnp.sin(freq * (_dot3(feat_ref[...], w1_ref[...]) + b1_ref[...]))
    hdn = jnp.sin(freq * (_dot3(hdn, w2_ref[...]) + b2_ref[...]))
    h = _dot3(hdn, w3_ref[...]) * (jnp.exp(-t_ref[...] * jnp.abs(dec_ref[...])) + HY_SHIFT)
    half = HY_ORDER * GROUP_W
    row = lax.broadcasted_iota(jnp.int32, (n, 1), 0)
    h0 = h[:, 0:half]
    h1 = jnp.where(row == 0, 0.0, h[:, half:2 * half])
    norm = jnp.sum(jnp.abs(h0), axis=0, keepdims=True) + jnp.sum(jnp.abs(h1), axis=0, keepdims=True)
    hp = (h0 + h1) / norm
    hm = (h0 - h1) / norm

    def dft(lo, x):
        xh = x.astype(BF16)
        xl = (x - xh.astype(F32)).astype(BF16)
        fh = fh_ref[lo:lo + n, :]
        return _dot(fh, xh) + _dot(fl_ref[lo:lo + n, :], xh) + _dot(fh, xl)

    kc = dft(0, hp)
    ks = dft(n, hm)
    kn = jnp.sum(hp * sign_ref[...], axis=0, keepdims=True)
    ka_ref[...] = kc
    kb_ref[...] = jnp.where(row == 0, 0.0, ks)
    ka2_ref[...] = jnp.where(row == 0, kn, kc)


def _hyena_filter(n, w1p, b1, w2, b2, w3, freq, decay):
    fwd, _, sign = _dft_consts(n)
    feats, t = _hyena_feats(n)
    fwd = jnp.asarray(fwd)
    fh = fwd.astype(BF16)
    fl = (fwd - fh.astype(F32)).astype(BF16)
    half = HY_ORDER * GROUP_W
    return pl.pallas_call(
        functools.partial(_hyfilt_kernel, n=n),
        out_shape=[jax.ShapeDtypeStruct((n, half), F32)] * 3,
        name="hyena_filter",
    )(jnp.asarray(feats), jnp.asarray(t), jnp.asarray(sign), w1p, b1, w2, b2, w3, freq, decay, fh, fl)


def _hyena_kernel(p_ref, cw_ref, ka_ref, kb_ref, ka2_ref, bias_ref, f_ref, g_ref, out_ref,
                  z_scr, x_scr, zf_scr, *, seq, conv_w):
    nch = seq // ML_CHUNK
    rowid = lax.broadcasted_iota(jnp.int32, (ML_CHUNK, 1), 0)

    def conv_chunk(ci, carry):
        r0 = pl.multiple_of(ci * ML_CHUNK, ML_CHUNK)
        zc = p_ref[pl.ds(r0, ML_CHUNK), :]
        zp = pltpu.roll(zc, 1, axis=0)
        zn = pltpu.roll(zc, ML_CHUNK - 1, axis=0)
        if conv_w == ML_CHUNK:
            prev_row = jnp.zeros((1, 3 * GROUP_W), F32)
            next_row = prev_row
        else:
            prev_row = p_ref[pl.ds(jnp.maximum(r0 - 1, 0), 1), :]
            prev_row = jnp.where(r0 % conv_w == 0, 0.0, prev_row)
            next_row = p_ref[pl.ds(jnp.minimum(r0 + ML_CHUNK, seq - 1), 1), :]
            next_row = jnp.where((r0 + ML_CHUNK) % conv_w == 0, 0.0, next_row)
        zp = jnp.where(rowid == 0, prev_row, zp)
        zn = jnp.where(rowid == ML_CHUNK - 1, next_row, zn)
        z_scr[pl.ds(r0, ML_CHUNK), :] = cw_ref[0:1, :] * zp + cw_ref[1:2, :] * zc + cw_ref[2:3, :] * zn
        return carry

    lax.fori_loop(0, nch, conv_chunk, 0)
    y = z_scr[:, 0:GROUP_W]
    for o in range(HY_ORDER):
        cols = slice(o * GROUP_W, (o + 1) * GROUP_W)
        x_scr[...] = _dot(f_ref[...], y.astype(BF16))
        xc = x_scr[0:seq, :]
        xs = x_scr[seq:2 * seq, :]
        kb = kb_ref[:, cols]
        zf_scr[0:seq, :] = (xc * ka_ref[:, cols] - xs * kb).astype(BF16)
        zf_scr[seq:2 * seq, :] = (xc * kb + xs * ka2_ref[:, cols]).astype(BF16)
        conv = _dot(g_ref[...], zf_scr[...])
        gate = z_scr[:, (o + 1) * GROUP_W:(o + 2) * GROUP_W]
        y = gate * (conv + bias_ref[o:o + 1, :] * y)
    out_ref[...] = y.astype(out_ref.dtype)


def _hyena(p3, conv_w3, filt, bias, conv_width):
    b, seq, _ = p3.shape
    ka, kb, ka2 = filt
    fwd, inv, _ = _dft_consts(seq)
    const2 = lambda i: (0, 0)
    half = HY_ORDER * GROUP_W
    return pl.pallas_call(
        functools.partial(_hyena_kernel, seq=seq, conv_w=conv_width),
        grid=(b,),
        in_specs=[
            pl.BlockSpec((None, seq, 3 * GROUP_W), lambda i: (i, 0, 0)),
            pl.BlockSpec((3, 3 * GROUP_W), const2),
            pl.BlockSpec((seq, half), const2),
            pl.BlockSpec((seq, half), const2),
            pl.BlockSpec((seq, half), const2),
            pl.BlockSpec((HY_ORDER, GROUP_W), const2),
            pl.BlockSpec((2 * seq, seq), const2),
            pl.BlockSpec((seq, 2 * seq), const2),
        ],
        out_specs=pl.BlockSpec((None, seq, GROUP_W), lambda i: (i, 0, 0)),
        out_shape=jax.ShapeDtypeStruct((b, seq, GROUP_W), BF16),
        scratch_shapes=[
            pltpu.VMEM((seq, 3 * GROUP_W), F32),
            pltpu.VMEM((2 * seq, GROUP_W), F32),
            pltpu.VMEM((2 * seq, GROUP_W), BF16),
        ],
        name="hyena",
    )(p3, conv_w3, ka, kb, ka2, bias, jnp.asarray(fwd, BF16), jnp.asarray(inv, BF16))


def _outproj_kernel(a_ref, b_ref, c_ref, d_ref, x_ref, g1_ref, sh2_ref, sc2_ref, n2_ref, wo_ref,
                    wr_ref, br_ref, x1_ref, h2_ref, gate_ref):
    acc = _dot(a_ref[...], wo_ref[0:GROUP_W, :])
    acc += _dot(b_ref[...], wo_ref[GROUP_W:2 * GROUP_W, :])
    acc += _dot(c_ref[...], wo_ref[2 * GROUP_W:3 * GROUP_W, :])
    acc += _dot(d_ref[...], wo_ref[3 * GROUP_W:4 * GROUP_W, :])
    x1 = x_ref[...] + g1_ref[0] * acc
    x1_ref[...] = x1
    h2 = _rmsnorm_rows(x1, n2_ref[...]) * (1.0 + sc2_ref[0]) + sh2_ref[0]
    h2_ref[...] = h2.astype(BF16)
    logits = _dot3(h2, wr_ref[...]) + br_ref[...]
    lane = lax.broadcasted_iota(jnp.int32, logits.shape, 1).astype(F32)
    big = float(LANES)
    gl = jnp.where(lane < N_EXPERT_GROUPS, logits, NEG_INF)
    gmax = jnp.max(gl, axis=1, keepdims=True)
    gsel = jnp.min(jnp.where(gl == gmax, lane, big), axis=1, keepdims=True)
    psel = 1.0 / jnp.sum(jnp.exp(gl - gmax), axis=1, keepdims=True)
    lo = N_EXPERT_GROUPS + EXPERTS_PER_GROUP * gsel
    el = jnp.where((lane >= lo) & (lane < lo + EXPERTS_PER_GROUP), logits, NEG_INF)
    v1 = jnp.max(el, axis=1, keepdims=True)
    i1 = jnp.min(jnp.where(el == v1, lane, big), axis=1, keepdims=True)
    el2 = jnp.where(lane == i1, NEG_INF, el)
    v2 = jnp.max(el2, axis=1, keepdims=True)
    i2 = jnp.min(jnp.where(el2 == v2, lane, big), axis=1, keepdims=True)
    e2 = jnp.exp(v2 - v1)
    w1 = psel / (1.0 + e2)
    w2 = psel * e2 / (1.0 + e2)
    gate_ref[...] = (jnp.where(lane == 0.0, gsel, 0.0) + jnp.where(lane == i1 - lo + 1.0, w1, 0.0)
                     + jnp.where(lane == i2 - lo + 1.0, w2, 0.0))


def _outproj(mix, x2d, mod3, row_of_tile, norm_g, w_out, w_r, b_r):
    t = x2d.shape[0]
    tm = TOK_TILE
    const = lambda i: (0, 0)
    tile = lambda i: (i, 0)
    return pl.pallas_call(
        _outproj_kernel,
        grid=(t // tm,),
        in_specs=[pl.BlockSpec((tm, GROUP_W), tile)] * 4 + [
            pl.BlockSpec((tm, D_MODEL), tile),
            _mod_spec(2, row_of_tile),
            _mod_spec(3, row_of_tile),
            _mod_spec(4, row_of_tile),
            pl.BlockSpec((1, D_MODEL), const),
            pl.BlockSpec((D_MODEL, D_MODEL), const),
            pl.BlockSpec((D_MODEL, LANES), const),
            pl.BlockSpec((1, LANES), const),
        ],
        out_specs=[pl.BlockSpec((tm, D_MODEL), tile), pl.BlockSpec((tm, D_MODEL), tile),
                   pl.BlockSpec((tm, LANES), tile)],
        out_shape=[jax.ShapeDtypeStruct((t, D_MODEL), F32), jax.ShapeDtypeStruct((t, D_MODEL), BF16),
                   jax.ShapeDtypeStruct((t, LANES), F32)],
        name="outproj_router",
    )(*mix, x2d, mod3, mod3, mod3, norm_g, w_out, w_r, b_r)


def _tile_positions(gates, tri_ref, extra):
    lane = lax.broadcasted_iota(jnp.int32, gates.shape, 1).astype(F32)
    gsel = gates[:, 0:1]
    member = jnp.where((lane == gsel) & (lane < N_EXPERT_GROUPS), 1.0, 0.0)
    before = _dot(tri_ref[...], member.astype(BF16))
    pos = gsel * MOE_TOK + jnp.sum(member * (before + extra), axis=1, keepdims=True)
    slot = lax.broadcasted_iota(jnp.int32, (MOE_TOK, N_EXPERT_GROUPS * MOE_TOK), 1).astype(F32)
    return jnp.where(slot == pos, 1.0, 0.0).astype(BF16)


def _moe_pack_kernel(off_ref, tot_ref, h_ref, gate_ref, tri_ref, xg_ref, gg_ref,
                     xs_scr, gs_scr, zx_scr, zg_scr, sem):
    i = pl.program_id(0)
    gates = gate_ref[...]
    pt = _tile_positions(gates, tri_ref, 0.0)
    xs_scr[...] = _dot_tn(pt, h_ref[...]).astype(BF16)
    hi, mid, lo = _split3(gates)
    gs_scr[...] = _dot_tn(pt, hi) + _dot_tn(pt, mid) + _dot_tn(pt, lo)

    def segment_copies(g):
        off = pl.multiple_of(off_ref[i * N_EXPERT_GROUPS + g], ROW_ALIGN)
        seg = pl.ds(g * MOE_TOK, MOE_TOK)
        return (pltpu.make_async_copy(xs_scr.at[seg], xg_ref.at[g, pl.ds(off, MOE_TOK)], sem.at[0, g]),
                pltpu.make_async_copy(gs_scr.at[seg], gg_ref.at[g, pl.ds(off, MOE_TOK)], sem.at[1, g]))

    for g in range(N_EXPERT_GROUPS):
        for cp in segment_copies(g):
            cp.start()
    for g in range(N_EXPERT_GROUPS):
        for cp in segment_copies(g):
            cp.wait()

    @pl.when(i == pl.num_programs(0) - 1)
    def _():
        zx_scr[...] = jnp.zeros_like(zx_scr)
        zg_scr[...] = jnp.zeros_like(zg_scr)

        def tail_copies(g):
            tot = pl.multiple_of(tot_ref[g], ROW_ALIGN)
            return (pltpu.make_async_copy(zx_scr, xg_ref.at[g, pl.ds(tot, MOE_ROWS)], sem.at[0, g]),
                    pltpu.make_async_copy(zg_scr, gg_ref.at[g, pl.ds(tot, MOE_ROWS)], sem.at[1, g]))

        for g in range(N_EXPERT_GROUPS):
            for cp in tail_copies(g):
                cp.start()
        for g in range(N_EXPERT_GROUPS):
            for cp in tail_copies(g):
                cp.wait()


def _moe_expert_kernel(grp_ref, chk_ref, nact_ref, x_ref, g_ref, wg_ref, wu_ref, wd_ref, y_ref):
    @pl.when(pl.program_id(0) < nact_ref[0])
    def _():
        x = x_ref[...]
        gates = g_ref[...]
        acc = None
        for e in range(EXPERTS_PER_GROUP):
            a = _dot(x, wg_ref[e])
            u = _dot(x, wu_ref[e])
            hid = (_silu(a) * u * gates[:, 1 + e:2 + e]).astype(BF16)
            part = _dot(hid, wd_ref[e])
            acc = part if acc is None else acc + part
        y_ref[...] = acc.astype(y_ref.dtype)


def _moe_unpack_kernel(st_ref, dl_ref, gate_ref, tri_ref, x1_ref, g2_ref, fn_ref, yg_ref, out_ref,
                       ys_scr, sem, *, final):
    i = pl.program_id(0)

    def window_copy(g):
        start = pl.multiple_of(st_ref[i * N_EXPERT_GROUPS + g], ROW_ALIGN)
        return pltpu.make_async_copy(yg_ref.at[g, pl.ds(start, MOE_TOK)],
                                     ys_scr.at[pl.ds(g * MOE_TOK, MOE_TOK)], sem.at[g])

    for g in range(N_EXPERT_GROUPS):
        window_copy(g).start()
    gates = gate_ref[...]
    lane = lax.broadcasted_iota(jnp.int32, (1, LANES), 1)
    delta = jnp.zeros((1, LANES), F32)
    for g in range(N_EXPERT_GROUPS):
        delta = jnp.where(lane == g, dl_ref[i * N_EXPERT_GROUPS + g].astype(F32), delta)
    pt = _tile_positions(gates, tri_ref, delta)
    for g in range(N_EXPERT_GROUPS):
        window_copy(g).wait()
    x2 = x1_ref[...] + g2_ref[0] * _dot(pt, ys_scr[...])
    if final:
        x2 = _rmsnorm_rows(x2, fn_ref[...])
    out_ref[...] = x2


def _moe_schedule(gsel, n_tiles, n_steps):
    ng = N_EXPERT_GROUPS
    member = gsel.reshape(n_tiles, MOE_TOK, 1) == jnp.arange(ng, dtype=jnp.int32)
    cnt = jnp.sum(member.astype(jnp.int32), axis=1)
    padded = (cnt + ROW_ALIGN - 1) // ROW_ALIGN * ROW_ALIGN
    off = jnp.cumsum(padded, axis=0) - padded
    tot = jnp.sum(padded, axis=0)
    n_chunks = jnp.maximum((tot + MOE_ROWS - 1) // MOE_ROWS, 1)
    start = jnp.minimum(off, n_chunks * MOE_ROWS - MOE_TOK)
    ends = jnp.cumsum(n_chunks)
    n_active = ends[-1]
    step = jnp.minimum(jnp.arange(n_steps, dtype=jnp.int32), n_active - 1)
    grp = jnp.sum((step[:, None] >= ends[None, :]).astype(jnp.int32), axis=1)
    chk = step - (ends - n_chunks)[grp]
    i32 = lambda a: a.astype(jnp.int32)
    return (i32(off).reshape(-1), i32(tot), i32(start).reshape(-1), i32(off - start).reshape(-1),
            i32(grp), i32(chk), i32(n_active).reshape(1))


def _moe(h2, gates, w_gate, w_up, w_down, x1, mod3, row_of_tile, final_g, final):
    t = h2.shape[0]
    ng = N_EXPERT_GROUPS
    n_tiles = t // MOE_TOK
    cap = -(-(t + ROW_ALIGN * n_tiles + MOE_ROWS) // MOE_ROWS) * MOE_ROWS
    n_steps = (t + ROW_ALIGN * n_tiles) // MOE_ROWS + ng
    off, tot, start, delta, grp, chk, n_active = _moe_schedule(gates[:, 0].astype(jnp.int32), n_tiles, n_steps)
    r = np.arange(MOE_TOK)
    tri = jnp.asarray((r[:, None] > r[None, :]).astype(np.float32), BF16)
    any_spec = pl.BlockSpec(memory_space=pl.ANY)

    xg, gg = pl.pallas_call(
        _moe_pack_kernel,
        grid_spec=pltpu.PrefetchScalarGridSpec(
            num_scalar_prefetch=2,
            grid=(n_tiles,),
            in_specs=[
                pl.BlockSpec((MOE_TOK, D_MODEL), lambda i, *_: (i, 0)),
                pl.BlockSpec((MOE_TOK, LANES), lambda i, *_: (i, 0)),
                pl.BlockSpec((MOE_TOK, MOE_TOK), lambda i, *_: (0, 0)),
            ],
            out_specs=[any_spec, any_spec],
            scratch_shapes=[
                pltpu.VMEM((ng * MOE_TOK, D_MODEL), BF16),
                pltpu.VMEM((ng * MOE_TOK, LANES), F32),
                pltpu.VMEM((MOE_ROWS, D_MODEL), BF16),
                pltpu.VMEM((MOE_ROWS, LANES), F32),
                pltpu.SemaphoreType.DMA((2, ng)),
            ],
        ),
        out_shape=[jax.ShapeDtypeStruct((ng, cap, D_MODEL), BF16), jax.ShapeDtypeStruct((ng, cap, LANES), F32)],
        name="moe_pack",
    )(off, tot, h2, gates, tri)

    group_w = lambda shape: pl.BlockSpec((None, EXPERTS_PER_GROUP) + shape, lambda k, grp, chk, na: (grp[k], 0, 0, 0))
    rows = lambda width: pl.BlockSpec((None, MOE_ROWS, width), lambda k, grp, chk, na: (grp[k], chk[k], 0))
    yg = pl.pallas_call(
        _moe_expert_kernel,
        grid_spec=pltpu.PrefetchScalarGridSpec(
            num_scalar_prefetch=3,
            grid=(n_steps,),
            in_specs=[rows(D_MODEL), rows(LANES), group_w((D_MODEL, D_EXPERT)), group_w((D_MODEL, D_EXPERT)),
                      group_w((D_EXPERT, D_MODEL))],
            out_specs=rows(D_MODEL),
        ),
        out_shape=jax.ShapeDtypeStruct((ng, cap, D_MODEL), BF16),
        name="moe_experts",
    )(grp, chk, n_active, xg, gg,
      w_gate.reshape(ng, EXPERTS_PER_GROUP, D_MODEL, D_EXPERT), w_up.reshape(ng, EXPERTS_PER_GROUP, D_MODEL, D_EXPERT),
      w_down.reshape(ng, EXPERTS_PER_GROUP, D_EXPERT, D_MODEL))

    return pl.pallas_call(
        functools.partial(_moe_unpack_kernel, final=final),
        grid_spec=pltpu.PrefetchScalarGridSpec(
            num_scalar_prefetch=2,
            grid=(n_tiles,),
            in_specs=[
                pl.BlockSpec((MOE_TOK, LANES), lambda i, *_: (i, 0)),
                pl.BlockSpec((MOE_TOK, MOE_TOK), lambda i, *_: (0, 0)),
                pl.BlockSpec((MOE_TOK, D_MODEL), lambda i, *_: (i, 0)),
                pl.BlockSpec((1, 1, D_MODEL), lambda i, *_: (row_of_tile(i), 0, 5)),
                pl.BlockSpec((1, D_MODEL), lambda i, *_: (0, 0)),
                any_spec,
            ],
            out_specs=pl.BlockSpec((MOE_TOK, D_MODEL), lambda i, *_: (i, 0)),
            scratch_shapes=[pltpu.VMEM((ng * MOE_TOK, D_MODEL), BF16), pltpu.SemaphoreType.DMA((ng,))],
        ),
        out_shape=jax.ShapeDtypeStruct((t, D_MODEL), F32),
        name="moe_unpack",
    )(start, delta, gates, tri, x1, mod3, final_g, yg)


def _block_diag_t(s):
    eye = jnp.eye(HEADS, dtype=s.dtype)
    b = s.shape[0]
    return jnp.einsum('bdhkv,hg->bdhvgk', s, eye).reshape(b, 2, GROUP_W, GROUP_W)


def _block_diag_t_inv(st):
    b = st.shape[0]
    s6 = st.reshape(b, 2, HEADS, HEAD_D, HEADS, HEAD_D)
    idx = jnp.arange(HEADS)
    diag = s6[:, :, idx, :, idx, :]
    return jnp.transpose(diag, (1, 2, 0, 4, 3))


def _prep_layer(l, w):
    gw = GROUP_W
    w_in = w['w_in'][l]
    w_g = w_in[:, 9 * gw:9 * gw + N_GATES]
    lb_all = jnp.cumsum(jax.nn.softmax(w['hg_lb'].astype(F32), axis=0), axis=0)
    lb = lb_all[l] - lb_all[0]
    lb_rows = jnp.zeros((8, gw), F32).at[0].set(jnp.log(lb)).at[1].set(jnp.log1p(-lb)).at[2].set(1.0 - lb)
    w_r = jnp.zeros((D_MODEL, LANES), F32)
    w_r = w_r.at[:, 0:N_EXPERT_GROUPS].set(w['moe_wg'][l])
    w_r = w_r.at[:, N_EXPERT_GROUPS:N_EXPERT_GROUPS + N_EXPERTS].set(w['moe_we'][l])
    b_r = jnp.zeros((1, LANES), F32)
    b_r = b_r.at[0, 0:N_EXPERT_GROUPS].set(w['moe_bg'][l])
    b_r = b_r.at[0, N_EXPERT_GROUPS:N_EXPERT_GROUPS + N_EXPERTS].set(w['moe_be'][l])
    gate_bias = jnp.concatenate([w['ml_i_bias'][l], w['ml_f_bias'][l]])
    w1p = jnp.zeros((LANES, HY_HIDDEN), F32).at[0:HY_EMB].set(w['hy_w1'][l])
    return dict(
        norm1=w['norm1'][l][None, :], norm2=w['norm2'][l][None, :],
        w_a=w_in[:, :9 * gw].astype(BF16), w_b=w_in[:, 9 * gw + N_GATES:].astype(BF16),
        w_g=w_g.astype(BF16), w_gt=w_g.T.astype(BF16),
        w_out=w['w_out'][l].astype(BF16),
        lb_rows=lb_rows, hg_norm=w['hg_norm'][l][None, :],
        ml_conv=w['ml_conv'][l], ml_brow=gate_bias[None, :],
        ml_brows=jnp.repeat(gate_bias.reshape(4, HEADS), HEAD_D, axis=1),
        ml_norm=w['ml_norm'][l][None, :],
        s5=_s5_params(w['s5_a_re'][l], w['s5_a_im'][l], w['s5_log_dt'][l], w['s5_b_re'][l], w['s5_b_im'][l],
                      w['s5_c_re'][l], w['s5_c_im'][l]),
        s5_d=w['s5_d'][l][None, :], s5_glu_w=w['s5_glu_w'][l].astype(BF16), s5_glu_b=w['s5_glu_b'][l][None, :],
        hy_conv=w['hy_conv'][l], hy_bias=w['hy_bias'][l],
        hy_mlp=(w1p, w['hy_b1'][l][None, :], w['hy_w2'][l], w['hy_b2'][l][None, :], w['hy_w3'][l],
                w['hy_freq'][l][None, :], w['hy_decay'][l][None, :]),
        w_r=w_r, b_r=b_r,
        moe_gate=w['moe_w_gate'][l].astype(BF16), moe_up=w['moe_w_up'][l].astype(BF16),
        moe_down=w['moe_w_down'][l].astype(BF16),
    )


def _trunk_layer(x2d, nb, seq, mod3, cond_row, st, lw, filt, conv_width, emit_state, final_g, final):
    if cond_row is None:
        row_of_tile = lambda i: i // (seq // TOK_TILE)
    else:
        row_of_tile = lambda i: cond_row
    hg, ml, su, hy, gt, gtt = _inproj(x2d, mod3, row_of_tile, lw['norm1'], lw['w_a'], lw['w_b'], lw['w_g'],
                                       lw['w_gt'])
    nch = seq // ML_CHUNK
    a_out, hg_st = _hgrn(hg.reshape(nb, seq, -1), st['hgrn'], lw['lb_rows'], lw['hg_norm'], emit_state)
    gr4 = jnp.transpose(gtt.reshape(4, HEADS, nb, nch, ML_CHUNK), (2, 0, 3, 1, 4)).reshape(nb, 4, nch, GROUP_W)
    b_out, mlc, mln, mlm = _mlstm(ml.reshape(nb, seq, -1), gt.reshape(nb, seq, N_GATES), gr4, lw['ml_conv'],
                                  lw['ml_brow'], lw['ml_brows'], lw['ml_norm'], st['ml_c'], st['ml_n'],
                                  st['ml_m'], conv_width, emit_state)
    c_out, s5re, s5im = _s5(su.reshape(nb, seq, -1), st['s5_re'], st['s5_im'], lw['s5'], lw['s5_d'],
                            lw['s5_glu_w'], lw['s5_glu_b'])
    d_out = _hyena(hy.reshape(nb, seq, -1), lw['hy_conv'], filt, lw['hy_bias'], conv_width)
    mix = [m.reshape(nb * seq, GROUP_W) for m in (a_out, b_out, c_out, d_out)]
    x1, h2, gates = _outproj(mix, x2d, mod3, row_of_tile, lw['norm2'], lw['w_out'], lw['w_r'], lw['b_r'])
    x2 = _moe(h2, gates, lw['moe_gate'], lw['moe_up'], lw['moe_down'], x1, mod3, row_of_tile,
              final_g, final)
    new_st = dict(hgrn=hg_st, ml_c=mlc, ml_n=mln, ml_m=mlm, s5_re=s5re, s5_im=s5im)
    return x2, new_st


def kernel(x_prompt, x_sample, c, state_hgrn, state_mlstm_c, state_mlstm_n, state_mlstm_m, state_s5_re, state_s5_im, c_ctx, w_ada, b_ada, norm1, norm2, w_in, w_out, hg_lb, hg_norm, ml_conv, ml_i_bias, ml_f_bias, ml_norm, s5_a_re, s5_a_im, s5_log_dt, s5_b_re, s5_b_im, s5_c_re, s5_c_im, s5_d, s5_glu_w, s5_glu_b, hy_conv, hy_w1, hy_b1, hy_w2, hy_b2, hy_w3, hy_freq, hy_decay, hy_bias, moe_wg, moe_bg, moe_we, moe_be, moe_w_gate, moe_w_up, moe_w_down, final_norm):
    w = dict(w_in=w_in, w_out=w_out, norm1=norm1, norm2=norm2, hg_lb=hg_lb, hg_norm=hg_norm, ml_conv=ml_conv,
             ml_i_bias=ml_i_bias, ml_f_bias=ml_f_bias, ml_norm=ml_norm, s5_a_re=s5_a_re, s5_a_im=s5_a_im,
             s5_log_dt=s5_log_dt, s5_b_re=s5_b_re, s5_b_im=s5_b_im, s5_c_re=s5_c_re, s5_c_im=s5_c_im,
             s5_d=s5_d, s5_glu_w=s5_glu_w, s5_glu_b=s5_glu_b, hy_conv=hy_conv, hy_w1=hy_w1, hy_b1=hy_b1,
             hy_w2=hy_w2, hy_b2=hy_b2, hy_w3=hy_w3, hy_freq=hy_freq, hy_decay=hy_decay, hy_bias=hy_bias,
             moe_wg=moe_wg, moe_bg=moe_bg, moe_we=moe_we, moe_be=moe_be, moe_w_gate=moe_w_gate,
             moe_w_up=moe_w_up, moe_w_down=moe_w_down)
    bp, lp, _ = x_prompt.shape
    bs, ls, _ = x_sample.shape
    ctx_row = bs
    cond = jnp.zeros((MOD_ROWS, D_MODEL), F32).at[0:bs].set(c).at[ctx_row].set(c_ctx)
    mod = _ada_mod(cond, w_ada, b_ada)
    final_g = final_norm[None, :]

    zero_st = dict(
        hgrn=jnp.zeros((bp, 2, GROUP_W, GROUP_W), F32),
        ml_c=jnp.zeros((bp, 2, HEADS, HEAD_D, HEAD_D), F32),
        ml_n=jnp.zeros((bp, 2, 1, GROUP_W), F32),
        ml_m=jnp.zeros((bp, 2, 1, GROUP_W), F32),
        s5_re=jnp.zeros((bp, 2, S5_STATE), F32),
        s5_im=jnp.zeros((bp, 2, S5_STATE), F32))

    xp = x_prompt.reshape(bp * lp, D_MODEL)
    xs = x_sample.reshape(bs * ls, D_MODEL)
    per_layer = []
    for l in range(DEPTH):
        lw = _prep_layer(l, w)
        mod3 = mod[l].reshape(MOD_ROWS, 1, 6 * D_MODEL)
        last = l == DEPTH - 1
        filt_p = _hyena_filter(lp, *lw['hy_mlp'])
        filt_s = filt_p if ls == lp else _hyena_filter(ls, *lw['hy_mlp'])
        xp, st = _trunk_layer(xp, bp, lp, mod3, ctx_row, zero_st, lw, filt_p, lp, True, final_g, last)
        per_layer.append(st)
        st_in = dict(
            hgrn=_block_diag_t(state_hgrn[:, l].astype(F32)),
            ml_c=state_mlstm_c[:, l].astype(F32),
            ml_n=state_mlstm_n[:, l].astype(F32).reshape(bs, 2, 1, GROUP_W),
            ml_m=jnp.repeat(state_mlstm_m[:, l].astype(F32), HEAD_D, axis=-1).reshape(bs, 2, 1, GROUP_W),
            s5_re=state_s5_re[:, l].astype(F32).reshape(bs, 2, S5_STATE),
            s5_im=state_s5_im[:, l].astype(F32).reshape(bs, 2, S5_STATE))
        xs, _ = _trunk_layer(xs, bs, ls, mod3, None, st_in, lw, filt_s, GRID_W, False, final_g, last)

    new_hgrn = jnp.stack([_block_diag_t_inv(s['hgrn']) for s in per_layer], axis=1)
    new_ml_c = jnp.stack([s['ml_c'] for s in per_layer], axis=1)
    new_ml_n = jnp.stack([s['ml_n'].reshape(bp, 2, HEADS, HEAD_D) for s in per_layer], axis=1)
    new_ml_m = jnp.stack([s['ml_m'][:, :, 0, ::HEAD_D] for s in per_layer], axis=1)
    new_s5_re = jnp.stack([s['s5_re'].reshape(bp, 2, S5_GROUPS, S5_P) for s in per_layer], axis=1)
    new_s5_im = jnp.stack([s['s5_im'].reshape(bp, 2, S5_GROUPS, S5_P) for s in per_layer], axis=1)
    return (xp.reshape(bp, lp, D_MODEL), xs.reshape(bs, ls, D_MODEL),
            new_hgrn, new_ml_c, new_ml_n, new_ml_m, new_s5_re, new_s5_im)
```

```python
import functools
import math

import numpy as np
import jax
import jax.numpy as jnp
from jax import lax
from jax.experimental import pallas as pl
from jax.experimental.pallas import tpu as pltpu

F32 = jnp.float32
BF16 = jnp.bfloat16

D_MODEL = 1024
DEPTH = 2
GRID_W = 64
GROUP_W = D_MODEL // 4
HEADS = 4
HEAD_D = GROUP_W // HEADS
S5_CH = 16
S5_GROUPS = GROUP_W // S5_CH
S5_P = 64
S5_STATE = S5_GROUPS * S5_P
HY_ORDER = 2
HY_BANDS = 16
HY_EMB = 1 + 2 * HY_BANDS
HY_HIDDEN = 64
HY_SHIFT = 0.05
N_EXPERT_GROUPS = 4
EXPERTS_PER_GROUP = 4
N_EXPERTS = N_EXPERT_GROUPS * EXPERTS_PER_GROUP
D_EXPERT = D_MODEL // 2
EPS = 1e-6
N_GATES = 4 * HEADS

LANES = 128
MOD_ROWS = 16
TOK_TILE = 256
MOE_TOK = TOK_TILE
MOE_PIECE = 64
MOE_ROWS = 512
ROW_ALIGN = 16
HG_BLK = 32
HG_SB = 128
HG_SAFE_LOG = -75.0
HG_SAFE_Q = 1e3
ML_CHUNK = 64
S5_TC = 64
NEG_INF = float("-inf")


def _dot(a, b):
    return jnp.dot(a, b, preferred_element_type=F32)


def _dot_nt(a, b):
    return lax.dot_general(a, b, (((1,), (1,)), ((), ())), preferred_element_type=F32)


def _dot_tn(a, b):
    return lax.dot_general(a, b, (((0,), (0,)), ((), ())), preferred_element_type=F32)


def _split3(x):
    hi = x.astype(BF16)
    r1 = x - hi.astype(F32)
    mid = r1.astype(BF16)
    lo = (r1 - mid.astype(F32)).astype(BF16)
    return hi, mid, lo


def _dot_sel_l(m, x):
    hi, mid, lo = _split3(x)
    return _dot(m, hi) + _dot(m, mid) + _dot(m, lo)


def _dot_sel_r(x, m):
    hi, mid, lo = _split3(x)
    return _dot(hi, m) + _dot(mid, m) + _dot(lo, m)


def _dot3(a, b):
    ah = a.astype(BF16)
    al = (a - ah.astype(F32)).astype(BF16)
    bh = b.astype(BF16)
    bl = (b - bh.astype(F32)).astype(BF16)
    return _dot(ah, bh) + _dot(al, bh) + _dot(ah, bl)


def _sigmoid(x):
    return 1.0 / (1.0 + jnp.exp(-x))


def _silu(x):
    return x * _sigmoid(x)


def _log_sigmoid(x):
    return jnp.minimum(x, 0.0) - jnp.log1p(jnp.exp(-jnp.abs(x)))


def _rmsnorm_rows(x, g):
    return x * lax.rsqrt(jnp.mean(x * x, axis=-1, keepdims=True) + EPS) * g


def _ada_kernel(c_ref, w_ref, b_ref, o_ref):
    s = _silu(c_ref[...]).astype(BF16)
    o_ref[0] = _dot(s, w_ref[0].astype(BF16)) + b_ref[0]


def _ada_mod(cond, w_ada, b_ada):
    tn = 1536
    n = w_ada.shape[-1]
    return pl.pallas_call(
        _ada_kernel,
        grid=(DEPTH, n // tn),
        in_specs=[
            pl.BlockSpec((MOD_ROWS, D_MODEL), lambda l, j: (0, 0)),
            pl.BlockSpec((1, D_MODEL, tn), lambda l, j: (l, 0, j)),
            pl.BlockSpec((1, 1, tn), lambda l, j: (l, 0, j)),
        ],
        out_specs=pl.BlockSpec((1, MOD_ROWS, tn), lambda l, j: (l, 0, j)),
        out_shape=jax.ShapeDtypeStruct((DEPTH, MOD_ROWS, n), F32),
        name="ada_mod",
    )(cond, w_ada, b_ada.reshape(DEPTH, 1, n))


def _inproj_kernel(x_ref, sh_ref, sc_ref, g_ref, wa_ref, wb_ref, wg_ref, wgt_ref,
                   hg_ref, ml_ref, su_ref, hy_ref, gt_ref, gtt_ref):
    h = _rmsnorm_rows(x_ref[...], g_ref[...])
    h = h * (1.0 + sc_ref[0]) + sh_ref[0]
    hb = h.astype(BF16)
    hg_ref[...] = _dot(hb, wa_ref[:, 0:5 * GROUP_W])
    ml_ref[...] = _dot(hb, wa_ref[:, 5 * GROUP_W:9 * GROUP_W])
    su_ref[...] = _dot(hb, wb_ref[:, 0:GROUP_W])
    hy_ref[...] = _dot(hb, wb_ref[:, GROUP_W:4 * GROUP_W])
    gt_ref[...] = _dot(hb, wg_ref[...])
    gtt_ref[...] = _dot_nt(wgt_ref[...], hb)


def _mod_spec(k, row_of_tile):
    return pl.BlockSpec((1, 1, D_MODEL), lambda i: (row_of_tile(i), 0, k))


def _inproj(x2d, mod3, row_of_tile, norm_g, w_a, w_b, w_g, w_gt):
    t = x2d.shape[0]
    tm = TOK_TILE
    const = lambda i: (0, 0)
    tile = lambda i: (i, 0)
    widths = (5 * GROUP_W, 4 * GROUP_W, GROUP_W, 3 * GROUP_W)
    return pl.pallas_call(
        _inproj_kernel,
        grid=(t // tm,),
        in_specs=[
            pl.BlockSpec((tm, D_MODEL), tile),
            _mod_spec(0, row_of_tile),
            _mod_spec(1, row_of_tile),
            pl.BlockSpec((1, D_MODEL), const),
            pl.BlockSpec(w_a.shape, const),
            pl.BlockSpec(w_b.shape, const),
            pl.BlockSpec(w_g.shape, const),
            pl.BlockSpec(w_gt.shape, const),
        ],
        out_specs=[pl.BlockSpec((tm, w), tile) for w in widths]
        + [pl.BlockSpec((tm, N_GATES), tile), pl.BlockSpec((N_GATES, tm), lambda i: (0, i))],
        out_shape=[jax.ShapeDtypeStruct((t, w), F32) for w in widths]
        + [jax.ShapeDtypeStruct((t, N_GATES), F32), jax.ShapeDtypeStruct((N_GATES, t), F32)],
        name="inproj",
    )(x2d, mod3, mod3, norm_g, w_a, w_b, w_g, w_gt)


@functools.lru_cache(maxsize=None)
def _hgrn_consts():
    r = np.arange(HG_SB)
    same = (r[:, None] // HG_BLK) == (r[None, :] // HG_BLK)
    tri_f = same & (r[:, None] >= r[None, :])
    tri_b = same & (r[:, None] <= r[None, :])
    c = np.arange(GROUP_W)
    head = (c[:, None] // HEAD_D) == (c[None, :] // HEAD_D)
    rr = np.arange(HEADS * HG_SB)
    head4 = (rr[:, None] // HG_SB) == (c[None, :] // HEAD_D)
    pairs = np.stack([np.tile(tri_f, (1, HEADS)), np.tile(tri_b, (1, HEADS))])
    return (np.stack([tri_f, tri_b]).astype(np.float32), same.astype(np.float32),
            head.astype(np.float32), head4.astype(np.float32), pairs.astype(np.float32))


def _hgrn_kernel(p_ref, s0_ref, lb_ref, ng_ref, tri_ref, bones_ref, e_ref, hmask_ref, hmask4_ref,
                 smask_ref, out_ref, *rest, seq, emit_state):
    if emit_state:
        st_ref = rest[0]
        rest = rest[1:]
    o_scr, st_scr, q_scr, k_scr, bc_scr, gq_scr, gk_scr, dec_scr, oi_scr = rest
    nsb = seq // HG_SB
    nblk = HG_SB // HG_BLK
    log_lb = lb_ref[0:1, :]
    log_1mlb = lb_ref[1:2, :]
    one_m_lb = lb_ref[2:3, :]
    row = lax.broadcasted_iota(jnp.int32, (HG_BLK, 1), 0)

    def superblock(sb, d):
        r0 = pl.multiple_of(sb * HG_SB, HG_SB)
        rows = pl.ds(r0, HG_SB)
        pq = p_ref[rows, 0:GROUP_W]
        pf = p_ref[rows, (1 + d) * GROUP_W:(2 + d) * GROUP_W]
        q = _silu(pq)
        b2 = log_1mlb + _log_sigmoid(pf)
        mx = jnp.maximum(log_lb, b2)
        mn = jnp.minimum(log_lb, b2)
        lf = mx + jnp.log1p(jnp.exp(jnp.where(mn == NEG_INF, NEG_INF, mn - mx)))
        k = one_m_lb * _sigmoid(-pf)
        bc = _dot_sel_l(tri_ref[d], lf)
        bt = _dot_sel_l(bones_ref[...], lf)
        gq = (q * jnp.exp(bc)).astype(BF16)
        gq_scr[d] = gq
        gk_scr[d] = (k * jnp.exp(bt - bc)).astype(BF16)
        dec_scr[d] = jnp.exp(bt)
        vb = p_ref[rows, 3 * GROUP_W:4 * GROUP_W].astype(BF16)

        def block_edge_path():
            ks = (k * jnp.exp(-bc)).astype(BF16)
            ksbd = jnp.concatenate([ks] * HEADS, axis=0) * hmask4_ref[...]
            vbd = jnp.concatenate([vb] * HEADS, axis=0) * hmask4_ref[...]
            s = _dot_nt(gq, ksbd) * smask_ref[d]
            oi_scr[d] = _dot(s.astype(BF16), vbd)

        def exact_path():
            q_scr[d] = q
            k_scr[d] = k
            bc_scr[d] = bc
            for j in range(nblk):
                sl = slice(HG_BLK * j, HG_BLK * (j + 1))
                qj = q_scr[d, sl, :]
                kj = k_scr[d, sl, :]
                vj = p_ref[pl.ds(r0 + HG_BLK * j, HG_BLK), 3 * GROUP_W:4 * GROUP_W]
                bcj = bc_scr[d, sl, :]
                parts = []
                for s in range(HG_BLK):
                    keep = (row >= s) if d == 0 else (row <= s)
                    dd = jnp.exp(jnp.where(keep, bcj - bcj[s:s + 1, :], NEG_INF))
                    parts.append((qj * dd * kj[s:s + 1, :]).astype(BF16))
                r_all = _dot(jnp.concatenate(parts, axis=0), e_ref[...])
                o = r_all[0:HG_BLK, :] * vj[0:1, :]
                for s in range(1, HG_BLK):
                    o = o + r_all[HG_BLK * s:HG_BLK * (s + 1), :] * vj[s:s + 1, :]
                oi_scr[d, sl, :] = o

        safe = (jnp.min(bc) >= HG_SAFE_LOG) & (jnp.max(jnp.abs(q)) <= HG_SAFE_Q)
        lax.cond(safe, block_edge_path, exact_path)

        order = range(nblk) if d == 0 else range(nblk - 1, -1, -1)
        for j in order:
            sl = slice(HG_BLK * j, HG_BLK * (j + 1))
            upd = _dot_tn(vb[sl, :], gk_scr[d, sl, :]) * hmask_ref[...]
            st = st_scr[d]
            o_scr[d, pl.ds(r0 + HG_BLK * j, HG_BLK), :] = oi_scr[d, sl, :] + _dot_nt(gq_scr[d, sl, :], st.astype(BF16))
            st_scr[d] = st * dec_scr[d, HG_BLK * j:HG_BLK * j + 1, :] + upd

    st_scr[...] = s0_ref[...]

    def body(i, carry):
        superblock(i, 0)
        superblock(nsb - 1 - i, 1)
        return carry

    lax.fori_loop(0, nsb, body, 0)

    def finish(i, carry):
        rows = pl.ds(pl.multiple_of(i * HG_SB, HG_SB), HG_SB)
        o = o_scr[0, rows, :] + o_scr[1, rows, :]
        ms = _dot_hilo(o * o, e_ref[...]) * (1.0 / HEAD_D)
        pg = p_ref[rows, 4 * GROUP_W:5 * GROUP_W]
        out_ref[rows, :] = (o * lax.rsqrt(ms + EPS) * ng_ref[...] * _silu(pg)).astype(out_ref.dtype)
        return carry

    lax.fori_loop(0, nsb, finish, 0)
    if emit_state:
        ri = lax.broadcasted_iota(jnp.int32, (GROUP_W, GROUP_W), 0)
        ci = lax.broadcasted_iota(jnp.int32, (GROUP_W, GROUP_W), 1)
        eye = jnp.where(ri == ci, 1.0, 0.0).astype(BF16)
        for d in range(2):
            hi, mid, lo = _split3(st_scr[d])
            s_t = _dot_tn(hi, eye) + _dot_tn(mid, eye) + _dot_tn(lo, eye)
            for h in range(HEADS):
                st_ref[d, h] = s_t[HEAD_D * h:HEAD_D * (h + 1), HEAD_D * h:HEAD_D * (h + 1)]


def _hgrn_kernel_old(p_ref, s0_ref, lb_ref, ng_ref, tri_ref, bones_ref, e_ref, hmask_ref,
                 out_ref, *rest, seq, emit_state):
    if emit_state:
        st_ref, of_scr, st_scr, q_scr, k_scr, bc_scr, gq_scr, gk_scr, dec_scr = rest
    else:
        of_scr, st_scr, q_scr, k_scr, bc_scr, gq_scr, gk_scr, dec_scr = rest
    nsb = seq // HG_SB
    nblk = HG_SB // HG_BLK
    log_lb = lb_ref[0:1, :]
    log_1mlb = lb_ref[1:2, :]
    one_m_lb = lb_ref[2:3, :]
    row = lax.broadcasted_iota(jnp.int32, (HG_BLK, 1), 0)

    def superblock(sb, d, finalize):
        r0 = pl.multiple_of(sb * HG_SB, HG_SB)
        rows = pl.ds(r0, HG_SB)
        pq = p_ref[rows, 0:GROUP_W]
        pf = p_ref[rows, (1 + d) * GROUP_W:(2 + d) * GROUP_W]
        q_scr[...] = _silu(pq)
        b2 = log_1mlb + _log_sigmoid(pf)
        mx = jnp.maximum(log_lb, b2)
        mn = jnp.minimum(log_lb, b2)
        delta = jnp.where(mn == NEG_INF, NEG_INF, mn - mx)
        lf = mx + jnp.log1p(jnp.exp(delta))
        k_scr[...] = one_m_lb * _sigmoid(-pf)
        bc = _dot_sel_l(tri_ref[d], lf)
        bt = _dot_sel_l(bones_ref[...], lf)
        bc_scr[...] = bc
        gq_scr[...] = (q_scr[...] * jnp.exp(bc)).astype(BF16)
        gk_scr[...] = (k_scr[...] * jnp.exp(bt - bc)).astype(BF16)
        dec_scr[...] = jnp.exp(bt)
        o_blocks = [None] * nblk
        order = range(nblk) if d == 0 else range(nblk - 1, -1, -1)
        for j in order:
            sl = slice(HG_BLK * j, HG_BLK * (j + 1))
            qj = q_scr[sl, :]
            kj = k_scr[sl, :]
            vj = p_ref[pl.ds(r0 + HG_BLK * j, HG_BLK), 3 * GROUP_W:4 * GROUP_W]
            bcj = bc_scr[sl, :]
            parts = []
            for s in range(HG_BLK):
                diff = bcj - bcj[s:s + 1, :]
                keep = (row >= s) if d == 0 else (row <= s)
                dd = jnp.exp(jnp.where(keep, diff, NEG_INF))
                parts.append((qj * dd * kj[s:s + 1, :]).astype(BF16))
            r_all = _dot(jnp.concatenate(parts, axis=0), e_ref[...])
            o = _dot_nt(gq_scr[sl, :], st_scr[...].astype(BF16))
            for s in range(HG_BLK):
                o = o + r_all[HG_BLK * s:HG_BLK * (s + 1), :] * vj[s:s + 1, :]
            upd = _dot_tn(vj.astype(BF16), gk_scr[sl, :])
            st_scr[...] = st_scr[...] * dec_scr[HG_BLK * j:HG_BLK * j + 1, :] + upd * hmask_ref[...]
            o_blocks[j] = o
        o_sb = jnp.concatenate(o_blocks, axis=0)
        if finalize:
            o_sb = o_sb + of_scr[rows, :]
            ms = _dot_sel_r(o_sb * o_sb, e_ref[...]) * (1.0 / HEAD_D)
            pg = p_ref[rows, 4 * GROUP_W:5 * GROUP_W]
            y = o_sb * lax.rsqrt(ms + EPS) * ng_ref[...] * _silu(pg)
            out_ref[rows, :] = y.astype(out_ref.dtype)
        else:
            of_scr[rows, :] = o_sb

    st_scr[...] = s0_ref[0]

    def fwd(i, carry):
        superblock(i, 0, False)
        return carry

    lax.fori_loop(0, nsb, fwd, 0)
    if emit_state:
        st_ref[0] = st_scr[...]
    st_scr[...] = s0_ref[1]

    def bwd(i, carry):
        superblock(nsb - 1 - i, 1, True)
        return carry

    lax.fori_loop(0, nsb, bwd, 0)
    if emit_state:
        st_ref[1] = st_scr[...]


def _hgrn(p3, s0_bd, lb_rows, norm_g, emit_state):
    b, seq, _ = p3.shape
    tri, bones, head, head4, pairs = _hgrn_consts()
    const2 = lambda i: (0, 0)
    const3 = lambda i: (0, 0, 0)
    out_shape = [jax.ShapeDtypeStruct((b, seq, GROUP_W), BF16)]
    out_specs = [pl.BlockSpec((None, seq, GROUP_W), lambda i: (i, 0, 0))]
    if emit_state:
        out_shape.append(jax.ShapeDtypeStruct((b, 2, HEADS, HEAD_D, HEAD_D), F32))
        out_specs.append(pl.BlockSpec((None, 2, HEADS, HEAD_D, HEAD_D), lambda i: (i, 0, 0, 0, 0)))
    sb_f32 = pltpu.VMEM((2, HG_SB, GROUP_W), F32)
    sb_bf16 = pltpu.VMEM((2, HG_SB, GROUP_W), BF16)
    res = pl.pallas_call(
        functools.partial(_hgrn_kernel, seq=seq, emit_state=emit_state),
        grid=(b,),
        in_specs=[
            pl.BlockSpec((None, seq, 5 * GROUP_W), lambda i: (i, 0, 0)),
            pl.BlockSpec((None, 2, GROUP_W, GROUP_W), lambda i: (i, 0, 0, 0)),
            pl.BlockSpec((8, GROUP_W), const2),
            pl.BlockSpec((1, GROUP_W), const2),
            pl.BlockSpec((2, HG_SB, HG_SB), const3),
            pl.BlockSpec((HG_SB, HG_SB), const2),
            pl.BlockSpec((GROUP_W, GROUP_W), const2),
            pl.BlockSpec((GROUP_W, GROUP_W), const2),
            pl.BlockSpec((HEADS * HG_SB, GROUP_W), const2),
            pl.BlockSpec((2, HG_SB, HEADS * HG_SB), const3),
        ],
        out_specs=out_specs,
        out_shape=out_shape,
        scratch_shapes=[
            pltpu.VMEM((2, seq, GROUP_W), F32),
            pltpu.VMEM((2, GROUP_W, GROUP_W), F32),
            sb_f32, sb_f32, sb_f32, sb_bf16, sb_bf16, sb_f32, sb_f32,
        ],
        name="hgrn2",
    )(p3, s0_bd, lb_rows, norm_g, jnp.asarray(tri, BF16), jnp.asarray(bones, BF16),
      jnp.asarray(head, BF16), jnp.asarray(head, F32), jnp.asarray(head4, BF16), jnp.asarray(pairs, F32))
    return res if emit_state else (res[0], None)


@functools.lru_cache(maxsize=None)
def _mlstm_consts():
    r = np.arange(ML_CHUNK)
    tri_f = (r[:, None] >= r[None, :]).astype(np.float32)
    tri = np.stack([tri_f, tri_f.T])
    c = np.arange(GROUP_W)
    head = (c[:, None] // HEAD_D) == (c[None, :] // HEAD_D)
    pos = c % HEAD_D
    row_f = head & (pos[:, None] <= pos[None, :])
    row_b = head & (pos[:, None] >= pos[None, :])
    rowtri = np.stack([row_f, row_b]).astype(np.float32)
    expand = np.zeros((2, N_GATES, 2 * GROUP_W), np.float32)
    for d in range(2):
        for h in range(HEADS):
            expand[d, d * HEADS + h, h * HEAD_D:(h + 1) * HEAD_D] = 1.0
            expand[d, 2 * HEADS + d * HEADS + h, GROUP_W + h * HEAD_D:GROUP_W + (h + 1) * HEAD_D] = 1.0
    return tri, rowtri, expand, head.astype(np.float32)


def _dot_hilo(x, m):
    hi = x.astype(BF16)
    lo = (x - hi.astype(F32)).astype(BF16)
    return _dot(hi, m) + _dot(lo, m)


def _mlstm_kernel(p_ref, g_ref, gr_ref, conv_ref, brow_ref, brows_ref, ng_ref, c0_ref, n0_ref, m0_ref,
                  tri_ref, rowtri_ref, exp_ref, e_ref, out_ref, *rest, seq, conv_w, emit_state):
    if emit_state:
        cst_ref, nst_ref, mst_ref = rest[:3]
        rest = rest[3:]
    (q_scr, k_scr, col_scr, mloc_scr, gb_scr, h_scr, row_scr, blast_scr, gmax_scr,
     c_scr, n_scr, m_scr) = rest
    nch = seq // ML_CHUNK
    rowid = lax.broadcasted_iota(jnp.int32, (ML_CHUNK, 1), 0)
    ti = lax.broadcasted_iota(jnp.int32, (ML_CHUNK, GROUP_W), 0)
    si = lax.broadcasted_iota(jnp.int32, (ML_CHUNK, GROUP_W), 1) % HEAD_D

    for d in range(2):
        li_row = gr_ref[d] + brows_ref[d:d + 1, :]
        lf_row = _log_sigmoid(gr_ref[2 + d] + brows_ref[2 + d:3 + d, :])
        row_scr[d] = li_row - _dot_sel_r(lf_row, rowtri_ref[d])

    def pre_chunk(ci, carry):
        r0 = pl.multiple_of(ci * ML_CHUNK, ML_CHUNK)
        rows = pl.ds(r0, ML_CHUNK)
        zc = p_ref[rows, 0:2 * GROUP_W]
        zp = pltpu.roll(zc, 1, axis=0)
        zn = pltpu.roll(zc, ML_CHUNK - 1, axis=0)
        if conv_w == ML_CHUNK:
            prev_row = jnp.zeros((1, 2 * GROUP_W), F32)
            next_row = prev_row
        else:
            prev_row = p_ref[pl.ds(jnp.maximum(r0 - 1, 0), 1), 0:2 * GROUP_W]
            prev_row = jnp.where(r0 % conv_w == 0, 0.0, prev_row)
            next_row = p_ref[pl.ds(jnp.minimum(r0 + ML_CHUNK, seq - 1), 1), 0:2 * GROUP_W]
            next_row = jnp.where((r0 + ML_CHUNK) % conv_w == 0, 0.0, next_row)
        zp = jnp.where(rowid == 0, prev_row, zp)
        zn = jnp.where(rowid == ML_CHUNK - 1, next_row, zn)
        qk = _silu(conv_ref[0:1, :] * zp + conv_ref[1:2, :] * zc + conv_ref[2:3, :] * zn)
        q_scr[rows, :] = qk[:, 0:GROUP_W].astype(BF16)
        k_scr[rows, :] = qk[:, GROUP_W:2 * GROUP_W] * (HEAD_D ** -0.5)
        ge = g_ref[rows, :] + brow_ref[...]
        for d in range(2):
            ex = _dot_sel_r(ge, exp_ref[d])
            li_e = ex[:, 0:GROUP_W]
            colb = _dot_sel_l(tri_ref[d], _log_sigmoid(ex[:, GROUP_W:2 * GROUP_W]))
            x = li_e - colb
            for sh in (1, 2, 4, 8, 16, 32):
                if d == 0:
                    shifted = jnp.where(rowid >= sh, pltpu.roll(x, sh, axis=0), NEG_INF)
                else:
                    shifted = jnp.where(rowid < ML_CHUNK - sh, pltpu.roll(x, ML_CHUNK - sh, axis=0), NEG_INF)
                x = jnp.maximum(x, shifted)
            edge = ML_CHUNK - 1 if d == 0 else 0
            blast = colb[edge:edge + 1, :]
            gb = blast - colb + li_e
            col_scr[d, rows, :] = colb
            mloc_scr[d, rows, :] = colb + x
            gb_scr[d, rows, :] = gb
            blast_scr[d, pl.ds(ci, 1), :] = blast
            gmax_scr[d, pl.ds(ci, 1), :] = jnp.max(gb, axis=0, keepdims=True)
        return carry

    lax.fori_loop(0, nch, pre_chunk, 0)

    for d in range(2):
        c_scr[d] = jnp.zeros((GROUP_W, GROUP_W), F32)
        for h in range(HEADS):
            lo = HEAD_D * h
            c_scr[d, lo:lo + HEAD_D, lo:lo + HEAD_D] = c0_ref[d, h]
    n_scr[...] = n0_ref[...]
    m_scr[...] = m0_ref[...]

    def chunk(ci, d):
        r0 = pl.multiple_of(ci * ML_CHUNK, ML_CHUNK)
        rows = pl.ds(r0, ML_CHUNK)
        qb = q_scr[rows, :]
        kc = k_scr[rows, :]
        vb = p_ref[rows, 2 * GROUP_W:3 * GROUP_W].astype(BF16)
        hm = e_ref[...]
        kbd = jnp.concatenate([kc.astype(BF16)] * HEADS, axis=0) * hm
        vbd = jnp.concatenate([vb] * HEADS, axis=0) * hm
        s = _dot_nt(qb, kbd)
        colb = col_scr[d, rows, :]
        m_b = m_scr[d]
        prev = colb + m_b
        m_t = jnp.maximum(prev, mloc_scr[d, rows, :])
        keep = (ti >= si) if d == 0 else (ti <= si)
        w = jnp.exp(jnp.where(keep, colb + row_scr[d, pl.ds(ci, 1), :], NEG_INF) - m_t) * s
        wp = jnp.exp(prev - m_t)
        cst = c_scr[d]
        nrow = n_scr[d]
        num = wp * _dot(qb, cst.astype(BF16)) + _dot(w.astype(BF16), vbd)
        den = _dot_hilo(wp * qb.astype(F32) * nrow + w, e_ref[...])
        h_scr[d, rows, :] = num / jnp.maximum(jnp.abs(den), jnp.exp(-m_t))
        blast = blast_scr[d, pl.ds(ci, 1), :]
        m_new = jnp.maximum(blast + m_b, gmax_scr[d, pl.ds(ci, 1), :])
        dec = jnp.exp(blast + m_b - m_new)
        kw = kc * jnp.exp(gb_scr[d, rows, :] - m_new)
        c_scr[d] = cst * dec + _dot_tn(kw.astype(BF16), vb) * hm.astype(F32)
        n_scr[d] = nrow * dec + jnp.sum(kw, axis=0, keepdims=True)
        m_scr[d] = m_new

    def body(i, carry):
        chunk(i, 0)
        chunk(nch - 1 - i, 1)
        return carry

    lax.fori_loop(0, nch, body, 0)

    def fin_chunk(ci, carry):
        rows = pl.ds(pl.multiple_of(ci * ML_CHUNK, ML_CHUNK), ML_CHUNK)
        hs = h_scr[0, rows, :] + h_scr[1, rows, :]
        ms = _dot_hilo(hs * hs, e_ref[...]) * (1.0 / HEAD_D)
        po = p_ref[rows, 3 * GROUP_W:4 * GROUP_W]
        out_ref[rows, :] = (hs * lax.rsqrt(ms + EPS) * ng_ref[...] * _sigmoid(po)).astype(out_ref.dtype)
        return carry

    lax.fori_loop(0, nch, fin_chunk, 0)
    if emit_state:
        for d in range(2):
            for h in range(HEADS):
                lo = HEAD_D * h
                cst_ref[d, h] = c_scr[d, lo:lo + HEAD_D, lo:lo + HEAD_D]
        nst_ref[...] = n_scr[...]
        mst_ref[...] = m_scr[...]


def _mlstm_kernel_old(p_ref, g_ref, gt_ref, conv_ref, brow_ref, bcol_ref, ng_ref,
                  c0_ref, n0_ref, m0_ref, tri_ref, out_ref, *rest, seq, conv_w, emit_state):
    if emit_state:
        cst_ref, nst_ref, mst_ref, qk_scr, hf_scr, c_scr, n_scr, m_scr = rest
    else:
        qk_scr, hf_scr, c_scr, n_scr, m_scr = rest
    nch = seq // ML_CHUNK
    rowid = lax.broadcasted_iota(jnp.int32, (ML_CHUNK, 1), 0)
    ti = lax.broadcasted_iota(jnp.int32, (ML_CHUNK, ML_CHUNK), 0)
    si = lax.broadcasted_iota(jnp.int32, (ML_CHUNK, ML_CHUNK), 1)

    def conv_chunk(ci, carry):
        r0 = pl.multiple_of(ci * ML_CHUNK, ML_CHUNK)
        zc = p_ref[pl.ds(r0, ML_CHUNK), 0:2 * GROUP_W]
        zp = pltpu.roll(zc, 1, axis=0)
        zn = pltpu.roll(zc, ML_CHUNK - 1, axis=0)
        if conv_w == ML_CHUNK:
            prev_row = jnp.zeros((1, 2 * GROUP_W), F32)
            next_row = prev_row
        else:
            prev_row = p_ref[pl.ds(jnp.maximum(r0 - 1, 0), 1), 0:2 * GROUP_W]
            prev_row = jnp.where(r0 % conv_w == 0, 0.0, prev_row)
            next_row = p_ref[pl.ds(jnp.minimum(r0 + ML_CHUNK, seq - 1), 1), 0:2 * GROUP_W]
            next_row = jnp.where((r0 + ML_CHUNK) % conv_w == 0, 0.0, next_row)
        zp = jnp.where(rowid == 0, prev_row, zp)
        zn = jnp.where(rowid == ML_CHUNK - 1, next_row, zn)
        out = conv_ref[0:1, :] * zp + conv_ref[1:2, :] * zc + conv_ref[2:3, :] * zn
        qk_scr[pl.ds(r0, ML_CHUNK), :] = _silu(out)
        return carry

    lax.fori_loop(0, nch, conv_chunk, 0)

    def chunk(ci, d, finalize):
        r0 = pl.multiple_of(ci * ML_CHUNK, ML_CHUNK)
        rows = pl.ds(r0, ML_CHUNK)
        g = g_ref[rows, :] + brow_ref[...]
        gt = gt_ref[ci] + bcol_ref[...]
        li = g[:, 0:2 * HEADS]
        lf = _log_sigmoid(g[:, 2 * HEADS:4 * HEADS])
        lit = gt[0:2 * HEADS, :]
        lft = _log_sigmoid(gt[2 * HEADS:4 * HEADS, :])
        bc_all = _dot_sel_l(tri_ref[d], lf)
        bct_all = _dot_sel_r(lft, tri_ref[1 - d])
        keep = (ti >= si) if d == 0 else (ti <= si)
        edge = ML_CHUNK - 1 if d == 0 else 0
        outs = []
        for h in range(HEADS):
            c = HEADS * d + h
            lo = HEAD_D * h
            q = qk_scr[rows, lo:lo + HEAD_D]
            k = qk_scr[rows, GROUP_W + lo:GROUP_W + lo + HEAD_D] * (HEAD_D ** -0.5)
            v = p_ref[rows, 2 * GROUP_W + lo:2 * GROUP_W + lo + HEAD_D]
            qb = q.astype(BF16)
            vb = v.astype(BF16)
            bcol = bc_all[:, c:c + 1]
            brow = bct_all[c:c + 1, :]
            m = m_scr[h]
            dmat = jnp.where(keep, bcol - brow + lit[c:c + 1, :], NEG_INF)
            prev = bcol + m
            m_t = jnp.maximum(prev, jnp.max(dmat, axis=1, keepdims=True))
            w = jnp.exp(dmat - m_t) * _dot_nt(qb, k.astype(BF16))
            wp = jnp.exp(prev - m_t)
            cst = c_scr[h]
            nrow = n_scr[h]
            num = wp * _dot(qb, cst.astype(BF16)) + _dot(w.astype(BF16), vb)
            den = wp * jnp.sum(q * nrow, axis=1, keepdims=True) + jnp.sum(w, axis=1, keepdims=True)
            hh = num / jnp.maximum(jnp.abs(den), jnp.exp(-m_t))
            blast = bcol[edge:edge + 1, :]
            gc = blast - bcol + li[:, c:c + 1]
            m_new = jnp.maximum(blast + m, jnp.max(gc, axis=0, keepdims=True))
            dec = jnp.exp(blast + m - m_new)
            kw = k * jnp.exp(gc - m_new)
            c_scr[h] = dec * cst + _dot_tn(kw.astype(BF16), vb)
            n_scr[h] = dec * nrow + jnp.sum(kw, axis=0, keepdims=True)
            m_scr[h] = m_new
            if finalize:
                hs = hh + hf_scr[rows, lo:lo + HEAD_D]
                hs = hs * lax.rsqrt(jnp.mean(hs * hs, axis=-1, keepdims=True) + EPS)
                outs.append(hs)
            else:
                outs.append(hh)
        o = jnp.concatenate(outs, axis=1)
        if finalize:
            po = p_ref[rows, 3 * GROUP_W:4 * GROUP_W]
            out_ref[rows, :] = (o * ng_ref[...] * _sigmoid(po)).astype(out_ref.dtype)
        else:
            hf_scr[rows, :] = o

    def run(d):
        c_scr[...] = c0_ref[d]
        n_scr[...] = n0_ref[d]
        m_scr[...] = m0_ref[d]

        def body(i, carry):
            chunk(i if d == 0 else nch - 1 - i, d, d == 1)
            return carry

        lax.fori_loop(0, nch, body, 0)
        if emit_state:
            cst_ref[d] = c_scr[...]
            nst_ref[d] = n_scr[...]
            mst_ref[d] = m_scr[...]

    run(0)
    run(1)


def _mlstm(p3, g3, gr4, conv_w3, brow, brows, norm_g, c0, n0, m0, conv_width, emit_state):
    b, seq, _ = p3.shape
    nch = seq // ML_CHUNK
    tri, rowtri, expand, head = _mlstm_consts()
    const2 = lambda i: (0, 0)
    const3 = lambda i: (0, 0, 0)
    out_shape = [jax.ShapeDtypeStruct((b, seq, GROUP_W), BF16)]
    out_specs = [pl.BlockSpec((None, seq, GROUP_W), lambda i: (i, 0, 0))]
    c_spec = pl.BlockSpec((None, 2, HEADS, HEAD_D, HEAD_D), lambda i: (i, 0, 0, 0, 0))
    row_spec = pl.BlockSpec((None, 2, 1, GROUP_W), lambda i: (i, 0, 0, 0))
    if emit_state:
        out_shape += [jax.ShapeDtypeStruct((b, 2, HEADS, HEAD_D, HEAD_D), F32),
                      jax.ShapeDtypeStruct((b, 2, 1, GROUP_W), F32), jax.ShapeDtypeStruct((b, 2, 1, GROUP_W), F32)]
        out_specs += [c_spec, row_spec, row_spec]
    dir_seq = (2, seq, GROUP_W)
    dir_chunks = (2, nch, GROUP_W)
    res = pl.pallas_call(
        functools.partial(_mlstm_kernel, seq=seq, conv_w=conv_width, emit_state=emit_state),
        grid=(b,),
        in_specs=[
            pl.BlockSpec((None, seq, 4 * GROUP_W), lambda i: (i, 0, 0)),
            pl.BlockSpec((None, seq, N_GATES), lambda i: (i, 0, 0)),
            pl.BlockSpec((None, 4, nch, GROUP_W), lambda i: (i, 0, 0, 0)),
            pl.BlockSpec((3, 2 * GROUP_W), const2),
            pl.BlockSpec((1, N_GATES), const2),
            pl.BlockSpec((4, GROUP_W), const2),
            pl.BlockSpec((1, GROUP_W), const2),
            c_spec, row_spec, row_spec,
            pl.BlockSpec((2, ML_CHUNK, ML_CHUNK), const3),
            pl.BlockSpec((2, GROUP_W, GROUP_W), const3),
            pl.BlockSpec((2, N_GATES, 2 * GROUP_W), const3),
            pl.BlockSpec((GROUP_W, GROUP_W), const2),
        ],
        out_specs=out_specs,
        out_shape=out_shape,
        scratch_shapes=[
            pltpu.VMEM((seq, GROUP_W), BF16),
            pltpu.VMEM((seq, GROUP_W), F32),
            pltpu.VMEM(dir_seq, F32), pltpu.VMEM(dir_seq, F32), pltpu.VMEM(dir_seq, F32), pltpu.VMEM(dir_seq, F32),
            pltpu.VMEM(dir_chunks, F32), pltpu.VMEM(dir_chunks, F32), pltpu.VMEM(dir_chunks, F32),
            pltpu.VMEM((2, GROUP_W, GROUP_W), F32),
            pltpu.VMEM((2, 1, GROUP_W), F32),
            pltpu.VMEM((2, 1, GROUP_W), F32),
        ],
        name="mlstm",
    )(p3, g3, gr4, conv_w3, brow, brows, norm_g, c0, n0, m0, jnp.asarray(tri, BF16), jnp.asarray(rowtri, BF16),
      jnp.asarray(expand, BF16), jnp.asarray(head, BF16))
    if emit_state:
        return res
    return res[0], None, None, None


def _s5_kernel(*refs, nb, reverse, final):
    if final:
        (u_ref, hre0_ref, him0_ref, are_ref, aim_ref, wb_ref, wc_ref, yin_ref, d_ref, gw_ref, gb_ref,
         y_ref, sre_ref, sim_ref, utm_scr, bu_scr, ytm_scr, hre_scr, him_scr) = refs
    else:
        (u_ref, hre0_ref, him0_ref, are_ref, aim_ref, wb_ref, wc_ref,
         y_ref, sre_ref, sim_ref, utm_scr, bu_scr, ytm_scr, hre_scr, him_scr) = refs
    tc = S5_TC
    step_id = pl.program_id(0)

    @pl.when(step_id == 0)
    def _():
        hre_scr[...] = hre0_ref[...]
        him_scr[...] = him0_ref[...]

    halves = GROUP_W // LANES
    for b in range(nb):
        for hv in range(halves):
            utm_scr[hv, pl.ds(b, tc, stride=nb), :] = u_ref[b, :, hv * LANES:(hv + 1) * LANES]
    u_tm = jnp.concatenate([utm_scr[hv] for hv in range(halves)], axis=1)
    bu_scr[...] = _dot(u_tm.astype(BF16), wb_ref[...])
    are = jnp.broadcast_to(are_ref[...], (nb, S5_STATE))
    aim = jnp.broadcast_to(aim_ref[...], (nb, S5_STATE))

    def step(i, carry):
        hre, him = carry
        t = (tc - 1 - i) if reverse else i
        r0 = pl.multiple_of(t * nb, nb)
        bu = bu_scr[pl.ds(r0, nb), :]
        nre = are * hre - aim * him + bu[:, 0:S5_STATE]
        nim = are * him + aim * hre + bu[:, S5_STATE:2 * S5_STATE]
        bu_scr[pl.ds(r0, nb), 0:S5_STATE] = nre
        bu_scr[pl.ds(r0, nb), S5_STATE:2 * S5_STATE] = nim
        return nre, nim

    hre, him = lax.fori_loop(0, tc, step, (hre_scr[...], him_scr[...]))
    hre_scr[...] = hre
    him_scr[...] = him
    y_tm = _dot(bu_scr[...].astype(BF16), wc_ref[...])
    for hv in range(halves):
        ytm_scr[hv] = y_tm[:, hv * LANES:(hv + 1) * LANES]
    for b in range(nb):
        yb = jnp.concatenate([ytm_scr[hv, pl.ds(b, tc, stride=nb), :] for hv in range(halves)], axis=1)
        if final:
            y = yb + yin_ref[b] + d_ref[...] * u_ref[b]
            z = 0.5 * y * (1.0 + jnp.tanh(math.sqrt(2.0 / math.pi) * (y + 0.044715 * (y * y * y))))
            gate = _sigmoid(_dot(z.astype(BF16), gw_ref[...]) + gb_ref[...])
            y_ref[b] = (z * gate).astype(y_ref.dtype)
        else:
            y_ref[b] = yb

    @pl.when(step_id == pl.num_programs(0) - 1)
    def _():
        sre_ref[...] = hre
        sim_ref[...] = him


def _s5_pass(u3, hre0, him0, are, aim, wb, wc, reverse, extra):
    nb, seq, _ = u3.shape
    nch = seq // S5_TC
    final = extra is not None
    chunk = (lambda i: (0, nch - 1 - i, 0)) if reverse else (lambda i: (0, i, 0))
    const2 = lambda i: (0, 0)
    in_specs = [
        pl.BlockSpec((nb, S5_TC, GROUP_W), chunk),
        pl.BlockSpec((nb, S5_STATE), const2),
        pl.BlockSpec((nb, S5_STATE), const2),
        pl.BlockSpec((1, S5_STATE), const2),
        pl.BlockSpec((1, S5_STATE), const2),
        pl.BlockSpec((GROUP_W, 2 * S5_STATE), const2),
        pl.BlockSpec((2 * S5_STATE, GROUP_W), const2),
    ]
    args = [u3, hre0, him0, are, aim, wb, wc]
    if final:
        yin, dvec, gw, gb = extra
        in_specs += [pl.BlockSpec((nb, S5_TC, GROUP_W), chunk), pl.BlockSpec((1, GROUP_W), const2),
                     pl.BlockSpec((GROUP_W, GROUP_W), const2), pl.BlockSpec((1, GROUP_W), const2)]
        args += [yin, dvec, gw, gb]
    return pl.pallas_call(
        functools.partial(_s5_kernel, nb=nb, reverse=reverse, final=final),
        grid=(nch,),
        in_specs=in_specs,
        out_specs=[pl.BlockSpec((nb, S5_TC, GROUP_W), chunk),
                   pl.BlockSpec((nb, S5_STATE), const2), pl.BlockSpec((nb, S5_STATE), const2)],
        out_shape=[jax.ShapeDtypeStruct((nb, seq, GROUP_W), BF16 if final else F32),
                   jax.ShapeDtypeStruct((nb, S5_STATE), F32), jax.ShapeDtypeStruct((nb, S5_STATE), F32)],
        scratch_shapes=[
            pltpu.VMEM((GROUP_W // LANES, nb * S5_TC, LANES), F32),
            pltpu.VMEM((nb * S5_TC, 2 * S5_STATE), F32),
            pltpu.VMEM((GROUP_W // LANES, nb * S5_TC, LANES), F32),
            pltpu.VMEM((nb, S5_STATE), F32),
            pltpu.VMEM((nb, S5_STATE), F32),
        ],
        name="s5_bwd" if reverse else "s5_fwd",
    )(*args)


def _s5_params(a_re, a_im, log_dt, b_re, b_im, c_re, c_im):
    eye = jnp.eye(S5_GROUPS, dtype=F32)
    dt = jnp.exp(log_dt)[..., None]
    mag = jnp.exp(a_re * dt)
    ab_re, ab_im = mag * jnp.cos(a_im * dt), mag * jnp.sin(a_im * dt)
    den = a_re * a_re + a_im * a_im
    g_re = ((ab_re - 1.0) * a_re + ab_im * a_im) / den
    g_im = (ab_im * a_re - (ab_re - 1.0) * a_im) / den
    bb_re = g_re[..., None] * b_re - g_im[..., None] * b_im
    bb_im = g_re[..., None] * b_im + g_im[..., None] * b_re

    def in_mat(bb):
        return jnp.einsum('dgpc,gh->dgchp', bb, eye).reshape(2, GROUP_W, S5_STATE)

    wb = jnp.concatenate([in_mat(bb_re), in_mat(bb_im)], axis=-1).astype(BF16)

    def out_mat(cc):
        return jnp.einsum('gcp,gh->gphc', cc, eye).reshape(S5_STATE, GROUP_W)

    wc = jnp.concatenate([out_mat(c_re), -out_mat(c_im)], axis=0).astype(BF16)
    return ab_re.reshape(2, 1, S5_STATE), ab_im.reshape(2, 1, S5_STATE), wb, wc


def _s5(u3, h0_re, h0_im, prm, d_vec, glu_w, glu_b):
    ab_re, ab_im, wb, wc = prm
    yf, fre, fim = _s5_pass(u3, h0_re[:, 0], h0_im[:, 0], ab_re[0], ab_im[0], wb[0], wc, False, None)
    out, bre, bim = _s5_pass(u3, h0_re[:, 1], h0_im[:, 1], ab_re[1], ab_im[1], wb[1], wc, True,
                             (yf, d_vec, glu_w, glu_b))
    return out, jnp.stack([fre, bre], axis=1), jnp.stack([fim, bim], axis=1)


@functools.lru_cache(maxsize=None)
def _dft_consts(n):
    idx = np.arange(n, dtype=np.int64)
    ang = (np.pi / n) * ((idx[:, None] * idx[None, :]) % (2 * n)).astype(np.float64)
    cos, sin = np.cos(ang), np.sin(ang)
    sign = np.where(idx % 2 == 0, 1.0, -1.0)
    sin_n = sin.copy()
    sin_n[0, :] = sign
    fwd = np.concatenate([cos, sin_n], axis=0)
    inv_c = cos.T / n
    inv_c[:, 0] = 1.0 / (2 * n)
    inv_s = sin.T / n
    inv_s[:, 0] = sign / (2 * n)
    inv = np.concatenate([inv_c, inv_s], axis=1)
    return fwd.astype(np.float32), inv.astype(np.float32), sign.astype(np.float32)[:, None]


@functools.lru_cache(maxsize=None)
def _hyena_feats(n):
    t = np.linspace(0.0, 1.0, n, dtype=np.float64)[:, None]
    w = (2.0 * np.pi / n) * np.arange(n, dtype=np.float64)[:, None]
    bands = np.linspace(1e-4, HY_BANDS - 1, HY_BANDS, dtype=np.float64)[None, :]
    feats = np.concatenate([t, np.cos(w * bands), -np.sin(w * bands)], axis=-1)
    pad = np.zeros((n, LANES - HY_EMB))
    return np.concatenate([feats, pad], axis=-1).astype(np.float32), t.astype(np.float32)


def _hyfilt_kernel(feat_ref, t_ref, sign_ref, w1_ref, b1_ref, w2_ref, b2_ref, w3_ref, fr_ref, dec_ref,
                   fh_ref, fl_ref, ka_ref, kb_ref, ka2_ref, *, n):
    freq = fr_ref[...]
    hdn = jnp.sin(freq * (_dot3(feat_ref[...], w1_ref[...]) + b1_ref[...]))
    hdn = jnp.sin(freq * (_dot3(hdn, w2_ref[...]) + b2_ref[...]))
    h = _dot3(hdn, w3_ref[...]) * (jnp.exp(-t_ref[...] * jnp.abs(dec_ref[...])) + HY_SHIFT)
    half = HY_ORDER * GROUP_W
    row = lax.broadcasted_iota(jnp.int32, (n, 1), 0)
    h0 = h[:, 0:half]
    h1 = jnp.where(row == 0, 0.0, h[:, half:2 * half])
    norm = jnp.sum(jnp.abs(h0), axis=0, keepdims=True) + jnp.sum(jnp.abs(h1), axis=0, keepdims=True)
    hp = (h0 + h1) / norm
    hm = (h0 - h1) / norm

    def dft(lo, x):
        xh = x.astype(BF16)
        xl = (x - xh.astype(F32)).astype(BF16)
        fh = fh_ref[lo:lo + n, :]
        return _dot(fh, xh) + _dot(fl_ref[lo:lo + n, :], xh) + _dot(fh, xl)

    kc = dft(0, hp)
    ks = dft(n, hm)
    kn = jnp.sum(hp * sign_ref[...], axis=0, keepdims=True)
    ka_ref[...] = kc
    kb_ref[...] = jnp.where(row == 0, 0.0, ks)
    ka2_ref[...] = jnp.where(row == 0, kn, kc)


def _hyena_filter(n, w1p, b1, w2, b2, w3, freq, decay):
    fwd, _, sign = _dft_consts(n)
    feats, t = _hyena_feats(n)
    fwd = jnp.asarray(fwd)
    fh = fwd.astype(BF16)
    fl = (fwd - fh.astype(F32)).astype(BF16)
    half = HY_ORDER * GROUP_W
    return pl.pallas_call(
        functools.partial(_hyfilt_kernel, n=n),
        out_shape=[jax.ShapeDtypeStruct((n, half), F32)] * 3,
        name="hyena_filter",
    )(jnp.asarray(feats), jnp.asarray(t), jnp.asarray(sign), w1p, b1, w2, b2, w3, freq, decay, fh, fl)


def _hyena_kernel(p_ref, cw_ref, ka_ref, kb_ref, ka2_ref, bias_ref, f_ref, g_ref, out_ref,
                  z_scr, x_scr, zf_scr, *, seq, conv_w):
    nch = seq // ML_CHUNK
    rowid = lax.broadcasted_iota(jnp.int32, (ML_CHUNK, 1), 0)

    def conv_chunk(ci, carry):
        r0 = pl.multiple_of(ci * ML_CHUNK, ML_CHUNK)
        zc = p_ref[pl.ds(r0, ML_CHUNK), :]
        zp = pltpu.roll(zc, 1, axis=0)
        zn = pltpu.roll(zc, ML_CHUNK - 1, axis=0)
        if conv_w == ML_CHUNK:
            prev_row = jnp.zeros((1, 3 * GROUP_W), F32)
            next_row = prev_row
        else:
            prev_row = p_ref[pl.ds(jnp.maximum(r0 - 1, 0), 1), :]
            prev_row = jnp.where(r0 % conv_w == 0, 0.0, prev_row)
            next_row = p_ref[pl.ds(jnp.minimum(r0 + ML_CHUNK, seq - 1), 1), :]
            next_row = jnp.where((r0 + ML_CHUNK) % conv_w == 0, 0.0, next_row)
        zp = jnp.where(rowid == 0, prev_row, zp)
        zn = jnp.where(rowid == ML_CHUNK - 1, next_row, zn)
        z_scr[pl.ds(r0, ML_CHUNK), :] = cw_ref[0:1, :] * zp + cw_ref[1:2, :] * zc + cw_ref[2:3, :] * zn
        return carry

    lax.fori_loop(0, nch, conv_chunk, 0)
    y = z_scr[:, 0:GROUP_W]
    for o in range(HY_ORDER):
        cols = slice(o * GROUP_W, (o + 1) * GROUP_W)
        x_scr[...] = _dot(f_ref[...], y.astype(BF16))
        xc = x_scr[0:seq, :]
        xs = x_scr[seq:2 * seq, :]
        kb = kb_ref[:, cols]
        zf_scr[0:seq, :] = (xc * ka_ref[:, cols] - xs * kb).astype(BF16)
        zf_scr[seq:2 * seq, :] = (xc * kb + xs * ka2_ref[:, cols]).astype(BF16)
        conv = _dot(g_ref[...], zf_scr[...])
        gate = z_scr[:, (o + 1) * GROUP_W:(o + 2) * GROUP_W]
        y = gate * (conv + bias_ref[o:o + 1, :] * y)
    out_ref[...] = y.astype(out_ref.dtype)


def _hyena(p3, conv_w3, filt, bias, conv_width):
    b, seq, _ = p3.shape
    ka, kb, ka2 = filt
    fwd, inv, _ = _dft_consts(seq)
    const2 = lambda i: (0, 0)
    half = HY_ORDER * GROUP_W
    return pl.pallas_call(
        functools.partial(_hyena_kernel, seq=seq, conv_w=conv_width),
        grid=(b,),
        in_specs=[
            pl.BlockSpec((None, seq, 3 * GROUP_W), lambda i: (i, 0, 0)),
            pl.BlockSpec((3, 3 * GROUP_W), const2),
            pl.BlockSpec((seq, half), const2),
            pl.BlockSpec((seq, half), const2),
            pl.BlockSpec((seq, half), const2),
            pl.BlockSpec((HY_ORDER, GROUP_W), const2),
            pl.BlockSpec((2 * seq, seq), const2),
            pl.BlockSpec((seq, 2 * seq), const2),
        ],
        out_specs=pl.BlockSpec((None, seq, GROUP_W), lambda i: (i, 0, 0)),
        out_shape=jax.ShapeDtypeStruct((b, seq, GROUP_W), BF16),
        scratch_shapes=[
            pltpu.VMEM((seq, 3 * GROUP_W), F32),
            pltpu.VMEM((2 * seq, GROUP_W), F32),
            pltpu.VMEM((2 * seq, GROUP_W), BF16),
        ],
        name="hyena",
    )(p3, conv_w3, ka, kb, ka2, bias, jnp.asarray(fwd, BF16), jnp.asarray(inv, BF16))


def _outproj_kernel(a_ref, b_ref, c_ref, d_ref, x_ref, g1_ref, sh2_ref, sc2_ref, n2_ref, wo_ref,
                    wr_ref, br_ref, x1_ref, h2_ref, gate_ref):
    acc = _dot(a_ref[...], wo_ref[0:GROUP_W, :])
    acc += _dot(b_ref[...], wo_ref[GROUP_W:2 * GROUP_W, :])
    acc += _dot(c_ref[...], wo_ref[2 * GROUP_W:3 * GROUP_W, :])
    acc += _dot(d_ref[...], wo_ref[3 * GROUP_W:4 * GROUP_W, :])
    x1 = x_ref[...] + g1_ref[0] * acc
    x1_ref[...] = x1
    h2 = _rmsnorm_rows(x1, n2_ref[...]) * (1.0 + sc2_ref[0]) + sh2_ref[0]
    h2_ref[...] = h2.astype(BF16)
    logits = _dot3(h2, wr_ref[...]) + br_ref[...]
    lane = lax.broadcasted_iota(jnp.int32, logits.shape, 1).astype(F32)
    big = float(LANES)
    gl = jnp.where(lane < N_EXPERT_GROUPS, logits, NEG_INF)
    gmax = jnp.max(gl, axis=1, keepdims=True)
    gsel = jnp.min(jnp.where(gl == gmax, lane, big), axis=1, keepdims=True)
    psel = 1.0 / jnp.sum(jnp.exp(gl - gmax), axis=1, keepdims=True)
    lo = N_EXPERT_GROUPS + EXPERTS_PER_GROUP * gsel
    el = jnp.where((lane >= lo) & (lane < lo + EXPERTS_PER_GROUP), logits, NEG_INF)
    v1 = jnp.max(el, axis=1, keepdims=True)
    i1 = jnp.min(jnp.where(el == v1, lane, big), axis=1, keepdims=True)
    el2 = jnp.where(lane == i1, NEG_INF, el)
    v2 = jnp.max(el2, axis=1, keepdims=True)
    i2 = jnp.min(jnp.where(el2 == v2, lane, big), axis=1, keepdims=True)
    e2 = jnp.exp(v2 - v1)
    w1 = psel / (1.0 + e2)
    w2 = psel * e2 / (1.0 + e2)
    gate_ref[...] = (jnp.where(lane == 0.0, gsel, 0.0) + jnp.where(lane == i1 - lo + 1.0, w1, 0.0)
                     + jnp.where(lane == i2 - lo + 1.0, w2, 0.0))


def _outproj(mix, x2d, mod3, row_of_tile, norm_g, w_out, w_r, b_r):
    t = x2d.shape[0]
    tm = TOK_TILE
    const = lambda i: (0, 0)
    tile = lambda i: (i, 0)
    return pl.pallas_call(
        _outproj_kernel,
        grid=(t // tm,),
        in_specs=[pl.BlockSpec((tm, GROUP_W), tile)] * 4 + [
            pl.BlockSpec((tm, D_MODEL), tile),
            _mod_spec(2, row_of_tile),
            _mod_spec(3, row_of_tile),
            _mod_spec(4, row_of_tile),
            pl.BlockSpec((1, D_MODEL), const),
            pl.BlockSpec((D_MODEL, D_MODEL), const),
            pl.BlockSpec((D_MODEL, LANES), const),
            pl.BlockSpec((1, LANES), const),
        ],
        out_specs=[pl.BlockSpec((tm, D_MODEL), tile), pl.BlockSpec((tm, D_MODEL), tile),
                   pl.BlockSpec((tm, LANES), tile)],
        out_shape=[jax.ShapeDtypeStruct((t, D_MODEL), F32), jax.ShapeDtypeStruct((t, D_MODEL), BF16),
                   jax.ShapeDtypeStruct((t, LANES), F32)],
        name="outproj_router",
    )(*mix, x2d, mod3, mod3, mod3, norm_g, w_out, w_r, b_r)


def _tile_positions(gates, tri_ref, extra):
    lane = lax.broadcasted_iota(jnp.int32, gates.shape, 1).astype(F32)
    gsel = gates[:, 0:1]
    member = jnp.where((lane == gsel) & (lane < N_EXPERT_GROUPS), 1.0, 0.0)
    before = _dot(tri_ref[...], member.astype(BF16))
    pos = gsel * MOE_TOK + jnp.sum(member * (before + extra), axis=1, keepdims=True)
    slot = lax.broadcasted_iota(jnp.int32, (MOE_TOK, N_EXPERT_GROUPS * MOE_TOK), 1).astype(F32)
    return jnp.where(slot == pos, 1.0, 0.0).astype(BF16)


def _moe_pack_kernel(off_ref, cnt_ref, tot_ref, h_ref, gate_ref, tri_ref, xg_ref, gg_ref,
                     xs_scr, gs_scr, zx_scr, zg_scr, sem):
    i = pl.program_id(0)
    gates = gate_ref[...]
    lane = lax.broadcasted_iota(jnp.int32, gates.shape, 1).astype(F32)
    member = jnp.where((lane == gates[:, 0:1]) & (lane < N_EXPERT_GROUPS), 1.0, 0.0).astype(BF16)
    grp = lax.broadcasted_iota(jnp.int32, (8, LANES), 0)
    pick = jnp.where(grp == lax.broadcasted_iota(jnp.int32, (8, LANES), 1), 1.0, 0.0).astype(BF16)
    member_t = _dot_nt(pick, member)
    before_t = _dot_nt(member_t.astype(BF16), tri_ref[...])
    gid = lax.broadcasted_iota(jnp.int32, member_t.shape, 0).astype(F32)
    pos_t = jnp.sum(member_t * (gid * MOE_TOK + before_t), axis=0, keepdims=True)
    slot = lax.broadcasted_iota(jnp.int32, (N_EXPERT_GROUPS * MOE_TOK, MOE_TOK), 0).astype(F32)
    p = jnp.where(slot == pos_t, 1.0, 0.0).astype(BF16)
    xs_scr[...] = _dot(p, h_ref[...]).astype(BF16)
    hi, mid, lo = _split3(gates)
    gs_scr[...] = _dot(p, hi) + _dot(p, mid) + _dot(p, lo)

    def for_each_piece(action):
        for g in range(N_EXPERT_GROUPS):
            off = pl.multiple_of(off_ref[i * N_EXPERT_GROUPS + g], ROW_ALIGN)
            for j in range(MOE_TOK // MOE_PIECE):
                src = pl.ds(g * MOE_TOK + j * MOE_PIECE, MOE_PIECE)
                dst = pl.ds(off + j * MOE_PIECE, MOE_PIECE)

                @pl.when(cnt_ref[i * N_EXPERT_GROUPS + g] > j * MOE_PIECE)
                def _():
                    action(pltpu.make_async_copy(xs_scr.at[src], xg_ref.at[g, dst], sem.at[0, g, j]))
                    action(pltpu.make_async_copy(gs_scr.at[src], gg_ref.at[g, dst], sem.at[1, g, j]))

    for_each_piece(lambda cp: cp.start())
    for_each_piece(lambda cp: cp.wait())

    @pl.when(i == pl.num_programs(0) - 1)
    def _():
        zx_scr[...] = jnp.zeros_like(zx_scr)
        zg_scr[...] = jnp.zeros_like(zg_scr)

        def tail_copies(g):
            tot = pl.multiple_of(tot_ref[g], ROW_ALIGN)
            return (pltpu.make_async_copy(zx_scr, xg_ref.at[g, pl.ds(tot, MOE_ROWS)], sem.at[0, g, 0]),
                    pltpu.make_async_copy(zg_scr, gg_ref.at[g, pl.ds(tot, MOE_ROWS)], sem.at[1, g, 0]))

        for g in range(N_EXPERT_GROUPS):
            for cp in tail_copies(g):
                cp.start()
        for g in range(N_EXPERT_GROUPS):
            for cp in tail_copies(g):
                cp.wait()


def _moe_expert_kernel(grp_ref, chk_ref, nact_ref, x_ref, g_ref, wg_ref, wu_ref, wd_ref, y_ref):
    @pl.when(pl.program_id(0) < nact_ref[0])
    def _():
        x = x_ref[...]
        gates = g_ref[...]
        acc = None
        for e in range(EXPERTS_PER_GROUP):
            a = _dot(x, wg_ref[e])
            u = _dot(x, wu_ref[e])
            hid = (_silu(a) * u * gates[:, 1 + e:2 + e]).astype(BF16)
            part = _dot(hid, wd_ref[e])
            acc = part if acc is None else acc + part
        y_ref[...] = acc.astype(y_ref.dtype)


def _moe_unpack_kernel(st_ref, dl_ref, cnt_ref, gate_ref, tri_ref, x1_ref, g2_ref, fn_ref, yg_ref, out_ref,
                       ys_scr, sem, *, final):
    i = pl.program_id(0)

    @pl.when(i == 0)
    def _():
        ys_scr[...] = jnp.zeros_like(ys_scr)

    def for_each_piece(action):
        for g in range(N_EXPERT_GROUPS):
            k = i * N_EXPERT_GROUPS + g
            start = pl.multiple_of(st_ref[k], ROW_ALIGN)
            for j in range(MOE_TOK // MOE_PIECE):
                lo, hi = j * MOE_PIECE, (j + 1) * MOE_PIECE

                @pl.when((dl_ref[k] < hi) & (dl_ref[k] + cnt_ref[k] > lo))
                def _():
                    action(pltpu.make_async_copy(yg_ref.at[g, pl.ds(start + lo, MOE_PIECE)],
                                                 ys_scr.at[pl.ds(g * MOE_TOK + lo, MOE_PIECE)], sem.at[g, j]))

    for_each_piece(lambda cp: cp.start())
    gates = gate_ref[...]
    lane = lax.broadcasted_iota(jnp.int32, (1, LANES), 1)
    delta = jnp.zeros((1, LANES), F32)
    for g in range(N_EXPERT_GROUPS):
        delta = jnp.where(lane == g, dl_ref[i * N_EXPERT_GROUPS + g].astype(F32), delta)
    pt = _tile_positions(gates, tri_ref, delta)
    for_each_piece(lambda cp: cp.wait())
    x2 =x1_ref[...] + g2_ref[0] * _dot(pt, ys_scr[...])
    if final:
        x2 = _rmsnorm_rows(x2, fn_ref[...])
    out_ref[...] = x2


def _moe_schedule(gsel, n_tiles, n_steps):
    ng = N_EXPERT_GROUPS
    member = gsel.reshape(n_tiles, MOE_TOK, 1) == jnp.arange(ng, dtype=jnp.int32)
    cnt = jnp.sum(member.astype(jnp.int32), axis=1)
    padded = (cnt + ROW_ALIGN - 1) // ROW_ALIGN * ROW_ALIGN
    off = jnp.cumsum(padded, axis=0) - padded
    tot = jnp.sum(padded, axis=0)
    n_chunks = jnp.maximum((tot + MOE_ROWS - 1) // MOE_ROWS, 1)
    start = jnp.minimum(off, n_chunks * MOE_ROWS - MOE_TOK)
    ends = jnp.cumsum(n_chunks)
    n_active = ends[-1]
    step = jnp.minimum(jnp.arange(n_steps, dtype=jnp.int32), n_active - 1)
    grp = jnp.sum((step[:, None] >= ends[None, :]).astype(jnp.int32), axis=1)
    chk = step - (ends - n_chunks)[grp]
    i32 = lambda a: a.astype(jnp.int32)
    return (i32(off).reshape(-1), i32(padded).reshape(-1), i32(tot), i32(start).reshape(-1),
            i32(off - start).reshape(-1), i32(grp), i32(chk), i32(n_active).reshape(1))


def _moe(h2, gates, layer, w_gate, w_up, w_down, x1, mod3, row_of_tile, final_g, final):
    t = h2.shape[0]
    ng = N_EXPERT_GROUPS
    n_tiles = t // MOE_TOK
    cap = -(-(t + ROW_ALIGN * n_tiles + MOE_ROWS) // MOE_ROWS) * MOE_ROWS
    n_steps = (t + ROW_ALIGN * n_tiles) // MOE_ROWS + ng
    off, cnt, tot, start, delta, grp, chk, n_active = _moe_schedule(gates[:, 0].astype(jnp.int32), n_tiles,
                                                                    n_steps)
    pieces = MOE_TOK // MOE_PIECE
    r = np.arange(MOE_TOK)
    tri = jnp.asarray((r[:, None] > r[None, :]).astype(np.float32), BF16)
    any_spec = pl.BlockSpec(memory_space=pl.ANY)

    xg, gg = pl.pallas_call(
        _moe_pack_kernel,
        grid_spec=pltpu.PrefetchScalarGridSpec(
            num_scalar_prefetch=3,
            grid=(n_tiles,),
            in_specs=[
                pl.BlockSpec((MOE_TOK, D_MODEL), lambda i, *_: (i, 0)),
                pl.BlockSpec((MOE_TOK, LANES), lambda i, *_: (i, 0)),
                pl.BlockSpec((MOE_TOK, MOE_TOK), lambda i, *_: (0, 0)),
            ],
            out_specs=[any_spec, any_spec],
            scratch_shapes=[
                pltpu.VMEM((ng * MOE_TOK, D_MODEL), BF16),
                pltpu.VMEM((ng * MOE_TOK, LANES), F32),
                pltpu.VMEM((MOE_ROWS, D_MODEL), BF16),
                pltpu.VMEM((MOE_ROWS, LANES), F32),
                pltpu.SemaphoreType.DMA((2, ng, pieces)),
            ],
        ),
        out_shape=[jax.ShapeDtypeStruct((ng, cap, D_MODEL), BF16), jax.ShapeDtypeStruct((ng, cap, LANES), F32)],
        name="moe_pack",
    )(off, cnt, tot, h2, gates, tri)

    group_w = lambda shape: pl.BlockSpec((None, None, EXPERTS_PER_GROUP) + shape,
                                         lambda k, grp, chk, na: (layer, grp[k], 0, 0, 0))
    rows = lambda width: pl.BlockSpec((None, MOE_ROWS, width), lambda k, grp, chk, na: (grp[k], chk[k], 0))
    yg = pl.pallas_call(
        _moe_expert_kernel,
        grid_spec=pltpu.PrefetchScalarGridSpec(
            num_scalar_prefetch=3,
            grid=(n_steps,),
            in_specs=[rows(D_MODEL), rows(LANES), group_w((D_MODEL, D_EXPERT)), group_w((D_MODEL, D_EXPERT)),
                      group_w((D_EXPERT, D_MODEL))],
            out_specs=rows(D_MODEL),
        ),
        out_shape=jax.ShapeDtypeStruct((ng, cap, D_MODEL), BF16),
        name="moe_experts",
    )(grp, chk, n_active, xg, gg,
      w_gate.reshape(DEPTH, ng, EXPERTS_PER_GROUP, D_MODEL, D_EXPERT),
      w_up.reshape(DEPTH, ng, EXPERTS_PER_GROUP, D_MODEL, D_EXPERT),
      w_down.reshape(DEPTH, ng, EXPERTS_PER_GROUP, D_EXPERT, D_MODEL))

    return pl.pallas_call(
        functools.partial(_moe_unpack_kernel, final=final),
        grid_spec=pltpu.PrefetchScalarGridSpec(
            num_scalar_prefetch=3,
            grid=(n_tiles,),
            in_specs=[
                pl.BlockSpec((MOE_TOK, LANES), lambda i, *_: (i, 0)),
                pl.BlockSpec((MOE_TOK, MOE_TOK), lambda i, *_: (0, 0)),
                pl.BlockSpec((MOE_TOK, D_MODEL), lambda i, *_: (i, 0)),
                pl.BlockSpec((1, 1, D_MODEL), lambda i, *_: (row_of_tile(i), 0, 5)),
                pl.BlockSpec((1, D_MODEL), lambda i, *_: (0, 0)),
                any_spec,
            ],
            out_specs=pl.BlockSpec((MOE_TOK, D_MODEL), lambda i, *_: (i, 0)),
            scratch_shapes=[pltpu.VMEM((ng * MOE_TOK, D_MODEL), BF16), pltpu.SemaphoreType.DMA((ng, pieces))],
        ),
        out_shape=jax.ShapeDtypeStruct((t, D_MODEL), F32),
        name="moe_unpack",
    )(start, delta, cnt, gates, tri, x1, mod3, final_g, yg)


def _block_diag_t(s):
    eye = jnp.eye(HEADS, dtype=s.dtype)
    b = s.shape[0]
    return jnp.einsum('bdhkv,hg->bdhvgk', s, eye).reshape(b, 2, GROUP_W, GROUP_W)


def _prep_layer(l, w):
    gw = GROUP_W
    w_in = w['w_in'][l]
    w_g = w_in[:, 9 * gw:9 * gw + N_GATES]
    lb_all = jnp.cumsum(jax.nn.softmax(w['hg_lb'].astype(F32), axis=0), axis=0)
    lb = lb_all[l] - lb_all[0]
    lb_rows = jnp.zeros((8, gw), F32).at[0].set(jnp.log(lb)).at[1].set(jnp.log1p(-lb)).at[2].set(1.0 - lb)
    w_r = jnp.zeros((D_MODEL, LANES), F32)
    w_r = w_r.at[:, 0:N_EXPERT_GROUPS].set(w['moe_wg'][l])
    w_r = w_r.at[:, N_EXPERT_GROUPS:N_EXPERT_GROUPS + N_EXPERTS].set(w['moe_we'][l])
    b_r = jnp.zeros((1, LANES), F32)
    b_r = b_r.at[0, 0:N_EXPERT_GROUPS].set(w['moe_bg'][l])
    b_r = b_r.at[0, N_EXPERT_GROUPS:N_EXPERT_GROUPS + N_EXPERTS].set(w['moe_be'][l])
    gate_bias = jnp.concatenate([w['ml_i_bias'][l], w['ml_f_bias'][l]])
    w1p = jnp.zeros((LANES, HY_HIDDEN), F32).at[0:HY_EMB].set(w['hy_w1'][l])
    return dict(
        norm1=w['norm1'][l][None, :], norm2=w['norm2'][l][None, :],
        w_a=w_in[:, :9 * gw].astype(BF16), w_b=w_in[:, 9 * gw + N_GATES:].astype(BF16),
        w_g=w_g.astype(BF16), w_gt=w_g.T.astype(BF16),
        w_out=w['w_out'][l].astype(BF16),
        lb_rows=lb_rows, hg_norm=w['hg_norm'][l][None, :],
        ml_conv=w['ml_conv'][l], ml_brow=gate_bias[None, :],
        ml_brows=jnp.repeat(gate_bias.reshape(4, HEADS), HEAD_D, axis=1),
        ml_norm=w['ml_norm'][l][None, :],
        s5=_s5_params(w['s5_a_re'][l], w['s5_a_im'][l], w['s5_log_dt'][l], w['s5_b_re'][l], w['s5_b_im'][l],
                      w['s5_c_re'][l], w['s5_c_im'][l]),
        s5_d=w['s5_d'][l][None, :], s5_glu_w=w['s5_glu_w'][l].astype(BF16), s5_glu_b=w['s5_glu_b'][l][None, :],
        hy_conv=w['hy_conv'][l], hy_bias=w['hy_bias'][l],
        hy_mlp=(w1p, w['hy_b1'][l][None, :], w['hy_w2'][l], w['hy_b2'][l][None, :], w['hy_w3'][l],
                w['hy_freq'][l][None, :], w['hy_decay'][l][None, :]),
        w_r=w_r, b_r=b_r,
        layer=l,
    )


def _trunk_layer(x2d, nb, seq, mod3, cond_row, st, lw, moe_w, filt, conv_width, emit_state, final_g, final):
    if cond_row is None:
        row_of_tile = lambda i: i // (seq // TOK_TILE)
    else:
        row_of_tile = lambda i: cond_row
    hg, ml, su, hy, gt, gtt = _inproj(x2d, mod3, row_of_tile, lw['norm1'], lw['w_a'], lw['w_b'], lw['w_g'],
                                       lw['w_gt'])
    nch = seq // ML_CHUNK
    a_out, hg_st = _hgrn(hg.reshape(nb, seq, -1), st['hgrn'], lw['lb_rows'], lw['hg_norm'], emit_state)
    gr4 = jnp.transpose(gtt.reshape(4, HEADS, nb, nch, ML_CHUNK), (2, 0, 3, 1, 4)).reshape(nb, 4, nch, GROUP_W)
    b_out, mlc, mln, mlm = _mlstm(ml.reshape(nb, seq, -1), gt.reshape(nb, seq, N_GATES), gr4, lw['ml_conv'],
                                  lw['ml_brow'], lw['ml_brows'], lw['ml_norm'], st['ml_c'], st['ml_n'],
                                  st['ml_m'], conv_width, emit_state)
    c_out, s5re, s5im = _s5(su.reshape(nb, seq, -1), st['s5_re'], st['s5_im'], lw['s5'], lw['s5_d'],
                            lw['s5_glu_w'], lw['s5_glu_b'])
    d_out = _hyena(hy.reshape(nb, seq, -1), lw['hy_conv'], filt, lw['hy_bias'], conv_width)
    mix = [m.reshape(nb * seq, GROUP_W) for m in (a_out, b_out, c_out, d_out)]
    x1, h2, gates = _outproj(mix, x2d, mod3, row_of_tile, lw['norm2'], lw['w_out'], lw['w_r'], lw['b_r'])
    x2 = _moe(h2, gates, lw['layer'], *moe_w, x1, mod3, row_of_tile, final_g, final)
    new_st = dict(hgrn=hg_st, ml_c=mlc, ml_n=mln, ml_m=mlm, s5_re=s5re, s5_im=s5im)
    return x2, new_st


def kernel(x_prompt, x_sample, c, state_hgrn, state_mlstm_c, state_mlstm_n, state_mlstm_m, state_s5_re, state_s5_im, c_ctx, w_ada, b_ada, norm1, norm2, w_in, w_out, hg_lb, hg_norm, ml_conv, ml_i_bias, ml_f_bias, ml_norm, s5_a_re, s5_a_im, s5_log_dt, s5_b_re, s5_b_im, s5_c_re, s5_c_im, s5_d, s5_glu_w, s5_glu_b, hy_conv, hy_w1, hy_b1, hy_w2, hy_b2, hy_w3, hy_freq, hy_decay, hy_bias, moe_wg, moe_bg, moe_we, moe_be, moe_w_gate, moe_w_up, moe_w_down, final_norm):
    w = dict(w_in=w_in, w_out=w_out, norm1=norm1, norm2=norm2, hg_lb=hg_lb, hg_norm=hg_norm, ml_conv=ml_conv,
             ml_i_bias=ml_i_bias, ml_f_bias=ml_f_bias, ml_norm=ml_norm, s5_a_re=s5_a_re, s5_a_im=s5_a_im,
             s5_log_dt=s5_log_dt, s5_b_re=s5_b_re, s5_b_im=s5_b_im, s5_c_re=s5_c_re, s5_c_im=s5_c_im,
             s5_d=s5_d, s5_glu_w=s5_glu_w, s5_glu_b=s5_glu_b, hy_conv=hy_conv, hy_w1=hy_w1, hy_b1=hy_b1,
             hy_w2=hy_w2, hy_b2=hy_b2, hy_w3=hy_w3, hy_freq=hy_freq, hy_decay=hy_decay, hy_bias=hy_bias,
             moe_wg=moe_wg, moe_bg=moe_bg, moe_we=moe_we, moe_be=moe_be, moe_w_gate=moe_w_gate,
             moe_w_up=moe_w_up, moe_w_down=moe_w_down)
    bp, lp, _ = x_prompt.shape
    bs, ls, _ = x_sample.shape
    ctx_row = bs
    cond = jnp.zeros((MOD_ROWS, D_MODEL), F32).at[0:bs].set(c).at[ctx_row].set(c_ctx)
    mod = _ada_mod(cond, w_ada, b_ada)
    final_g = final_norm[None, :]

    zero_st = dict(
        hgrn=jnp.zeros((bp, 2, GROUP_W, GROUP_W), F32),
        ml_c=jnp.zeros((bp, 2, HEADS, HEAD_D, HEAD_D), F32),
        ml_n=jnp.zeros((bp, 2, 1, GROUP_W), F32),
        ml_m=jnp.zeros((bp, 2, 1, GROUP_W), F32),
        s5_re=jnp.zeros((bp, 2, S5_STATE), F32),
        s5_im=jnp.zeros((bp, 2, S5_STATE), F32))

    xp = x_prompt.reshape(bp * lp, D_MODEL)
    xs = x_sample.reshape(bs * ls, D_MODEL)
    moe_w = (moe_w_gate.astype(BF16), moe_w_up.astype(BF16), moe_w_down.astype(BF16))
    per_layer = []
    for l in range(DEPTH):
        lw = _prep_layer(l, w)
        mod3 = mod[l].reshape(MOD_ROWS, 1, 6 * D_MODEL)
        last = l == DEPTH - 1
        filt_p = _hyena_filter(lp, *lw['hy_mlp'])
        filt_s = filt_p if ls == lp else _hyena_filter(ls, *lw['hy_mlp'])
        xp, st = _trunk_layer(xp, bp, lp, mod3, ctx_row, zero_st, lw, moe_w, filt_p, lp, True, final_g, last)
        per_layer.append(st)
        st_in = dict(
            hgrn=_block_diag_t(state_hgrn[:, l].astype(F32)),
            ml_c=state_mlstm_c[:, l].astype(F32),
            ml_n=state_mlstm_n[:, l].astype(F32).reshape(bs, 2, 1, GROUP_W),
            ml_m=jnp.repeat(state_mlstm_m[:, l].astype(F32), HEAD_D, axis=-1).reshape(bs, 2, 1, GROUP_W),
            s5_re=state_s5_re[:, l].astype(F32).reshape(bs, 2, S5_STATE),
            s5_im=state_s5_im[:, l].astype(F32).reshape(bs, 2, S5_STATE))
        xs, _ = _trunk_layer(xs, bs, ls, mod3, None, st_in, lw, moe_w, filt_s, GRID_W, False, final_g, last)

    new_hgrn = jnp.stack([s['hgrn'] for s in per_layer], axis=1)
    new_ml_c = jnp.stack([s['ml_c'] for s in per_layer], axis=1)
    new_ml_n = jnp.stack([s['ml_n'].reshape(bp, 2, HEADS, HEAD_D) for s in per_layer], axis=1)
    new_ml_m = jnp.stack([s['ml_m'][:, :, 0, ::HEAD_D] for s in per_layer], axis=1)
    new_s5_re = jnp.stack([s['s5_re'].reshape(bp, 2, S5_GROUPS, S5_P) for s in per_layer], axis=1)
    new_s5_im = jnp.stack([s['s5_im'].reshape(bp, 2, S5_GROUPS, S5_P) for s in per_layer], axis=1)
    return (xp.reshape(bp, lp, D_MODEL), xs.reshape(bs, ls, D_MODEL),
            new_hgrn, new_ml_c, new_ml_n, new_ml_m, new_s5_re, new_s5_im)
```

```python
import functools
import math

import numpy as np
import jax
import jax.numpy as jnp
from jax import lax
from jax.experimental import pallas as pl
from jax.experimental.pallas import tpu as pltpu

F32 = jnp.float32
BF16 = jnp.bfloat16

D_MODEL = 1024
DEPTH = 2
GRID_W = 64
GROUP_W = D_MODEL // 4
HEADS = 4
HEAD_D = GROUP_W // HEADS
S5_CH = 16
S5_GROUPS = GROUP_W // S5_CH
S5_P = 64
S5_STATE = S5_GROUPS * S5_P
HY_ORDER = 2
HY_BANDS = 16
HY_EMB = 1 + 2 * HY_BANDS
HY_HIDDEN = 64
HY_SHIFT = 0.05
N_EXPERT_GROUPS = 4
EXPERTS_PER_GROUP = 4
N_EXPERTS = N_EXPERT_GROUPS * EXPERTS_PER_GROUP
D_EXPERT = D_MODEL // 2
EPS = 1e-6
N_GATES = 4 * HEADS

LANES = 128
MOD_ROWS = 16
TOK_TILE = 256
MOE_TOK = TOK_TILE
MOE_PIECE = 64
MOE_ROWS = 512
ROW_ALIGN = 16
HG_BLK = 32
HG_SB = 128
HG_SAFE_LOG = -75.0
HG_SAFE_Q = 1e3
ML_CHUNK = 64
S5_TC = 64
NEG_INF = float("-inf")


def _dot(a, b):
    return jnp.dot(a, b, preferred_element_type=F32)


def _dot_nt(a, b):
    return lax.dot_general(a, b, (((1,), (1,)), ((), ())), preferred_element_type=F32)


def _dot_tn(a, b):
    return lax.dot_general(a, b, (((0,), (0,)), ((), ())), preferred_element_type=F32)


def _split3(x):
    hi = x.astype(BF16)
    r1 = x - hi.astype(F32)
    mid = r1.astype(BF16)
    lo = (r1 - mid.astype(F32)).astype(BF16)
    return hi, mid, lo


def _dot_sel_l(m, x):
    hi, mid, lo = _split3(x)
    return _dot(m, hi) + _dot(m, mid) + _dot(m, lo)


def _dot_sel_r(x, m):
    hi, mid, lo = _split3(x)
    return _dot(hi, m) + _dot(mid, m) + _dot(lo, m)


def _dot3(a, b):
    ah = a.astype(BF16)
    al = (a - ah.astype(F32)).astype(BF16)
    bh = b.astype(BF16)
    bl = (b - bh.astype(F32)).astype(BF16)
    return _dot(ah, bh) + _dot(al, bh) + _dot(ah, bl)


def _sigmoid(x):
    return 1.0 / (1.0 + jnp.exp(-x))


def _silu(x):
    return x * _sigmoid(x)


def _log_sigmoid(x):
    return jnp.minimum(x, 0.0) - jnp.log1p(jnp.exp(-jnp.abs(x)))


def _rmsnorm_rows(x, g):
    return x * lax.rsqrt(jnp.mean(x * x, axis=-1, keepdims=True) + EPS) * g


def _ada_kernel(c_ref, w_ref, b_ref, o_ref):
    s = _silu(c_ref[...]).astype(BF16)
    o_ref[0] = _dot(s, w_ref[0].astype(BF16)) + b_ref[0]


def _ada_mod(cond, w_ada, b_ada):
    tn = 1536
    n = w_ada.shape[-1]
    return pl.pallas_call(
        _ada_kernel,
        grid=(DEPTH, n // tn),
        in_specs=[
            pl.BlockSpec((MOD_ROWS, D_MODEL), lambda l, j: (0, 0)),
            pl.BlockSpec((1, D_MODEL, tn), lambda l, j: (l, 0, j)),
            pl.BlockSpec((1, 1, tn), lambda l, j: (l, 0, j)),
        ],
        out_specs=pl.BlockSpec((1, MOD_ROWS, tn), lambda l, j: (l, 0, j)),
        out_shape=jax.ShapeDtypeStruct((DEPTH, MOD_ROWS, n), F32),
        name="ada_mod",
    )(cond, w_ada, b_ada.reshape(DEPTH, 1, n))


def _inproj_kernel(x_ref, sh_ref, sc_ref, g_ref, wa_ref, wb_ref, wg_ref, wgt_ref,
                   hg_ref, ml_ref, su_ref, hy_ref, gt_ref, gtt_ref):
    h = _rmsnorm_rows(x_ref[...], g_ref[...])
    h = h * (1.0 + sc_ref[0]) + sh_ref[0]
    hb = h.astype(BF16)
    hg_ref[...] = _dot(hb, wa_ref[:, 0:5 * GROUP_W])
    ml_ref[...] = _dot(hb, wa_ref[:, 5 * GROUP_W:9 * GROUP_W])
    su_ref[...] = _dot(hb, wb_ref[:, 0:GROUP_W])
    hy_ref[...] = _dot(hb, wb_ref[:, GROUP_W:4 * GROUP_W])
    gt_ref[...] = _dot(hb, wg_ref[...])
    gtt_ref[...] = _dot_nt(wgt_ref[...], hb)


def _mod_spec(k, row_of_tile):
    return pl.BlockSpec((1, 1, D_MODEL), lambda i: (row_of_tile(i), 0, k))


def _inproj(x2d, mod3, row_of_tile, norm_g, w_a, w_b, w_g, w_gt):
    t = x2d.shape[0]
    tm = TOK_TILE
    const = lambda i: (0, 0)
    tile = lambda i: (i, 0)
    widths = (5 * GROUP_W, 4 * GROUP_W, GROUP_W, 3 * GROUP_W)
    return pl.pallas_call(
        _inproj_kernel,
        grid=(t // tm,),
        in_specs=[
            pl.BlockSpec((tm, D_MODEL), tile),
            _mod_spec(0, row_of_tile),
            _mod_spec(1, row_of_tile),
            pl.BlockSpec((1, D_MODEL), const),
            pl.BlockSpec(w_a.shape, const),
            pl.BlockSpec(w_b.shape, const),
            pl.BlockSpec(w_g.shape, const),
            pl.BlockSpec(w_gt.shape, const),
        ],
        out_specs=[pl.BlockSpec((tm, w), tile) for w in widths]
        + [pl.BlockSpec((tm, LANES), tile), pl.BlockSpec((N_GATES, tm), lambda i: (0, i))],
        out_shape=[jax.ShapeDtypeStruct((t, w), F32) for w in widths]
        + [jax.ShapeDtypeStruct((t, LANES), F32), jax.ShapeDtypeStruct((N_GATES, t), F32)],
        name="inproj",
    )(x2d, mod3, mod3, norm_g, w_a, w_b, w_g, w_gt)


@functools.lru_cache(maxsize=None)
def _hgrn_consts():
    r = np.arange(HG_SB)
    same = (r[:, None] // HG_BLK) == (r[None, :] // HG_BLK)
    tri_f = same & (r[:, None] >= r[None, :])
    tri_b = same & (r[:, None] <= r[None, :])
    c = np.arange(GROUP_W)
    head = (c[:, None] // HEAD_D) == (c[None, :] // HEAD_D)
    rr = np.arange(HEADS * HG_SB)
    head4 = (rr[:, None] // HG_SB) == (c[None, :] // HEAD_D)
    pairs = np.stack([np.tile(tri_f, (1, HEADS)), np.tile(tri_b, (1, HEADS))])
    return (np.stack([tri_f, tri_b]).astype(np.float32), same.astype(np.float32),
            head.astype(np.float32), head4.astype(np.float32), pairs.astype(np.float32))


def _hgrn_kernel(p_ref, s0_ref, lb_ref, ng_ref, tri_ref, bones_ref, e_ref, hmask_ref, hmask4_ref,
                 smask_ref, out_ref, *rest, seq, emit_state):
    if emit_state:
        st_ref = rest[0]
        rest = rest[1:]
    o_scr, st_scr, q_scr, k_scr, bc_scr, gq_scr, gk_scr, dec_scr, oi_scr = rest
    nsb = seq // HG_SB
    nblk = HG_SB // HG_BLK
    log_lb = lb_ref[0:1, :]
    log_1mlb = lb_ref[1:2, :]
    one_m_lb = lb_ref[2:3, :]
    row = lax.broadcasted_iota(jnp.int32, (HG_BLK, 1), 0)

    def superblock(sb, d):
        r0 = pl.multiple_of(sb * HG_SB, HG_SB)
        rows = pl.ds(r0, HG_SB)
        pq = p_ref[rows, 0:GROUP_W]
        pf = p_ref[rows, (1 + d) * GROUP_W:(2 + d) * GROUP_W]
        q = _silu(pq)
        b2 = log_1mlb + _log_sigmoid(pf)
        mx = jnp.maximum(log_lb, b2)
        mn = jnp.minimum(log_lb, b2)
        lf = mx + jnp.log1p(jnp.exp(jnp.where(mn == NEG_INF, NEG_INF, mn - mx)))
        k = one_m_lb * _sigmoid(-pf)
        bc = _dot_sel_l(tri_ref[d], lf)
        bt = _dot_sel_l(bones_ref[...], lf)
        gq = (q * jnp.exp(bc)).astype(BF16)
        gq_scr[d] = gq
        gk_scr[d] = (k * jnp.exp(bt - bc)).astype(BF16)
        dec_scr[d] = jnp.exp(bt)
        vb = p_ref[rows, 3 * GROUP_W:4 * GROUP_W].astype(BF16)

        def block_edge_path():
            ks = (k * jnp.exp(-bc)).astype(BF16)
            ksbd = jnp.concatenate([ks] * HEADS, axis=0) * hmask4_ref[...]
            vbd = jnp.concatenate([vb] * HEADS, axis=0) * hmask4_ref[...]
            s = _dot_nt(gq, ksbd) * smask_ref[d]
            oi_scr[d] = _dot(s.astype(BF16), vbd)

        def exact_path():
            q_scr[d] = q
            k_scr[d] = k
            bc_scr[d] = bc
            for j in range(nblk):
                sl = slice(HG_BLK * j, HG_BLK * (j + 1))
                qj = q_scr[d, sl, :]
                kj = k_scr[d, sl, :]
                vj = p_ref[pl.ds(r0 + HG_BLK * j, HG_BLK), 3 * GROUP_W:4 * GROUP_W]
                bcj = bc_scr[d, sl, :]
                parts = []
                for s in range(HG_BLK):
                    keep = (row >= s) if d == 0 else (row <= s)
                    dd = jnp.exp(jnp.where(keep, bcj - bcj[s:s + 1, :], NEG_INF))
                    parts.append((qj * dd * kj[s:s + 1, :]).astype(BF16))
                r_all = _dot(jnp.concatenate(parts, axis=0), e_ref[...])
                o = r_all[0:HG_BLK, :] * vj[0:1, :]
                for s in range(1, HG_BLK):
                    o = o + r_all[HG_BLK * s:HG_BLK * (s + 1), :] * vj[s:s + 1, :]
                oi_scr[d, sl, :] = o

        safe = (jnp.min(bc) >= HG_SAFE_LOG) & (jnp.max(jnp.abs(q)) <= HG_SAFE_Q)
        lax.cond(safe, block_edge_path, exact_path)

        order = range(nblk) if d == 0 else range(nblk - 1, -1, -1)
        for j in order:
            sl = slice(HG_BLK * j, HG_BLK * (j + 1))
            upd = _dot_tn(vb[sl, :], gk_scr[d, sl, :]) * hmask_ref[...]
            st = st_scr[d]
            o_scr[d, pl.ds(r0 + HG_BLK * j, HG_BLK), :] = oi_scr[d, sl, :] + _dot_nt(gq_scr[d, sl, :], st.astype(BF16))
            st_scr[d] = st * dec_scr[d, HG_BLK * j:HG_BLK * j + 1, :] + upd

    st_scr[...] = s0_ref[...]

    def body(i, carry):
        superblock(i, 0)
        superblock(nsb - 1 - i, 1)
        return carry

    lax.fori_loop(0, nsb, body, 0)

    def finish(i, carry):
        rows = pl.ds(pl.multiple_of(i * HG_SB, HG_SB), HG_SB)
        o = o_scr[0, rows, :] + o_scr[1, rows, :]
        ms = _dot_hilo(o * o, e_ref[...]) * (1.0 / HEAD_D)
        pg = p_ref[rows, 4 * GROUP_W:5 * GROUP_W]
        out_ref[rows, :] = (o * lax.rsqrt(ms + EPS) * ng_ref[...] * _silu(pg)).astype(out_ref.dtype)
        return carry

    lax.fori_loop(0, nsb, finish, 0)
    if emit_state:
        ri = lax.broadcasted_iota(jnp.int32, (GROUP_W, GROUP_W), 0)
        ci = lax.broadcasted_iota(jnp.int32, (GROUP_W, GROUP_W), 1)
        eye = jnp.where(ri == ci, 1.0, 0.0).astype(BF16)
        for d in range(2):
            hi, mid, lo = _split3(st_scr[d])
            s_t = _dot_tn(hi, eye) + _dot_tn(mid, eye) + _dot_tn(lo, eye)
            for h in range(HEADS):
                st_ref[d, h] = s_t[HEAD_D * h:HEAD_D * (h + 1), HEAD_D * h:HEAD_D * (h + 1)]


def _hgrn_kernel_old(p_ref, s0_ref, lb_ref, ng_ref, tri_ref, bones_ref, e_ref, hmask_ref,
                 out_ref, *rest, seq, emit_state):
    if emit_state:
        st_ref, of_scr, st_scr, q_scr, k_scr, bc_scr, gq_scr, gk_scr, dec_scr = rest
    else:
        of_scr, st_scr, q_scr, k_scr, bc_scr, gq_scr, gk_scr, dec_scr = rest
    nsb = seq // HG_SB
    nblk = HG_SB // HG_BLK
    log_lb = lb_ref[0:1, :]
    log_1mlb = lb_ref[1:2, :]
    one_m_lb = lb_ref[2:3, :]
    row = lax.broadcasted_iota(jnp.int32, (HG_BLK, 1), 0)

    def superblock(sb, d, finalize):
        r0 = pl.multiple_of(sb * HG_SB, HG_SB)
        rows = pl.ds(r0, HG_SB)
        pq = p_ref[rows, 0:GROUP_W]
        pf = p_ref[rows, (1 + d) * GROUP_W:(2 + d) * GROUP_W]
        q_scr[...] = _silu(pq)
        b2 = log_1mlb + _log_sigmoid(pf)
        mx = jnp.maximum(log_lb, b2)
        mn = jnp.minimum(log_lb, b2)
        delta = jnp.where(mn == NEG_INF, NEG_INF, mn - mx)
        lf = mx + jnp.log1p(jnp.exp(delta))
        k_scr[...] = one_m_lb * _sigmoid(-pf)
        bc = _dot_sel_l(tri_ref[d], lf)
        bt = _dot_sel_l(bones_ref[...], lf)
        bc_scr[...] = bc
        gq_scr[...] = (q_scr[...] * jnp.exp(bc)).astype(BF16)
        gk_scr[...] = (k_scr[...] * jnp.exp(bt - bc)).astype(BF16)
        dec_scr[...] = jnp.exp(bt)
        o_blocks = [None] * nblk
        order = range(nblk) if d == 0 else range(nblk - 1, -1, -1)
        for j in order:
            sl = slice(HG_BLK * j, HG_BLK * (j + 1))
            qj = q_scr[sl, :]
            kj = k_scr[sl, :]
            vj = p_ref[pl.ds(r0 + HG_BLK * j, HG_BLK), 3 * GROUP_W:4 * GROUP_W]
            bcj = bc_scr[sl, :]
            parts = []
            for s in range(HG_BLK):
                diff = bcj - bcj[s:s + 1, :]
                keep = (row >= s) if d == 0 else (row <= s)
                dd = jnp.exp(jnp.where(keep, diff, NEG_INF))
                parts.append((qj * dd * kj[s:s + 1, :]).astype(BF16))
            r_all = _dot(jnp.concatenate(parts, axis=0), e_ref[...])
            o = _dot_nt(gq_scr[sl, :], st_scr[...].astype(BF16))
            for s in range(HG_BLK):
                o = o + r_all[HG_BLK * s:HG_BLK * (s + 1), :] * vj[s:s + 1, :]
            upd = _dot_tn(vj.astype(BF16), gk_scr[sl, :])
            st_scr[...] = st_scr[...] * dec_scr[HG_BLK * j:HG_BLK * j + 1, :] + upd * hmask_ref[...]
            o_blocks[j] = o
        o_sb = jnp.concatenate(o_blocks, axis=0)
        if finalize:
            o_sb = o_sb + of_scr[rows, :]
            ms = _dot_sel_r(o_sb * o_sb, e_ref[...]) * (1.0 / HEAD_D)
            pg = p_ref[rows, 4 * GROUP_W:5 * GROUP_W]
            y = o_sb * lax.rsqrt(ms + EPS) * ng_ref[...] * _silu(pg)
            out_ref[rows, :] = y.astype(out_ref.dtype)
        else:
            of_scr[rows, :] = o_sb

    st_scr[...] = s0_ref[0]

    def fwd(i, carry):
        superblock(i, 0, False)
        return carry

    lax.fori_loop(0, nsb, fwd, 0)
    if emit_state:
        st_ref[0] = st_scr[...]
    st_scr[...] = s0_ref[1]

    def bwd(i, carry):
        superblock(nsb - 1 - i, 1, True)
        return carry

    lax.fori_loop(0, nsb, bwd, 0)
    if emit_state:
        st_ref[1] = st_scr[...]


def _hgrn(p3, s0_bd, lb_rows, norm_g, emit_state):
    b, seq, _ = p3.shape
    tri, bones, head, head4, pairs = _hgrn_consts()
    const2 = lambda i: (0, 0)
    const3 = lambda i: (0, 0, 0)
    out_shape = [jax.ShapeDtypeStruct((b, seq, GROUP_W), BF16)]
    out_specs = [pl.BlockSpec((None, seq, GROUP_W), lambda i: (i, 0, 0))]
    if emit_state:
        out_shape.append(jax.ShapeDtypeStruct((b, 2, HEADS, HEAD_D, HEAD_D), F32))
        out_specs.append(pl.BlockSpec((None, 2, HEADS, HEAD_D, HEAD_D), lambda i: (i, 0, 0, 0, 0)))
    sb_f32 = pltpu.VMEM((2, HG_SB, GROUP_W), F32)
    sb_bf16 = pltpu.VMEM((2, HG_SB, GROUP_W), BF16)
    res = pl.pallas_call(
        functools.partial(_hgrn_kernel, seq=seq, emit_state=emit_state),
        grid=(b,),
        in_specs=[
            pl.BlockSpec((None, seq, 5 * GROUP_W), lambda i: (i, 0, 0)),
            pl.BlockSpec((None, 2, GROUP_W, GROUP_W), lambda i: (i, 0, 0, 0)),
            pl.BlockSpec((8, GROUP_W), const2),
            pl.BlockSpec((1, GROUP_W), const2),
            pl.BlockSpec((2, HG_SB, HG_SB), const3),
            pl.BlockSpec((HG_SB, HG_SB), const2),
            pl.BlockSpec((GROUP_W, GROUP_W), const2),
            pl.BlockSpec((GROUP_W, GROUP_W), const2),
            pl.BlockSpec((HEADS * HG_SB, GROUP_W), const2),
            pl.BlockSpec((2, HG_SB, HEADS * HG_SB), const3),
        ],
        out_specs=out_specs,
        out_shape=out_shape,
        scratch_shapes=[
            pltpu.VMEM((2, seq, GROUP_W), F32),
            pltpu.VMEM((2, GROUP_W, GROUP_W), F32),
            sb_f32, sb_f32, sb_f32, sb_bf16, sb_bf16, sb_f32, sb_f32,
        ],
        name="hgrn2",
    )(p3, s0_bd, lb_rows, norm_g, jnp.asarray(tri, BF16), jnp.asarray(bones, BF16),
      jnp.asarray(head, BF16), jnp.asarray(head, F32), jnp.asarray(head4, BF16), jnp.asarray(pairs, F32))
    return res if emit_state else (res[0], None)


@functools.lru_cache(maxsize=None)
def _mlstm_consts():
    r = np.arange(ML_CHUNK)
    tri_f = (r[:, None] >= r[None, :]).astype(np.float32)
    tri = np.stack([tri_f, tri_f.T])
    c = np.arange(GROUP_W)
    head = (c[:, None] // HEAD_D) == (c[None, :] // HEAD_D)
    pos = c % HEAD_D
    row_f = head & (pos[:, None] <= pos[None, :])
    row_b = head & (pos[:, None] >= pos[None, :])
    rowtri = np.stack([row_f, row_b]).astype(np.float32)
    expand = np.zeros((2, LANES, GROUP_W), np.float32)
    for d in range(2):
        for h in range(HEADS):
            expand[d, 2 * HEADS + d * HEADS + h, h * HEAD_D:(h + 1) * HEAD_D] = 1.0
    return tri, rowtri, expand, head.astype(np.float32)


def _dot_hilo(x, m):
    hi = x.astype(BF16)
    lo = (x - hi.astype(F32)).astype(BF16)
    return _dot(hi, m) + _dot(lo, m)


def _mlstm_kernel(p_ref, g_ref, gr_ref, conv_ref, brow_ref, brows_ref, ng_ref, c0_ref, n0_ref, m0_ref,
                  tri_ref, rowtri_ref, exp_ref, e_ref, out_ref, *rest, seq, conv_w, emit_state):
    if emit_state:
        cst_ref, nst_ref, mst_ref = rest[:3]
        rest = rest[3:]
    (q_scr, k_scr, col_scr, mloc_scr, gb_scr, h_scr, row_scr, blast_scr, gmax_scr,
     c_scr, n_scr, m_scr) = rest
    nch = seq // ML_CHUNK
    rowid = lax.broadcasted_iota(jnp.int32, (ML_CHUNK, 1), 0)
    ti = lax.broadcasted_iota(jnp.int32, (ML_CHUNK, GROUP_W), 0)
    si = lax.broadcasted_iota(jnp.int32, (ML_CHUNK, GROUP_W), 1) % HEAD_D

    for d in range(2):
        li_row = gr_ref[d] + brows_ref[d:d + 1, :]
        lf_row = _log_sigmoid(gr_ref[2 + d] + brows_ref[2 + d:3 + d, :])
        row_scr[d] = li_row - _dot_sel_r(lf_row, rowtri_ref[d])

    def pre_chunk(ci, carry):
        r0 = pl.multiple_of(ci * ML_CHUNK, ML_CHUNK)
        rows = pl.ds(r0, ML_CHUNK)
        zc = p_ref[rows, 0:2 * GROUP_W]
        zp = pltpu.roll(zc, 1, axis=0)
        zn = pltpu.roll(zc, ML_CHUNK - 1, axis=0)
        if conv_w == ML_CHUNK:
            prev_row = jnp.zeros((1, 2 * GROUP_W), F32)
            next_row = prev_row
        else:
            prev_row = p_ref[pl.ds(jnp.maximum(r0 - 1, 0), 1), 0:2 * GROUP_W]
            prev_row = jnp.where(r0 % conv_w == 0, 0.0, prev_row)
            next_row = p_ref[pl.ds(jnp.minimum(r0 + ML_CHUNK, seq - 1), 1), 0:2 * GROUP_W]
            next_row = jnp.where((r0 + ML_CHUNK) % conv_w == 0, 0.0, next_row)
        zp = jnp.where(rowid == 0, prev_row, zp)
        zn = jnp.where(rowid == ML_CHUNK - 1, next_row, zn)
        qk = _silu(conv_ref[0:1, :] * zp + conv_ref[1:2, :] * zc + conv_ref[2:3, :] * zn)
        q_scr[rows, :] = qk[:, 0:GROUP_W].astype(BF16)
        k_scr[rows, :] = qk[:, GROUP_W:2 * GROUP_W] * (HEAD_D ** -0.5)
        ge = g_ref[rows, :] + brow_ref[...]
        lane = lax.broadcasted_iota(jnp.int32, (1, LANES), 1)
        bwd = lane >= 3 * HEADS
        lf = _log_sigmoid(ge)
        bc = jnp.where(bwd, _dot_sel_l(tri_ref[1], lf), _dot_sel_l(tri_ref[0], lf))
        li = pltpu.roll(ge, 2 * HEADS, axis=1)
        pre = li - bc
        suf = pre
        for sh in (1, 2, 4, 8, 16, 32):
            pre = jnp.maximum(pre, jnp.where(rowid >= sh, pltpu.roll(pre, sh, axis=0), NEG_INF))
            suf = jnp.maximum(suf, jnp.where(rowid < ML_CHUNK - sh, pltpu.roll(suf, ML_CHUNK - sh, axis=0), NEG_INF))
        blast = jnp.where(bwd, bc[0:1, :], bc[ML_CHUNK - 1:ML_CHUNK, :])
        gb = blast - bc + li
        edge_rows = jnp.concatenate([blast, jnp.max(gb, axis=0, keepdims=True), jnp.zeros((6, LANES), F32)], axis=0)
        stacked = jnp.concatenate([bc, bc + jnp.where(bwd, suf, pre), gb, edge_rows], axis=0)
        for d in range(2):
            ex = _dot_sel_r(stacked, exp_ref[d])
            col_scr[d, rows, :] = ex[0:ML_CHUNK, :]
            mloc_scr[d, rows, :] = ex[ML_CHUNK:2 * ML_CHUNK, :]
            gb_scr[d, rows, :] = ex[2 * ML_CHUNK:3 * ML_CHUNK, :]
            blast_scr[d, pl.ds(ci, 1), :] = ex[3 * ML_CHUNK:3 * ML_CHUNK + 1, :]
            gmax_scr[d, pl.ds(ci, 1), :] = ex[3 * ML_CHUNK + 1:3 * ML_CHUNK + 2, :]
        return carry

    lax.fori_loop(0, nch, pre_chunk, 0)

    for d in range(2):
        c_scr[d] = jnp.zeros((GROUP_W, GROUP_W), F32)
        for h in range(HEADS):
            lo = HEAD_D * h
            c_scr[d, lo:lo + HEAD_D, lo:lo + HEAD_D] = c0_ref[d, h]
    n_scr[...] = n0_ref[...]
    m_scr[...] = m0_ref[...]

    def chunk(ci, d):
        r0 = pl.multiple_of(ci * ML_CHUNK, ML_CHUNK)
        rows = pl.ds(r0, ML_CHUNK)
        qb = q_scr[rows, :]
        kc = k_scr[rows, :]
        vb = p_ref[rows, 2 * GROUP_W:3 * GROUP_W].astype(BF16)
        hm = e_ref[...]
        kbd = jnp.concatenate([kc.astype(BF16)] * HEADS, axis=0) * hm
        vbd = jnp.concatenate([vb] * HEADS, axis=0) * hm
        s = _dot_nt(qb, kbd)
        colb = col_scr[d, rows, :]
        m_b = m_scr[d]
        prev = colb + m_b
        m_t = jnp.maximum(prev, mloc_scr[d, rows, :])
        keep = (ti >= si) if d == 0 else (ti <= si)
        w = jnp.exp(jnp.where(keep, colb + row_scr[d, pl.ds(ci, 1), :], NEG_INF) - m_t) * s
        wp = jnp.exp(prev - m_t)
        cst = c_scr[d]
        nrow = n_scr[d]
        num = wp * _dot(qb, cst.astype(BF16)) + _dot(w.astype(BF16), vbd)
        den = _dot_hilo(wp * qb.astype(F32) * nrow + w, e_ref[...])
        h_scr[d, rows, :] = num / jnp.maximum(jnp.abs(den), jnp.exp(-m_t))
        blast = blast_scr[d, pl.ds(ci, 1), :]
        m_new = jnp.maximum(blast + m_b, gmax_scr[d, pl.ds(ci, 1), :])
        dec = jnp.exp(blast + m_b - m_new)
        kw = kc * jnp.exp(gb_scr[d, rows, :] - m_new)
        c_scr[d] = cst * dec + _dot_tn(kw.astype(BF16), vb) * hm.astype(F32)
        n_scr[d] = nrow * dec + jnp.sum(kw, axis=0, keepdims=True)
        m_scr[d] = m_new

    def body(i, carry):
        chunk(i, 0)
        chunk(nch - 1 - i, 1)
        return carry

    lax.fori_loop(0, nch, body, 0, unroll=2)

    def fin_chunk(ci, carry):
        rows = pl.ds(pl.multiple_of(ci * ML_CHUNK, ML_CHUNK), ML_CHUNK)
        hs = h_scr[0, rows, :] + h_scr[1, rows, :]
        ms = _dot_hilo(hs * hs, e_ref[...]) * (1.0 / HEAD_D)
        po = p_ref[rows, 3 * GROUP_W:4 * GROUP_W]
        out_ref[rows, :] = (hs * lax.rsqrt(ms + EPS) * ng_ref[...] * _sigmoid(po)).astype(out_ref.dtype)
        return carry

    lax.fori_loop(0, nch, fin_chunk, 0)
    if emit_state:
        for d in range(2):
            for h in range(HEADS):
                lo = HEAD_D * h
                cst_ref[d, h] = c_scr[d, lo:lo + HEAD_D, lo:lo + HEAD_D]
        nst_ref[...] = n_scr[...]
        mst_ref[...] = m_scr[...]


def _mlstm_kernel_old(p_ref, g_ref, gt_ref, conv_ref, brow_ref, bcol_ref, ng_ref,
                  c0_ref, n0_ref, m0_ref, tri_ref, out_ref, *rest, seq, conv_w, emit_state):
    if emit_state:
        cst_ref, nst_ref, mst_ref, qk_scr, hf_scr, c_scr, n_scr, m_scr = rest
    else:
        qk_scr, hf_scr, c_scr, n_scr, m_scr = rest
    nch = seq // ML_CHUNK
    rowid = lax.broadcasted_iota(jnp.int32, (ML_CHUNK, 1), 0)
    ti = lax.broadcasted_iota(jnp.int32, (ML_CHUNK, ML_CHUNK), 0)
    si = lax.broadcasted_iota(jnp.int32, (ML_CHUNK, ML_CHUNK), 1)

    def conv_chunk(ci, carry):
        r0 = pl.multiple_of(ci * ML_CHUNK, ML_CHUNK)
        zc = p_ref[pl.ds(r0, ML_CHUNK), 0:2 * GROUP_W]
        zp = pltpu.roll(zc, 1, axis=0)
        zn = pltpu.roll(zc, ML_CHUNK - 1, axis=0)
        if conv_w == ML_CHUNK:
            prev_row = jnp.zeros((1, 2 * GROUP_W), F32)
            next_row = prev_row
        else:
            prev_row = p_ref[pl.ds(jnp.maximum(r0 - 1, 0), 1), 0:2 * GROUP_W]
            prev_row = jnp.where(r0 % conv_w == 0, 0.0, prev_row)
            next_row = p_ref[pl.ds(jnp.minimum(r0 + ML_CHUNK, seq - 1), 1), 0:2 * GROUP_W]
            next_row = jnp.where((r0 + ML_CHUNK) % conv_w == 0, 0.0, next_row)
        zp = jnp.where(rowid == 0, prev_row, zp)
        zn = jnp.where(rowid == ML_CHUNK - 1, next_row, zn)
        out = conv_ref[0:1, :] * zp + conv_ref[1:2, :] * zc + conv_ref[2:3, :] * zn
        qk_scr[pl.ds(r0, ML_CHUNK), :] = _silu(out)
        return carry

    lax.fori_loop(0, nch, conv_chunk, 0)

    def chunk(ci, d, finalize):
        r0 = pl.multiple_of(ci * ML_CHUNK, ML_CHUNK)
        rows = pl.ds(r0, ML_CHUNK)
        g = g_ref[rows, :] + brow_ref[...]
        gt = gt_ref[ci] + bcol_ref[...]
        li = g[:, 0:2 * HEADS]
        lf = _log_sigmoid(g[:, 2 * HEADS:4 * HEADS])
        lit = gt[0:2 * HEADS, :]
        lft = _log_sigmoid(gt[2 * HEADS:4 * HEADS, :])
        bc_all = _dot_sel_l(tri_ref[d], lf)
        bct_all = _dot_sel_r(lft, tri_ref[1 - d])
        keep = (ti >= si) if d == 0 else (ti <= si)
        edge = ML_CHUNK - 1 if d == 0 else 0
        outs = []
        for h in range(HEADS):
            c = HEADS * d + h
            lo = HEAD_D * h
            q = qk_scr[rows, lo:lo + HEAD_D]
            k = qk_scr[rows, GROUP_W + lo:GROUP_W + lo + HEAD_D] * (HEAD_D ** -0.5)
            v = p_ref[rows, 2 * GROUP_W + lo:2 * GROUP_W + lo + HEAD_D]
            qb = q.astype(BF16)
            vb = v.astype(BF16)
            bcol = bc_all[:, c:c + 1]
            brow = bct_all[c:c + 1, :]
            m = m_scr[h]
            dmat = jnp.where(keep, bcol - brow + lit[c:c + 1, :], NEG_INF)
            prev = bcol + m
            m_t = jnp.maximum(prev, jnp.max(dmat, axis=1, keepdims=True))
            w = jnp.exp(dmat - m_t) * _dot_nt(qb, k.astype(BF16))
            wp = jnp.exp(prev - m_t)
            cst = c_scr[h]
            nrow = n_scr[h]
            num = wp * _dot(qb, cst.astype(BF16)) + _dot(w.astype(BF16), vb)
            den = wp * jnp.sum(q * nrow, axis=1, keepdims=True) + jnp.sum(w, axis=1, keepdims=True)
            hh = num / jnp.maximum(jnp.abs(den), jnp.exp(-m_t))
            blast = bcol[edge:edge + 1, :]
            gc = blast - bcol + li[:, c:c + 1]
            m_new = jnp.maximum(blast + m, jnp.max(gc, axis=0, keepdims=True))
            dec = jnp.exp(blast + m - m_new)
            kw = k * jnp.exp(gc - m_new)
            c_scr[h] = dec * cst + _dot_tn(kw.astype(BF16), vb)
            n_scr[h] = dec * nrow + jnp.sum(kw, axis=0, keepdims=True)
            m_scr[h] = m_new
            if finalize:
                hs = hh + hf_scr[rows, lo:lo + HEAD_D]
                hs = hs * lax.rsqrt(jnp.mean(hs * hs, axis=-1, keepdims=True) + EPS)
                outs.append(hs)
            else:
                outs.append(hh)
        o = jnp.concatenate(outs, axis=1)
        if finalize:
            po = p_ref[rows, 3 * GROUP_W:4 * GROUP_W]
            out_ref[rows, :] = (o * ng_ref[...] * _sigmoid(po)).astype(out_ref.dtype)
        else:
            hf_scr[rows, :] = o

    def run(d):
        c_scr[...] = c0_ref[d]
        n_scr[...] = n0_ref[d]
        m_scr[...] = m0_ref[d]

        def body(i, carry):
            chunk(i if d == 0 else nch - 1 - i, d, d == 1)
            return carry

        lax.fori_loop(0, nch, body, 0)
        if emit_state:
            cst_ref[d] = c_scr[...]
            nst_ref[d] = n_scr[...]
            mst_ref[d] = m_scr[...]

    run(0)
    run(1)


def _mlstm(p3, g3, gr4, conv_w3, brow, brows, norm_g, c0, n0, m0, conv_width, emit_state):
    b, seq, _ = p3.shape
    nch = seq // ML_CHUNK
    tri, rowtri, expand, head = _mlstm_consts()
    const2 = lambda i: (0, 0)
    const3 = lambda i: (0, 0, 0)
    out_shape = [jax.ShapeDtypeStruct((b, seq, GROUP_W), BF16)]
    out_specs = [pl.BlockSpec((None, seq, GROUP_W), lambda i: (i, 0, 0))]
    c_spec = pl.BlockSpec((None, 2, HEADS, HEAD_D, HEAD_D), lambda i: (i, 0, 0, 0, 0))
    row_spec = pl.BlockSpec((None, 2, 1, GROUP_W), lambda i: (i, 0, 0, 0))
    if emit_state:
        out_shape += [jax.ShapeDtypeStruct((b, 2, HEADS, HEAD_D, HEAD_D), F32),
                      jax.ShapeDtypeStruct((b, 2, 1, GROUP_W), F32), jax.ShapeDtypeStruct((b, 2, 1, GROUP_W), F32)]
        out_specs += [c_spec, row_spec, row_spec]
    dir_seq = (2, seq, GROUP_W)
    dir_chunks = (2, nch, GROUP_W)
    res = pl.pallas_call(
        functools.partial(_mlstm_kernel, seq=seq, conv_w=conv_width, emit_state=emit_state),
        grid=(b,),
        in_specs=[
            pl.BlockSpec((None, seq, 4 * GROUP_W), lambda i: (i, 0, 0)),
            pl.BlockSpec((None, seq, LANES), lambda i: (i, 0, 0)),
            pl.BlockSpec((None, 4, nch, GROUP_W), lambda i: (i, 0, 0, 0)),
            pl.BlockSpec((3, 2 * GROUP_W), const2),
            pl.BlockSpec((1, LANES), const2),
            pl.BlockSpec((4, GROUP_W), const2),
            pl.BlockSpec((1, GROUP_W), const2),
            c_spec, row_spec, row_spec,
            pl.BlockSpec((2, ML_CHUNK, ML_CHUNK), const3),
            pl.BlockSpec((2, GROUP_W, GROUP_W), const3),
            pl.BlockSpec((2, LANES, GROUP_W), const3),
            pl.BlockSpec((GROUP_W, GROUP_W), const2),
        ],
        out_specs=out_specs,
        out_shape=out_shape,
        scratch_shapes=[
            pltpu.VMEM((seq, GROUP_W), BF16),
            pltpu.VMEM((seq, GROUP_W), F32),
            pltpu.VMEM(dir_seq, F32), pltpu.VMEM(dir_seq, F32), pltpu.VMEM(dir_seq, F32), pltpu.VMEM(dir_seq, F32),
            pltpu.VMEM(dir_chunks, F32), pltpu.VMEM(dir_chunks, F32), pltpu.VMEM(dir_chunks, F32),
            pltpu.VMEM((2, GROUP_W, GROUP_W), F32),
            pltpu.VMEM((2, 1, GROUP_W), F32),
            pltpu.VMEM((2, 1, GROUP_W), F32),
        ],
        name="mlstm",
    )(p3, g3, gr4, conv_w3, brow, brows, norm_g, c0, n0, m0, jnp.asarray(tri, BF16), jnp.asarray(rowtri, BF16),
      jnp.asarray(expand, BF16), jnp.asarray(head, BF16))
    if emit_state:
        return res
    return res[0], None, None, None


def _s5_kernel(*refs, nb, reverse, final):
    if final:
        (u_ref, hre0_ref, him0_ref, are_ref, aim_ref, wb_ref, wc_ref, yin_ref, d_ref, gw_ref, gb_ref,
         y_ref, sre_ref, sim_ref, utm_scr, bu_scr, ytm_scr, hre_scr, him_scr) = refs
    else:
        (u_ref, hre0_ref, him0_ref, are_ref, aim_ref, wb_ref, wc_ref,
         y_ref, sre_ref, sim_ref, utm_scr, bu_scr, ytm_scr, hre_scr, him_scr) = refs
    tc = S5_TC
    step_id = pl.program_id(0)

    @pl.when(step_id == 0)
    def _():
        hre_scr[...] = hre0_ref[...]
        him_scr[...] = him0_ref[...]

    halves = GROUP_W // LANES
    for b in range(nb):
        for hv in range(halves):
            utm_scr[hv, pl.ds(b, tc, stride=nb), :] = u_ref[b, :, hv * LANES:(hv + 1) * LANES]
    u_tm = jnp.concatenate([utm_scr[hv] for hv in range(halves)], axis=1)
    bu_scr[...] = _dot(u_tm.astype(BF16), wb_ref[...])
    are = jnp.broadcast_to(are_ref[...], (nb, S5_STATE))
    aim = jnp.broadcast_to(aim_ref[...], (nb, S5_STATE))

    def step(i, carry):
        hre, him = carry
        t = (tc - 1 - i) if reverse else i
        r0 = pl.multiple_of(t * nb, nb)
        bu = bu_scr[pl.ds(r0, nb), :]
        nre = are * hre - aim * him + bu[:, 0:S5_STATE]
        nim = are * him + aim * hre + bu[:, S5_STATE:2 * S5_STATE]
        bu_scr[pl.ds(r0, nb), 0:S5_STATE] = nre
        bu_scr[pl.ds(r0, nb), S5_STATE:2 * S5_STATE] = nim
        return nre, nim

    hre, him = lax.fori_loop(0, tc, step, (hre_scr[...], him_scr[...]))
    hre_scr[...] = hre
    him_scr[...] = him
    y_tm = _dot(bu_scr[...].astype(BF16), wc_ref[...])
    for hv in range(halves):
        ytm_scr[hv] = y_tm[:, hv * LANES:(hv + 1) * LANES]
    for b in range(nb):
        yb = jnp.concatenate([ytm_scr[hv, pl.ds(b, tc, stride=nb), :] for hv in range(halves)], axis=1)
        if final:
            y = yb + yin_ref[b] + d_ref[...] * u_ref[b]
            z = 0.5 * y * (1.0 + jnp.tanh(math.sqrt(2.0 / math.pi) * (y + 0.044715 * (y * y * y))))
            gate = _sigmoid(_dot(z.astype(BF16), gw_ref[...]) + gb_ref[...])
            y_ref[b] = (z * gate).astype(y_ref.dtype)
        else:
            y_ref[b] = yb

    @pl.when(step_id == pl.num_programs(0) - 1)
    def _():
        sre_ref[...] = hre
        sim_ref[...] = him


def _s5_pass(u3, hre0, him0, are, aim, wb, wc, reverse, extra):
    nb, seq, _ = u3.shape
    nch = seq // S5_TC
    final = extra is not None
    chunk = (lambda i: (0, nch - 1 - i, 0)) if reverse else (lambda i: (0, i, 0))
    const2 = lambda i: (0, 0)
    in_specs = [
        pl.BlockSpec((nb, S5_TC, GROUP_W), chunk),
        pl.BlockSpec((nb, S5_STATE), const2),
        pl.BlockSpec((nb, S5_STATE), const2),
        pl.BlockSpec((1, S5_STATE), const2),
        pl.BlockSpec((1, S5_STATE), const2),
        pl.BlockSpec((GROUP_W, 2 * S5_STATE), const2),
        pl.BlockSpec((2 * S5_STATE, GROUP_W), const2),
    ]
    args = [u3, hre0, him0, are, aim, wb, wc]
    if final:
        yin, dvec, gw, gb = extra
        in_specs += [pl.BlockSpec((nb, S5_TC, GROUP_W), chunk), pl.BlockSpec((1, GROUP_W), const2),
                     pl.BlockSpec((GROUP_W, GROUP_W), const2), pl.BlockSpec((1, GROUP_W), const2)]
        args += [yin, dvec, gw, gb]
    return pl.pallas_call(
        functools.partial(_s5_kernel, nb=nb, reverse=reverse, final=final),
        grid=(nch,),
        in_specs=in_specs,
        out_specs=[pl.BlockSpec((nb, S5_TC, GROUP_W), chunk),
                   pl.BlockSpec((nb, S5_STATE), const2), pl.BlockSpec((nb, S5_STATE), const2)],
        out_shape=[jax.ShapeDtypeStruct((nb, seq, GROUP_W), BF16 if final else F32),
                   jax.ShapeDtypeStruct((nb, S5_STATE), F32), jax.ShapeDtypeStruct((nb, S5_STATE), F32)],
        scratch_shapes=[
            pltpu.VMEM((GROUP_W // LANES, nb * S5_TC, LANES), F32),
            pltpu.VMEM((nb * S5_TC, 2 * S5_STATE), F32),
            pltpu.VMEM((GROUP_W // LANES, nb * S5_TC, LANES), F32),
            pltpu.VMEM((nb, S5_STATE), F32),
            pltpu.VMEM((nb, S5_STATE), F32),
        ],
        name="s5_bwd" if reverse else "s5_fwd",
    )(*args)


def _s5_params(a_re, a_im, log_dt, b_re, b_im, c_re, c_im):
    eye = jnp.eye(S5_GROUPS, dtype=F32)
    dt = jnp.exp(log_dt)[..., None]
    mag = jnp.exp(a_re * dt)
    ab_re, ab_im = mag * jnp.cos(a_im * dt), mag * jnp.sin(a_im * dt)
    den = a_re * a_re + a_im * a_im
    g_re = ((ab_re - 1.0) * a_re + ab_im * a_im) / den
    g_im = (ab_im * a_re - (ab_re - 1.0) * a_im) / den
    bb_re = g_re[..., None] * b_re - g_im[..., None] * b_im
    bb_im = g_re[..., None] * b_im + g_im[..., None] * b_re

    def in_mat(bb):
        return jnp.einsum('dgpc,gh->dgchp', bb, eye).reshape(2, GROUP_W, S5_STATE)

    wb = jnp.concatenate([in_mat(bb_re), in_mat(bb_im)], axis=-1).astype(BF16)

    def out_mat(cc):
        return jnp.einsum('gcp,gh->gphc', cc, eye).reshape(S5_STATE, GROUP_W)

    wc = jnp.concatenate([out_mat(c_re), -out_mat(c_im)], axis=0).astype(BF16)
    return ab_re.reshape(2, 1, S5_STATE), ab_im.reshape(2, 1, S5_STATE), wb, wc


def _s5(u3, h0_re, h0_im, prm, d_vec, glu_w, glu_b):
    ab_re, ab_im, wb, wc = prm
    yf, fre, fim = _s5_pass(u3, h0_re[:, 0], h0_im[:, 0], ab_re[0], ab_im[0], wb[0], wc, False, None)
    out, bre, bim = _s5_pass(u3, h0_re[:, 1], h0_im[:, 1], ab_re[1], ab_im[1], wb[1], wc, True,
                             (yf, d_vec, glu_w, glu_b))
    return out, jnp.stack([fre, bre], axis=1), jnp.stack([fim, bim], axis=1)


@functools.lru_cache(maxsize=None)
def _dft_consts(n):
    idx = np.arange(n, dtype=np.int64)
    ang = (np.pi / n) * ((idx[:, None] * idx[None, :]) % (2 * n)).astype(np.float64)
    cos, sin = np.cos(ang), np.sin(ang)
    sign = np.where(idx % 2 == 0, 1.0, -1.0)
    sin_n = sin.copy()
    sin_n[0, :] = sign
    fwd = np.concatenate([cos, sin_n], axis=0)
    inv_c = cos.T / n
    inv_c[:, 0] = 1.0 / (2 * n)
    inv_s = sin.T / n
    inv_s[:, 0] = sign / (2 * n)
    inv = np.concatenate([inv_c, inv_s], axis=1)
    return fwd.astype(np.float32), inv.astype(np.float32), sign.astype(np.float32)[:, None]


@functools.lru_cache(maxsize=None)
def _hyena_feats(n):
    t = np.linspace(0.0, 1.0, n, dtype=np.float64)[:, None]
    w = (2.0 * np.pi / n) * np.arange(n, dtype=np.float64)[:, None]
    bands = np.linspace(1e-4, HY_BANDS - 1, HY_BANDS, dtype=np.float64)[None, :]
    feats = np.concatenate([t, np.cos(w * bands), -np.sin(w * bands)], axis=-1)
    pad = np.zeros((n, LANES - HY_EMB))
    return np.concatenate([feats, pad], axis=-1).astype(np.float32), t.astype(np.float32)


def _hyfilt_kernel(feat_ref, t_ref, sign_ref, w1_ref, b1_ref, w2_ref, b2_ref, w3_ref, fr_ref, dec_ref,
                   fh_ref, fl_ref, ka_ref, kb_ref, ka2_ref, *, n):
    freq = fr_ref[...]
    hdn = jnp.sin(freq * (_dot3(feat_ref[...], w1_ref[...]) + b1_ref[...]))
    hdn = jnp.sin(freq * (_dot3(hdn, w2_ref[...]) + b2_ref[...]))
    h = _dot3(hdn, w3_ref[...]) * (jnp.exp(-t_ref[...] * jnp.abs(dec_ref[...])) + HY_SHIFT)
    half = HY_ORDER * GROUP_W
    row = lax.broadcasted_iota(jnp.int32, (n, 1), 0)
    h0 = h[:, 0:half]
    h1 = jnp.where(row == 0, 0.0, h[:, half:2 * half])
    norm = jnp.sum(jnp.abs(h0), axis=0, keepdims=True) + jnp.sum(jnp.abs(h1), axis=0, keepdims=True)
    hp = (h0 + h1) / norm
    hm = (h0 - h1) / norm

    def dft(lo, x):
        xh = x.astype(BF16)
        xl = (x - xh.astype(F32)).astype(BF16)
        fh = fh_ref[lo:lo + n, :]
        return _dot(fh, xh) + _dot(fl_ref[lo:lo + n, :], xh) + _dot(fh, xl)

    kc = dft(0, hp)
    ks = dft(n, hm)
    kn = jnp.sum(hp * sign_ref[...], axis=0, keepdims=True)
    ka_ref[...] = kc
    kb_ref[...] = jnp.where(row == 0, 0.0, ks)
    ka2_ref[...] = jnp.where(row == 0, kn, kc)


def _hyena_filter(n, w1p, b1, w2, b2, w3, freq, decay):
    fwd, _, sign = _dft_consts(n)
    feats, t = _hyena_feats(n)
    fwd = jnp.asarray(fwd)
    fh = fwd.astype(BF16)
    fl = (fwd - fh.astype(F32)).astype(BF16)
    half = HY_ORDER * GROUP_W
    return pl.pallas_call(
        functools.partial(_hyfilt_kernel, n=n),
        out_shape=[jax.ShapeDtypeStruct((n, half), F32)] * 3,
        name="hyena_filter",
    )(jnp.asarray(feats), jnp.asarray(t), jnp.asarray(sign), w1p, b1, w2, b2, w3, freq, decay, fh, fl)


def _hyena_kernel(p_ref, cw_ref, ka_ref, kb_ref, ka2_ref, bias_ref, f_ref, g_ref, out_ref,
                  z_scr, x_scr, zf_scr, *, seq, conv_w):
    nch = seq // ML_CHUNK
    rowid = lax.broadcasted_iota(jnp.int32, (ML_CHUNK, 1), 0)

    def conv_chunk(ci, carry):
        r0 = pl.multiple_of(ci * ML_CHUNK, ML_CHUNK)
        zc = p_ref[pl.ds(r0, ML_CHUNK), :]
        zp = pltpu.roll(zc, 1, axis=0)
        zn = pltpu.roll(zc, ML_CHUNK - 1, axis=0)
        if conv_w == ML_CHUNK:
            prev_row = jnp.zeros((1, 3 * GROUP_W), F32)
            next_row = prev_row
        else:
            prev_row = p_ref[pl.ds(jnp.maximum(r0 - 1, 0), 1), :]
            prev_row = jnp.where(r0 % conv_w == 0, 0.0, prev_row)
            next_row = p_ref[pl.ds(jnp.minimum(r0 + ML_CHUNK, seq - 1), 1), :]
            next_row = jnp.where((r0 + ML_CHUNK) % conv_w == 0, 0.0, next_row)
        zp = jnp.where(rowid == 0, prev_row, zp)
        zn = jnp.where(rowid == ML_CHUNK - 1, next_row, zn)
        z_scr[pl.ds(r0, ML_CHUNK), :] = cw_ref[0:1, :] * zp + cw_ref[1:2, :] * zc + cw_ref[2:3, :] * zn
        return carry

    lax.fori_loop(0, nch, conv_chunk, 0)
    y = z_scr[:, 0:GROUP_W]
    for o in range(HY_ORDER):
        cols = slice(o * GROUP_W, (o + 1) * GROUP_W)
        x_scr[...] = _dot(f_ref[...], y.astype(BF16))
        xc = x_scr[0:seq, :]
        xs = x_scr[seq:2 * seq, :]
        kb = kb_ref[:, cols]
        zf_scr[0:seq, :] = (xc * ka_ref[:, cols] - xs * kb).astype(BF16)
        zf_scr[seq:2 * seq, :] = (xc * kb + xs * ka2_ref[:, cols]).astype(BF16)
        conv = _dot(g_ref[...], zf_scr[...])
        gate = z_scr[:, (o + 1) * GROUP_W:(o + 2) * GROUP_W]
        y = gate * (conv + bias_ref[o:o + 1, :] * y)
    out_ref[...] = y.astype(out_ref.dtype)


def _hyena(p3, conv_w3, filt, bias, conv_width):
    b, seq, _ = p3.shape
    ka, kb, ka2 = filt
    fwd, inv, _ = _dft_consts(seq)
    const2 = lambda i: (0, 0)
    half = HY_ORDER * GROUP_W
    return pl.pallas_call(
        functools.partial(_hyena_kernel, seq=seq, conv_w=conv_width),
        grid=(b,),
        in_specs=[
            pl.BlockSpec((None, seq, 3 * GROUP_W), lambda i: (i, 0, 0)),
            pl.BlockSpec((3, 3 * GROUP_W), const2),
            pl.BlockSpec((seq, half), const2),
            pl.BlockSpec((seq, half), const2),
            pl.BlockSpec((seq, half), const2),
            pl.BlockSpec((HY_ORDER, GROUP_W), const2),
            pl.BlockSpec((2 * seq, seq), const2),
            pl.BlockSpec((seq, 2 * seq), const2),
        ],
        out_specs=pl.BlockSpec((None, seq, GROUP_W), lambda i: (i, 0, 0)),
        out_shape=jax.ShapeDtypeStruct((b, seq, GROUP_W), BF16),
        scratch_shapes=[
            pltpu.VMEM((seq, 3 * GROUP_W), F32),
            pltpu.VMEM((2 * seq, GROUP_W), F32),
            pltpu.VMEM((2 * seq, GROUP_W), BF16),
        ],
        name="hyena",
    )(p3, conv_w3, ka, kb, ka2, bias, jnp.asarray(fwd, BF16), jnp.asarray(inv, BF16))


def _outproj_kernel(a_ref, b_ref, c_ref, d_ref, x_ref, g1_ref, sh2_ref, sc2_ref, n2_ref, wo_ref,
                    wr_ref, br_ref, x1_ref, h2_ref, gate_ref):
    acc = _dot(a_ref[...], wo_ref[0:GROUP_W, :])
    acc += _dot(b_ref[...], wo_ref[GROUP_W:2 * GROUP_W, :])
    acc += _dot(c_ref[...], wo_ref[2 * GROUP_W:3 * GROUP_W, :])
    acc += _dot(d_ref[...], wo_ref[3 * GROUP_W:4 * GROUP_W, :])
    x1 = x_ref[...] + g1_ref[0] * acc
    x1_ref[...] = x1
    h2 = _rmsnorm_rows(x1, n2_ref[...]) * (1.0 + sc2_ref[0]) + sh2_ref[0]
    h2_ref[...] = h2.astype(BF16)
    logits = _dot3(h2, wr_ref[...]) + br_ref[...]
    lane = lax.broadcasted_iota(jnp.int32, logits.shape, 1).astype(F32)
    big = float(LANES)
    gl = jnp.where(lane < N_EXPERT_GROUPS, logits, NEG_INF)
    gmax = jnp.max(gl, axis=1, keepdims=True)
    gsel = jnp.min(jnp.where(gl == gmax, lane, big), axis=1, keepdims=True)
    psel = 1.0 / jnp.sum(jnp.exp(gl - gmax), axis=1, keepdims=True)
    lo = N_EXPERT_GROUPS + EXPERTS_PER_GROUP * gsel
    el = jnp.where((lane >= lo) & (lane < lo + EXPERTS_PER_GROUP), logits, NEG_INF)
    v1 = jnp.max(el, axis=1, keepdims=True)
    i1 = jnp.min(jnp.where(el == v1, lane, big), axis=1, keepdims=True)
    el2 = jnp.where(lane == i1, NEG_INF, el)
    v2 = jnp.max(el2, axis=1, keepdims=True)
    i2 = jnp.min(jnp.where(el2 == v2, lane, big), axis=1, keepdims=True)
    e2 = jnp.exp(v2 - v1)
    w1 = psel / (1.0 + e2)
    w2 = psel * e2 / (1.0 + e2)
    gate_ref[...] = (jnp.where(lane == 0.0, gsel, 0.0) + jnp.where(lane == i1 - lo + 1.0, w1, 0.0)
                     + jnp.where(lane == i2 - lo + 1.0, w2, 0.0))


def _outproj(mix, x2d, mod3, row_of_tile, norm_g, w_out, w_r, b_r):
    t = x2d.shape[0]
    tm = TOK_TILE
    const = lambda i: (0, 0)
    tile = lambda i: (i, 0)
    return pl.pallas_call(
        _outproj_kernel,
        grid=(t // tm,),
        in_specs=[pl.BlockSpec((tm, GROUP_W), tile)] * 4 + [
            pl.BlockSpec((tm, D_MODEL), tile),
            _mod_spec(2, row_of_tile),
            _mod_spec(3, row_of_tile),
            _mod_spec(4, row_of_tile),
            pl.BlockSpec((1, D_MODEL), const),
            pl.BlockSpec((D_MODEL, D_MODEL), const),
            pl.BlockSpec((D_MODEL, LANES), const),
            pl.BlockSpec((1, LANES), const),
        ],
        out_specs=[pl.BlockSpec((tm, D_MODEL), tile), pl.BlockSpec((tm, D_MODEL), tile),
                   pl.BlockSpec((tm, LANES), tile)],
        out_shape=[jax.ShapeDtypeStruct((t, D_MODEL), F32), jax.ShapeDtypeStruct((t, D_MODEL), BF16),
                   jax.ShapeDtypeStruct((t, LANES), F32)],
        name="outproj_router",
    )(*mix, x2d, mod3, mod3, mod3, norm_g, w_out, w_r, b_r)


def _tile_positions(gates, tri_ref, extra):
    lane = lax.broadcasted_iota(jnp.int32, gates.shape, 1).astype(F32)
    gsel = gates[:, 0:1]
    member = jnp.where((lane == gsel) & (lane < N_EXPERT_GROUPS), 1.0, 0.0)
    before = _dot(tri_ref[...], member.astype(BF16))
    pos = gsel * MOE_TOK + jnp.sum(member * (before + extra), axis=1, keepdims=True)
    slot = lax.broadcasted_iota(jnp.int32, (MOE_TOK, N_EXPERT_GROUPS * MOE_TOK), 1).astype(F32)
    return jnp.where(slot == pos, 1.0, 0.0).astype(BF16)


def _moe_pack_kernel(off_ref, cnt_ref, tot_ref, h_ref, gate_ref, tri_ref, xg_ref, gg_ref,
                     xs_scr, gs_scr, zx_scr, zg_scr, sem):
    i = pl.program_id(0)
    slot = i % 2
    gates = gate_ref[...]
    lane = lax.broadcasted_iota(jnp.int32, gates.shape, 1).astype(F32)
    member = jnp.where((lane == gates[:, 0:1]) & (lane < N_EXPERT_GROUPS), 1.0, 0.0).astype(BF16)
    grp = lax.broadcasted_iota(jnp.int32, (8, LANES), 0)
    pick = jnp.where(grp == lax.broadcasted_iota(jnp.int32, (8, LANES), 1), 1.0, 0.0).astype(BF16)
    member_t = _dot_nt(pick, member)
    before_t = _dot_nt(member_t.astype(BF16), tri_ref[...])
    gid = lax.broadcasted_iota(jnp.int32, member_t.shape, 0).astype(F32)
    pos_t = jnp.sum(member_t * (gid * MOE_TOK + before_t), axis=0, keepdims=True)
    row_id = lax.broadcasted_iota(jnp.int32, (N_EXPERT_GROUPS * MOE_TOK, MOE_TOK), 0).astype(F32)
    p = jnp.where(row_id == pos_t, 1.0, 0.0).astype(BF16)
    xs_scr[slot] = _dot(p, h_ref[...]).astype(BF16)
    hi, mid, lo = _split3(gates)
    gs_scr[slot] = _dot(p, hi) + _dot(p, mid) + _dot(p, lo)

    def for_each_piece(tile, buf, action):
        for g in range(N_EXPERT_GROUPS):
            off = pl.multiple_of(off_ref[tile * N_EXPERT_GROUPS + g], ROW_ALIGN)
            for j in range(MOE_TOK // MOE_PIECE):
                src = pl.ds(g * MOE_TOK + j * MOE_PIECE, MOE_PIECE)
                dst = pl.ds(off + j * MOE_PIECE, MOE_PIECE)

                @pl.when(cnt_ref[tile * N_EXPERT_GROUPS + g] > j * MOE_PIECE)
                def _():
                    action(pltpu.make_async_copy(xs_scr.at[buf, src], xg_ref.at[g, dst], sem.at[buf, 0, g, j]))
                    action(pltpu.make_async_copy(gs_scr.at[buf, src], gg_ref.at[g, dst], sem.at[buf, 1, g, j]))

    @pl.when(i > 0)
    def _():
        for_each_piece(i - 1, 1 - slot, lambda cp: cp.wait())

    for_each_piece(i, slot, lambda cp: cp.start())

    @pl.when(i == pl.num_programs(0) - 1)
    def _():
        for_each_piece(i, slot, lambda cp: cp.wait())
        zx_scr[...] = jnp.zeros_like(zx_scr)
        zg_scr[...] = jnp.zeros_like(zg_scr)

        def tail_copies(g):
            tot = pl.multiple_of(tot_ref[g], ROW_ALIGN)
            return (pltpu.make_async_copy(zx_scr, xg_ref.at[g, pl.ds(tot, MOE_ROWS)], sem.at[0, 0, g, 0]),
                    pltpu.make_async_copy(zg_scr, gg_ref.at[g, pl.ds(tot, MOE_ROWS)], sem.at[0, 1, g, 0]))

        for g in range(N_EXPERT_GROUPS):
            for cp in tail_copies(g):
                cp.start()
        for g in range(N_EXPERT_GROUPS):
            for cp in tail_copies(g):
                cp.wait()


def _moe_expert_kernel(grp_ref, chk_ref, nact_ref, x_ref, g_ref, wg_ref, wu_ref, wd_ref, y_ref):
    @pl.when(pl.program_id(0) < nact_ref[0])
    def _():
        x = x_ref[...]
        gates = g_ref[...]
        acc = None
        for e in range(EXPERTS_PER_GROUP):
            a = _dot(x, wg_ref[e])
            u = _dot(x, wu_ref[e])
            hid = (_silu(a) * u * gates[:, 1 + e:2 + e]).astype(BF16)
            part = _dot(hid, wd_ref[e])
            acc = part if acc is None else acc + part
        y_ref[...] = acc.astype(y_ref.dtype)


def _moe_unpack_kernel(st_ref, dl_ref, cnt_ref, gate_ref, tri_ref, x1_ref, g2_ref, fn_ref, yg_ref, out_ref,
                       ys_scr, sem, *, final):
    i = pl.program_id(0)
    slot = i % 2

    def for_each_piece(tile, buf, action):
        for g in range(N_EXPERT_GROUPS):
            k = tile * N_EXPERT_GROUPS + g
            start = pl.multiple_of(st_ref[k], ROW_ALIGN)
            for j in range(MOE_TOK // MOE_PIECE):
                lo, hi = j * MOE_PIECE, (j + 1) * MOE_PIECE

                @pl.when((dl_ref[k] < hi) & (dl_ref[k] + cnt_ref[k] > lo))
                def _():
                    action(pltpu.make_async_copy(yg_ref.at[g, pl.ds(start + lo, MOE_PIECE)],
                                                 ys_scr.at[buf, pl.ds(g * MOE_TOK + lo, MOE_PIECE)],
                                                 sem.at[buf, g, j]))

    @pl.when(i == 0)
    def _():
        ys_scr[...] = jnp.zeros_like(ys_scr)
        for_each_piece(0, 0, lambda cp: cp.start())

    @pl.when(i + 1 < pl.num_programs(0))
    def _():
        for_each_piece(i + 1, 1 - slot, lambda cp: cp.start())

    gates = gate_ref[...]
    lane = lax.broadcasted_iota(jnp.int32, (1, LANES), 1)
    delta = jnp.zeros((1, LANES), F32)
    for g in range(N_EXPERT_GROUPS):
        delta = jnp.where(lane == g, dl_ref[i * N_EXPERT_GROUPS + g].astype(F32), delta)
    pt = _tile_positions(gates, tri_ref, delta)
    for_each_piece(i, slot, lambda cp: cp.wait())
    x2 = x1_ref[...] + g2_ref[0] * _dot(pt, ys_scr[slot])
    if final:
        x2 = _rmsnorm_rows(x2, fn_ref[...])
    out_ref[...] = x2


def _moe_schedule(gsel, n_tiles, n_steps):
    ng = N_EXPERT_GROUPS
    member = gsel.reshape(n_tiles, MOE_TOK, 1) == jnp.arange(ng, dtype=jnp.int32)
    cnt = jnp.sum(member.astype(jnp.int32), axis=1)
    padded = (cnt + ROW_ALIGN - 1) // ROW_ALIGN * ROW_ALIGN
    off = jnp.cumsum(padded, axis=0) - padded
    tot = jnp.sum(padded, axis=0)
    n_chunks = jnp.maximum((tot + MOE_ROWS - 1) // MOE_ROWS, 1)
    start = jnp.minimum(off, n_chunks * MOE_ROWS - MOE_TOK)
    ends = jnp.cumsum(n_chunks)
    n_active = ends[-1]
    step = jnp.minimum(jnp.arange(n_steps, dtype=jnp.int32), n_active - 1)
    grp = jnp.sum((step[:, None] >= ends[None, :]).astype(jnp.int32), axis=1)
    chk = step - (ends - n_chunks)[grp]
    i32 = lambda a: a.astype(jnp.int32)
    return (i32(off).reshape(-1), i32(padded).reshape(-1), i32(tot), i32(start).reshape(-1),
            i32(off - start).reshape(-1), i32(grp), i32(chk), i32(n_active).reshape(1))


def _moe(h2, gates, layer, w_gate, w_up, w_down, x1, mod3, row_of_tile, final_g, final):
    t = h2.shape[0]
    ng = N_EXPERT_GROUPS
    n_tiles = t // MOE_TOK
    cap = -(-(t + ROW_ALIGN * n_tiles + MOE_ROWS) // MOE_ROWS) * MOE_ROWS
    n_steps = (t + ROW_ALIGN * n_tiles) // MOE_ROWS + ng
    off, cnt, tot, start, delta, grp, chk, n_active = _moe_schedule(gates[:, 0].astype(jnp.int32), n_tiles,
                                                                    n_steps)
    pieces = MOE_TOK // MOE_PIECE
    r = np.arange(MOE_TOK)
    tri = jnp.asarray((r[:, None] > r[None, :]).astype(np.float32), BF16)
    any_spec = pl.BlockSpec(memory_space=pl.ANY)

    xg, gg = pl.pallas_call(
        _moe_pack_kernel,
        grid_spec=pltpu.PrefetchScalarGridSpec(
            num_scalar_prefetch=3,
            grid=(n_tiles,),
            in_specs=[
                pl.BlockSpec((MOE_TOK, D_MODEL), lambda i, *_: (i, 0)),
                pl.BlockSpec((MOE_TOK, LANES), lambda i, *_: (i, 0)),
                pl.BlockSpec((MOE_TOK, MOE_TOK), lambda i, *_: (0, 0)),
            ],
            out_specs=[any_spec, any_spec],
            scratch_shapes=[
                pltpu.VMEM((2, ng * MOE_TOK, D_MODEL), BF16),
                pltpu.VMEM((2, ng * MOE_TOK, LANES), F32),
                pltpu.VMEM((MOE_ROWS, D_MODEL), BF16),
                pltpu.VMEM((MOE_ROWS, LANES), F32),
                pltpu.SemaphoreType.DMA((2, 2, ng, pieces)),
            ],
        ),
        out_shape=[jax.ShapeDtypeStruct((ng, cap, D_MODEL), BF16), jax.ShapeDtypeStruct((ng, cap, LANES), F32)],
        name="moe_pack",
    )(off, cnt, tot, h2, gates, tri)

    group_w = lambda shape: pl.BlockSpec((None, None, EXPERTS_PER_GROUP) + shape,
                                         lambda k, grp, chk, na: (layer, grp[k], 0, 0, 0))
    rows = lambda width: pl.BlockSpec((None, MOE_ROWS, width), lambda k, grp, chk, na: (grp[k], chk[k], 0))
    yg = pl.pallas_call(
        _moe_expert_kernel,
        grid_spec=pltpu.PrefetchScalarGridSpec(
            num_scalar_prefetch=3,
            grid=(n_steps,),
            in_specs=[rows(D_MODEL), rows(LANES), group_w((D_MODEL, D_EXPERT)), group_w((D_MODEL, D_EXPERT)),
                      group_w((D_EXPERT, D_MODEL))],
            out_specs=rows(D_MODEL),
        ),
        out_shape=jax.ShapeDtypeStruct((ng, cap, D_MODEL), BF16),
        name="moe_experts",
    )(grp, chk, n_active, xg, gg,
      w_gate.reshape(DEPTH, ng, EXPERTS_PER_GROUP, D_MODEL, D_EXPERT),
      w_up.reshape(DEPTH, ng, EXPERTS_PER_GROUP, D_MODEL, D_EXPERT),
      w_down.reshape(DEPTH, ng, EXPERTS_PER_GROUP, D_EXPERT, D_MODEL))

    return pl.pallas_call(
        functools.partial(_moe_unpack_kernel, final=final),
        grid_spec=pltpu.PrefetchScalarGridSpec(
            num_scalar_prefetch=3,
            grid=(n_tiles,),
            in_specs=[
                pl.BlockSpec((MOE_TOK, LANES), lambda i, *_: (i, 0)),
                pl.BlockSpec((MOE_TOK, MOE_TOK), lambda i, *_: (0, 0)),
                pl.BlockSpec((MOE_TOK, D_MODEL), lambda i, *_: (i, 0)),
                pl.BlockSpec((1, 1, D_MODEL), lambda i, *_: (row_of_tile(i), 0, 5)),
                pl.BlockSpec((1, D_MODEL), lambda i, *_: (0, 0)),
                any_spec,
            ],
            out_specs=pl.BlockSpec((MOE_TOK, D_MODEL), lambda i, *_: (i, 0)),
            scratch_shapes=[pltpu.VMEM((2, ng * MOE_TOK, D_MODEL), BF16),
                            pltpu.SemaphoreType.DMA((2, ng, pieces))],
        ),
        out_shape=jax.ShapeDtypeStruct((t, D_MODEL), F32),
        name="moe_unpack",
    )(start, delta, cnt, gates, tri, x1, mod3, final_g, yg)


def _block_diag_t(s):
    eye = jnp.eye(HEADS, dtype=s.dtype)
    b = s.shape[0]
    return jnp.einsum('bdhkv,hg->bdhvgk', s, eye).reshape(b, 2, GROUP_W, GROUP_W)


def _prep_layer(l, w):
    gw = GROUP_W
    w_in = w['w_in'][l]
    w_g = w_in[:, 9 * gw:9 * gw + N_GATES]
    lb_all = jnp.cumsum(jax.nn.softmax(w['hg_lb'].astype(F32), axis=0), axis=0)
    lb = lb_all[l] - lb_all[0]
    lb_rows = jnp.zeros((8, gw), F32).at[0].set(jnp.log(lb)).at[1].set(jnp.log1p(-lb)).at[2].set(1.0 - lb)
    w_r = jnp.zeros((D_MODEL, LANES), F32)
    w_r = w_r.at[:, 0:N_EXPERT_GROUPS].set(w['moe_wg'][l])
    w_r = w_r.at[:, N_EXPERT_GROUPS:N_EXPERT_GROUPS + N_EXPERTS].set(w['moe_we'][l])
    b_r = jnp.zeros((1, LANES), F32)
    b_r = b_r.at[0, 0:N_EXPERT_GROUPS].set(w['moe_bg'][l])
    b_r = b_r.at[0, N_EXPERT_GROUPS:N_EXPERT_GROUPS + N_EXPERTS].set(w['moe_be'][l])
    gate_bias = jnp.concatenate([w['ml_i_bias'][l], w['ml_f_bias'][l]])
    w1p = jnp.zeros((LANES, HY_HIDDEN), F32).at[0:HY_EMB].set(w['hy_w1'][l])
    return dict(
        norm1=w['norm1'][l][None, :], norm2=w['norm2'][l][None, :],
        w_a=w_in[:, :9 * gw].astype(BF16), w_b=w_in[:, 9 * gw + N_GATES:].astype(BF16),
        w_g=jnp.zeros((D_MODEL, LANES), BF16).at[:, 0:N_GATES].set(w_g.astype(BF16)), w_gt=w_g.T.astype(BF16),
        w_out=w['w_out'][l].astype(BF16),
        lb_rows=lb_rows, hg_norm=w['hg_norm'][l][None, :],
        ml_conv=w['ml_conv'][l], ml_brow=jnp.zeros((1, LANES), F32).at[0, 0:N_GATES].set(gate_bias),
        ml_brows=jnp.repeat(gate_bias.reshape(4, HEADS), HEAD_D, axis=1),
        ml_norm=w['ml_norm'][l][None, :],
        s5=_s5_params(w['s5_a_re'][l], w['s5_a_im'][l], w['s5_log_dt'][l], w['s5_b_re'][l], w['s5_b_im'][l],
                      w['s5_c_re'][l], w['s5_c_im'][l]),
        s5_d=w['s5_d'][l][None, :], s5_glu_w=w['s5_glu_w'][l].astype(BF16), s5_glu_b=w['s5_glu_b'][l][None, :],
        hy_conv=w['hy_conv'][l], hy_bias=w['hy_bias'][l],
        hy_mlp=(w1p, w['hy_b1'][l][None, :], w['hy_w2'][l], w['hy_b2'][l][None, :], w['hy_w3'][l],
                w['hy_freq'][l][None, :], w['hy_decay'][l][None, :]),
        w_r=w_r, b_r=b_r,
        layer=l,
    )


def _trunk_layer(x2d, nb, seq, mod3, cond_row, st, lw, moe_w, filt, conv_width, emit_state, final_g, final):
    if cond_row is None:
        row_of_tile = lambda i: i // (seq // TOK_TILE)
    else:
        row_of_tile = lambda i: cond_row
    hg, ml, su, hy, gt, gtt = _inproj(x2d, mod3, row_of_tile, lw['norm1'], lw['w_a'], lw['w_b'], lw['w_g'],
                                       lw['w_gt'])
    nch = seq // ML_CHUNK
    a_out, hg_st = _hgrn(hg.reshape(nb, seq, -1), st['hgrn'], lw['lb_rows'], lw['hg_norm'], emit_state)
    gr4 = jnp.transpose(gtt.reshape(4, HEADS, nb, nch, ML_CHUNK), (2, 0, 3, 1, 4)).reshape(nb, 4, nch, GROUP_W)
    b_out, mlc, mln, mlm = _mlstm(ml.reshape(nb, seq, -1), gt.reshape(nb, seq, LANES), gr4, lw['ml_conv'],
                                  lw['ml_brow'], lw['ml_brows'], lw['ml_norm'], st['ml_c'], st['ml_n'],
                                  st['ml_m'], conv_width, emit_state)
    c_out, s5re, s5im = _s5(su.reshape(nb, seq, -1), st['s5_re'], st['s5_im'], lw['s5'], lw['s5_d'],
                            lw['s5_glu_w'], lw['s5_glu_b'])
    d_out = _hyena(hy.reshape(nb, seq, -1), lw['hy_conv'], filt, lw['hy_bias'], conv_width)
    mix = [m.reshape(nb * seq, GROUP_W) for m in (a_out, b_out, c_out, d_out)]
    x1, h2, gates = _outproj(mix, x2d, mod3, row_of_tile, lw['norm2'], lw['w_out'], lw['w_r'], lw['b_r'])
    x2 = _moe(h2, gates, lw['layer'], *moe_w, x1, mod3, row_of_tile, final_g, final)
    new_st = dict(hgrn=hg_st, ml_c=mlc, ml_n=mln, ml_m=mlm, s5_re=s5re, s5_im=s5im)
    return x2, new_st


def kernel(x_prompt, x_sample, c, state_hgrn, state_mlstm_c, state_mlstm_n, state_mlstm_m, state_s5_re, state_s5_im, c_ctx, w_ada, b_ada, norm1, norm2, w_in, w_out, hg_lb, hg_norm, ml_conv, ml_i_bias, ml_f_bias, ml_norm, s5_a_re, s5_a_im, s5_log_dt, s5_b_re, s5_b_im, s5_c_re, s5_c_im, s5_d, s5_glu_w, s5_glu_b, hy_conv, hy_w1, hy_b1, hy_w2, hy_b2, hy_w3, hy_freq, hy_decay, hy_bias, moe_wg, moe_bg, moe_we, moe_be, moe_w_gate, moe_w_up, moe_w_down, final_norm):
    w = dict(w_in=w_in, w_out=w_out, norm1=norm1, norm2=norm2, hg_lb=hg_lb, hg_norm=hg_norm, ml_conv=ml_conv,
             ml_i_bias=ml_i_bias, ml_f_bias=ml_f_bias, ml_norm=ml_norm, s5_a_re=s5_a_re, s5_a_im=s5_a_im,
             s5_log_dt=s5_log_dt, s5_b_re=s5_b_re, s5_b_im=s5_b_im, s5_c_re=s5_c_re, s5_c_im=s5_c_im,
             s5_d=s5_d, s5_glu_w=s5_glu_w, s5_glu_b=s5_glu_b, hy_conv=hy_conv, hy_w1=hy_w1, hy_b1=hy_b1,
             hy_w2=hy_w2, hy_b2=hy_b2, hy_w3=hy_w3, hy_freq=hy_freq, hy_decay=hy_decay, hy_bias=hy_bias,
             moe_wg=moe_wg, moe_bg=moe_bg, moe_we=moe_we, moe_be=moe_be, moe_w_gate=moe_w_gate,
             moe_w_up=moe_w_up, moe_w_down=moe_w_down)
    bp, lp, _ = x_prompt.shape
    bs, ls, _ = x_sample.shape
    ctx_row = bs
    cond = jnp.zeros((MOD_ROWS, D_MODEL), F32).at[0:bs].set(c).at[ctx_row].set(c_ctx)
    mod = _ada_mod(cond, w_ada, b_ada)
    final_g = final_norm[None, :]

    zero_st = dict(
        hgrn=jnp.zeros((bp, 2, GROUP_W, GROUP_W), F32),
        ml_c=jnp.zeros((bp, 2, HEADS, HEAD_D, HEAD_D), F32),
        ml_n=jnp.zeros((bp, 2, 1, GROUP_W), F32),
        ml_m=jnp.zeros((bp, 2, 1, GROUP_W), F32),
        s5_re=jnp.zeros((bp, 2, S5_STATE), F32),
        s5_im=jnp.zeros((bp, 2, S5_STATE), F32))

    xp = x_prompt.reshape(bp * lp, D_MODEL)
    xs = x_sample.reshape(bs * ls, D_MODEL)
    moe_w = (moe_w_gate.astype(BF16), moe_w_up.astype(BF16), moe_w_down.astype(BF16))
    per_layer = []
    for l in range(DEPTH):
        lw = _prep_layer(l, w)
        mod3 = mod[l].reshape(MOD_ROWS, 1, 6 * D_MODEL)
        last = l == DEPTH - 1
        filt_p = _hyena_filter(lp, *lw['hy_mlp'])
        filt_s = filt_p if ls == lp else _hyena_filter(ls, *lw['hy_mlp'])
        xp, st = _trunk_layer(xp, bp, lp, mod3, ctx_row, zero_st, lw, moe_w, filt_p, lp, True, final_g, last)
        per_layer.append(st)
        st_in = dict(
            hgrn=_block_diag_t(state_hgrn[:, l].astype(F32)),
            ml_c=state_mlstm_c[:, l].astype(F32),
            ml_n=state_mlstm_n[:, l].astype(F32).reshape(bs, 2, 1, GROUP_W),
            ml_m=jnp.repeat(state_mlstm_m[:, l].astype(F32), HEAD_D, axis=-1).reshape(bs, 2, 1, GROUP_W),
            s5_re=state_s5_re[:, l].astype(F32).reshape(bs, 2, S5_STATE),
            s5_im=state_s5_im[:, l].astype(F32).reshape(bs, 2, S5_STATE))
        xs, _ = _trunk_layer(xs, bs, ls, mod3, None, st_in, lw, moe_w, filt_s, GRID_W, False, final_g, last)

    new_hgrn = jnp.stack([s['hgrn'] for s in per_layer], axis=1)
    new_ml_c = jnp.stack([s['ml_c'] for s in per_layer], axis=1)
    new_ml_n = jnp.stack([s['ml_n'].reshape(bp, 2, HEADS, HEAD_D) for s in per_layer], axis=1)
    new_ml_m = jnp.stack([s['ml_m'][:, :, 0, ::HEAD_D] for s in per_layer], axis=1)
    new_s5_re = jnp.stack([s['s5_re'].reshape(bp, 2, S5_GROUPS, S5_P) for s in per_layer], axis=1)
    new_s5_im = jnp.stack([s['s5_im'].reshape(bp, 2, S5_GROUPS, S5_P) for s in per_layer], axis=1)
    return (xp.reshape(bp, lp, D_MODEL), xs.reshape(bs, ls, D_MODEL),
            new_hgrn, new_ml_c, new_ml_n, new_ml_m, new_s5_re, new_s5_im)
```

```python
import functools
import math

import numpy as np
import jax
import jax.numpy as jnp
from jax import lax
from jax.experimental import pallas as pl
from jax.experimental.pallas import tpu as pltpu

F32 = jnp.float32
BF16 = jnp.bfloat16

D_MODEL = 1024
DEPTH = 2
GRID_W = 64
GROUP_W = D_MODEL // 4
HEADS = 4
HEAD_D = GROUP_W // HEADS
S5_CH = 16
S5_GROUPS = GROUP_W // S5_CH
S5_P = 64
S5_STATE = S5_GROUPS * S5_P
HY_ORDER = 2
HY_BANDS = 16
HY_EMB = 1 + 2 * HY_BANDS
HY_HIDDEN = 64
HY_SHIFT = 0.05
N_EXPERT_GROUPS = 4
EXPERTS_PER_GROUP = 4
N_EXPERTS = N_EXPERT_GROUPS * EXPERTS_PER_GROUP
D_EXPERT = D_MODEL // 2
EPS = 1e-6
N_GATES = 4 * HEADS

LANES = 128
MOD_ROWS = 16
TOK_TILE = 256
MOE_TOK = TOK_TILE
MOE_PIECE = 64
MOE_ROWS = 512
ROW_ALIGN = 16
HG_BLK = 32
HG_SB = 128
HG_SAFE_LOG = -75.0
HG_SAFE_Q = 1e3
ML_CHUNK = 64
S5_TC = 64
NEG_INF = float("-inf")


def _dot(a, b):
    return jnp.dot(a, b, preferred_element_type=F32)


def _dot_nt(a, b):
    return lax.dot_general(a, b, (((1,), (1,)), ((), ())), preferred_element_type=F32)


def _dot_tn(a, b):
    return lax.dot_general(a, b, (((0,), (0,)), ((), ())), preferred_element_type=F32)


def _split3(x):
    hi = x.astype(BF16)
    r1 = x - hi.astype(F32)
    mid = r1.astype(BF16)
    lo = (r1 - mid.astype(F32)).astype(BF16)
    return hi, mid, lo


def _dot_sel_l(m, x):
    hi, mid, lo = _split3(x)
    return _dot(m, hi) + _dot(m, mid) + _dot(m, lo)


def _dot_sel_r(x, m):
    hi, mid, lo = _split3(x)
    return _dot(hi, m) + _dot(mid, m) + _dot(lo, m)


def _dot3(a, b):
    ah = a.astype(BF16)
    al = (a - ah.astype(F32)).astype(BF16)
    bh = b.astype(BF16)
    bl = (b - bh.astype(F32)).astype(BF16)
    return _dot(ah, bh) + _dot(al, bh) + _dot(ah, bl)


def _sigmoid(x):
    return 1.0 / (1.0 + jnp.exp(-x))


def _silu(x):
    return x * _sigmoid(x)


def _log_sigmoid(x):
    return jnp.minimum(x, 0.0) - jnp.log(1.0 + jnp.exp(-jnp.abs(x)))


def _rmsnorm_rows(x, g):
    return x * lax.rsqrt(jnp.mean(x * x, axis=-1, keepdims=True) + EPS) * g


def _ada_kernel(c_ref, w_ref, b_ref, o_ref):
    s = _silu(c_ref[...]).astype(BF16)
    o_ref[0] = _dot(s, w_ref[0].astype(BF16)) + b_ref[0]


def _ada_mod(cond, w_ada, b_ada):
    tn = 1536
    n = w_ada.shape[-1]
    return pl.pallas_call(
        _ada_kernel,
        grid=(DEPTH, n // tn),
        in_specs=[
            pl.BlockSpec((MOD_ROWS, D_MODEL), lambda l, j: (0, 0)),
            pl.BlockSpec((1, D_MODEL, tn), lambda l, j: (l, 0, j)),
            pl.BlockSpec((1, 1, tn), lambda l, j: (l, 0, j)),
        ],
        out_specs=pl.BlockSpec((1, MOD_ROWS, tn), lambda l, j: (l, 0, j)),
        out_shape=jax.ShapeDtypeStruct((DEPTH, MOD_ROWS, n), F32),
        name="ada_mod",
    )(cond, w_ada, b_ada.reshape(DEPTH, 1, n))


def _inproj_kernel(x_ref, sh_ref, sc_ref, g_ref, wa_ref, wb_ref, wg_ref, wgt_ref,
                   hg_ref, ml_ref, su_ref, hy_ref, gt_ref, gtt_ref):
    h = _rmsnorm_rows(x_ref[...], g_ref[...])
    h = h * (1.0 + sc_ref[0]) + sh_ref[0]
    hb = h.astype(BF16)
    hg_ref[...] = _dot(hb, wa_ref[:, 0:5 * GROUP_W])
    ml_ref[...] = _dot(hb, wa_ref[:, 5 * GROUP_W:9 * GROUP_W])
    su_ref[...] = _dot(hb, wb_ref[:, 0:GROUP_W])
    hy_ref[...] = _dot(hb, wb_ref[:, GROUP_W:4 * GROUP_W])
    gt_ref[...] = _dot(hb, wg_ref[...])
    gtt_ref[...] = _dot_nt(wgt_ref[...], hb)


def _mod_spec(k, row_of_tile):
    return pl.BlockSpec((1, 1, D_MODEL), lambda i: (row_of_tile(i), 0, k))


def _inproj(x2d, mod3, row_of_tile, norm_g, w_a, w_b, w_g, w_gt):
    t = x2d.shape[0]
    tm = TOK_TILE
    const = lambda i: (0, 0)
    tile = lambda i: (i, 0)
    widths = (5 * GROUP_W, 4 * GROUP_W, GROUP_W, 3 * GROUP_W)
    return pl.pallas_call(
        _inproj_kernel,
        grid=(t // tm,),
        in_specs=[
            pl.BlockSpec((tm, D_MODEL), tile),
            _mod_spec(0, row_of_tile),
            _mod_spec(1, row_of_tile),
            pl.BlockSpec((1, D_MODEL), const),
            pl.BlockSpec(w_a.shape, const),
            pl.BlockSpec(w_b.shape, const),
            pl.BlockSpec(w_g.shape, const),
            pl.BlockSpec(w_gt.shape, const),
        ],
        out_specs=[pl.BlockSpec((tm, w), tile) for w in widths]
        + [pl.BlockSpec((tm, LANES), tile), pl.BlockSpec((N_GATES, tm), lambda i: (0, i))],
        out_shape=[jax.ShapeDtypeStruct((t, w), F32) for w in widths]
        + [jax.ShapeDtypeStruct((t, LANES), F32), jax.ShapeDtypeStruct((N_GATES, t), F32)],
        name="inproj",
    )(x2d, mod3, mod3, norm_g, w_a, w_b, w_g, w_gt)


@functools.lru_cache(maxsize=None)
def _hgrn_consts():
    r = np.arange(HG_SB)
    same = (r[:, None] // HG_BLK) == (r[None, :] // HG_BLK)
    tri_f = same & (r[:, None] >= r[None, :])
    tri_b = same & (r[:, None] <= r[None, :])
    c = np.arange(GROUP_W)
    head = (c[:, None] // HEAD_D) == (c[None, :] // HEAD_D)
    rr = np.arange(HEADS * HG_SB)
    head4 = (rr[:, None] // HG_SB) == (c[None, :] // HEAD_D)
    pairs = np.stack([np.tile(tri_f, (1, HEADS)), np.tile(tri_b, (1, HEADS))])
    return (np.stack([tri_f, tri_b]).astype(np.float32), same.astype(np.float32),
            head.astype(np.float32), head4.astype(np.float32), pairs.astype(np.float32))


def _hgrn_kernel(p_ref, s0_ref, lb_ref, ng_ref, tri_ref, bones_ref, e_ref, hmask_ref, hmask4_ref,
                 smask_ref, out_ref, *rest, seq, emit_state):
    if emit_state:
        st_ref = rest[0]
        rest = rest[1:]
    o_scr, st_scr, q_scr, k_scr, bc_scr, gq_scr, gk_scr, ks_scr, dec_scr, oi_scr = rest
    nsb = seq // HG_SB
    nblk = HG_SB // HG_BLK
    log_lb = lb_ref[0:1, :]
    log_1mlb = lb_ref[1:2, :]
    one_m_lb = lb_ref[2:3, :]
    row = lax.broadcasted_iota(jnp.int32, (HG_BLK, 1), 0)

    def stage(r0, d):
        rows = pl.ds(r0, HG_SB)
        pq = p_ref[rows, 0:GROUP_W]
        pf = p_ref[rows, (1 + d) * GROUP_W:(2 + d) * GROUP_W]
        q = _silu(pq)
        e = jnp.exp(-jnp.abs(pf))
        b2 = log_1mlb + jnp.minimum(pf, 0.0) - jnp.log(1.0 + e)
        mx = jnp.maximum(log_lb, b2)
        mn = jnp.minimum(log_lb, b2)
        lf = mx + jnp.log(1.0 + jnp.exp(mn - mx))
        k = one_m_lb * jnp.where(pf >= 0.0, e, 1.0) / (1.0 + e)
        hi, mid, lo = _split3(lf)
        bc = _dot(tri_ref[d], hi) + _dot(tri_ref[d], mid) + _dot(tri_ref[d], lo)
        bt = _dot(bones_ref[...], hi) + _dot(bones_ref[...], mid) + _dot(bones_ref[...], lo)
        gq_scr[d] = (q * jnp.exp(bc)).astype(BF16)
        gk_scr[d] = (k * jnp.exp(bt - bc)).astype(BF16)
        dec_scr[d] = jnp.exp(bt)
        ks_scr[d] = (k * jnp.exp(-bc)).astype(BF16)
        q_scr[d] = q
        k_scr[d] = k
        bc_scr[d] = bc
        return (jnp.min(bc) >= HG_SAFE_LOG) & (jnp.max(jnp.abs(q)) <= HG_SAFE_Q)

    def block_edge_path(r0, d):
        vb = p_ref[pl.ds(r0, HG_SB), 3 * GROUP_W:4 * GROUP_W].astype(BF16)
        ksbd = jnp.concatenate([ks_scr[d]] * HEADS, axis=0) * hmask4_ref[...]
        vbd = jnp.concatenate([vb] * HEADS, axis=0) * hmask4_ref[...]
        s = _dot_nt(gq_scr[d], ksbd) * smask_ref[d]
        oi_scr[d] = _dot(s.astype(BF16), vbd)

    def exact_path(r0, d):
        for j in range(nblk):
            sl = slice(HG_BLK * j, HG_BLK * (j + 1))
            qj = q_scr[d, sl, :]
            kj = k_scr[d, sl, :]
            vj = p_ref[pl.ds(r0 + HG_BLK * j, HG_BLK), 3 * GROUP_W:4 * GROUP_W]
            bcj = bc_scr[d, sl, :]
            parts = []
            for s in range(HG_BLK):
                keep = (row >= s) if d == 0 else (row <= s)
                dd = jnp.exp(jnp.where(keep, bcj - bcj[s:s + 1, :], NEG_INF))
                parts.append((qj * dd * kj[s:s + 1, :]).astype(BF16))
            r_all = _dot(jnp.concatenate(parts, axis=0), e_ref[...])
            o = r_all[0:HG_BLK, :] * vj[0:1, :]
            for s in range(1, HG_BLK):
                o = o + r_all[HG_BLK * s:HG_BLK * (s + 1), :] * vj[s:s + 1, :]
            oi_scr[d, sl, :] = o

    def chain(r0, d):
        order = range(nblk) if d == 0 else range(nblk - 1, -1, -1)
        for j in order:
            sl = slice(HG_BLK * j, HG_BLK * (j + 1))
            vb = p_ref[pl.ds(r0 + HG_BLK * j, HG_BLK), 3 * GROUP_W:4 * GROUP_W].astype(BF16)
            upd = _dot_tn(vb, gk_scr[d, sl, :]) * hmask_ref[...]
            st = st_scr[d]
            o_scr[d, pl.ds(r0 + HG_BLK * j, HG_BLK), :] = oi_scr[d, sl, :] + _dot_nt(gq_scr[d, sl, :], st.astype(BF16))
            st_scr[d] = st * dec_scr[d, HG_BLK * j:HG_BLK * j + 1, :] + upd

    st_scr[...] = s0_ref[...]

    def body(i, carry):
        rf = pl.multiple_of(i * HG_SB, HG_SB)
        rb = pl.multiple_of((nsb - 1 - i) * HG_SB, HG_SB)
        safe = stage(rf, 0) & stage(rb, 1)

        def both_block_edge():
            block_edge_path(rf, 0)
            block_edge_path(rb, 1)

        def both_exact():
            exact_path(rf, 0)
            exact_path(rb, 1)

        lax.cond(safe, both_block_edge, both_exact)
        chain(rf, 0)
        chain(rb, 1)
        return carry

    lax.fori_loop(0, nsb, body, 0)

    def finish(i, carry):
        rows = pl.ds(pl.multiple_of(i * HG_SB, HG_SB), HG_SB)
        o = o_scr[0, rows, :] + o_scr[1, rows, :]
        ms = _dot_hilo(o * o, e_ref[...]) * (1.0 / HEAD_D)
        pg = p_ref[rows, 4 * GROUP_W:5 * GROUP_W]
        out_ref[rows, :] = (o * lax.rsqrt(ms + EPS) * ng_ref[...] * _silu(pg)).astype(out_ref.dtype)
        return carry

    lax.fori_loop(0, nsb, finish, 0, unroll=2)
    if emit_state:
        ri = lax.broadcasted_iota(jnp.int32, (GROUP_W, GROUP_W), 0)
        ci = lax.broadcasted_iota(jnp.int32, (GROUP_W, GROUP_W), 1)
        eye = jnp.where(ri == ci, 1.0, 0.0).astype(BF16)
        for d in range(2):
            hi, mid, lo = _split3(st_scr[d])
            s_t = _dot_tn(hi, eye) + _dot_tn(mid, eye) + _dot_tn(lo, eye)
            for h in range(HEADS):
                st_ref[d, h] = s_t[HEAD_D * h:HEAD_D * (h + 1), HEAD_D * h:HEAD_D * (h + 1)]


def _hgrn_kernel_old(p_ref, s0_ref, lb_ref, ng_ref, tri_ref, bones_ref, e_ref, hmask_ref,
                 out_ref, *rest, seq, emit_state):
    if emit_state:
        st_ref, of_scr, st_scr, q_scr, k_scr, bc_scr, gq_scr, gk_scr, dec_scr = rest
    else:
        of_scr, st_scr, q_scr, k_scr, bc_scr, gq_scr, gk_scr, dec_scr = rest
    nsb = seq // HG_SB
    nblk = HG_SB // HG_BLK
    log_lb = lb_ref[0:1, :]
    log_1mlb = lb_ref[1:2, :]
    one_m_lb = lb_ref[2:3, :]
    row = lax.broadcasted_iota(jnp.int32, (HG_BLK, 1), 0)

    def superblock(sb, d, finalize):
        r0 = pl.multiple_of(sb * HG_SB, HG_SB)
        rows = pl.ds(r0, HG_SB)
        pq = p_ref[rows, 0:GROUP_W]
        pf = p_ref[rows, (1 + d) * GROUP_W:(2 + d) * GROUP_W]
        q_scr[...] = _silu(pq)
        b2 = log_1mlb + _log_sigmoid(pf)
        mx = jnp.maximum(log_lb, b2)
        mn = jnp.minimum(log_lb, b2)
        delta = jnp.where(mn == NEG_INF, NEG_INF, mn - mx)
        lf = mx + jnp.log1p(jnp.exp(delta))
        k_scr[...] = one_m_lb * _sigmoid(-pf)
        bc = _dot_sel_l(tri_ref[d], lf)
        bt = _dot_sel_l(bones_ref[...], lf)
        bc_scr[...] = bc
        gq_scr[...] = (q_scr[...] * jnp.exp(bc)).astype(BF16)
        gk_scr[...] = (k_scr[...] * jnp.exp(bt - bc)).astype(BF16)
        dec_scr[...] = jnp.exp(bt)
        o_blocks = [None] * nblk
        order = range(nblk) if d == 0 else range(nblk - 1, -1, -1)
        for j in order:
            sl = slice(HG_BLK * j, HG_BLK * (j + 1))
            qj = q_scr[sl, :]
            kj = k_scr[sl, :]
            vj = p_ref[pl.ds(r0 + HG_BLK * j, HG_BLK), 3 * GROUP_W:4 * GROUP_W]
            bcj = bc_scr[sl, :]
            parts = []
            for s in range(HG_BLK):
                diff = bcj - bcj[s:s + 1, :]
                keep = (row >= s) if d == 0 else (row <= s)
                dd = jnp.exp(jnp.where(keep, diff, NEG_INF))
                parts.append((qj * dd * kj[s:s + 1, :]).astype(BF16))
            r_all = _dot(jnp.concatenate(parts, axis=0), e_ref[...])
            o = _dot_nt(gq_scr[sl, :], st_scr[...].astype(BF16))
            for s in range(HG_BLK):
                o = o + r_all[HG_BLK * s:HG_BLK * (s + 1), :] * vj[s:s + 1, :]
            upd = _dot_tn(vj.astype(BF16), gk_scr[sl, :])
            st_scr[...] = st_scr[...] * dec_scr[HG_BLK * j:HG_BLK * j + 1, :] + upd * hmask_ref[...]
            o_blocks[j] = o
        o_sb = jnp.concatenate(o_blocks, axis=0)
        if finalize:
            o_sb = o_sb + of_scr[rows, :]
            ms = _dot_sel_r(o_sb * o_sb, e_ref[...]) * (1.0 / HEAD_D)
            pg = p_ref[rows, 4 * GROUP_W:5 * GROUP_W]
            y = o_sb * lax.rsqrt(ms + EPS) * ng_ref[...] * _silu(pg)
            out_ref[rows, :] = y.astype(out_ref.dtype)
        else:
            of_scr[rows, :] = o_sb

    st_scr[...] = s0_ref[0]

    def fwd(i, carry):
        superblock(i, 0, False)
        return carry

    lax.fori_loop(0, nsb, fwd, 0)
    if emit_state:
        st_ref[0] = st_scr[...]
    st_scr[...] = s0_ref[1]

    def bwd(i, carry):
        superblock(nsb - 1 - i, 1, True)
        return carry

    lax.fori_loop(0, nsb, bwd, 0)
    if emit_state:
        st_ref[1] = st_scr[...]


def _hgrn(p3, s0_bd, lb_rows, norm_g, emit_state):
    b, seq, _ = p3.shape
    tri, bones, head, head4, pairs = _hgrn_consts()
    const2 = lambda i: (0, 0)
    const3 = lambda i: (0, 0, 0)
    out_shape = [jax.ShapeDtypeStruct((b, seq, GROUP_W), BF16)]
    out_specs = [pl.BlockSpec((None, seq, GROUP_W), lambda i: (i, 0, 0))]
    if emit_state:
        out_shape.append(jax.ShapeDtypeStruct((b, 2, HEADS, HEAD_D, HEAD_D), F32))
        out_specs.append(pl.BlockSpec((None, 2, HEADS, HEAD_D, HEAD_D), lambda i: (i, 0, 0, 0, 0)))
    sb_f32 = pltpu.VMEM((2, HG_SB, GROUP_W), F32)
    sb_bf16 = pltpu.VMEM((2, HG_SB, GROUP_W), BF16)
    res = pl.pallas_call(
        functools.partial(_hgrn_kernel, seq=seq, emit_state=emit_state),
        grid=(b,),
        in_specs=[
            pl.BlockSpec((None, seq, 5 * GROUP_W), lambda i: (i, 0, 0)),
            pl.BlockSpec((None, 2, GROUP_W, GROUP_W), lambda i: (i, 0, 0, 0)),
            pl.BlockSpec((8, GROUP_W), const2),
            pl.BlockSpec((1, GROUP_W), const2),
            pl.BlockSpec((2, HG_SB, HG_SB), const3),
            pl.BlockSpec((HG_SB, HG_SB), const2),
            pl.BlockSpec((GROUP_W, GROUP_W), const2),
            pl.BlockSpec((GROUP_W, GROUP_W), const2),
            pl.BlockSpec((HEADS * HG_SB, GROUP_W), const2),
            pl.BlockSpec((2, HG_SB, HEADS * HG_SB), const3),
        ],
        out_specs=out_specs,
        out_shape=out_shape,
        scratch_shapes=[
            pltpu.VMEM((2, seq, GROUP_W), F32),
            pltpu.VMEM((2, GROUP_W, GROUP_W), F32),
            sb_f32, sb_f32, sb_f32, sb_bf16, sb_bf16, sb_bf16, sb_f32, sb_f32,
        ],
        name="hgrn2",
    )(p3, s0_bd, lb_rows, norm_g, jnp.asarray(tri, BF16), jnp.asarray(bones, BF16),
      jnp.asarray(head, BF16), jnp.asarray(head, F32), jnp.asarray(head4, BF16), jnp.asarray(pairs, F32))
    return res if emit_state else (res[0], None)


@functools.lru_cache(maxsize=None)
def _mlstm_consts():
    r = np.arange(ML_CHUNK)
    tri_f = (r[:, None] >= r[None, :]).astype(np.float32)
    tri = np.stack([tri_f, tri_f.T])
    c = np.arange(GROUP_W)
    head = (c[:, None] // HEAD_D) == (c[None, :] // HEAD_D)
    pos = c % HEAD_D
    row_f = head & (pos[:, None] <= pos[None, :])
    row_b = head & (pos[:, None] >= pos[None, :])
    rowtri = np.stack([row_f, row_b]).astype(np.float32)
    expand = np.zeros((2, LANES, GROUP_W), np.float32)
    for d in range(2):
        for h in range(HEADS):
            expand[d, 2 * HEADS + d * HEADS + h, h * HEAD_D:(h + 1) * HEAD_D] = 1.0
    return tri, rowtri, expand, head.astype(np.float32)


def _dot_hilo(x, m):
    hi = x.astype(BF16)
    lo = (x - hi.astype(F32)).astype(BF16)
    return _dot(hi, m) + _dot(lo, m)


def _mlstm_kernel(p_ref, g_ref, gr_ref, conv_ref, brow_ref, brows_ref, ng_ref, c0_ref, n0_ref, m0_ref,
                  tri_ref, rowtri_ref, exp_ref, e_ref, out_ref, *rest, seq, conv_w, emit_state):
    if emit_state:
        cst_ref, nst_ref, mst_ref = rest[:3]
        rest = rest[3:]
    (q_scr, k_scr, col_scr, mloc_scr, gb_scr, h_scr, row_scr, blast_scr, gmax_scr,
     c_scr, n_scr, m_scr) = rest
    nch = seq // ML_CHUNK
    rowid = lax.broadcasted_iota(jnp.int32, (ML_CHUNK, 1), 0)
    ti = lax.broadcasted_iota(jnp.int32, (ML_CHUNK, GROUP_W), 0)
    si = lax.broadcasted_iota(jnp.int32, (ML_CHUNK, GROUP_W), 1) % HEAD_D

    for d in range(2):
        li_row = gr_ref[d] + brows_ref[d:d + 1, :]
        lf_row = _log_sigmoid(gr_ref[2 + d] + brows_ref[2 + d:3 + d, :])
        row_scr[d] = li_row - _dot_sel_r(lf_row, rowtri_ref[d])

    def pre_chunk(ci, carry):
        r0 = pl.multiple_of(ci * ML_CHUNK, ML_CHUNK)
        rows = pl.ds(r0, ML_CHUNK)
        zc = p_ref[rows, 0:2 * GROUP_W]
        zp = pltpu.roll(zc, 1, axis=0)
        zn = pltpu.roll(zc, ML_CHUNK - 1, axis=0)
        if conv_w == ML_CHUNK:
            prev_row = jnp.zeros((1, 2 * GROUP_W), F32)
            next_row = prev_row
        else:
            prev_row = p_ref[pl.ds(jnp.maximum(r0 - 1, 0), 1), 0:2 * GROUP_W]
            prev_row = jnp.where(r0 % conv_w == 0, 0.0, prev_row)
            next_row = p_ref[pl.ds(jnp.minimum(r0 + ML_CHUNK, seq - 1), 1), 0:2 * GROUP_W]
            next_row = jnp.where((r0 + ML_CHUNK) % conv_w == 0, 0.0, next_row)
        zp = jnp.where(rowid == 0, prev_row, zp)
        zn = jnp.where(rowid == ML_CHUNK - 1, next_row, zn)
        qk = _silu(conv_ref[0:1, :] * zp + conv_ref[1:2, :] * zc + conv_ref[2:3, :] * zn)
        q_scr[rows, :] = qk[:, 0:GROUP_W].astype(BF16)
        k_scr[rows, :] = qk[:, GROUP_W:2 * GROUP_W] * (HEAD_D ** -0.5)
        ge = g_ref[rows, :] + brow_ref[...]
        lane = lax.broadcasted_iota(jnp.int32, (1, LANES), 1)
        bwd = lane >= 3 * HEADS
        lf = _log_sigmoid(ge)
        bc = jnp.where(bwd, _dot_sel_l(tri_ref[1], lf), _dot_sel_l(tri_ref[0], lf))
        li = pltpu.roll(ge, 2 * HEADS, axis=1)
        pre = li - bc
        suf = pre
        for sh in (1, 2, 4, 8, 16, 32):
            pre = jnp.maximum(pre, jnp.where(rowid >= sh, pltpu.roll(pre, sh, axis=0), NEG_INF))
            suf = jnp.maximum(suf, jnp.where(rowid < ML_CHUNK - sh, pltpu.roll(suf, ML_CHUNK - sh, axis=0), NEG_INF))
        blast = jnp.where(bwd, bc[0:1, :], bc[ML_CHUNK - 1:ML_CHUNK, :])
        gb = blast - bc + li
        edge_rows = jnp.concatenate([blast, jnp.max(gb, axis=0, keepdims=True), jnp.zeros((6, LANES), F32)], axis=0)
        stacked = jnp.concatenate([bc, bc + jnp.where(bwd, suf, pre), gb, edge_rows], axis=0)
        for d in range(2):
            ex = _dot_sel_r(stacked, exp_ref[d])
            col_scr[d, rows, :] = ex[0:ML_CHUNK, :]
            mloc_scr[d, rows, :] = ex[ML_CHUNK:2 * ML_CHUNK, :]
            gb_scr[d, rows, :] = ex[2 * ML_CHUNK:3 * ML_CHUNK, :]
            blast_scr[d, pl.ds(ci, 1), :] = ex[3 * ML_CHUNK:3 * ML_CHUNK + 1, :]
            gmax_scr[d, pl.ds(ci, 1), :] = ex[3 * ML_CHUNK + 1:3 * ML_CHUNK + 2, :]
        return carry

    lax.fori_loop(0, nch, pre_chunk, 0, unroll=2)

    for d in range(2):
        c_scr[d] = jnp.zeros((GROUP_W, GROUP_W), F32)
        for h in range(HEADS):
            lo = HEAD_D * h
            c_scr[d, lo:lo + HEAD_D, lo:lo + HEAD_D] = c0_ref[d, h]
    n_scr[...] = n0_ref[...]
    m_scr[...] = m0_ref[...]

    def chunk(ci, d):
        r0 = pl.multiple_of(ci * ML_CHUNK, ML_CHUNK)
        rows = pl.ds(r0, ML_CHUNK)
        qb = q_scr[rows, :]
        kc = k_scr[rows, :]
        vb = p_ref[rows, 2 * GROUP_W:3 * GROUP_W].astype(BF16)
        hm = e_ref[...]
        kbd = jnp.concatenate([kc.astype(BF16)] * HEADS, axis=0) * hm
        vbd = jnp.concatenate([vb] * HEADS, axis=0) * hm
        s = _dot_nt(qb, kbd)
        colb = col_scr[d, rows, :]
        m_b = m_scr[d]
        prev = colb + m_b
        m_t = jnp.maximum(prev, mloc_scr[d, rows, :])
        keep = (ti >= si) if d == 0 else (ti <= si)
        w = jnp.exp(jnp.where(keep, colb + row_scr[d, pl.ds(ci, 1), :], NEG_INF) - m_t) * s
        wp = jnp.exp(prev - m_t)
        cst = c_scr[d]
        nrow = n_scr[d]
        num = wp * _dot(qb, cst.astype(BF16)) + _dot(w.astype(BF16), vbd)
        den = _dot_hilo(wp * qb.astype(F32) * nrow + w, e_ref[...])
        h_scr[d, rows, :] = num / jnp.maximum(jnp.abs(den), jnp.exp(-m_t))
        blast = blast_scr[d, pl.ds(ci, 1), :]
        m_new = jnp.maximum(blast + m_b, gmax_scr[d, pl.ds(ci, 1), :])
        dec = jnp.exp(blast + m_b - m_new)
        kw = kc * jnp.exp(gb_scr[d, rows, :] - m_new)
        c_scr[d] = cst * dec + _dot_tn(kw.astype(BF16), vb) * hm.astype(F32)
        n_scr[d] = nrow * dec + jnp.sum(kw, axis=0, keepdims=True)
        m_scr[d] = m_new

    def body(i, carry):
        chunk(i, 0)
        chunk(nch - 1 - i, 1)
        return carry

    lax.fori_loop(0, nch, body, 0, unroll=2)

    def fin_chunk(ci, carry):
        rows = pl.ds(pl.multiple_of(ci * ML_CHUNK, ML_CHUNK), ML_CHUNK)
        hs = h_scr[0, rows, :] + h_scr[1, rows, :]
        ms = _dot_hilo(hs * hs, e_ref[...]) * (1.0 / HEAD_D)
        po = p_ref[rows, 3 * GROUP_W:4 * GROUP_W]
        out_ref[rows, :] = (hs * lax.rsqrt(ms + EPS) * ng_ref[...] * _sigmoid(po)).astype(out_ref.dtype)
        return carry

    lax.fori_loop(0, nch, fin_chunk, 0, unroll=4)
    if emit_state:
        for d in range(2):
            for h in range(HEADS):
                lo = HEAD_D * h
                cst_ref[d, h] = c_scr[d, lo:lo + HEAD_D, lo:lo + HEAD_D]
        nst_ref[...] = n_scr[...]
        mst_ref[...] = m_scr[...]


def _mlstm_kernel_old(p_ref, g_ref, gt_ref, conv_ref, brow_ref, bcol_ref, ng_ref,
                  c0_ref, n0_ref, m0_ref, tri_ref, out_ref, *rest, seq, conv_w, emit_state):
    if emit_state:
        cst_ref, nst_ref, mst_ref, qk_scr, hf_scr, c_scr, n_scr, m_scr = rest
    else:
        qk_scr, hf_scr, c_scr, n_scr, m_scr = rest
    nch = seq // ML_CHUNK
    rowid = lax.broadcasted_iota(jnp.int32, (ML_CHUNK, 1), 0)
    ti = lax.broadcasted_iota(jnp.int32, (ML_CHUNK, ML_CHUNK), 0)
    si = lax.broadcasted_iota(jnp.int32, (ML_CHUNK, ML_CHUNK), 1)

    def conv_chunk(ci, carry):
        r0 = pl.multiple_of(ci * ML_CHUNK, ML_CHUNK)
        zc = p_ref[pl.ds(r0, ML_CHUNK), 0:2 * GROUP_W]
        zp = pltpu.roll(zc, 1, axis=0)
        zn = pltpu.roll(zc, ML_CHUNK - 1, axis=0)
        if conv_w == ML_CHUNK:
            prev_row = jnp.zeros((1, 2 * GROUP_W), F32)
            next_row = prev_row
        else:
            prev_row = p_ref[pl.ds(jnp.maximum(r0 - 1, 0), 1), 0:2 * GROUP_W]
            prev_row = jnp.where(r0 % conv_w == 0, 0.0, prev_row)
            next_row = p_ref[pl.ds(jnp.minimum(r0 + ML_CHUNK, seq - 1), 1), 0:2 * GROUP_W]
            next_row = jnp.where((r0 + ML_CHUNK) % conv_w == 0, 0.0, next_row)
        zp = jnp.where(rowid == 0, prev_row, zp)
        zn = jnp.where(rowid == ML_CHUNK - 1, next_row, zn)
        out = conv_ref[0:1, :] * zp + conv_ref[1:2, :] * zc + conv_ref[2:3, :] * zn
        qk_scr[pl.ds(r0, ML_CHUNK), :] = _silu(out)
        return carry

    lax.fori_loop(0, nch, conv_chunk, 0)

    def chunk(ci, d, finalize):
        r0 = pl.multiple_of(ci * ML_CHUNK, ML_CHUNK)
        rows = pl.ds(r0, ML_CHUNK)
        g = g_ref[rows, :] + brow_ref[...]
        gt = gt_ref[ci] + bcol_ref[...]
        li = g[:, 0:2 * HEADS]
        lf = _log_sigmoid(g[:, 2 * HEADS:4 * HEADS])
        lit = gt[0:2 * HEADS, :]
        lft = _log_sigmoid(gt[2 * HEADS:4 * HEADS, :])
        bc_all = _dot_sel_l(tri_ref[d], lf)
        bct_all = _dot_sel_r(lft, tri_ref[1 - d])
        keep = (ti >= si) if d == 0 else (ti <= si)
        edge = ML_CHUNK - 1 if d == 0 else 0
        outs = []
        for h in range(HEADS):
            c = HEADS * d + h
            lo = HEAD_D * h
            q = qk_scr[rows, lo:lo + HEAD_D]
            k = qk_scr[rows, GROUP_W + lo:GROUP_W + lo + HEAD_D] * (HEAD_D ** -0.5)
            v = p_ref[rows, 2 * GROUP_W + lo:2 * GROUP_W + lo + HEAD_D]
            qb = q.astype(BF16)
            vb = v.astype(BF16)
            bcol = bc_all[:, c:c + 1]
            brow = bct_all[c:c + 1, :]
            m = m_scr[h]
            dmat = jnp.where(keep, bcol - brow + lit[c:c + 1, :], NEG_INF)
            prev = bcol + m
            m_t = jnp.maximum(prev, jnp.max(dmat, axis=1, keepdims=True))
            w = jnp.exp(dmat - m_t) * _dot_nt(qb, k.astype(BF16))
            wp = jnp.exp(prev - m_t)
            cst = c_scr[h]
            nrow = n_scr[h]
            num = wp * _dot(qb, cst.astype(BF16)) + _dot(w.astype(BF16), vb)
            den = wp * jnp.sum(q * nrow, axis=1, keepdims=True) + jnp.sum(w, axis=1, keepdims=True)
            hh = num / jnp.maximum(jnp.abs(den), jnp.exp(-m_t))
            blast = bcol[edge:edge + 1, :]
            gc = blast - bcol + li[:, c:c + 1]
            m_new = jnp.maximum(blast + m, jnp.max(gc, axis=0, keepdims=True))
            dec = jnp.exp(blast + m - m_new)
            kw = k * jnp.exp(gc - m_new)
            c_scr[h] = dec * cst + _dot_tn(kw.astype(BF16), vb)
            n_scr[h] = dec * nrow + jnp.sum(kw, axis=0, keepdims=True)
            m_scr[h] = m_new
            if finalize:
                hs = hh + hf_scr[rows, lo:lo + HEAD_D]
                hs = hs * lax.rsqrt(jnp.mean(hs * hs, axis=-1, keepdims=True) + EPS)
                outs.append(hs)
            else:
                outs.append(hh)
        o = jnp.concatenate(outs, axis=1)
        if finalize:
            po = p_ref[rows, 3 * GROUP_W:4 * GROUP_W]
            out_ref[rows, :] = (o * ng_ref[...] * _sigmoid(po)).astype(out_ref.dtype)
        else:
            hf_scr[rows, :] = o

    def run(d):
        c_scr[...] = c0_ref[d]
        n_scr[...] = n0_ref[d]
        m_scr[...] = m0_ref[d]

        def body(i, carry):
            chunk(i if d == 0 else nch - 1 - i, d, d == 1)
            return carry

        lax.fori_loop(0, nch, body, 0)
        if emit_state:
            cst_ref[d] = c_scr[...]
            nst_ref[d] = n_scr[...]
            mst_ref[d] = m_scr[...]

    run(0)
    run(1)


def _mlstm(p3, g3, gr4, conv_w3, brow, brows, norm_g, c0, n0, m0, conv_width, emit_state):
    b, seq, _ = p3.shape
    nch = seq // ML_CHUNK
    tri, rowtri, expand, head = _mlstm_consts()
    const2 = lambda i: (0, 0)
    const3 = lambda i: (0, 0, 0)
    out_shape = [jax.ShapeDtypeStruct((b, seq, GROUP_W), BF16)]
    out_specs = [pl.BlockSpec((None, seq, GROUP_W), lambda i: (i, 0, 0))]
    c_spec = pl.BlockSpec((None, 2, HEADS, HEAD_D, HEAD_D), lambda i: (i, 0, 0, 0, 0))
    row_spec = pl.BlockSpec((None, 2, 1, GROUP_W), lambda i: (i, 0, 0, 0))
    if emit_state:
        out_shape += [jax.ShapeDtypeStruct((b, 2, HEADS, HEAD_D, HEAD_D), F32),
                      jax.ShapeDtypeStruct((b, 2, 1, GROUP_W), F32), jax.ShapeDtypeStruct((b, 2, 1, GROUP_W), F32)]
        out_specs += [c_spec, row_spec, row_spec]
    dir_seq = (2, seq, GROUP_W)
    dir_chunks = (2, nch, GROUP_W)
    res = pl.pallas_call(
        functools.partial(_mlstm_kernel, seq=seq, conv_w=conv_width, emit_state=emit_state),
        grid=(b,),
        in_specs=[
            pl.BlockSpec((None, seq, 4 * GROUP_W), lambda i: (i, 0, 0)),
            pl.BlockSpec((None, seq, LANES), lambda i: (i, 0, 0)),
            pl.BlockSpec((None, 4, nch, GROUP_W), lambda i: (i, 0, 0, 0)),
            pl.BlockSpec((3, 2 * GROUP_W), const2),
            pl.BlockSpec((1, LANES), const2),
            pl.BlockSpec((4, GROUP_W), const2),
            pl.BlockSpec((1, GROUP_W), const2),
            c_spec, row_spec, row_spec,
            pl.BlockSpec((2, ML_CHUNK, ML_CHUNK), const3),
            pl.BlockSpec((2, GROUP_W, GROUP_W), const3),
            pl.BlockSpec((2, LANES, GROUP_W), const3),
            pl.BlockSpec((GROUP_W, GROUP_W), const2),
        ],
        out_specs=out_specs,
        out_shape=out_shape,
        scratch_shapes=[
            pltpu.VMEM((seq, GROUP_W), BF16),
            pltpu.VMEM((seq, GROUP_W), F32),
            pltpu.VMEM(dir_seq, F32), pltpu.VMEM(dir_seq, F32), pltpu.VMEM(dir_seq, F32), pltpu.VMEM(dir_seq, F32),
            pltpu.VMEM(dir_chunks, F32), pltpu.VMEM(dir_chunks, F32), pltpu.VMEM(dir_chunks, F32),
            pltpu.VMEM((2, GROUP_W, GROUP_W), F32),
            pltpu.VMEM((2, 1, GROUP_W), F32),
            pltpu.VMEM((2, 1, GROUP_W), F32),
        ],
        name="mlstm",
    )(p3, g3, gr4, conv_w3, brow, brows, norm_g, c0, n0, m0, jnp.asarray(tri, BF16), jnp.asarray(rowtri, BF16),
      jnp.asarray(expand, BF16), jnp.asarray(head, BF16))
    if emit_state:
        return res
    return res[0], None, None, None


def _s5_kernel(*refs, nb, reverse, final):
    if final:
        (u_ref, hre0_ref, him0_ref, are_ref, aim_ref, wb_ref, wc_ref, yin_ref, d_ref, gw_ref, gb_ref,
         y_ref, sre_ref, sim_ref, utm_scr, bu_scr, hb_scr, ytm_scr, hre_scr, him_scr) = refs
    else:
        (u_ref, hre0_ref, him0_ref, are_ref, aim_ref, wb_ref, wc_ref,
         y_ref, sre_ref, sim_ref, utm_scr, bu_scr, hb_scr, ytm_scr, hre_scr, him_scr) = refs
    tc = S5_TC
    step_id = pl.program_id(0)

    @pl.when(step_id == 0)
    def _():
        hre_scr[...] = hre0_ref[...]
        him_scr[...] = him0_ref[...]

    halves = GROUP_W // LANES
    for b in range(nb):
        for hv in range(halves):
            utm_scr[hv, pl.ds(b, tc, stride=nb), :] = u_ref[b, :, hv * LANES:(hv + 1) * LANES]
    u_tm = jnp.concatenate([utm_scr[hv] for hv in range(halves)], axis=1)
    bu_scr[...] = _dot(u_tm.astype(BF16), wb_ref[...])
    are = jnp.broadcast_to(are_ref[...], (nb, S5_STATE))
    aim = jnp.broadcast_to(aim_ref[...], (nb, S5_STATE))

    spi = max(ROW_ALIGN // nb, 1)
    rows_it = spi * nb

    def step(i, carry):
        hre, him = carry
        r0 = pl.multiple_of(((tc // spi - 1 - i) if reverse else i) * rows_it, rows_it)
        res = [None] * spi
        ims = [None] * spi
        for s in (range(spi - 1, -1, -1) if reverse else range(spi)):
            bu = bu_scr[pl.ds(r0 + s * nb, nb), :]
            hre, him = (are * hre - aim * him + bu[:, 0:S5_STATE],
                        are * him + aim * hre + bu[:, S5_STATE:2 * S5_STATE])
            res[s], ims[s] = hre, him
        hb_scr[pl.ds(r0, rows_it), 0:S5_STATE] = jnp.concatenate(res, axis=0).astype(BF16)
        hb_scr[pl.ds(r0, rows_it), S5_STATE:2 * S5_STATE] = jnp.concatenate(ims, axis=0).astype(BF16)
        return hre, him

    hre, him = lax.fori_loop(0, tc // spi, step, (hre_scr[...], him_scr[...]))
    hre_scr[...] = hre
    him_scr[...] = him
    y_tm = _dot(hb_scr[...], wc_ref[...])
    for hv in range(halves):
        ytm_scr[hv] = y_tm[:, hv * LANES:(hv + 1) * LANES]
    for b in range(nb):
        yb = jnp.concatenate([ytm_scr[hv, pl.ds(b, tc, stride=nb), :] for hv in range(halves)], axis=1)
        if final:
            y = yb + yin_ref[b] + d_ref[...] * u_ref[b]
            z = 0.5 * y * (1.0 + jnp.tanh(math.sqrt(2.0 / math.pi) * (y + 0.044715 * (y * y * y))))
            gate = _sigmoid(_dot(z.astype(BF16), gw_ref[...]) + gb_ref[...])
            y_ref[b] = (z * gate).astype(y_ref.dtype)
        else:
            y_ref[b] = yb

    @pl.when(step_id == pl.num_programs(0) - 1)
    def _():
        sre_ref[...] = hre
        sim_ref[...] = him


def _s5_pass(u3, hre0, him0, are, aim, wb, wc, reverse, extra):
    nb, seq, _ = u3.shape
    nch = seq // S5_TC
    final = extra is not None
    chunk = (lambda i: (0, nch - 1 - i, 0)) if reverse else (lambda i: (0, i, 0))
    const2 = lambda i: (0, 0)
    in_specs = [
        pl.BlockSpec((nb, S5_TC, GROUP_W), chunk),
        pl.BlockSpec((nb, S5_STATE), const2),
        pl.BlockSpec((nb, S5_STATE), const2),
        pl.BlockSpec((1, S5_STATE), const2),
        pl.BlockSpec((1, S5_STATE), const2),
        pl.BlockSpec((GROUP_W, 2 * S5_STATE), const2),
        pl.BlockSpec((2 * S5_STATE, GROUP_W), const2),
    ]
    args = [u3, hre0, him0, are, aim, wb, wc]
    if final:
        yin, dvec, gw, gb = extra
        in_specs += [pl.BlockSpec((nb, S5_TC, GROUP_W), chunk), pl.BlockSpec((1, GROUP_W), const2),
                     pl.BlockSpec((GROUP_W, GROUP_W), const2), pl.BlockSpec((1, GROUP_W), const2)]
        args += [yin, dvec, gw, gb]
    return pl.pallas_call(
        functools.partial(_s5_kernel, nb=nb, reverse=reverse, final=final),
        grid=(nch,),
        in_specs=in_specs,
        out_specs=[pl.BlockSpec((nb, S5_TC, GROUP_W), chunk),
                   pl.BlockSpec((nb, S5_STATE), const2), pl.BlockSpec((nb, S5_STATE), const2)],
        out_shape=[jax.ShapeDtypeStruct((nb, seq, GROUP_W), BF16 if final else F32),
                   jax.ShapeDtypeStruct((nb, S5_STATE), F32), jax.ShapeDtypeStruct((nb, S5_STATE), F32)],
        scratch_shapes=[
            pltpu.VMEM((GROUP_W // LANES, nb * S5_TC, LANES), F32),
            pltpu.VMEM((nb * S5_TC, 2 * S5_STATE), F32),
            pltpu.VMEM((nb * S5_TC, 2 * S5_STATE), BF16),
            pltpu.VMEM((GROUP_W // LANES, nb * S5_TC, LANES), F32),
            pltpu.VMEM((nb, S5_STATE), F32),
            pltpu.VMEM((nb, S5_STATE), F32),
        ],
        name="s5_bwd" if reverse else "s5_fwd",
    )(*args)


def _s5_params(a_re, a_im, log_dt, b_re, b_im, c_re, c_im):
    eye = jnp.eye(S5_GROUPS, dtype=F32)
    dt = jnp.exp(log_dt)[..., None]
    mag = jnp.exp(a_re * dt)
    ab_re, ab_im = mag * jnp.cos(a_im * dt), mag * jnp.sin(a_im * dt)
    den = a_re * a_re + a_im * a_im
    g_re = ((ab_re - 1.0) * a_re + ab_im * a_im) / den
    g_im = (ab_im * a_re - (ab_re - 1.0) * a_im) / den
    bb_re = g_re[..., None] * b_re - g_im[..., None] * b_im
    bb_im = g_re[..., None] * b_im + g_im[..., None] * b_re

    def in_mat(bb):
        return jnp.einsum('dgpc,gh->dgchp', bb, eye).reshape(2, GROUP_W, S5_STATE)

    wb = jnp.concatenate([in_mat(bb_re), in_mat(bb_im)], axis=-1).astype(BF16)

    def out_mat(cc):
        return jnp.einsum('gcp,gh->gphc', cc, eye).reshape(S5_STATE, GROUP_W)

    wc = jnp.concatenate([out_mat(c_re), -out_mat(c_im)], axis=0).astype(BF16)
    return ab_re.reshape(2, 1, S5_STATE), ab_im.reshape(2, 1, S5_STATE), wb, wc


def _s5(u3, h0_re, h0_im, prm, d_vec, glu_w, glu_b):
    ab_re, ab_im, wb, wc = prm
    yf, fre, fim = _s5_pass(u3, h0_re[:, 0], h0_im[:, 0], ab_re[0], ab_im[0], wb[0], wc, False, None)
    out, bre, bim = _s5_pass(u3, h0_re[:, 1], h0_im[:, 1], ab_re[1], ab_im[1], wb[1], wc, True,
                             (yf, d_vec, glu_w, glu_b))
    return out, jnp.stack([fre, bre], axis=1), jnp.stack([fim, bim], axis=1)


@functools.lru_cache(maxsize=None)
def _dft_consts(n):
    idx = np.arange(n, dtype=np.int64)
    ang = (np.pi / n) * ((idx[:, None] * idx[None, :]) % (2 * n)).astype(np.float64)
    cos, sin = np.cos(ang), np.sin(ang)
    sign = np.where(idx % 2 == 0, 1.0, -1.0)
    sin_n = sin.copy()
    sin_n[0, :] = sign
    fwd = np.concatenate([cos, sin_n], axis=0)
    inv_c = cos.T / n
    inv_c[:, 0] = 1.0 / (2 * n)
    inv_s = sin.T / n
    inv_s[:, 0] = sign / (2 * n)
    inv = np.concatenate([inv_c, inv_s], axis=1)
    return fwd.astype(np.float32), inv.astype(np.float32), sign.astype(np.float32)[:, None]


@functools.lru_cache(maxsize=None)
def _hyena_feats(n):
    t = np.linspace(0.0, 1.0, n, dtype=np.float64)[:, None]
    w = (2.0 * np.pi / n) * np.arange(n, dtype=np.float64)[:, None]
    bands = np.linspace(1e-4, HY_BANDS - 1, HY_BANDS, dtype=np.float64)[None, :]
    feats = np.concatenate([t, np.cos(w * bands), -np.sin(w * bands)], axis=-1)
    pad = np.zeros((n, LANES - HY_EMB))
    return np.concatenate([feats, pad], axis=-1).astype(np.float32), t.astype(np.float32)


def _hyfilt_kernel(feat_ref, t_ref, sign_ref, w1_ref, b1_ref, w2_ref, b2_ref, w3_ref, fr_ref, dec_ref,
                   fh_ref, fl_ref, ka_ref, kb_ref, ka2_ref, *, n):
    freq = fr_ref[...]
    hdn = jnp.sin(freq * (_dot3(feat_ref[...], w1_ref[...]) + b1_ref[...]))
    hdn = jnp.sin(freq * (_dot3(hdn, w2_ref[...]) + b2_ref[...]))
    h = _dot3(hdn, w3_ref[...]) * (jnp.exp(-t_ref[...] * jnp.abs(dec_ref[...])) + HY_SHIFT)
    half = HY_ORDER * GROUP_W
    row = lax.broadcasted_iota(jnp.int32, (n, 1), 0)
    h0 = h[:, 0:half]
    h1 = jnp.where(row == 0, 0.0, h[:, half:2 * half])
    norm = jnp.sum(jnp.abs(h0), axis=0, keepdims=True) + jnp.sum(jnp.abs(h1), axis=0, keepdims=True)
    hp = (h0 + h1) / norm
    hm = (h0 - h1) / norm

    def dft(lo, x):
        xh = x.astype(BF16)
        xl = (x - xh.astype(F32)).astype(BF16)
        fh = fh_ref[lo:lo + n, :]
        return _dot(fh, xh) + _dot(fl_ref[lo:lo + n, :], xh) + _dot(fh, xl)

    kc = dft(0, hp)
    ks = dft(n, hm)
    kn = jnp.sum(hp * sign_ref[...], axis=0, keepdims=True)
    ka_ref[...] = kc
    kb_ref[...] = jnp.where(row == 0, 0.0, ks)
    ka2_ref[...] = jnp.where(row == 0, kn, kc)


def _hyena_filter(n, w1p, b1, w2, b2, w3, freq, decay):
    fwd, _, sign = _dft_consts(n)
    feats, t = _hyena_feats(n)
    fwd = jnp.asarray(fwd)
    fh = fwd.astype(BF16)
    fl = (fwd - fh.astype(F32)).astype(BF16)
    half = HY_ORDER * GROUP_W
    return pl.pallas_call(
        functools.partial(_hyfilt_kernel, n=n),
        out_shape=[jax.ShapeDtypeStruct((n, half), F32)] * 3,
        name="hyena_filter",
    )(jnp.asarray(feats), jnp.asarray(t), jnp.asarray(sign), w1p, b1, w2, b2, w3, freq, decay, fh, fl)


def _hyena_kernel(p_ref, cw_ref, ka_ref, kb_ref, ka2_ref, bias_ref, f_ref, g_ref, out_ref,
                  z_scr, x_scr, zf_scr, *, seq, conv_w):
    nch = seq // ML_CHUNK
    rowid = lax.broadcasted_iota(jnp.int32, (ML_CHUNK, 1), 0)

    def conv_chunk(ci, carry):
        r0 = pl.multiple_of(ci * ML_CHUNK, ML_CHUNK)
        zc = p_ref[pl.ds(r0, ML_CHUNK), :]
        zp = pltpu.roll(zc, 1, axis=0)
        zn = pltpu.roll(zc, ML_CHUNK - 1, axis=0)
        if conv_w == ML_CHUNK:
            prev_row = jnp.zeros((1, 3 * GROUP_W), F32)
            next_row = prev_row
        else:
            prev_row = p_ref[pl.ds(jnp.maximum(r0 - 1, 0), 1), :]
            prev_row = jnp.where(r0 % conv_w == 0, 0.0, prev_row)
            next_row = p_ref[pl.ds(jnp.minimum(r0 + ML_CHUNK, seq - 1), 1), :]
            next_row = jnp.where((r0 + ML_CHUNK) % conv_w == 0, 0.0, next_row)
        zp = jnp.where(rowid == 0, prev_row, zp)
        zn = jnp.where(rowid == ML_CHUNK - 1, next_row, zn)
        z_scr[pl.ds(r0, ML_CHUNK), :] = cw_ref[0:1, :] * zp + cw_ref[1:2, :] * zc + cw_ref[2:3, :] * zn
        return carry

    lax.fori_loop(0, nch, conv_chunk, 0)
    y = z_scr[:, 0:GROUP_W]
    for o in range(HY_ORDER):
        cols = slice(o * GROUP_W, (o + 1) * GROUP_W)
        x_scr[...] = _dot(f_ref[...], y.astype(BF16))
        xc = x_scr[0:seq, :]
        xs = x_scr[seq:2 * seq, :]
        kb = kb_ref[:, cols]
        zf_scr[0:seq, :] = (xc * ka_ref[:, cols] - xs * kb).astype(BF16)
        zf_scr[seq:2 * seq, :] = (xc * kb + xs * ka2_ref[:, cols]).astype(BF16)
        conv = _dot(g_ref[...], zf_scr[...])
        gate = z_scr[:, (o + 1) * GROUP_W:(o + 2) * GROUP_W]
        y = gate * (conv + bias_ref[o:o + 1, :] * y)
    out_ref[...] = y.astype(out_ref.dtype)


def _hyena(p3, conv_w3, filt, bias, conv_width):
    b, seq, _ = p3.shape
    ka, kb, ka2 = filt
    fwd, inv, _ = _dft_consts(seq)
    const2 = lambda i: (0, 0)
    half = HY_ORDER * GROUP_W
    return pl.pallas_call(
        functools.partial(_hyena_kernel, seq=seq, conv_w=conv_width),
        grid=(b,),
        in_specs=[
            pl.BlockSpec((None, seq, 3 * GROUP_W), lambda i: (i, 0, 0)),
            pl.BlockSpec((3, 3 * GROUP_W), const2),
            pl.BlockSpec((seq, half), const2),
            pl.BlockSpec((seq, half), const2),
            pl.BlockSpec((seq, half), const2),
            pl.BlockSpec((HY_ORDER, GROUP_W), const2),
            pl.BlockSpec((2 * seq, seq), const2),
            pl.BlockSpec((seq, 2 * seq), const2),
        ],
        out_specs=pl.BlockSpec((None, seq, GROUP_W), lambda i: (i, 0, 0)),
        out_shape=jax.ShapeDtypeStruct((b, seq, GROUP_W), BF16),
        scratch_shapes=[
            pltpu.VMEM((seq, 3 * GROUP_W), F32),
            pltpu.VMEM((2 * seq, GROUP_W), F32),
            pltpu.VMEM((2 * seq, GROUP_W), BF16),
        ],
        name="hyena",
    )(p3, conv_w3, ka, kb, ka2, bias, jnp.asarray(fwd, BF16), jnp.asarray(inv, BF16))


def _outproj_kernel(a_ref, b_ref, c_ref, d_ref, x_ref, g1_ref, sh2_ref, sc2_ref, n2_ref, wo_ref,
                    wr_ref, br_ref, x1_ref, h2_ref, gate_ref):
    acc = _dot(a_ref[...], wo_ref[0:GROUP_W, :])
    acc += _dot(b_ref[...], wo_ref[GROUP_W:2 * GROUP_W, :])
    acc += _dot(c_ref[...], wo_ref[2 * GROUP_W:3 * GROUP_W, :])
    acc += _dot(d_ref[...], wo_ref[3 * GROUP_W:4 * GROUP_W, :])
    x1 = x_ref[...] + g1_ref[0] * acc
    x1_ref[...] = x1
    h2 = _rmsnorm_rows(x1, n2_ref[...]) * (1.0 + sc2_ref[0]) + sh2_ref[0]
    h2_ref[...] = h2.astype(BF16)
    logits = _dot3(h2, wr_ref[...]) + br_ref[...]
    lane = lax.broadcasted_iota(jnp.int32, logits.shape, 1).astype(F32)
    big = float(LANES)
    gl = jnp.where(lane < N_EXPERT_GROUPS, logits, NEG_INF)
    gmax = jnp.max(gl, axis=1, keepdims=True)
    gsel = jnp.min(jnp.where(gl == gmax, lane, big), axis=1, keepdims=True)
    psel = 1.0 / jnp.sum(jnp.exp(gl - gmax), axis=1, keepdims=True)
    lo = N_EXPERT_GROUPS + EXPERTS_PER_GROUP * gsel
    el = jnp.where((lane >= lo) & (lane < lo + EXPERTS_PER_GROUP), logits, NEG_INF)
    v1 = jnp.max(el, axis=1, keepdims=True)
    i1 = jnp.min(jnp.where(el == v1, lane, big), axis=1, keepdims=True)
    el2 = jnp.where(lane == i1, NEG_INF, el)
    v2 = jnp.max(el2, axis=1, keepdims=True)
    i2 = jnp.min(jnp.where(el2 == v2, lane, big), axis=1, keepdims=True)
    e2 = jnp.exp(v2 - v1)
    w1 = psel / (1.0 + e2)
    w2 = psel * e2 / (1.0 + e2)
    gate_ref[...] = (jnp.where(lane == 0.0, gsel, 0.0) + jnp.where(lane == i1 - lo + 1.0, w1, 0.0)
                     + jnp.where(lane == i2 - lo + 1.0, w2, 0.0))


def _outproj(mix, x2d, mod3, row_of_tile, norm_g, w_out, w_r, b_r):
    t = x2d.shape[0]
    tm = TOK_TILE
    const = lambda i: (0, 0)
    tile = lambda i: (i, 0)
    return pl.pallas_call(
        _outproj_kernel,
        grid=(t // tm,),
        in_specs=[pl.BlockSpec((tm, GROUP_W), tile)] * 4 + [
            pl.BlockSpec((tm, D_MODEL), tile),
            _mod_spec(2, row_of_tile),
            _mod_spec(3, row_of_tile),
            _mod_spec(4, row_of_tile),
            pl.BlockSpec((1, D_MODEL), const),
            pl.BlockSpec((D_MODEL, D_MODEL), const),
            pl.BlockSpec((D_MODEL, LANES), const),
            pl.BlockSpec((1, LANES), const),
        ],
        out_specs=[pl.BlockSpec((tm, D_MODEL), tile), pl.BlockSpec((tm, D_MODEL), tile),
                   pl.BlockSpec((tm, LANES), tile)],
        out_shape=[jax.ShapeDtypeStruct((t, D_MODEL), F32), jax.ShapeDtypeStruct((t, D_MODEL), BF16),
                   jax.ShapeDtypeStruct((t, LANES), F32)],
        name="outproj_router",
    )(*mix, x2d, mod3, mod3, mod3, norm_g, w_out, w_r, b_r)


def _tile_positions(gates, tri_ref, extra):
    lane = lax.broadcasted_iota(jnp.int32, gates.shape, 1).astype(F32)
    gsel = gates[:, 0:1]
    member = jnp.where((lane == gsel) & (lane < N_EXPERT_GROUPS), 1.0, 0.0)
    before = _dot(tri_ref[...], member.astype(BF16))
    pos = gsel * MOE_TOK + jnp.sum(member * (before + extra), axis=1, keepdims=True)
    slot = lax.broadcasted_iota(jnp.int32, (MOE_TOK, N_EXPERT_GROUPS * MOE_TOK), 1).astype(F32)
    return jnp.where(slot == pos, 1.0, 0.0).astype(BF16)


def _moe_pack_kernel(off_ref, cnt_ref, tot_ref, h_ref, gate_ref, tri_ref, xg_ref, gg_ref,
                     xs_scr, gs_scr, zx_scr, zg_scr, sem):
    i = pl.program_id(0)
    slot = i % 2
    gates = gate_ref[...]
    lane = lax.broadcasted_iota(jnp.int32, gates.shape, 1).astype(F32)
    member = jnp.where((lane == gates[:, 0:1]) & (lane < N_EXPERT_GROUPS), 1.0, 0.0).astype(BF16)
    grp = lax.broadcasted_iota(jnp.int32, (8, LANES), 0)
    pick = jnp.where(grp == lax.broadcasted_iota(jnp.int32, (8, LANES), 1), 1.0, 0.0).astype(BF16)
    member_t = _dot_nt(pick, member)
    before_t = _dot_nt(member_t.astype(BF16), tri_ref[...])
    gid = lax.broadcasted_iota(jnp.int32, member_t.shape, 0).astype(F32)
    pos_t = jnp.sum(member_t * (gid * MOE_TOK + before_t), axis=0, keepdims=True)
    row_id = lax.broadcasted_iota(jnp.int32, (N_EXPERT_GROUPS * MOE_TOK, MOE_TOK), 0).astype(F32)
    p = jnp.where(row_id == pos_t, 1.0, 0.0).astype(BF16)
    xs_scr[slot] = _dot(p, h_ref[...]).astype(BF16)
    hi, mid, lo = _split3(gates)
    gs_scr[slot] = _dot(p, hi) + _dot(p, mid) + _dot(p, lo)

    def for_each_piece(tile, buf, action):
        for g in range(N_EXPERT_GROUPS):
            off = pl.multiple_of(off_ref[tile * N_EXPERT_GROUPS + g], ROW_ALIGN)
            for j in range(MOE_TOK // MOE_PIECE):
                src = pl.ds(g * MOE_TOK + j * MOE_PIECE, MOE_PIECE)
                dst = pl.ds(off + j * MOE_PIECE, MOE_PIECE)

                @pl.when(cnt_ref[tile * N_EXPERT_GROUPS + g] > j * MOE_PIECE)
                def _():
                    action(pltpu.make_async_copy(xs_scr.at[buf, src], xg_ref.at[g, dst], sem.at[buf, 0, g, j]))
                    action(pltpu.make_async_copy(gs_scr.at[buf, src], gg_ref.at[g, dst], sem.at[buf, 1, g, j]))

    @pl.when(i > 0)
    def _():
        for_each_piece(i - 1, 1 - slot, lambda cp: cp.wait())

    for_each_piece(i, slot, lambda cp: cp.start())

    @pl.when(i == pl.num_programs(0) - 1)
    def _():
        for_each_piece(i, slot, lambda cp: cp.wait())
        zx_scr[...] = jnp.zeros_like(zx_scr)
        zg_scr[...] = jnp.zeros_like(zg_scr)

        def tail_copies(g):
            tot = pl.multiple_of(tot_ref[g], ROW_ALIGN)
            return (pltpu.make_async_copy(zx_scr, xg_ref.at[g, pl.ds(tot, MOE_ROWS)], sem.at[0, 0, g, 0]),
                    pltpu.make_async_copy(zg_scr, gg_ref.at[g, pl.ds(tot, MOE_ROWS)], sem.at[0, 1, g, 0]))

        for g in range(N_EXPERT_GROUPS):
            for cp in tail_copies(g):
                cp.start()
        for g in range(N_EXPERT_GROUPS):
            for cp in tail_copies(g):
                cp.wait()


def _moe_expert_kernel(grp_ref, chk_ref, nact_ref, x_ref, g_ref, wg_ref, wu_ref, wd_ref, y_ref):
    @pl.when(pl.program_id(0) < nact_ref[0])
    def _():
        x = x_ref[...]
        gates = g_ref[...]
        acc = None
        for e in range(EXPERTS_PER_GROUP):
            a = _dot(x, wg_ref[e])
            u = _dot(x, wu_ref[e])
            hid = (_silu(a) * u * gates[:, 1 + e:2 + e]).astype(BF16)
            part = _dot(hid, wd_ref[e])
            acc = part if acc is None else acc + part
        y_ref[...] = acc.astype(y_ref.dtype)


def _moe_unpack_kernel(st_ref, dl_ref, cnt_ref, gate_ref, tri_ref, x1_ref, g2_ref, fn_ref, yg_ref, out_ref,
                       ys_scr, sem, *, final):
    i = pl.program_id(0)
    slot = i % 2

    def for_each_piece(tile, buf, action):
        for g in range(N_EXPERT_GROUPS):
            k = tile * N_EXPERT_GROUPS + g
            start = pl.multiple_of(st_ref[k], ROW_ALIGN)
            for j in range(MOE_TOK // MOE_PIECE):
                lo, hi = j * MOE_PIECE, (j + 1) * MOE_PIECE

                @pl.when((dl_ref[k] < hi) & (dl_ref[k] + cnt_ref[k] > lo))
                def _():
                    action(pltpu.make_async_copy(yg_ref.at[g, pl.ds(start + lo, MOE_PIECE)],
                                                 ys_scr.at[buf, pl.ds(g * MOE_TOK + lo, MOE_PIECE)],
                                                 sem.at[buf, g, j]))

    @pl.when(i == 0)
    def _():
        ys_scr[...] = jnp.zeros_like(ys_scr)
        for_each_piece(0, 0, lambda cp: cp.start())

    @pl.when(i + 1 < pl.num_programs(0))
    def _():
        for_each_piece(i + 1, 1 - slot, lambda cp: cp.start())

    gates = gate_ref[...]
    lane = lax.broadcasted_iota(jnp.int32, (1, LANES), 1)
    delta = jnp.zeros((1, LANES), F32)
    for g in range(N_EXPERT_GROUPS):
        delta = jnp.where(lane == g, dl_ref[i * N_EXPERT_GROUPS + g].astype(F32), delta)
    pt = _tile_positions(gates, tri_ref, delta)
    for_each_piece(i, slot, lambda cp: cp.wait())
    x2 = x1_ref[...] + g2_ref[0] * _dot(pt, ys_scr[slot])
    if final:
        x2 = _rmsnorm_rows(x2, fn_ref[...])
    out_ref[...] = x2


def _moe_schedule(gsel, n_tiles, n_steps):
    ng = N_EXPERT_GROUPS
    member = gsel.reshape(n_tiles, MOE_TOK, 1) == jnp.arange(ng, dtype=jnp.int32)
    cnt = jnp.sum(member.astype(jnp.int32), axis=1)
    padded = (cnt + ROW_ALIGN - 1) // ROW_ALIGN * ROW_ALIGN
    off = jnp.cumsum(padded, axis=0) - padded
    tot = jnp.sum(padded, axis=0)
    n_chunks = jnp.maximum((tot + MOE_ROWS - 1) // MOE_ROWS, 1)
    start = jnp.minimum(off, n_chunks * MOE_ROWS - MOE_TOK)
    ends = jnp.cumsum(n_chunks)
    n_active = ends[-1]
    step = jnp.minimum(jnp.arange(n_steps, dtype=jnp.int32), n_active - 1)
    grp = jnp.sum((step[:, None] >= ends[None, :]).astype(jnp.int32), axis=1)
    chk = step - (ends - n_chunks)[grp]
    i32 = lambda a: a.astype(jnp.int32)
    return (i32(off).reshape(-1), i32(padded).reshape(-1), i32(tot), i32(start).reshape(-1),
            i32(off - start).reshape(-1), i32(grp), i32(chk), i32(n_active).reshape(1))


def _moe(h2, gates, layer, w_gate, w_up, w_down, x1, mod3, row_of_tile, final_g, final):
    t = h2.shape[0]
    ng = N_EXPERT_GROUPS
    n_tiles = t // MOE_TOK
    cap = -(-(t + ROW_ALIGN * n_tiles + MOE_ROWS) // MOE_ROWS) * MOE_ROWS
    n_steps = (t + ROW_ALIGN * n_tiles) // MOE_ROWS + ng
    off, cnt, tot, start, delta, grp, chk, n_active = _moe_schedule(gates[:, 0].astype(jnp.int32), n_tiles,
                                                                    n_steps)
    pieces = MOE_TOK // MOE_PIECE
    r = np.arange(MOE_TOK)
    tri = jnp.asarray((r[:, None] > r[None, :]).astype(np.float32), BF16)
    any_spec = pl.BlockSpec(memory_space=pl.ANY)

    xg, gg = pl.pallas_call(
        _moe_pack_kernel,
        grid_spec=pltpu.PrefetchScalarGridSpec(
            num_scalar_prefetch=3,
            grid=(n_tiles,),
            in_specs=[
                pl.BlockSpec((MOE_TOK, D_MODEL), lambda i, *_: (i, 0)),
                pl.BlockSpec((MOE_TOK, LANES), lambda i, *_: (i, 0)),
                pl.BlockSpec((MOE_TOK, MOE_TOK), lambda i, *_: (0, 0)),
            ],
            out_specs=[any_spec, any_spec],
            scratch_shapes=[
                pltpu.VMEM((2, ng * MOE_TOK, D_MODEL), BF16),
                pltpu.VMEM((2, ng * MOE_TOK, LANES), F32),
                pltpu.VMEM((MOE_ROWS, D_MODEL), BF16),
                pltpu.VMEM((MOE_ROWS, LANES), F32),
                pltpu.SemaphoreType.DMA((2, 2, ng, pieces)),
            ],
        ),
        out_shape=[jax.ShapeDtypeStruct((ng, cap, D_MODEL), BF16), jax.ShapeDtypeStruct((ng, cap, LANES), F32)],
        name="moe_pack",
    )(off, cnt, tot, h2, gates, tri)

    group_w = lambda shape: pl.BlockSpec((None, None, EXPERTS_PER_GROUP) + shape,
                                         lambda k, grp, chk, na: (layer, grp[k], 0, 0, 0))
    rows = lambda width: pl.BlockSpec((None, MOE_ROWS, width), lambda k, grp, chk, na: (grp[k], chk[k], 0))
    yg = pl.pallas_call(
        _moe_expert_kernel,
        grid_spec=pltpu.PrefetchScalarGridSpec(
            num_scalar_prefetch=3,
            grid=(n_steps,),
            in_specs=[rows(D_MODEL), rows(LANES), group_w((D_MODEL, D_EXPERT)), group_w((D_MODEL, D_EXPERT)),
                      group_w((D_EXPERT, D_MODEL))],
            out_specs=rows(D_MODEL),
        ),
        out_shape=jax.ShapeDtypeStruct((ng, cap, D_MODEL), BF16),
        name="moe_experts",
    )(grp, chk, n_active, xg, gg,
      w_gate.reshape(DEPTH, ng, EXPERTS_PER_GROUP, D_MODEL, D_EXPERT),
      w_up.reshape(DEPTH, ng, EXPERTS_PER_GROUP, D_MODEL, D_EXPERT),
      w_down.reshape(DEPTH, ng, EXPERTS_PER_GROUP, D_EXPERT, D_MODEL))

    return pl.pallas_call(
        functools.partial(_moe_unpack_kernel, final=final),
        grid_spec=pltpu.PrefetchScalarGridSpec(
            num_scalar_prefetch=3,
            grid=(n_tiles,),
            in_specs=[
                pl.BlockSpec((MOE_TOK, LANES), lambda i, *_: (i, 0)),
                pl.BlockSpec((MOE_TOK, MOE_TOK), lambda i, *_: (0, 0)),
                pl.BlockSpec((MOE_TOK, D_MODEL), lambda i, *_: (i, 0)),
                pl.BlockSpec((1, 1, D_MODEL), lambda i, *_: (row_of_tile(i), 0, 5)),
                pl.BlockSpec((1, D_MODEL), lambda i, *_: (0, 0)),
                any_spec,
            ],
            out_specs=pl.BlockSpec((MOE_TOK, D_MODEL), lambda i, *_: (i, 0)),
            scratch_shapes=[pltpu.VMEM((2, ng * MOE_TOK, D_MODEL), BF16),
                            pltpu.SemaphoreType.DMA((2, ng, pieces))],
        ),
        out_shape=jax.ShapeDtypeStruct((t, D_MODEL), F32),
        name="moe_unpack",
    )(start, delta, cnt, gates, tri, x1, mod3, final_g, yg)


def _block_diag_t(s):
    eye = jnp.eye(HEADS, dtype=s.dtype)
    b = s.shape[0]
    return jnp.einsum('bdhkv,hg->bdhvgk', s, eye).reshape(b, 2, GROUP_W, GROUP_W)


def _prep_layer(l, w):
    gw = GROUP_W
    w_in = w['w_in'][l]
    w_g = w_in[:, 9 * gw:9 * gw + N_GATES]
    lb_all = jnp.cumsum(jax.nn.softmax(w['hg_lb'].astype(F32), axis=0), axis=0)
    lb = lb_all[l] - lb_all[0]
    lb_rows = jnp.zeros((8, gw), F32).at[0].set(jnp.log(lb)).at[1].set(jnp.log1p(-lb)).at[2].set(1.0 - lb)
    w_r = jnp.zeros((D_MODEL, LANES), F32)
    w_r = w_r.at[:, 0:N_EXPERT_GROUPS].set(w['moe_wg'][l])
    w_r = w_r.at[:, N_EXPERT_GROUPS:N_EXPERT_GROUPS + N_EXPERTS].set(w['moe_we'][l])
    b_r = jnp.zeros((1, LANES), F32)
    b_r = b_r.at[0, 0:N_EXPERT_GROUPS].set(w['moe_bg'][l])
    b_r = b_r.at[0, N_EXPERT_GROUPS:N_EXPERT_GROUPS + N_EXPERTS].set(w['moe_be'][l])
    gate_bias = jnp.concatenate([w['ml_i_bias'][l], w['ml_f_bias'][l]])
    w1p = jnp.zeros((LANES, HY_HIDDEN), F32).at[0:HY_EMB].set(w['hy_w1'][l])
    return dict(
        norm1=w['norm1'][l][None, :], norm2=w['norm2'][l][None, :],
        w_a=w_in[:, :9 * gw].astype(BF16), w_b=w_in[:, 9 * gw + N_GATES:].astype(BF16),
        w_g=jnp.zeros((D_MODEL, LANES), BF16).at[:, 0:N_GATES].set(w_g.astype(BF16)), w_gt=w_g.T.astype(BF16),
        w_out=w['w_out'][l].astype(BF16),
        lb_rows=lb_rows, hg_norm=w['hg_norm'][l][None, :],
        ml_conv=w['ml_conv'][l], ml_brow=jnp.zeros((1, LANES), F32).at[0, 0:N_GATES].set(gate_bias),
        ml_brows=jnp.repeat(gate_bias.reshape(4, HEADS), HEAD_D, axis=1),
        ml_norm=w['ml_norm'][l][None, :],
        s5=_s5_params(w['s5_a_re'][l], w['s5_a_im'][l], w['s5_log_dt'][l], w['s5_b_re'][l], w['s5_b_im'][l],
                      w['s5_c_re'][l], w['s5_c_im'][l]),
        s5_d=w['s5_d'][l][None, :], s5_glu_w=w['s5_glu_w'][l].astype(BF16), s5_glu_b=w['s5_glu_b'][l][None, :],
        hy_conv=w['hy_conv'][l], hy_bias=w['hy_bias'][l],
        hy_mlp=(w1p, w['hy_b1'][l][None, :], w['hy_w2'][l], w['hy_b2'][l][None, :], w['hy_w3'][l],
                w['hy_freq'][l][None, :], w['hy_decay'][l][None, :]),
        w_r=w_r, b_r=b_r,
        layer=l,
    )


def _trunk_layer(x2d, nb, seq, mod3, cond_row, st, lw, moe_w, filt, conv_width, emit_state, final_g, final):
    if cond_row is None:
        row_of_tile = lambda i: i // (seq // TOK_TILE)
    else:
        row_of_tile = lambda i: cond_row
    hg, ml, su, hy, gt, gtt = _inproj(x2d, mod3, row_of_tile, lw['norm1'], lw['w_a'], lw['w_b'], lw['w_g'],
                                       lw['w_gt'])
    nch = seq // ML_CHUNK
    a_out, hg_st = _hgrn(hg.reshape(nb, seq, -1), st['hgrn'], lw['lb_rows'], lw['hg_norm'], emit_state)
    gr4 = jnp.transpose(gtt.reshape(4, HEADS, nb, nch, ML_CHUNK), (2, 0, 3, 1, 4)).reshape(nb, 4, nch, GROUP_W)
    b_out, mlc, mln, mlm = _mlstm(ml.reshape(nb, seq, -1), gt.reshape(nb, seq, LANES), gr4, lw['ml_conv'],
                                  lw['ml_brow'], lw['ml_brows'], lw['ml_norm'], st['ml_c'], st['ml_n'],
                                  st['ml_m'], conv_width, emit_state)
    c_out, s5re, s5im = _s5(su.reshape(nb, seq, -1), st['s5_re'], st['s5_im'], lw['s5'], lw['s5_d'],
                            lw['s5_glu_w'], lw['s5_glu_b'])
    d_out = _hyena(hy.reshape(nb, seq, -1), lw['hy_conv'], filt, lw['hy_bias'], conv_width)
    mix = [m.reshape(nb * seq, GROUP_W) for m in (a_out, b_out, c_out, d_out)]
    x1, h2, gates = _outproj(mix, x2d, mod3, row_of_tile, lw['norm2'], lw['w_out'], lw['w_r'], lw['b_r'])
    x2 = _moe(h2, gates, lw['layer'], *moe_w, x1, mod3, row_of_tile, final_g, final)
    new_st = dict(hgrn=hg_st, ml_c=mlc, ml_n=mln, ml_m=mlm, s5_re=s5re, s5_im=s5im)
    return x2, new_st


def kernel(x_prompt, x_sample, c, state_hgrn, state_mlstm_c, state_mlstm_n, state_mlstm_m, state_s5_re, state_s5_im, c_ctx, w_ada, b_ada, norm1, norm2, w_in, w_out, hg_lb, hg_norm, ml_conv, ml_i_bias, ml_f_bias, ml_norm, s5_a_re, s5_a_im, s5_log_dt, s5_b_re, s5_b_im, s5_c_re, s5_c_im, s5_d, s5_glu_w, s5_glu_b, hy_conv, hy_w1, hy_b1, hy_w2, hy_b2, hy_w3, hy_freq, hy_decay, hy_bias, moe_wg, moe_bg, moe_we, moe_be, moe_w_gate, moe_w_up, moe_w_down, final_norm):
    w = dict(w_in=w_in, w_out=w_out, norm1=norm1, norm2=norm2, hg_lb=hg_lb, hg_norm=hg_norm, ml_conv=ml_conv,
             ml_i_bias=ml_i_bias, ml_f_bias=ml_f_bias, ml_norm=ml_norm, s5_a_re=s5_a_re, s5_a_im=s5_a_im,
             s5_log_dt=s5_log_dt, s5_b_re=s5_b_re, s5_b_im=s5_b_im, s5_c_re=s5_c_re, s5_c_im=s5_c_im,
             s5_d=s5_d, s5_glu_w=s5_glu_w, s5_glu_b=s5_glu_b, hy_conv=hy_conv, hy_w1=hy_w1, hy_b1=hy_b1,
             hy_w2=hy_w2, hy_b2=hy_b2, hy_w3=hy_w3, hy_freq=hy_freq, hy_decay=hy_decay, hy_bias=hy_bias,
             moe_wg=moe_wg, moe_bg=moe_bg, moe_we=moe_we, moe_be=moe_be, moe_w_gate=moe_w_gate,
             moe_w_up=moe_w_up, moe_w_down=moe_w_down)
    bp, lp, _ = x_prompt.shape
    bs, ls, _ = x_sample.shape
    ctx_row = bs
    cond = jnp.zeros((MOD_ROWS, D_MODEL), F32).at[0:bs].set(c).at[ctx_row].set(c_ctx)
    mod = _ada_mod(cond, w_ada, b_ada)
    final_g = final_norm[None, :]

    zero_st = dict(
        hgrn=jnp.zeros((bp, 2, GROUP_W, GROUP_W), F32),
        ml_c=jnp.zeros((bp, 2, HEADS, HEAD_D, HEAD_D), F32),
        ml_n=jnp.zeros((bp, 2, 1, GROUP_W), F32),
        ml_m=jnp.zeros((bp, 2, 1, GROUP_W), F32),
        s5_re=jnp.zeros((bp, 2, S5_STATE), F32),
        s5_im=jnp.zeros((bp, 2, S5_STATE), F32))

    xp = x_prompt.reshape(bp * lp, D_MODEL)
    xs = x_sample.reshape(bs * ls, D_MODEL)
    moe_w = (moe_w_gate.astype(BF16), moe_w_up.astype(BF16), moe_w_down.astype(BF16))
    per_layer = []
    for l in range(DEPTH):
        lw = _prep_layer(l, w)
        mod3 = mod[l].reshape(MOD_ROWS, 1, 6 * D_MODEL)
        last = l == DEPTH - 1
        filt_p = _hyena_filter(lp, *lw['hy_mlp'])
        filt_s = filt_p if ls == lp else _hyena_filter(ls, *lw['hy_mlp'])
        xp, st = _trunk_layer(xp, bp, lp, mod3, ctx_row, zero_st, lw, moe_w, filt_p, lp, True, final_g, last)
        per_layer.append(st)
        st_in = dict(
            hgrn=_block_diag_t(state_hgrn[:, l].astype(F32)),
            ml_c=state_mlstm_c[:, l].astype(F32),
            ml_n=state_mlstm_n[:, l].astype(F32).reshape(bs, 2, 1, GROUP_W),
            ml_m=jnp.repeat(state_mlstm_m[:, l].astype(F32), HEAD_D, axis=-1).reshape(bs, 2, 1, GROUP_W),
            s5_re=state_s5_re[:, l].astype(F32).reshape(bs, 2, S5_STATE),
            s5_im=state_s5_im[:, l].astype(F32).reshape(bs, 2, S5_STATE))
        xs, _ = _trunk_layer(xs, bs, ls, mod3, None, st_in, lw, moe_w, filt_s, GRID_W, False, final_g, last)

    new_hgrn = jnp.stack([s['hgrn'] for s in per_layer], axis=1)
    new_ml_c = jnp.stack([s['ml_c'] for s in per_layer], axis=1)
    new_ml_n = jnp.stack([s['ml_n'].reshape(bp, 2, HEADS, HEAD_D) for s in per_layer], axis=1)
    new_ml_m = jnp.stack([s['ml_m'][:, :, 0, ::HEAD_D] for s in per_layer], axis=1)
    new_s5_re = jnp.stack([s['s5_re'].reshape(bp, 2, S5_GROUPS, S5_P) for s in per_layer], axis=1)
    new_s5_im = jnp.stack([s['s5_im'].reshape(bp, 2, S5_GROUPS, S5_P) for s in per_layer], axis=1)
    return (xp.reshape(bp, lp, D_MODEL), xs.reshape(bs, ls, D_MODEL),
            new_hgrn, new_ml_c, new_ml_n, new_ml_m, new_s5_re, new_s5_im)
```

```python
import functools
import math

import numpy as np
import jax
import jax.numpy as jnp
from jax import lax
from jax.experimental import pallas as pl
from jax.experimental.pallas import tpu as pltpu

F32 = jnp.float32
BF16 = jnp.bfloat16

D_MODEL = 1024
DEPTH = 2
GRID_W = 64
GROUP_W = D_MODEL // 4
HEADS = 4
HEAD_D = GROUP_W // HEADS
S5_CH = 16
S5_GROUPS = GROUP_W // S5_CH
S5_P = 64
S5_STATE = S5_GROUPS * S5_P
HY_ORDER = 2
HY_BANDS = 16
HY_EMB = 1 + 2 * HY_BANDS
HY_HIDDEN = 64
HY_SHIFT = 0.05
N_EXPERT_GROUPS = 4
EXPERTS_PER_GROUP = 4
N_EXPERTS = N_EXPERT_GROUPS * EXPERTS_PER_GROUP
D_EXPERT = D_MODEL // 2
EPS = 1e-6
N_GATES = 4 * HEADS

LANES = 128
MOD_ROWS = 16
TOK_TILE = 256
IN_TILE = 512
MOE_TOK = TOK_TILE
MOE_PIECE = 64
MOE_ROWS = 512
ROW_ALIGN = 16
HG_BLK = 32
HG_SB = 128
HG_SAFE_LOG = -75.0
HG_SAFE_Q = 1e3
ML_CHUNK = 64
S5_TC = 64
NEG_INF = float("-inf")


def _dot(a, b):
    return jnp.dot(a, b, preferred_element_type=F32)


def _dot_nt(a, b):
    return lax.dot_general(a, b, (((1,), (1,)), ((), ())), preferred_element_type=F32)


def _dot_tn(a, b):
    return lax.dot_general(a, b, (((0,), (0,)), ((), ())), preferred_element_type=F32)


def _split3(x):
    hi = x.astype(BF16)
    r1 = x - hi.astype(F32)
    mid = r1.astype(BF16)
    lo = (r1 - mid.astype(F32)).astype(BF16)
    return hi, mid, lo


def _dot_sel_l(m, x):
    hi, mid, lo = _split3(x)
    return _dot(m, hi) + _dot(m, mid) + _dot(m, lo)


def _dot_sel_r(x, m):
    hi, mid, lo = _split3(x)
    return _dot(hi, m) + _dot(mid, m) + _dot(lo, m)


def _dot3(a, b):
    ah = a.astype(BF16)
    al = (a - ah.astype(F32)).astype(BF16)
    bh = b.astype(BF16)
    bl = (b - bh.astype(F32)).astype(BF16)
    return _dot(ah, bh) + _dot(al, bh) + _dot(ah, bl)


def _sigmoid(x):
    return 1.0 / (1.0 + jnp.exp(-x))


def _silu(x):
    return x * _sigmoid(x)


def _log_sigmoid(x):
    return jnp.minimum(x, 0.0) - jnp.log(1.0 + jnp.exp(-jnp.abs(x)))


def _rmsnorm_rows(x, g):
    return x * lax.rsqrt(jnp.mean(x * x, axis=-1, keepdims=True) + EPS) * g


def _ada_kernel(c_ref, w_ref, b_ref, o_ref):
    s = _silu(c_ref[...]).astype(BF16)
    o_ref[0] = _dot(s, w_ref[0].astype(BF16)) + b_ref[0]


def _ada_mod(cond, w_ada, b_ada):
    tn = 1536
    n = w_ada.shape[-1]
    return pl.pallas_call(
        _ada_kernel,
        grid=(DEPTH, n // tn),
        in_specs=[
            pl.BlockSpec((MOD_ROWS, D_MODEL), lambda l, j: (0, 0)),
            pl.BlockSpec((1, D_MODEL, tn), lambda l, j: (l, 0, j)),
            pl.BlockSpec((1, 1, tn), lambda l, j: (l, 0, j)),
        ],
        out_specs=pl.BlockSpec((1, MOD_ROWS, tn), lambda l, j: (l, 0, j)),
        out_shape=jax.ShapeDtypeStruct((DEPTH, MOD_ROWS, n), F32),
        name="ada_mod",
    )(cond, w_ada, b_ada.reshape(DEPTH, 1, n))


def _inproj_kernel(x_ref, sh_ref, sc_ref, g_ref, wa_ref, wb_ref, wg_ref, wgt_ref,
                   hg_ref, ml_ref, su_ref, hy_ref, gt_ref, gtt_ref):
    h = _rmsnorm_rows(x_ref[...], g_ref[...])
    h = h * (1.0 + sc_ref[0]) + sh_ref[0]
    hb = h.astype(BF16)
    hg_ref[...] = _dot(hb, wa_ref[:, 0:5 * GROUP_W])
    ml_ref[...] = _dot(hb, wa_ref[:, 5 * GROUP_W:9 * GROUP_W])
    su_ref[...] = _dot(hb, wb_ref[:, 0:GROUP_W])
    hy_ref[...] = _dot(hb, wb_ref[:, GROUP_W:4 * GROUP_W])
    gt_ref[...] = _dot(hb, wg_ref[...])
    gtt_ref[...] = _dot_nt(wgt_ref[...], hb)


def _mod_spec(k, row_of_tile):
    return pl.BlockSpec((1, 1, D_MODEL), lambda i: (row_of_tile(i), 0, k))


def _inproj(x2d, mod3, row_of_tile, norm_g, w_a, w_b, w_g, w_gt):
    t = x2d.shape[0]
    tm = IN_TILE
    const = lambda i: (0, 0)
    tile = lambda i: (i, 0)
    widths = (5 * GROUP_W, 4 * GROUP_W, GROUP_W, 3 * GROUP_W)
    return pl.pallas_call(
        _inproj_kernel,
        grid=(t // tm,),
        in_specs=[
            pl.BlockSpec((tm, D_MODEL), tile),
            _mod_spec(0, row_of_tile),
            _mod_spec(1, row_of_tile),
            pl.BlockSpec((1, D_MODEL), const),
            pl.BlockSpec(w_a.shape, const),
            pl.BlockSpec(w_b.shape, const),
            pl.BlockSpec(w_g.shape, const),
            pl.BlockSpec(w_gt.shape, const),
        ],
        out_specs=[pl.BlockSpec((tm, w), tile) for w in widths]
        + [pl.BlockSpec((tm, LANES), tile), pl.BlockSpec((N_GATES, tm), lambda i: (0, i))],
        out_shape=[jax.ShapeDtypeStruct((t, w), F32) for w in widths]
        + [jax.ShapeDtypeStruct((t, LANES), F32), jax.ShapeDtypeStruct((N_GATES, t), F32)],
        name="inproj",
    )(x2d, mod3, mod3, norm_g, w_a, w_b, w_g, w_gt)


@functools.lru_cache(maxsize=None)
def _hgrn_consts():
    r = np.arange(HG_SB)
    same = (r[:, None] // HG_BLK) == (r[None, :] // HG_BLK)
    tri_f = same & (r[:, None] >= r[None, :])
    tri_b = same & (r[:, None] <= r[None, :])
    c = np.arange(GROUP_W)
    head = (c[:, None] // HEAD_D) == (c[None, :] // HEAD_D)
    rr = np.arange(HEADS * HG_SB)
    head4 = (rr[:, None] // HG_SB) == (c[None, :] // HEAD_D)
    pairs = np.stack([np.tile(tri_f, (1, HEADS)), np.tile(tri_b, (1, HEADS))])
    return (np.stack([tri_f, tri_b]).astype(np.float32), same.astype(np.float32),
            head.astype(np.float32), head4.astype(np.float32), pairs.astype(np.float32))


def _hgrn_kernel(p_ref, s0_ref, lb_ref, ng_ref, tri_ref, bones_ref, e_ref, hmask_ref, hmask4_ref,
                 smask_ref, out_ref, *rest, seq, emit_state):
    if emit_state:
        st_ref = rest[0]
        rest = rest[1:]
    o_scr, st_scr, q_scr, k_scr, bc_scr, gq_scr, gk_scr, ks_scr, dec_scr, oi_scr = rest
    nsb = seq // HG_SB
    nblk = HG_SB // HG_BLK
    log_lb = lb_ref[0:1, :]
    log_1mlb = lb_ref[1:2, :]
    one_m_lb = lb_ref[2:3, :]
    row = lax.broadcasted_iota(jnp.int32, (HG_BLK, 1), 0)

    def stage(r0, d):
        rows = pl.ds(r0, HG_SB)
        pq = p_ref[rows, 0:GROUP_W]
        pf = p_ref[rows, (1 + d) * GROUP_W:(2 + d) * GROUP_W]
        q = _silu(pq)
        e = jnp.exp(-jnp.abs(pf))
        b2 = log_1mlb + jnp.minimum(pf, 0.0) - jnp.log(1.0 + e)
        mx = jnp.maximum(log_lb, b2)
        mn = jnp.minimum(log_lb, b2)
        lf = mx + jnp.log(1.0 + jnp.exp(mn - mx))
        k = one_m_lb * jnp.where(pf >= 0.0, e, 1.0) / (1.0 + e)
        hi, mid, lo = _split3(lf)
        bc = _dot(tri_ref[d], hi) + _dot(tri_ref[d], mid) + _dot(tri_ref[d], lo)
        bt = _dot(bones_ref[...], hi) + _dot(bones_ref[...], mid) + _dot(bones_ref[...], lo)
        gq_scr[d] = (q * jnp.exp(bc)).astype(BF16)
        gk_scr[d] = (k * jnp.exp(bt - bc)).astype(BF16)
        dec_scr[d] = jnp.exp(bt)
        ks_scr[d] = (k * jnp.exp(-bc)).astype(BF16)
        q_scr[d] = q
        k_scr[d] = k
        bc_scr[d] = bc
        return (jnp.min(bc) >= HG_SAFE_LOG) & (jnp.max(jnp.abs(q)) <= HG_SAFE_Q)

    def block_edge_path(r0, d):
        vb = p_ref[pl.ds(r0, HG_SB), 3 * GROUP_W:4 * GROUP_W].astype(BF16)
        ksbd = jnp.concatenate([ks_scr[d]] * HEADS, axis=0) * hmask4_ref[...]
        vbd = jnp.concatenate([vb] * HEADS, axis=0) * hmask4_ref[...]
        s = _dot_nt(gq_scr[d], ksbd) * smask_ref[d]
        oi_scr[d] = _dot(s.astype(BF16), vbd)

    def exact_path(r0, d):
        for j in range(nblk):
            sl = slice(HG_BLK * j, HG_BLK * (j + 1))
            qj = q_scr[d, sl, :]
            kj = k_scr[d, sl, :]
            vj = p_ref[pl.ds(r0 + HG_BLK * j, HG_BLK), 3 * GROUP_W:4 * GROUP_W]
            bcj = bc_scr[d, sl, :]
            parts = []
            for s in range(HG_BLK):
                keep = (row >= s) if d == 0 else (row <= s)
                dd = jnp.exp(jnp.where(keep, bcj - bcj[s:s + 1, :], NEG_INF))
                parts.append((qj * dd * kj[s:s + 1, :]).astype(BF16))
            r_all = _dot(jnp.concatenate(parts, axis=0), e_ref[...])
            o = r_all[0:HG_BLK, :] * vj[0:1, :]
            for s in range(1, HG_BLK):
                o = o + r_all[HG_BLK * s:HG_BLK * (s + 1), :] * vj[s:s + 1, :]
            oi_scr[d, sl, :] = o

    def chain(r0, d):
        order = range(nblk) if d == 0 else range(nblk - 1, -1, -1)
        for j in order:
            sl = slice(HG_BLK * j, HG_BLK * (j + 1))
            vb = p_ref[pl.ds(r0 + HG_BLK * j, HG_BLK), 3 * GROUP_W:4 * GROUP_W].astype(BF16)
            upd = _dot_tn(vb, gk_scr[d, sl, :]) * hmask_ref[...]
            st = st_scr[d]
            o_scr[d, pl.ds(r0 + HG_BLK * j, HG_BLK), :] = oi_scr[d, sl, :] + _dot_nt(gq_scr[d, sl, :], st.astype(BF16))
            st_scr[d] = st * dec_scr[d, HG_BLK * j:HG_BLK * j + 1, :] + upd

    eye_h = jnp.where(lax.broadcasted_iota(jnp.int32, (HEAD_D, HEAD_D), 0)
                      == lax.broadcasted_iota(jnp.int32, (HEAD_D, HEAD_D), 1), 1.0, 0.0).astype(BF16)
    st_scr[...] = jnp.zeros_like(st_scr)
    for d in range(2):
        for h in range(HEADS):
            hi, mid, lo = _split3(s0_ref[d, h])
            lanes = slice(HEAD_D * h, HEAD_D * (h + 1))
            st_scr[d, lanes, lanes] = _dot_tn(hi, eye_h) + _dot_tn(mid, eye_h) + _dot_tn(lo, eye_h)

    def body(i, carry):
        rf = pl.multiple_of(i * HG_SB, HG_SB)
        rb = pl.multiple_of((nsb - 1 - i) * HG_SB, HG_SB)
        safe = stage(rf, 0) & stage(rb, 1)

        def both_block_edge():
            block_edge_path(rf, 0)
            block_edge_path(rb, 1)

        def both_exact():
            exact_path(rf, 0)
            exact_path(rb, 1)

        lax.cond(safe, both_block_edge, both_exact)
        chain(rf, 0)
        chain(rb, 1)
        return carry

    lax.fori_loop(0, nsb, body, 0)

    def finish(i, carry):
        rows = pl.ds(pl.multiple_of(i * HG_SB, HG_SB), HG_SB)
        o = o_scr[0, rows, :] + o_scr[1, rows, :]
        ms = _dot_hilo(o * o, e_ref[...]) * (1.0 / HEAD_D)
        pg = p_ref[rows, 4 * GROUP_W:5 * GROUP_W]
        out_ref[rows, :] = (o * lax.rsqrt(ms + EPS) * ng_ref[...] * _silu(pg)).astype(out_ref.dtype)
        return carry

    lax.fori_loop(0, nsb, finish, 0, unroll=2)
    if emit_state:
        ri = lax.broadcasted_iota(jnp.int32, (GROUP_W, GROUP_W), 0)
        ci = lax.broadcasted_iota(jnp.int32, (GROUP_W, GROUP_W), 1)
        eye = jnp.where(ri == ci, 1.0, 0.0).astype(BF16)
        for d in range(2):
            hi, mid, lo = _split3(st_scr[d])
            s_t = _dot_tn(hi, eye) + _dot_tn(mid, eye) + _dot_tn(lo, eye)
            for h in range(HEADS):
                st_ref[d, h] = s_t[HEAD_D * h:HEAD_D * (h + 1), HEAD_D * h:HEAD_D * (h + 1)]


def _hgrn(p3, s0_bd, lb_rows, norm_g, emit_state):
    b, seq, _ = p3.shape
    tri, bones, head, head4, pairs = _hgrn_consts()
    const2 = lambda i: (0, 0)
    const3 = lambda i: (0, 0, 0)
    out_shape = [jax.ShapeDtypeStruct((b, seq, GROUP_W), BF16)]
    out_specs = [pl.BlockSpec((None, seq, GROUP_W), lambda i: (i, 0, 0))]
    if emit_state:
        out_shape.append(jax.ShapeDtypeStruct((b, 2, HEADS, HEAD_D, HEAD_D), F32))
        out_specs.append(pl.BlockSpec((None, 2, HEADS, HEAD_D, HEAD_D), lambda i: (i, 0, 0, 0, 0)))
    sb_f32 = pltpu.VMEM((2, HG_SB, GROUP_W), F32)
    sb_bf16 = pltpu.VMEM((2, HG_SB, GROUP_W), BF16)
    res = pl.pallas_call(
        functools.partial(_hgrn_kernel, seq=seq, emit_state=emit_state),
        grid=(b,),
        in_specs=[
            pl.BlockSpec((None, seq, 5 * GROUP_W), lambda i: (i, 0, 0)),
            pl.BlockSpec((None, 2, HEADS, HEAD_D, HEAD_D), lambda i: (i, 0, 0, 0, 0)),
            pl.BlockSpec((8, GROUP_W), const2),
            pl.BlockSpec((1, GROUP_W), const2),
            pl.BlockSpec((2, HG_SB, HG_SB), const3),
            pl.BlockSpec((HG_SB, HG_SB), const2),
            pl.BlockSpec((GROUP_W, GROUP_W), const2),
            pl.BlockSpec((GROUP_W, GROUP_W), const2),
            pl.BlockSpec((HEADS * HG_SB, GROUP_W), const2),
            pl.BlockSpec((2, HG_SB, HEADS * HG_SB), const3),
        ],
        out_specs=out_specs,
        out_shape=out_shape,
        scratch_shapes=[
            pltpu.VMEM((2, seq, GROUP_W), F32),
            pltpu.VMEM((2, GROUP_W, GROUP_W), F32),
            sb_f32, sb_f32, sb_f32, sb_bf16, sb_bf16, sb_bf16, sb_f32, sb_f32,
        ],
        name="hgrn2",
    )(p3, s0_bd, lb_rows, norm_g, jnp.asarray(tri, BF16), jnp.asarray(bones, BF16),
      jnp.asarray(head, BF16), jnp.asarray(head, F32), jnp.asarray(head4, BF16), jnp.asarray(pairs, F32))
    return res if emit_state else (res[0], None)


@functools.lru_cache(maxsize=None)
def _mlstm_consts():
    r = np.arange(ML_CHUNK)
    tri_f = (r[:, None] >= r[None, :]).astype(np.float32)
    tri = np.stack([tri_f, tri_f.T])
    c = np.arange(GROUP_W)
    head = (c[:, None] // HEAD_D) == (c[None, :] // HEAD_D)
    pos = c % HEAD_D
    row_f = head & (pos[:, None] <= pos[None, :])
    row_b = head & (pos[:, None] >= pos[None, :])
    rowtri = np.stack([row_f, row_b]).astype(np.float32)
    expand = np.zeros((2, LANES, GROUP_W), np.float32)
    for d in range(2):
        for h in range(HEADS):
            expand[d, 2 * HEADS + d * HEADS + h, h * HEAD_D:(h + 1) * HEAD_D] = 1.0
    return tri, rowtri, expand, head.astype(np.float32)


def _dot_hilo(x, m):
    hi = x.astype(BF16)
    lo = (x - hi.astype(F32)).astype(BF16)
    return _dot(hi, m) + _dot(lo, m)


def _mlstm_kernel(p_ref, g_ref, gr_ref, conv_ref, brow_ref, brows_ref, ng_ref, c0_ref, n0_ref, m0_ref,
                  tri_ref, rowtri_ref, exp_ref, e_ref, out_ref, *rest, seq, conv_w, emit_state):
    if emit_state:
        cst_ref, nst_ref, mst_ref = rest[:3]
        rest = rest[3:]
    (q_scr, k_scr, col_scr, mloc_scr, gb_scr, h_scr, row_scr, blast_scr, gmax_scr,
     c_scr, n_scr, m_scr) = rest
    nch = seq // ML_CHUNK
    rowid = lax.broadcasted_iota(jnp.int32, (ML_CHUNK, 1), 0)
    ti = lax.broadcasted_iota(jnp.int32, (ML_CHUNK, GROUP_W), 0)
    si = lax.broadcasted_iota(jnp.int32, (ML_CHUNK, GROUP_W), 1) % HEAD_D

    for d in range(2):
        li_row = gr_ref[d] + brows_ref[d:d + 1, :]
        lf_row = _log_sigmoid(gr_ref[2 + d] + brows_ref[2 + d:3 + d, :])
        row_scr[d] = li_row - _dot_sel_r(lf_row, rowtri_ref[d])

    def pre_chunk(ci, carry):
        r0 = pl.multiple_of(ci * ML_CHUNK, ML_CHUNK)
        rows = pl.ds(r0, ML_CHUNK)
        zc = p_ref[rows, 0:2 * GROUP_W]
        zp = pltpu.roll(zc, 1, axis=0)
        zn = pltpu.roll(zc, ML_CHUNK - 1, axis=0)
        if conv_w == ML_CHUNK:
            prev_row = jnp.zeros((1, 2 * GROUP_W), F32)
            next_row = prev_row
        else:
            prev_row = p_ref[pl.ds(jnp.maximum(r0 - 1, 0), 1), 0:2 * GROUP_W]
            prev_row = jnp.where(r0 % conv_w == 0, 0.0, prev_row)
            next_row = p_ref[pl.ds(jnp.minimum(r0 + ML_CHUNK, seq - 1), 1), 0:2 * GROUP_W]
            next_row = jnp.where((r0 + ML_CHUNK) % conv_w == 0, 0.0, next_row)
        zp = jnp.where(rowid == 0, prev_row, zp)
        zn = jnp.where(rowid == ML_CHUNK - 1, next_row, zn)
        qk = _silu(conv_ref[0:1, :] * zp + conv_ref[1:2, :] * zc + conv_ref[2:3, :] * zn)
        q_scr[rows, :] = qk[:, 0:GROUP_W].astype(BF16)
        k_scr[rows, :] = qk[:, GROUP_W:2 * GROUP_W] * (HEAD_D ** -0.5)
        ge = g_ref[rows, :] + brow_ref[...]
        lane = lax.broadcasted_iota(jnp.int32, (1, LANES), 1)
        bwd = lane >= 3 * HEADS
        lf = _log_sigmoid(ge)
        bc = jnp.where(bwd, _dot_sel_l(tri_ref[1], lf), _dot_sel_l(tri_ref[0], lf))
        li = pltpu.roll(ge, 2 * HEADS, axis=1)
        pre = li - bc
        suf = pre
        for sh in (1, 2, 4, 8, 16, 32):
            pre = jnp.maximum(pre, jnp.where(rowid >= sh, pltpu.roll(pre, sh, axis=0), NEG_INF))
            suf = jnp.maximum(suf, jnp.where(rowid < ML_CHUNK - sh, pltpu.roll(suf, ML_CHUNK - sh, axis=0), NEG_INF))
        blast = jnp.where(bwd, bc[0:1, :], bc[ML_CHUNK - 1:ML_CHUNK, :])
        gb = blast - bc + li
        edge_rows = jnp.concatenate([blast, jnp.max(gb, axis=0, keepdims=True), jnp.zeros((6, LANES), F32)], axis=0)
        stacked = jnp.concatenate([bc, bc + jnp.where(bwd, suf, pre), gb, edge_rows], axis=0)
        for d in range(2):
            ex = _dot_sel_r(stacked, exp_ref[d])
            col_scr[d, rows, :] = ex[0:ML_CHUNK, :]
            mloc_scr[d, rows, :] = ex[ML_CHUNK:2 * ML_CHUNK, :]
            gb_scr[d, rows, :] = ex[2 * ML_CHUNK:3 * ML_CHUNK, :]
            blast_scr[d, pl.ds(ci, 1), :] = ex[3 * ML_CHUNK:3 * ML_CHUNK + 1, :]
            gmax_scr[d, pl.ds(ci, 1), :] = ex[3 * ML_CHUNK + 1:3 * ML_CHUNK + 2, :]
        return carry

    lax.fori_loop(0, nch, pre_chunk, 0, unroll=2)

    for d in range(2):
        c_scr[d] = jnp.zeros((GROUP_W, GROUP_W), F32)
        for h in range(HEADS):
            lo = HEAD_D * h
            c_scr[d, lo:lo + HEAD_D, lo:lo + HEAD_D] = c0_ref[d, h]
    n_scr[...] = n0_ref[...]
    m_scr[...] = m0_ref[...]

    def chunk(ci, d):
        r0 = pl.multiple_of(ci * ML_CHUNK, ML_CHUNK)
        rows = pl.ds(r0, ML_CHUNK)
        qb = q_scr[rows, :]
        kc = k_scr[rows, :]
        vb = p_ref[rows, 2 * GROUP_W:3 * GROUP_W].astype(BF16)
        hm = e_ref[...]
        kbd = jnp.concatenate([kc.astype(BF16)] * HEADS, axis=0) * hm
        vbd = jnp.concatenate([vb] * HEADS, axis=0) * hm
        s = _dot_nt(qb, kbd)
        colb = col_scr[d, rows, :]
        m_b = m_scr[d]
        prev = colb + m_b
        m_t = jnp.maximum(prev, mloc_scr[d, rows, :])
        keep = (ti >= si) if d == 0 else (ti <= si)
        w = jnp.exp(jnp.where(keep, colb + row_scr[d, pl.ds(ci, 1), :], NEG_INF) - m_t) * s
        wp = jnp.exp(prev - m_t)
        cst = c_scr[d]
        nrow = n_scr[d]
        num = wp * _dot(qb, cst.astype(BF16)) + _dot(w.astype(BF16), vbd)
        den = _dot_hilo(wp * qb.astype(F32) * nrow + w, e_ref[...])
        h_scr[d, rows, :] = num / jnp.maximum(jnp.abs(den), jnp.exp(-m_t))
        blast = blast_scr[d, pl.ds(ci, 1), :]
        m_new = jnp.maximum(blast + m_b, gmax_scr[d, pl.ds(ci, 1), :])
        dec = jnp.exp(blast + m_b - m_new)
        kw = kc * jnp.exp(gb_scr[d, rows, :] - m_new)
        c_scr[d] = cst * dec + _dot_tn(kw.astype(BF16), vb) * hm.astype(F32)
        n_scr[d] = nrow * dec + jnp.sum(kw, axis=0, keepdims=True)
        m_scr[d] = m_new

    def body(i, carry):
        chunk(i, 0)
        chunk(nch - 1 - i, 1)
        return carry

    lax.fori_loop(0, nch, body, 0, unroll=2)

    def fin_chunk(ci, carry):
        rows = pl.ds(pl.multiple_of(ci * ML_CHUNK, ML_CHUNK), ML_CHUNK)
        hs = h_scr[0, rows, :] + h_scr[1, rows, :]
        ms = _dot_hilo(hs * hs, e_ref[...]) * (1.0 / HEAD_D)
        po = p_ref[rows, 3 * GROUP_W:4 * GROUP_W]
        out_ref[rows, :] = (hs * lax.rsqrt(ms + EPS) * ng_ref[...] * _sigmoid(po)).astype(out_ref.dtype)
        return carry

    lax.fori_loop(0, nch, fin_chunk, 0, unroll=4)
    if emit_state:
        for d in range(2):
            for h in range(HEADS):
                lo = HEAD_D * h
                cst_ref[d, h] = c_scr[d, lo:lo + HEAD_D, lo:lo + HEAD_D]
        nst_ref[...] = n_scr[...]
        mst_ref[...] = m_scr[...]


def _mlstm(p3, g3, gr4, conv_w3, brow, brows, norm_g, c0, n0, m0, conv_width, emit_state):
    b, seq, _ = p3.shape
    nch = seq // ML_CHUNK
    tri, rowtri, expand, head = _mlstm_consts()
    const2 = lambda i: (0, 0)
    const3 = lambda i: (0, 0, 0)
    out_shape = [jax.ShapeDtypeStruct((b, seq, GROUP_W), BF16)]
    out_specs = [pl.BlockSpec((None, seq, GROUP_W), lambda i: (i, 0, 0))]
    c_spec = pl.BlockSpec((None, 2, HEADS, HEAD_D, HEAD_D), lambda i: (i, 0, 0, 0, 0))
    row_spec = pl.BlockSpec((None, 2, 1, GROUP_W), lambda i: (i, 0, 0, 0))
    if emit_state:
        out_shape += [jax.ShapeDtypeStruct((b, 2, HEADS, HEAD_D, HEAD_D), F32),
                      jax.ShapeDtypeStruct((b, 2, 1, GROUP_W), F32), jax.ShapeDtypeStruct((b, 2, 1, GROUP_W), F32)]
        out_specs += [c_spec, row_spec, row_spec]
    dir_seq = (2, seq, GROUP_W)
    dir_chunks = (2, nch, GROUP_W)
    res = pl.pallas_call(
        functools.partial(_mlstm_kernel, seq=seq, conv_w=conv_width, emit_state=emit_state),
        grid=(b,),
        in_specs=[
            pl.BlockSpec((None, seq, 4 * GROUP_W), lambda i: (i, 0, 0)),
            pl.BlockSpec((None, seq, LANES), lambda i: (i, 0, 0)),
            pl.BlockSpec((None, 4, nch, GROUP_W), lambda i: (i, 0, 0, 0)),
            pl.BlockSpec((3, 2 * GROUP_W), const2),
            pl.BlockSpec((1, LANES), const2),
            pl.BlockSpec((4, GROUP_W), const2),
            pl.BlockSpec((1, GROUP_W), const2),
            c_spec, row_spec, row_spec,
            pl.BlockSpec((2, ML_CHUNK, ML_CHUNK), const3),
            pl.BlockSpec((2, GROUP_W, GROUP_W), const3),
            pl.BlockSpec((2, LANES, GROUP_W), const3),
            pl.BlockSpec((GROUP_W, GROUP_W), const2),
        ],
        out_specs=out_specs,
        out_shape=out_shape,
        scratch_shapes=[
            pltpu.VMEM((seq, GROUP_W), BF16),
            pltpu.VMEM((seq, GROUP_W), F32),
            pltpu.VMEM(dir_seq, F32), pltpu.VMEM(dir_seq, F32), pltpu.VMEM(dir_seq, F32), pltpu.VMEM(dir_seq, F32),
            pltpu.VMEM(dir_chunks, F32), pltpu.VMEM(dir_chunks, F32), pltpu.VMEM(dir_chunks, F32),
            pltpu.VMEM((2, GROUP_W, GROUP_W), F32),
            pltpu.VMEM((2, 1, GROUP_W), F32),
            pltpu.VMEM((2, 1, GROUP_W), F32),
        ],
        name="mlstm",
    )(p3, g3, gr4, conv_w3, brow, brows, norm_g, c0, n0, m0, jnp.asarray(tri, BF16), jnp.asarray(rowtri, BF16),
      jnp.asarray(expand, BF16), jnp.asarray(head, BF16))
    if emit_state:
        return res
    return res[0], None, None, None


def _s5_kernel(*refs, nb, reverse, final):
    if final:
        (u_ref, hre0_ref, him0_ref, are_ref, aim_ref, wb_ref, wc_ref, yin_ref, d_ref, gw_ref, gb_ref,
         y_ref, sre_ref, sim_ref, utm_scr, bu_scr, hb_scr, ytm_scr, hre_scr, him_scr) = refs
    else:
        (u_ref, hre0_ref, him0_ref, are_ref, aim_ref, wb_ref, wc_ref,
         y_ref, sre_ref, sim_ref, utm_scr, bu_scr, hb_scr, ytm_scr, hre_scr, him_scr) = refs
    tc = S5_TC
    step_id = pl.program_id(0)

    @pl.when(step_id == 0)
    def _():
        hre_scr[...] = hre0_ref[...]
        him_scr[...] = him0_ref[...]

    halves = GROUP_W // LANES
    for b in range(nb):
        for hv in range(halves):
            utm_scr[hv, pl.ds(b, tc, stride=nb), :] = u_ref[b, :, hv * LANES:(hv + 1) * LANES]
    u_tm = jnp.concatenate([utm_scr[hv] for hv in range(halves)], axis=1)
    bu_scr[...] = _dot(u_tm.astype(BF16), wb_ref[...])
    are = jnp.broadcast_to(are_ref[...], (nb, S5_STATE))
    aim = jnp.broadcast_to(aim_ref[...], (nb, S5_STATE))

    spi = max(ROW_ALIGN // nb, 1)
    rows_it = spi * nb

    def step(i, carry):
        hre, him = carry
        r0 = pl.multiple_of(((tc // spi - 1 - i) if reverse else i) * rows_it, rows_it)
        res = [None] * spi
        ims = [None] * spi
        for s in (range(spi - 1, -1, -1) if reverse else range(spi)):
            bu = bu_scr[pl.ds(r0 + s * nb, nb), :]
            hre, him = (are * hre - aim * him + bu[:, 0:S5_STATE],
                        are * him + aim * hre + bu[:, S5_STATE:2 * S5_STATE])
            res[s], ims[s] = hre, him
        hb_scr[pl.ds(r0, rows_it), 0:S5_STATE] = jnp.concatenate(res, axis=0).astype(BF16)
        hb_scr[pl.ds(r0, rows_it), S5_STATE:2 * S5_STATE] = jnp.concatenate(ims, axis=0).astype(BF16)
        return hre, him

    hre, him = lax.fori_loop(0, tc // spi, step, (hre_scr[...], him_scr[...]))
    hre_scr[...] = hre
    him_scr[...] = him
    y_tm = _dot(hb_scr[...], wc_ref[...])
    for hv in range(halves):
        ytm_scr[hv] = y_tm[:, hv * LANES:(hv + 1) * LANES]
    for b in range(nb):
        yb = jnp.concatenate([ytm_scr[hv, pl.ds(b, tc, stride=nb), :] for hv in range(halves)], axis=1)
        if final:
            y = yb + yin_ref[b] + d_ref[...] * u_ref[b]
            z = 0.5 * y * (1.0 + jnp.tanh(math.sqrt(2.0 / math.pi) * (y + 0.044715 * (y * y * y))))
            gate = _sigmoid(_dot(z.astype(BF16), gw_ref[...]) + gb_ref[...])
            y_ref[b] = (z * gate).astype(y_ref.dtype)
        else:
            y_ref[b] = yb

    @pl.when(step_id == pl.num_programs(0) - 1)
    def _():
        sre_ref[...] = hre
        sim_ref[...] = him


def _s5_pass(u3, hre0, him0, are, aim, wb, wc, reverse, extra):
    nb, seq, _ = u3.shape
    nch = seq // S5_TC
    final = extra is not None
    chunk = (lambda i: (0, nch - 1 - i, 0)) if reverse else (lambda i: (0, i, 0))
    const2 = lambda i: (0, 0)
    in_specs = [
        pl.BlockSpec((nb, S5_TC, GROUP_W), chunk),
        pl.BlockSpec((nb, S5_STATE), const2),
        pl.BlockSpec((nb, S5_STATE), const2),
        pl.BlockSpec((1, S5_STATE), const2),
        pl.BlockSpec((1, S5_STATE), const2),
        pl.BlockSpec((GROUP_W, 2 * S5_STATE), const2),
        pl.BlockSpec((2 * S5_STATE, GROUP_W), const2),
    ]
    args = [u3, hre0, him0, are, aim, wb, wc]
    if final:
        yin, dvec, gw, gb = extra
        in_specs += [pl.BlockSpec((nb, S5_TC, GROUP_W), chunk), pl.BlockSpec((1, GROUP_W), const2),
                     pl.BlockSpec((GROUP_W, GROUP_W), const2), pl.BlockSpec((1, GROUP_W), const2)]
        args += [yin, dvec, gw, gb]
    return pl.pallas_call(
        functools.partial(_s5_kernel, nb=nb, reverse=reverse, final=final),
        grid=(nch,),
        in_specs=in_specs,
        out_specs=[pl.BlockSpec((nb, S5_TC, GROUP_W), chunk),
                   pl.BlockSpec((nb, S5_STATE), const2), pl.BlockSpec((nb, S5_STATE), const2)],
        out_shape=[jax.ShapeDtypeStruct((nb, seq, GROUP_W), BF16 if final else F32),
                   jax.ShapeDtypeStruct((nb, S5_STATE), F32), jax.ShapeDtypeStruct((nb, S5_STATE), F32)],
        scratch_shapes=[
            pltpu.VMEM((GROUP_W // LANES, nb * S5_TC, LANES), F32),
            pltpu.VMEM((nb * S5_TC, 2 * S5_STATE), F32),
            pltpu.VMEM((nb * S5_TC, 2 * S5_STATE), BF16),
            pltpu.VMEM((GROUP_W // LANES, nb * S5_TC, LANES), F32),
            pltpu.VMEM((nb, S5_STATE), F32),
            pltpu.VMEM((nb, S5_STATE), F32),
        ],
        name="s5_bwd" if reverse else "s5_fwd",
    )(*args)


def _s5_params(a_re, a_im, log_dt, b_re, b_im, c_re, c_im):
    eye = jnp.eye(S5_GROUPS, dtype=F32)
    dt = jnp.exp(log_dt)[..., None]
    mag = jnp.exp(a_re * dt)
    ab_re, ab_im = mag * jnp.cos(a_im * dt), mag * jnp.sin(a_im * dt)
    den = a_re * a_re + a_im * a_im
    g_re = ((ab_re - 1.0) * a_re + ab_im * a_im) / den
    g_im = (ab_im * a_re - (ab_re - 1.0) * a_im) / den
    bb_re = g_re[..., None] * b_re - g_im[..., None] * b_im
    bb_im = g_re[..., None] * b_im + g_im[..., None] * b_re

    def in_mat(bb):
        return jnp.einsum('dgpc,gh->dgchp', bb, eye).reshape(2, GROUP_W, S5_STATE)

    wb = jnp.concatenate([in_mat(bb_re), in_mat(bb_im)], axis=-1).astype(BF16)

    def out_mat(cc):
        return jnp.einsum('gcp,gh->gphc', cc, eye).reshape(S5_STATE, GROUP_W)

    wc = jnp.concatenate([out_mat(c_re), -out_mat(c_im)], axis=0).astype(BF16)
    return ab_re.reshape(2, 1, S5_STATE), ab_im.reshape(2, 1, S5_STATE), wb, wc


def _s5(u3, h0_re, h0_im, prm, d_vec, glu_w, glu_b):
    ab_re, ab_im, wb, wc = prm
    yf, fre, fim = _s5_pass(u3, h0_re[:, 0], h0_im[:, 0], ab_re[0], ab_im[0], wb[0], wc, False, None)
    out, bre, bim = _s5_pass(u3, h0_re[:, 1], h0_im[:, 1], ab_re[1], ab_im[1], wb[1], wc, True,
                             (yf, d_vec, glu_w, glu_b))
    return out, jnp.stack([fre, bre], axis=1), jnp.stack([fim, bim], axis=1)


@functools.lru_cache(maxsize=None)
def _dft_consts(n):
    idx = np.arange(n, dtype=np.int64)
    ang = (np.pi / n) * ((idx[:, None] * idx[None, :]) % (2 * n)).astype(np.float64)
    cos, sin = np.cos(ang), np.sin(ang)
    sign = np.where(idx % 2 == 0, 1.0, -1.0)
    sin_n = sin.copy()
    sin_n[0, :] = sign
    fwd = np.concatenate([cos, sin_n], axis=0)
    inv_c = cos.T / n
    inv_c[:, 0] = 1.0 / (2 * n)
    inv_s = sin.T / n
    inv_s[:, 0] = sign / (2 * n)
    inv = np.concatenate([inv_c, inv_s], axis=1)
    return fwd.astype(np.float32), inv.astype(np.float32), sign.astype(np.float32)[:, None]


@functools.lru_cache(maxsize=None)
def _hyena_feats(n):
    t = np.linspace(0.0, 1.0, n, dtype=np.float64)[:, None]
    w = (2.0 * np.pi / n) * np.arange(n, dtype=np.float64)[:, None]
    bands = np.linspace(1e-4, HY_BANDS - 1, HY_BANDS, dtype=np.float64)[None, :]
    feats = np.concatenate([t, np.cos(w * bands), -np.sin(w * bands)], axis=-1)
    pad = np.zeros((n, LANES - HY_EMB))
    return np.concatenate([feats, pad], axis=-1).astype(np.float32), t.astype(np.float32)


def _hyfilt_kernel(feat_ref, t_ref, sign_ref, w1_ref, b1_ref, w2_ref, b2_ref, w3_ref, fr_ref, dec_ref,
                   fh_ref, fl_ref, ka_ref, kb_ref, ka2_ref, *, n):
    freq = fr_ref[...]
    hdn = jnp.sin(freq * (_dot3(feat_ref[...], w1_ref[...]) + b1_ref[...]))
    hdn = jnp.sin(freq * (_dot3(hdn, w2_ref[...]) + b2_ref[...]))
    h = _dot3(hdn, w3_ref[...]) * (jnp.exp(-t_ref[...] * jnp.abs(dec_ref[...])) + HY_SHIFT)
    half = HY_ORDER * GROUP_W
    row = lax.broadcasted_iota(jnp.int32, (n, 1), 0)
    h0 = h[:, 0:half]
    h1 = jnp.where(row == 0, 0.0, h[:, half:2 * half])
    norm = jnp.sum(jnp.abs(h0), axis=0, keepdims=True) + jnp.sum(jnp.abs(h1), axis=0, keepdims=True)
    hp = (h0 + h1) / norm
    hm = (h0 - h1) / norm

    def dft(lo, x):
        xh = x.astype(BF16)
        xl = (x - xh.astype(F32)).astype(BF16)
        fh = fh_ref[lo:lo + n, :]
        return _dot(fh, xh) + _dot(fl_ref[lo:lo + n, :], xh) + _dot(fh, xl)

    kc = dft(0, hp)
    ks = dft(n, hm)
    kn = jnp.sum(hp * sign_ref[...], axis=0, keepdims=True)
    ka_ref[...] = kc
    kb_ref[...] = jnp.where(row == 0, 0.0, ks)
    ka2_ref[...] = jnp.where(row == 0, kn, kc)


def _hyena_filter(n, w1p, b1, w2, b2, w3, freq, decay):
    fwd, _, sign = _dft_consts(n)
    feats, t = _hyena_feats(n)
    fwd = jnp.asarray(fwd)
    fh = fwd.astype(BF16)
    fl = (fwd - fh.astype(F32)).astype(BF16)
    half = HY_ORDER * GROUP_W
    return pl.pallas_call(
        functools.partial(_hyfilt_kernel, n=n),
        out_shape=[jax.ShapeDtypeStruct((n, half), F32)] * 3,
        name="hyena_filter",
    )(jnp.asarray(feats), jnp.asarray(t), jnp.asarray(sign), w1p, b1, w2, b2, w3, freq, decay, fh, fl)


def _hyena_kernel(p_ref, cw_ref, ka_ref, kb_ref, ka2_ref, bias_ref, f_ref, g_ref, out_ref,
                  z_scr, x_scr, zf_scr, *, seq, conv_w):
    nch = seq // ML_CHUNK
    rowid = lax.broadcasted_iota(jnp.int32, (ML_CHUNK, 1), 0)

    def conv_chunk(ci, carry):
        r0 = pl.multiple_of(ci * ML_CHUNK, ML_CHUNK)
        zc = p_ref[pl.ds(r0, ML_CHUNK), :]
        zp = pltpu.roll(zc, 1, axis=0)
        zn = pltpu.roll(zc, ML_CHUNK - 1, axis=0)
        if conv_w == ML_CHUNK:
            prev_row = jnp.zeros((1, 3 * GROUP_W), F32)
            next_row = prev_row
        else:
            prev_row = p_ref[pl.ds(jnp.maximum(r0 - 1, 0), 1), :]
            prev_row = jnp.where(r0 % conv_w == 0, 0.0, prev_row)
            next_row = p_ref[pl.ds(jnp.minimum(r0 + ML_CHUNK, seq - 1), 1), :]
            next_row = jnp.where((r0 + ML_CHUNK) % conv_w == 0, 0.0, next_row)
        zp = jnp.where(rowid == 0, prev_row, zp)
        zn = jnp.where(rowid == ML_CHUNK - 1, next_row, zn)
        z_scr[pl.ds(r0, ML_CHUNK), :] = cw_ref[0:1, :] * zp + cw_ref[1:2, :] * zc + cw_ref[2:3, :] * zn
        return carry

    lax.fori_loop(0, nch, conv_chunk, 0)
    y = z_scr[:, 0:GROUP_W]
    for o in range(HY_ORDER):
        cols = slice(o * GROUP_W, (o + 1) * GROUP_W)
        x_scr[...] = _dot(f_ref[...], y.astype(BF16))
        xc = x_scr[0:seq, :]
        xs = x_scr[seq:2 * seq, :]
        kb = kb_ref[:, cols]
        zf_scr[0:seq, :] = (xc * ka_ref[:, cols] - xs * kb).astype(BF16)
        zf_scr[seq:2 * seq, :] = (xc * kb + xs * ka2_ref[:, cols]).astype(BF16)
        conv = _dot(g_ref[...], zf_scr[...])
        gate = z_scr[:, (o + 1) * GROUP_W:(o + 2) * GROUP_W]
        y = gate * (conv + bias_ref[o:o + 1, :] * y)
    out_ref[...] = y.astype(out_ref.dtype)


def _hyena(p3, conv_w3, filt, bias, conv_width):
    b, seq, _ = p3.shape
    ka, kb, ka2 = filt
    fwd, inv, _ = _dft_consts(seq)
    const2 = lambda i: (0, 0)
    half = HY_ORDER * GROUP_W
    return pl.pallas_call(
        functools.partial(_hyena_kernel, seq=seq, conv_w=conv_width),
        grid=(b,),
        in_specs=[
            pl.BlockSpec((None, seq, 3 * GROUP_W), lambda i: (i, 0, 0)),
            pl.BlockSpec((3, 3 * GROUP_W), const2),
            pl.BlockSpec((seq, half), const2),
            pl.BlockSpec((seq, half), const2),
            pl.BlockSpec((seq, half), const2),
            pl.BlockSpec((HY_ORDER, GROUP_W), const2),
            pl.BlockSpec((2 * seq, seq), const2),
            pl.BlockSpec((seq, 2 * seq), const2),
        ],
        out_specs=pl.BlockSpec((None, seq, GROUP_W), lambda i: (i, 0, 0)),
        out_shape=jax.ShapeDtypeStruct((b, seq, GROUP_W), BF16),
        scratch_shapes=[
            pltpu.VMEM((seq, 3 * GROUP_W), F32),
            pltpu.VMEM((2 * seq, GROUP_W), F32),
            pltpu.VMEM((2 * seq, GROUP_W), BF16),
        ],
        name="hyena",
    )(p3, conv_w3, ka, kb, ka2, bias, jnp.asarray(fwd, BF16), jnp.asarray(inv, BF16))


def _outproj_kernel(a_ref, b_ref, c_ref, d_ref, x_ref, g1_ref, sh2_ref, sc2_ref, n2_ref, wo_ref,
                    wr_ref, br_ref, x1_ref, h2_ref, gate_ref):
    acc = _dot(a_ref[...], wo_ref[0:GROUP_W, :])
    acc += _dot(b_ref[...], wo_ref[GROUP_W:2 * GROUP_W, :])
    acc += _dot(c_ref[...], wo_ref[2 * GROUP_W:3 * GROUP_W, :])
    acc += _dot(d_ref[...], wo_ref[3 * GROUP_W:4 * GROUP_W, :])
    x1 = x_ref[...] + g1_ref[0] * acc
    x1_ref[...] = x1
    h2 = _rmsnorm_rows(x1, n2_ref[...]) * (1.0 + sc2_ref[0]) + sh2_ref[0]
    h2_ref[...] = h2.astype(BF16)
    logits = _dot3(h2, wr_ref[...]) + br_ref[...]
    lane = lax.broadcasted_iota(jnp.int32, logits.shape, 1).astype(F32)
    big = float(LANES)
    gl = jnp.where(lane < N_EXPERT_GROUPS, logits, NEG_INF)
    gmax = jnp.max(gl, axis=1, keepdims=True)
    gsel = jnp.min(jnp.where(gl == gmax, lane, big), axis=1, keepdims=True)
    psel = 1.0 / jnp.sum(jnp.exp(gl - gmax), axis=1, keepdims=True)
    lo = N_EXPERT_GROUPS + EXPERTS_PER_GROUP * gsel
    el = jnp.where((lane >= lo) & (lane < lo + EXPERTS_PER_GROUP), logits, NEG_INF)
    v1 = jnp.max(el, axis=1, keepdims=True)
    i1 = jnp.min(jnp.where(el == v1, lane, big), axis=1, keepdims=True)
    el2 = jnp.where(lane == i1, NEG_INF, el)
    v2 = jnp.max(el2, axis=1, keepdims=True)
    i2 = jnp.min(jnp.where(el2 == v2, lane, big), axis=1, keepdims=True)
    e2 = jnp.exp(v2 - v1)
    w1 = psel / (1.0 + e2)
    w2 = psel * e2 / (1.0 + e2)
    gate_ref[...] = (jnp.where(lane == 0.0, gsel, 0.0) + jnp.where(lane == i1 - lo + 1.0, w1, 0.0)
                     + jnp.where(lane == i2 - lo + 1.0, w2, 0.0))


def _outproj(mix, x2d, mod3, row_of_tile, norm_g, w_out, w_r, b_r):
    t = x2d.shape[0]
    tm = TOK_TILE
    const = lambda i: (0, 0)
    tile = lambda i: (i, 0)
    return pl.pallas_call(
        _outproj_kernel,
        grid=(t // tm,),
        in_specs=[pl.BlockSpec((tm, GROUP_W), tile)] * 4 + [
            pl.BlockSpec((tm, D_MODEL), tile),
            _mod_spec(2, row_of_tile),
            _mod_spec(3, row_of_tile),
            _mod_spec(4, row_of_tile),
            pl.BlockSpec((1, D_MODEL), const),
            pl.BlockSpec((D_MODEL, D_MODEL), const),
            pl.BlockSpec((D_MODEL, LANES), const),
            pl.BlockSpec((1, LANES), const),
        ],
        out_specs=[pl.BlockSpec((tm, D_MODEL), tile), pl.BlockSpec((tm, D_MODEL), tile),
                   pl.BlockSpec((tm, LANES), tile)],
        out_shape=[jax.ShapeDtypeStruct((t, D_MODEL), F32), jax.ShapeDtypeStruct((t, D_MODEL), BF16),
                   jax.ShapeDtypeStruct((t, LANES), F32)],
        name="outproj_router",
    )(*mix, x2d, mod3, mod3, mod3, norm_g, w_out, w_r, b_r)


def _tile_positions(gates, tri_ref, extra):
    lane = lax.broadcasted_iota(jnp.int32, gates.shape, 1).astype(F32)
    gsel = gates[:, 0:1]
    member = jnp.where((lane == gsel) & (lane < N_EXPERT_GROUPS), 1.0, 0.0)
    before = _dot(tri_ref[...], member.astype(BF16))
    pos = gsel * MOE_TOK + jnp.sum(member * (before + extra), axis=1, keepdims=True)
    slot = lax.broadcasted_iota(jnp.int32, (MOE_TOK, N_EXPERT_GROUPS * MOE_TOK), 1).astype(F32)
    return jnp.where(slot == pos, 1.0, 0.0).astype(BF16)


def _moe_pack_kernel(off_ref, cnt_ref, tot_ref, h_ref, gate_ref, tri_ref, xg_ref, gg_ref,
                     xs_scr, gs_scr, zx_scr, zg_scr, sem):
    i = pl.program_id(0)
    slot = i % 2
    gates = gate_ref[...]
    lane = lax.broadcasted_iota(jnp.int32, gates.shape, 1).astype(F32)
    member = jnp.where((lane == gates[:, 0:1]) & (lane < N_EXPERT_GROUPS), 1.0, 0.0).astype(BF16)
    grp = lax.broadcasted_iota(jnp.int32, (8, LANES), 0)
    pick = jnp.where(grp == lax.broadcasted_iota(jnp.int32, (8, LANES), 1), 1.0, 0.0).astype(BF16)
    member_t = _dot_nt(pick, member)
    before_t = _dot_nt(member_t.astype(BF16), tri_ref[...])
    gid = lax.broadcasted_iota(jnp.int32, member_t.shape, 0).astype(F32)
    pos_t = jnp.sum(member_t * (gid * MOE_TOK + before_t), axis=0, keepdims=True)
    row_id = lax.broadcasted_iota(jnp.int32, (N_EXPERT_GROUPS * MOE_TOK, MOE_TOK), 0).astype(F32)
    p = jnp.where(row_id == pos_t, 1.0, 0.0).astype(BF16)
    xs_scr[slot] = _dot(p, h_ref[...]).astype(BF16)
    hi, mid, lo = _split3(gates)
    gs_scr[slot] = _dot(p, hi) + _dot(p, mid) + _dot(p, lo)

    def for_each_piece(tile, buf, action):
        for g in range(N_EXPERT_GROUPS):
            off = pl.multiple_of(off_ref[tile * N_EXPERT_GROUPS + g], ROW_ALIGN)
            for j in range(MOE_TOK // MOE_PIECE):
                src = pl.ds(g * MOE_TOK + j * MOE_PIECE, MOE_PIECE)
                dst = pl.ds(off + j * MOE_PIECE, MOE_PIECE)

                @pl.when(cnt_ref[tile * N_EXPERT_GROUPS + g] > j * MOE_PIECE)
                def _():
                    action(pltpu.make_async_copy(xs_scr.at[buf, src], xg_ref.at[g, dst], sem.at[buf, 0, g, j]))
                    action(pltpu.make_async_copy(gs_scr.at[buf, src], gg_ref.at[g, dst], sem.at[buf, 1, g, j]))

    @pl.when(i > 0)
    def _():
        for_each_piece(i - 1, 1 - slot, lambda cp: cp.wait())

    for_each_piece(i, slot, lambda cp: cp.start())

    @pl.when(i == pl.num_programs(0) - 1)
    def _():
        for_each_piece(i, slot, lambda cp: cp.wait())
        zx_scr[...] = jnp.zeros_like(zx_scr)
        zg_scr[...] = jnp.zeros_like(zg_scr)

        def tail_copies(g):
            tot = pl.multiple_of(tot_ref[g], ROW_ALIGN)
            return (pltpu.make_async_copy(zx_scr, xg_ref.at[g, pl.ds(tot, MOE_ROWS)], sem.at[0, 0, g, 0]),
                    pltpu.make_async_copy(zg_scr, gg_ref.at[g, pl.ds(tot, MOE_ROWS)], sem.at[0, 1, g, 0]))

        for g in range(N_EXPERT_GROUPS):
            for cp in tail_copies(g):
                cp.start()
        for g in range(N_EXPERT_GROUPS):
            for cp in tail_copies(g):
                cp.wait()


def _moe_expert_kernel(grp_ref, chk_ref, nact_ref, x_ref, g_ref, wg_ref, wu_ref, wd_ref, y_ref):
    @pl.when(pl.program_id(0) < nact_ref[0])
    def _():
        x = x_ref[...]
        gates = g_ref[...]
        acc = None
        for e in range(EXPERTS_PER_GROUP):
            a = _dot(x, wg_ref[e])
            u = _dot(x, wu_ref[e])
            hid = (_silu(a) * u * gates[:, 1 + e:2 + e]).astype(BF16)
            part = _dot(hid, wd_ref[e])
            acc = part if acc is None else acc + part
        y_ref[...] = acc.astype(y_ref.dtype)


def _moe_unpack_kernel(st_ref, dl_ref, cnt_ref, gate_ref, tri_ref, x1_ref, g2_ref, fn_ref, yg_ref, out_ref,
                       ys_scr, sem, *, final):
    i = pl.program_id(0)
    slot = i % 2

    def for_each_piece(tile, buf, action):
        for g in range(N_EXPERT_GROUPS):
            k = tile * N_EXPERT_GROUPS + g
            start = pl.multiple_of(st_ref[k], ROW_ALIGN)
            for j in range(MOE_TOK // MOE_PIECE):
                lo, hi = j * MOE_PIECE, (j + 1) * MOE_PIECE

                @pl.when((dl_ref[k] < hi) & (dl_ref[k] + cnt_ref[k] > lo))
                def _():
                    action(pltpu.make_async_copy(yg_ref.at[g, pl.ds(start + lo, MOE_PIECE)],
                                                 ys_scr.at[buf, pl.ds(g * MOE_TOK + lo, MOE_PIECE)],
                                                 sem.at[buf, g, j]))

    @pl.when(i == 0)
    def _():
        ys_scr[...] = jnp.zeros_like(ys_scr)
        for_each_piece(0, 0, lambda cp: cp.start())

    @pl.when(i + 1 < pl.num_programs(0))
    def _():
        for_each_piece(i + 1, 1 - slot, lambda cp: cp.start())

    gates = gate_ref[...]
    lane = lax.broadcasted_iota(jnp.int32, (1, LANES), 1)
    delta = jnp.zeros((1, LANES), F32)
    for g in range(N_EXPERT_GROUPS):
        delta = jnp.where(lane == g, dl_ref[i * N_EXPERT_GROUPS + g].astype(F32), delta)
    pt = _tile_positions(gates, tri_ref, delta)
    for_each_piece(i, slot, lambda cp: cp.wait())
    x2 = x1_ref[...] + g2_ref[0] * _dot(pt, ys_scr[slot])
    if final:
        x2 = _rmsnorm_rows(x2, fn_ref[...])
    out_ref[...] = x2


def _moe_schedule(gsel, n_tiles, n_steps):
    ng = N_EXPERT_GROUPS
    member = gsel.reshape(n_tiles, MOE_TOK, 1) == jnp.arange(ng, dtype=jnp.int32)
    cnt = jnp.sum(member.astype(jnp.int32), axis=1)
    padded = (cnt + ROW_ALIGN - 1) // ROW_ALIGN * ROW_ALIGN
    off = jnp.cumsum(padded, axis=0) - padded
    tot = jnp.sum(padded, axis=0)
    n_chunks = jnp.maximum((tot + MOE_ROWS - 1) // MOE_ROWS, 1)
    start = jnp.minimum(off, n_chunks * MOE_ROWS - MOE_TOK)
    ends = jnp.cumsum(n_chunks)
    n_active = ends[-1]
    step = jnp.minimum(jnp.arange(n_steps, dtype=jnp.int32), n_active - 1)
    grp = jnp.sum((step[:, None] >= ends[None, :]).astype(jnp.int32), axis=1)
    chk = step - (ends - n_chunks)[grp]
    i32 = lambda a: a.astype(jnp.int32)
    return (i32(off).reshape(-1), i32(padded).reshape(-1), i32(tot), i32(start).reshape(-1),
            i32(off - start).reshape(-1), i32(grp), i32(chk), i32(n_active).reshape(1))


def _moe(h2, gates, layer, w_gate, w_up, w_down, x1, mod3, row_of_tile, final_g, final):
    t = h2.shape[0]
    ng = N_EXPERT_GROUPS
    n_tiles = t // MOE_TOK
    cap = -(-(t + ROW_ALIGN * n_tiles + MOE_ROWS) // MOE_ROWS) * MOE_ROWS
    n_steps = (t + ROW_ALIGN * n_tiles) // MOE_ROWS + ng
    off, cnt, tot, start, delta, grp, chk, n_active = _moe_schedule(gates[:, 0].astype(jnp.int32), n_tiles,
                                                                    n_steps)
    pieces = MOE_TOK // MOE_PIECE
    r = np.arange(MOE_TOK)
    tri = jnp.asarray((r[:, None] > r[None, :]).astype(np.float32), BF16)
    any_spec = pl.BlockSpec(memory_space=pl.ANY)

    xg, gg = pl.pallas_call(
        _moe_pack_kernel,
        grid_spec=pltpu.PrefetchScalarGridSpec(
            num_scalar_prefetch=3,
            grid=(n_tiles,),
            in_specs=[
                pl.BlockSpec((MOE_TOK, D_MODEL), lambda i, *_: (i, 0)),
                pl.BlockSpec((MOE_TOK, LANES), lambda i, *_: (i, 0)),
                pl.BlockSpec((MOE_TOK, MOE_TOK), lambda i, *_: (0, 0)),
            ],
            out_specs=[any_spec, any_spec],
            scratch_shapes=[
                pltpu.VMEM((2, ng * MOE_TOK, D_MODEL), BF16),
                pltpu.VMEM((2, ng * MOE_TOK, LANES), F32),
                pltpu.VMEM((MOE_ROWS, D_MODEL), BF16),
                pltpu.VMEM((MOE_ROWS, LANES), F32),
                pltpu.SemaphoreType.DMA((2, 2, ng, pieces)),
            ],
        ),
        out_shape=[jax.ShapeDtypeStruct((ng, cap, D_MODEL), BF16), jax.ShapeDtypeStruct((ng, cap, LANES), F32)],
        name="moe_pack",
    )(off, cnt, tot, h2, gates, tri)

    group_w = lambda shape: pl.BlockSpec((None, None, EXPERTS_PER_GROUP) + shape,
                                         lambda k, grp, chk, na: (layer, grp[k], 0, 0, 0))
    rows = lambda width: pl.BlockSpec((None, MOE_ROWS, width), lambda k, grp, chk, na: (grp[k], chk[k], 0))
    yg = pl.pallas_call(
        _moe_expert_kernel,
        grid_spec=pltpu.PrefetchScalarGridSpec(
            num_scalar_prefetch=3,
            grid=(n_steps,),
            in_specs=[rows(D_MODEL), rows(LANES), group_w((D_MODEL, D_EXPERT)), group_w((D_MODEL, D_EXPERT)),
                      group_w((D_EXPERT, D_MODEL))],
            out_specs=rows(D_MODEL),
        ),
        out_shape=jax.ShapeDtypeStruct((ng, cap, D_MODEL), BF16),
        name="moe_experts",
    )(grp, chk, n_active, xg, gg,
      w_gate.reshape(DEPTH, ng, EXPERTS_PER_GROUP, D_MODEL, D_EXPERT),
      w_up.reshape(DEPTH, ng, EXPERTS_PER_GROUP, D_MODEL, D_EXPERT),
      w_down.reshape(DEPTH, ng, EXPERTS_PER_GROUP, D_EXPERT, D_MODEL))

    return pl.pallas_call(
        functools.partial(_moe_unpack_kernel, final=final),
        grid_spec=pltpu.PrefetchScalarGridSpec(
            num_scalar_prefetch=3,
            grid=(n_tiles,),
            in_specs=[
                pl.BlockSpec((MOE_TOK, LANES), lambda i, *_: (i, 0)),
                pl.BlockSpec((MOE_TOK, MOE_TOK), lambda i, *_: (0, 0)),
                pl.BlockSpec((MOE_TOK, D_MODEL), lambda i, *_: (i, 0)),
                pl.BlockSpec((1, 1, D_MODEL), lambda i, *_: (row_of_tile(i), 0, 5)),
                pl.BlockSpec((1, D_MODEL), lambda i, *_: (0, 0)),
                any_spec,
            ],
            out_specs=pl.BlockSpec((MOE_TOK, D_MODEL), lambda i, *_: (i, 0)),
            scratch_shapes=[pltpu.VMEM((2, ng * MOE_TOK, D_MODEL), BF16),
                            pltpu.SemaphoreType.DMA((2, ng, pieces))],
        ),
        out_shape=jax.ShapeDtypeStruct((t, D_MODEL), F32),
        name="moe_unpack",
    )(start, delta, cnt, gates, tri, x1, mod3, final_g, yg)


def _prep_layer(l, w):
    gw = GROUP_W
    w_in = w['w_in'][l]
    w_g = w_in[:, 9 * gw:9 * gw + N_GATES]
    lb_all = jnp.cumsum(jax.nn.softmax(w['hg_lb'].astype(F32), axis=0), axis=0)
    lb = lb_all[l] - lb_all[0]
    lb_rows = jnp.zeros((8, gw), F32).at[0].set(jnp.log(lb)).at[1].set(jnp.log1p(-lb)).at[2].set(1.0 - lb)
    w_r = jnp.zeros((D_MODEL, LANES), F32)
    w_r = w_r.at[:, 0:N_EXPERT_GROUPS].set(w['moe_wg'][l])
    w_r = w_r.at[:, N_EXPERT_GROUPS:N_EXPERT_GROUPS + N_EXPERTS].set(w['moe_we'][l])
    b_r = jnp.zeros((1, LANES), F32)
    b_r = b_r.at[0, 0:N_EXPERT_GROUPS].set(w['moe_bg'][l])
    b_r = b_r.at[0, N_EXPERT_GROUPS:N_EXPERT_GROUPS + N_EXPERTS].set(w['moe_be'][l])
    gate_bias = jnp.concatenate([w['ml_i_bias'][l], w['ml_f_bias'][l]])
    w1p = jnp.zeros((LANES, HY_HIDDEN), F32).at[0:HY_EMB].set(w['hy_w1'][l])
    return dict(
        norm1=w['norm1'][l][None, :], norm2=w['norm2'][l][None, :],
        w_a=w_in[:, :9 * gw].astype(BF16), w_b=w_in[:, 9 * gw + N_GATES:].astype(BF16),
        w_g=jnp.zeros((D_MODEL, LANES), BF16).at[:, 0:N_GATES].set(w_g.astype(BF16)), w_gt=w_g.T.astype(BF16),
        w_out=w['w_out'][l].astype(BF16),
        lb_rows=lb_rows, hg_norm=w['hg_norm'][l][None, :],
        ml_conv=w['ml_conv'][l], ml_brow=jnp.zeros((1, LANES), F32).at[0, 0:N_GATES].set(gate_bias),
        ml_brows=jnp.repeat(gate_bias.reshape(4, HEADS), HEAD_D, axis=1),
        ml_norm=w['ml_norm'][l][None, :],
        s5=_s5_params(w['s5_a_re'][l], w['s5_a_im'][l], w['s5_log_dt'][l], w['s5_b_re'][l], w['s5_b_im'][l],
                      w['s5_c_re'][l], w['s5_c_im'][l]),
        s5_d=w['s5_d'][l][None, :], s5_glu_w=w['s5_glu_w'][l].astype(BF16), s5_glu_b=w['s5_glu_b'][l][None, :],
        hy_conv=w['hy_conv'][l], hy_bias=w['hy_bias'][l],
        hy_mlp=(w1p, w['hy_b1'][l][None, :], w['hy_w2'][l], w['hy_b2'][l][None, :], w['hy_w3'][l],
                w['hy_freq'][l][None, :], w['hy_decay'][l][None, :]),
        w_r=w_r, b_r=b_r,
        layer=l,
    )


def _trunk_layer(x2d, nb, seq, mod3, cond_row, st, lw, moe_w, filt, conv_width, emit_state, final_g, final):
    if cond_row is None:
        row_of_tile = lambda i: i // (seq // TOK_TILE)
        in_row_of_tile = lambda i: i // (seq // IN_TILE)
    else:
        row_of_tile = in_row_of_tile = lambda i: cond_row
    hg, ml, su, hy, gt, gtt = _inproj(x2d, mod3, in_row_of_tile, lw['norm1'], lw['w_a'], lw['w_b'], lw['w_g'],
                                       lw['w_gt'])
    nch = seq // ML_CHUNK
    a_out, hg_st = _hgrn(hg.reshape(nb, seq, -1), st['hgrn'], lw['lb_rows'], lw['hg_norm'], emit_state)
    gr4 = jnp.transpose(gtt.reshape(4, HEADS, nb, nch, ML_CHUNK), (2, 0, 3, 1, 4)).reshape(nb, 4, nch, GROUP_W)
    b_out, mlc, mln, mlm = _mlstm(ml.reshape(nb, seq, -1), gt.reshape(nb, seq, LANES), gr4, lw['ml_conv'],
                                  lw['ml_brow'], lw['ml_brows'], lw['ml_norm'], st['ml_c'], st['ml_n'],
                                  st['ml_m'], conv_width, emit_state)
    c_out, s5re, s5im = _s5(su.reshape(nb, seq, -1), st['s5_re'], st['s5_im'], lw['s5'], lw['s5_d'],
                            lw['s5_glu_w'], lw['s5_glu_b'])
    d_out = _hyena(hy.reshape(nb, seq, -1), lw['hy_conv'], filt, lw['hy_bias'], conv_width)
    mix = [m.reshape(nb * seq, GROUP_W) for m in (a_out, b_out, c_out, d_out)]
    x1, h2, gates = _outproj(mix, x2d, mod3, row_of_tile, lw['norm2'], lw['w_out'], lw['w_r'], lw['b_r'])
    x2 = _moe(h2, gates, lw['layer'], *moe_w, x1, mod3, row_of_tile, final_g, final)
    new_st = dict(hgrn=hg_st, ml_c=mlc, ml_n=mln, ml_m=mlm, s5_re=s5re, s5_im=s5im)
    return x2, new_st


def kernel(x_prompt, x_sample, c, state_hgrn, state_mlstm_c, state_mlstm_n, state_mlstm_m, state_s5_re, state_s5_im, c_ctx, w_ada, b_ada, norm1, norm2, w_in, w_out, hg_lb, hg_norm, ml_conv, ml_i_bias, ml_f_bias, ml_norm, s5_a_re, s5_a_im, s5_log_dt, s5_b_re, s5_b_im, s5_c_re, s5_c_im, s5_d, s5_glu_w, s5_glu_b, hy_conv, hy_w1, hy_b1, hy_w2, hy_b2, hy_w3, hy_freq, hy_decay, hy_bias, moe_wg, moe_bg, moe_we, moe_be, moe_w_gate, moe_w_up, moe_w_down, final_norm):
    w = dict(w_in=w_in, w_out=w_out, norm1=norm1, norm2=norm2, hg_lb=hg_lb, hg_norm=hg_norm, ml_conv=ml_conv,
             ml_i_bias=ml_i_bias, ml_f_bias=ml_f_bias, ml_norm=ml_norm, s5_a_re=s5_a_re, s5_a_im=s5_a_im,
             s5_log_dt=s5_log_dt, s5_b_re=s5_b_re, s5_b_im=s5_b_im, s5_c_re=s5_c_re, s5_c_im=s5_c_im,
             s5_d=s5_d, s5_glu_w=s5_glu_w, s5_glu_b=s5_glu_b, hy_conv=hy_conv, hy_w1=hy_w1, hy_b1=hy_b1,
             hy_w2=hy_w2, hy_b2=hy_b2, hy_w3=hy_w3, hy_freq=hy_freq, hy_decay=hy_decay, hy_bias=hy_bias,
             moe_wg=moe_wg, moe_bg=moe_bg, moe_we=moe_we, moe_be=moe_be, moe_w_gate=moe_w_gate,
             moe_w_up=moe_w_up, moe_w_down=moe_w_down)
    bp, lp, _ = x_prompt.shape
    bs, ls, _ = x_sample.shape
    ctx_row = bs
    cond = jnp.zeros((MOD_ROWS, D_MODEL), F32).at[0:bs].set(c).at[ctx_row].set(c_ctx)
    mod = _ada_mod(cond, w_ada, b_ada)
    final_g = final_norm[None, :]

    zero_st = dict(
        hgrn=jnp.zeros((bp, 2, HEADS, HEAD_D, HEAD_D), F32),
        ml_c=jnp.zeros((bp, 2, HEADS, HEAD_D, HEAD_D), F32),
        ml_n=jnp.zeros((bp, 2, 1, GROUP_W), F32),
        ml_m=jnp.zeros((bp, 2, 1, GROUP_W), F32),
        s5_re=jnp.zeros((bp, 2, S5_STATE), F32),
        s5_im=jnp.zeros((bp, 2, S5_STATE), F32))

    xp = x_prompt.reshape(bp * lp, D_MODEL)
    xs = x_sample.reshape(bs * ls, D_MODEL)
    moe_w = (moe_w_gate.astype(BF16), moe_w_up.astype(BF16), moe_w_down.astype(BF16))
    per_layer = []
    for l in range(DEPTH):
        lw = _prep_layer(l, w)
        mod3 = mod[l].reshape(MOD_ROWS, 1, 6 * D_MODEL)
        last = l == DEPTH - 1
        filt_p = _hyena_filter(lp, *lw['hy_mlp'])
        filt_s = filt_p if ls == lp else _hyena_filter(ls, *lw['hy_mlp'])
        xp, st = _trunk_layer(xp, bp, lp, mod3, ctx_row, zero_st, lw, moe_w, filt_p, lp, True, final_g, last)
        per_layer.append(st)
        st_in = dict(
            hgrn=state_hgrn[:, l].astype(F32),
            ml_c=state_mlstm_c[:, l].astype(F32),
            ml_n=state_mlstm_n[:, l].astype(F32).reshape(bs, 2, 1, GROUP_W),
            ml_m=jnp.repeat(state_mlstm_m[:, l].astype(F32), HEAD_D, axis=-1).reshape(bs, 2, 1, GROUP_W),
            s5_re=state_s5_re[:, l].astype(F32).reshape(bs, 2, S5_STATE),
            s5_im=state_s5_im[:, l].astype(F32).reshape(bs, 2, S5_STATE))
        xs, _ = _trunk_layer(xs, bs, ls, mod3, None, st_in, lw, moe_w, filt_s, GRID_W, False, final_g, last)

    new_hgrn = jnp.stack([s['hgrn'] for s in per_layer], axis=1)
    new_ml_c = jnp.stack([s['ml_c'] for s in per_layer], axis=1)
    new_ml_n = jnp.stack([s['ml_n'].reshape(bp, 2, HEADS, HEAD_D) for s in per_layer], axis=1)
    new_ml_m = jnp.stack([s['ml_m'][:, :, 0, ::HEAD_D] for s in per_layer], axis=1)
    new_s5_re = jnp.stack([s['s5_re'].reshape(bp, 2, S5_GROUPS, S5_P) for s in per_layer], axis=1)
    new_s5_im = jnp.stack([s['s5_im'].reshape(bp, 2, S5_GROUPS, S5_P) for s in per_layer], axis=1)
    return (xp.reshape(bp, lp, D_MODEL), xs.reshape(bs, ls, D_MODEL),
            new_hgrn, new_ml_c, new_ml_n, new_ml_m, new_s5_re, new_s5_im)
```

```python
import functools
import math

import numpy as np
import jax
import jax.numpy as jnp
from jax import lax
from jax.experimental import pallas as pl
from jax.experimental.pallas import tpu as pltpu

F32 = jnp.float32
BF16 = jnp.bfloat16

D_MODEL = 1024
DEPTH = 2
GRID_W = 64
GROUP_W = D_MODEL // 4
HEADS = 4
HEAD_D = GROUP_W // HEADS
S5_CH = 16
S5_GROUPS = GROUP_W // S5_CH
S5_P = 64
S5_STATE = S5_GROUPS * S5_P
HY_ORDER = 2
HY_BANDS = 16
HY_EMB = 1 + 2 * HY_BANDS
HY_HIDDEN = 64
HY_SHIFT = 0.05
N_EXPERT_GROUPS = 4
EXPERTS_PER_GROUP = 4
N_EXPERTS = N_EXPERT_GROUPS * EXPERTS_PER_GROUP
D_EXPERT = D_MODEL // 2
EPS = 1e-6
N_GATES = 4 * HEADS

LANES = 128
MOD_ROWS = 16
TOK_TILE = 256
IN_TILE = 512
MOE_TOK = TOK_TILE
MOE_PIECE = 64
MOE_ROWS = 512
ROW_ALIGN = 16
HG_BLK = 32
HG_SB = 128
HG_SAFE_LOG = -75.0
HG_SAFE_Q = 1e3
ML_CHUNK = 64
S5_TC = 64
HY_BLK = 256
NEG_INF = float("-inf")


def _dot(a, b):
    return jnp.dot(a, b, preferred_element_type=F32)


def _dot_nt(a, b):
    return lax.dot_general(a, b, (((1,), (1,)), ((), ())), preferred_element_type=F32)


def _dot_tn(a, b):
    return lax.dot_general(a, b, (((0,), (0,)), ((), ())), preferred_element_type=F32)


def _split3(x):
    hi = x.astype(BF16)
    r1 = x - hi.astype(F32)
    mid = r1.astype(BF16)
    lo = (r1 - mid.astype(F32)).astype(BF16)
    return hi, mid, lo


def _dot_sel_l(m, x):
    hi, mid, lo = _split3(x)
    return _dot(m, hi) + _dot(m, mid) + _dot(m, lo)


def _dot_sel_r(x, m):
    hi, mid, lo = _split3(x)
    return _dot(hi, m) + _dot(mid, m) + _dot(lo, m)


def _dot3(a, b):
    ah = a.astype(BF16)
    al = (a - ah.astype(F32)).astype(BF16)
    bh = b.astype(BF16)
    bl = (b - bh.astype(F32)).astype(BF16)
    return _dot(ah, bh) + _dot(al, bh) + _dot(ah, bl)


def _sigmoid(x):
    return 1.0 / (1.0 + jnp.exp(-x))


def _silu(x):
    return x * _sigmoid(x)


def _log_sigmoid(x):
    return jnp.minimum(x, 0.0) - jnp.log(1.0 + jnp.exp(-jnp.abs(x)))


def _rmsnorm_rows(x, g):
    return x * lax.rsqrt(jnp.mean(x * x, axis=-1, keepdims=True) + EPS) * g


def _ada_kernel(c_ref, w_ref, b_ref, o_ref):
    s = _silu(c_ref[...]).astype(BF16)
    o_ref[0] = _dot(s, w_ref[0].astype(BF16)) + b_ref[0]


def _ada_mod(cond, w_ada, b_ada):
    tn = 1536
    n = w_ada.shape[-1]
    return pl.pallas_call(
        _ada_kernel,
        grid=(DEPTH, n // tn),
        in_specs=[
            pl.BlockSpec((MOD_ROWS, D_MODEL), lambda l, j: (0, 0)),
            pl.BlockSpec((1, D_MODEL, tn), lambda l, j: (l, 0, j)),
            pl.BlockSpec((1, 1, tn), lambda l, j: (l, 0, j)),
        ],
        out_specs=pl.BlockSpec((1, MOD_ROWS, tn), lambda l, j: (l, 0, j)),
        out_shape=jax.ShapeDtypeStruct((DEPTH, MOD_ROWS, n), F32),
        name="ada_mod",
    )(cond, w_ada, b_ada.reshape(DEPTH, 1, n))


def _inproj_kernel(x_ref, sh_ref, sc_ref, g_ref, wa_ref, wb_ref, wg_ref, wgt_ref,
                   hg_ref, ml_ref, su_ref, hy_ref, gt_ref, gtt_ref):
    h = _rmsnorm_rows(x_ref[...], g_ref[...])
    h = h * (1.0 + sc_ref[0]) + sh_ref[0]
    hb = h.astype(BF16)
    hg_ref[...] = _dot(hb, wa_ref[:, 0:5 * GROUP_W])
    ml_ref[...] = _dot(hb, wa_ref[:, 5 * GROUP_W:9 * GROUP_W])
    su_ref[...] = _dot(hb, wb_ref[:, 0:GROUP_W])
    hy_ref[...] = _dot(hb, wb_ref[:, GROUP_W:4 * GROUP_W])
    gt_ref[...] = _dot(hb, wg_ref[...])
    gtt_ref[...] = _dot_nt(wgt_ref[...], hb)


def _mod_spec(k, row_of_tile):
    return pl.BlockSpec((1, 1, D_MODEL), lambda i: (row_of_tile(i), 0, k))


def _inproj(x2d, mod3, row_of_tile, norm_g, w_a, w_b, w_g, w_gt):
    t = x2d.shape[0]
    tm = IN_TILE
    const = lambda i: (0, 0)
    tile = lambda i: (i, 0)
    widths = (5 * GROUP_W, 4 * GROUP_W, GROUP_W, 3 * GROUP_W)
    return pl.pallas_call(
        _inproj_kernel,
        grid=(t // tm,),
        in_specs=[
            pl.BlockSpec((tm, D_MODEL), tile),
            _mod_spec(0, row_of_tile),
            _mod_spec(1, row_of_tile),
            pl.BlockSpec((1, D_MODEL), const),
            pl.BlockSpec(w_a.shape, const),
            pl.BlockSpec(w_b.shape, const),
            pl.BlockSpec(w_g.shape, const),
            pl.BlockSpec(w_gt.shape, const),
        ],
        out_specs=[pl.BlockSpec((tm, w), tile) for w in widths]
        + [pl.BlockSpec((tm, LANES), tile), pl.BlockSpec((N_GATES, tm), lambda i: (0, i))],
        out_shape=[jax.ShapeDtypeStruct((t, w), F32) for w in widths]
        + [jax.ShapeDtypeStruct((t, LANES), F32), jax.ShapeDtypeStruct((N_GATES, t), F32)],
        name="inproj",
    )(x2d, mod3, mod3, norm_g, w_a, w_b, w_g, w_gt)


@functools.lru_cache(maxsize=None)
def _hgrn_consts():
    r = np.arange(HG_SB)
    same = (r[:, None] // HG_BLK) == (r[None, :] // HG_BLK)
    tri_f = same & (r[:, None] >= r[None, :])
    tri_b = same & (r[:, None] <= r[None, :])
    c = np.arange(GROUP_W)
    head = (c[:, None] // HEAD_D) == (c[None, :] // HEAD_D)
    rr = np.arange(HEADS * HG_SB)
    head4 = (rr[:, None] // HG_SB) == (c[None, :] // HEAD_D)
    pairs = np.stack([np.tile(tri_f, (1, HEADS)), np.tile(tri_b, (1, HEADS))])
    return (np.stack([tri_f, tri_b]).astype(np.float32), same.astype(np.float32),
            head.astype(np.float32), head4.astype(np.float32), pairs.astype(np.float32))


def _hgrn_kernel(p_ref, s0_ref, lb_ref, ng_ref, tri_ref, bones_ref, e_ref, hmask_ref, hmask4_ref,
                 smask_ref, out_ref, *rest, seq, emit_state):
    if emit_state:
        st_ref = rest[0]
        rest = rest[1:]
    o_scr, st_scr, q_scr, k_scr, bc_scr, gq_scr, gk_scr, ks_scr, dec_scr, oi_scr = rest
    nsb = seq // HG_SB
    nblk = HG_SB // HG_BLK
    log_lb = lb_ref[0:1, :]
    log_1mlb = lb_ref[1:2, :]
    one_m_lb = lb_ref[2:3, :]
    row = lax.broadcasted_iota(jnp.int32, (HG_BLK, 1), 0)

    def stage(r0, d):
        rows = pl.ds(r0, HG_SB)
        pq = p_ref[rows, 0:GROUP_W]
        pf = p_ref[rows, (1 + d) * GROUP_W:(2 + d) * GROUP_W]
        q = _silu(pq)
        e = jnp.exp(-jnp.abs(pf))
        b2 = log_1mlb + jnp.minimum(pf, 0.0) - jnp.log(1.0 + e)
        mx = jnp.maximum(log_lb, b2)
        mn = jnp.minimum(log_lb, b2)
        lf = mx + jnp.log(1.0 + jnp.exp(mn - mx))
        k = one_m_lb * jnp.where(pf >= 0.0, e, 1.0) / (1.0 + e)
        hi, mid, lo = _split3(lf)
        bc = _dot(tri_ref[d], hi) + _dot(tri_ref[d], mid) + _dot(tri_ref[d], lo)
        bt = _dot(bones_ref[...], hi) + _dot(bones_ref[...], mid) + _dot(bones_ref[...], lo)
        gq_scr[d] = (q * jnp.exp(bc)).astype(BF16)
        gk_scr[d] = (k * jnp.exp(bt - bc)).astype(BF16)
        dec_scr[d] = jnp.exp(bt)
        ks_scr[d] = (k * jnp.exp(-bc)).astype(BF16)
        q_scr[d] = q
        k_scr[d] = k
        bc_scr[d] = bc
        return (jnp.min(bc) >= HG_SAFE_LOG) & (jnp.max(jnp.abs(q)) <= HG_SAFE_Q)

    def block_edge_path(r0, d):
        vb = p_ref[pl.ds(r0, HG_SB), 3 * GROUP_W:4 * GROUP_W].astype(BF16)
        ksbd = jnp.concatenate([ks_scr[d]] * HEADS, axis=0) * hmask4_ref[...]
        vbd = jnp.concatenate([vb] * HEADS, axis=0) * hmask4_ref[...]
        s = _dot_nt(gq_scr[d], ksbd) * smask_ref[d]
        oi_scr[d] = _dot(s.astype(BF16), vbd)

    def exact_path(r0, d):
        for j in range(nblk):
            sl = slice(HG_BLK * j, HG_BLK * (j + 1))
            qj = q_scr[d, sl, :]
            kj = k_scr[d, sl, :]
            vj = p_ref[pl.ds(r0 + HG_BLK * j, HG_BLK), 3 * GROUP_W:4 * GROUP_W]
            bcj = bc_scr[d, sl, :]
            parts = []
            for s in range(HG_BLK):
                keep = (row >= s) if d == 0 else (row <= s)
                dd = jnp.exp(jnp.where(keep, bcj - bcj[s:s + 1, :], NEG_INF))
                parts.append((qj * dd * kj[s:s + 1, :]).astype(BF16))
            r_all = _dot(jnp.concatenate(parts, axis=0), e_ref[...])
            o = r_all[0:HG_BLK, :] * vj[0:1, :]
            for s in range(1, HG_BLK):
                o = o + r_all[HG_BLK * s:HG_BLK * (s + 1), :] * vj[s:s + 1, :]
            oi_scr[d, sl, :] = o

    def chain(r0, d):
        order = range(nblk) if d == 0 else range(nblk - 1, -1, -1)
        for j in order:
            sl = slice(HG_BLK * j, HG_BLK * (j + 1))
            vb = p_ref[pl.ds(r0 + HG_BLK * j, HG_BLK), 3 * GROUP_W:4 * GROUP_W].astype(BF16)
            upd = _dot_tn(vb, gk_scr[d, sl, :]) * hmask_ref[...]
            st = st_scr[d]
            o_scr[d, pl.ds(r0 + HG_BLK * j, HG_BLK), :] = oi_scr[d, sl, :] + _dot_nt(gq_scr[d, sl, :], st.astype(BF16))
            st_scr[d] = st * dec_scr[d, HG_BLK * j:HG_BLK * j + 1, :] + upd

    eye_h = jnp.where(lax.broadcasted_iota(jnp.int32, (HEAD_D, HEAD_D), 0)
                      == lax.broadcasted_iota(jnp.int32, (HEAD_D, HEAD_D), 1), 1.0, 0.0).astype(BF16)
    st_scr[...] = jnp.zeros_like(st_scr)
    for d in range(2):
        for h in range(HEADS):
            hi, mid, lo = _split3(s0_ref[d, h])
            lanes = slice(HEAD_D * h, HEAD_D * (h + 1))
            st_scr[d, lanes, lanes] = _dot_tn(hi, eye_h) + _dot_tn(mid, eye_h) + _dot_tn(lo, eye_h)

    def body(i, carry):
        rf = pl.multiple_of(i * HG_SB, HG_SB)
        rb = pl.multiple_of((nsb - 1 - i) * HG_SB, HG_SB)
        safe = stage(rf, 0) & stage(rb, 1)

        def both_block_edge():
            block_edge_path(rf, 0)
            block_edge_path(rb, 1)

        def both_exact():
            exact_path(rf, 0)
            exact_path(rb, 1)

        lax.cond(safe, both_block_edge, both_exact)
        chain(rf, 0)
        chain(rb, 1)
        return carry

    lax.fori_loop(0, nsb, body, 0)

    def finish(i, carry):
        rows = pl.ds(pl.multiple_of(i * HG_SB, HG_SB), HG_SB)
        o = o_scr[0, rows, :] + o_scr[1, rows, :]
        ms = _dot_hilo(o * o, e_ref[...]) * (1.0 / HEAD_D)
        pg = p_ref[rows, 4 * GROUP_W:5 * GROUP_W]
        out_ref[rows, :] = (o * lax.rsqrt(ms + EPS) * ng_ref[...] * _silu(pg)).astype(out_ref.dtype)
        return carry

    lax.fori_loop(0, nsb, finish, 0, unroll=2)
    if emit_state:
        ri = lax.broadcasted_iota(jnp.int32, (GROUP_W, GROUP_W), 0)
        ci = lax.broadcasted_iota(jnp.int32, (GROUP_W, GROUP_W), 1)
        eye = jnp.where(ri == ci, 1.0, 0.0).astype(BF16)
        for d in range(2):
            hi, mid, lo = _split3(st_scr[d])
            s_t = _dot_tn(hi, eye) + _dot_tn(mid, eye) + _dot_tn(lo, eye)
            for h in range(HEADS):
                st_ref[d, h] = s_t[HEAD_D * h:HEAD_D * (h + 1), HEAD_D * h:HEAD_D * (h + 1)]


def _hgrn(p3, s0_bd, lb_rows, norm_g, emit_state):
    b, seq, _ = p3.shape
    tri, bones, head, head4, pairs = _hgrn_consts()
    const2 = lambda i: (0, 0)
    const3 = lambda i: (0, 0, 0)
    out_shape = [jax.ShapeDtypeStruct((b, seq, GROUP_W), BF16)]
    out_specs = [pl.BlockSpec((None, seq, GROUP_W), lambda i: (i, 0, 0))]
    if emit_state:
        out_shape.append(jax.ShapeDtypeStruct((b, 2, HEADS, HEAD_D, HEAD_D), F32))
        out_specs.append(pl.BlockSpec((None, 2, HEADS, HEAD_D, HEAD_D), lambda i: (i, 0, 0, 0, 0)))
    sb_f32 = pltpu.VMEM((2, HG_SB, GROUP_W), F32)
    sb_bf16 = pltpu.VMEM((2, HG_SB, GROUP_W), BF16)
    res = pl.pallas_call(
        functools.partial(_hgrn_kernel, seq=seq, emit_state=emit_state),
        grid=(b,),
        in_specs=[
            pl.BlockSpec((None, seq, 5 * GROUP_W), lambda i: (i, 0, 0)),
            pl.BlockSpec((None, 2, HEADS, HEAD_D, HEAD_D), lambda i: (i, 0, 0, 0, 0)),
            pl.BlockSpec((8, GROUP_W), const2),
            pl.BlockSpec((1, GROUP_W), const2),
            pl.BlockSpec((2, HG_SB, HG_SB), const3),
            pl.BlockSpec((HG_SB, HG_SB), const2),
            pl.BlockSpec((GROUP_W, GROUP_W), const2),
            pl.BlockSpec((GROUP_W, GROUP_W), const2),
            pl.BlockSpec((HEADS * HG_SB, GROUP_W), const2),
            pl.BlockSpec((2, HG_SB, HEADS * HG_SB), const3),
        ],
        out_specs=out_specs,
        out_shape=out_shape,
        scratch_shapes=[
            pltpu.VMEM((2, seq, GROUP_W), F32),
            pltpu.VMEM((2, GROUP_W, GROUP_W), F32),
            sb_f32, sb_f32, sb_f32, sb_bf16, sb_bf16, sb_bf16, sb_f32, sb_f32,
        ],
        name="hgrn2",
    )(p3, s0_bd, lb_rows, norm_g, jnp.asarray(tri, BF16), jnp.asarray(bones, BF16),
      jnp.asarray(head, BF16), jnp.asarray(head, F32), jnp.asarray(head4, BF16), jnp.asarray(pairs, F32))
    return res if emit_state else (res[0], None)


@functools.lru_cache(maxsize=None)
def _mlstm_consts():
    r = np.arange(ML_CHUNK)
    tri_f = (r[:, None] >= r[None, :]).astype(np.float32)
    tri = np.stack([tri_f, tri_f.T])
    c = np.arange(GROUP_W)
    head = (c[:, None] // HEAD_D) == (c[None, :] // HEAD_D)
    pos = c % HEAD_D
    row_f = head & (pos[:, None] <= pos[None, :])
    row_b = head & (pos[:, None] >= pos[None, :])
    rowtri = np.stack([row_f, row_b]).astype(np.float32)
    expand = np.zeros((2, LANES, GROUP_W), np.float32)
    for d in range(2):
        for h in range(HEADS):
            expand[d, 2 * HEADS + d * HEADS + h, h * HEAD_D:(h + 1) * HEAD_D] = 1.0
    return tri, rowtri, expand, head.astype(np.float32)


def _dot_hilo(x, m):
    hi = x.astype(BF16)
    lo = (x - hi.astype(F32)).astype(BF16)
    return _dot(hi, m) + _dot(lo, m)


def _mlstm_kernel(p_ref, g_ref, gr_ref, conv_ref, brow_ref, brows_ref, ng_ref, c0_ref, n0_ref, m0_ref,
                  tri_ref, rowtri_ref, exp_ref, e_ref, out_ref, *rest, seq, conv_w, emit_state):
    if emit_state:
        cst_ref, nst_ref, mst_ref = rest[:3]
        rest = rest[3:]
    (q_scr, k_scr, col_scr, mloc_scr, gb_scr, h_scr, row_scr, blast_scr, gmax_scr,
     c_scr, n_scr, m_scr) = rest
    nch = seq // ML_CHUNK
    rowid = lax.broadcasted_iota(jnp.int32, (ML_CHUNK, 1), 0)
    ti = lax.broadcasted_iota(jnp.int32, (ML_CHUNK, GROUP_W), 0)
    si = lax.broadcasted_iota(jnp.int32, (ML_CHUNK, GROUP_W), 1) % HEAD_D

    for d in range(2):
        li_row = gr_ref[d] + brows_ref[d:d + 1, :]
        lf_row = _log_sigmoid(gr_ref[2 + d] + brows_ref[2 + d:3 + d, :])
        row_scr[d] = li_row - _dot_sel_r(lf_row, rowtri_ref[d])

    def pre_chunk(ci, carry):
        r0 = pl.multiple_of(ci * ML_CHUNK, ML_CHUNK)
        rows = pl.ds(r0, ML_CHUNK)
        zc = p_ref[rows, 0:2 * GROUP_W]
        zp = pltpu.roll(zc, 1, axis=0)
        zn = pltpu.roll(zc, ML_CHUNK - 1, axis=0)
        if conv_w == ML_CHUNK:
            prev_row = jnp.zeros((1, 2 * GROUP_W), F32)
            next_row = prev_row
        else:
            prev_row = p_ref[pl.ds(jnp.maximum(r0 - 1, 0), 1), 0:2 * GROUP_W]
            prev_row = jnp.where(r0 % conv_w == 0, 0.0, prev_row)
            next_row = p_ref[pl.ds(jnp.minimum(r0 + ML_CHUNK, seq - 1), 1), 0:2 * GROUP_W]
            next_row = jnp.where((r0 + ML_CHUNK) % conv_w == 0, 0.0, next_row)
        zp = jnp.where(rowid == 0, prev_row, zp)
        zn = jnp.where(rowid == ML_CHUNK - 1, next_row, zn)
        qk = _silu(conv_ref[0:1, :] * zp + conv_ref[1:2, :] * zc + conv_ref[2:3, :] * zn)
        q_scr[rows, :] = qk[:, 0:GROUP_W].astype(BF16)
        k_scr[rows, :] = qk[:, GROUP_W:2 * GROUP_W] * (HEAD_D ** -0.5)
        ge = g_ref[rows, :] + brow_ref[...]
        lane = lax.broadcasted_iota(jnp.int32, (1, LANES), 1)
        bwd = lane >= 3 * HEADS
        lf = _log_sigmoid(ge)
        bc = jnp.where(bwd, _dot_sel_l(tri_ref[1], lf), _dot_sel_l(tri_ref[0], lf))
        li = pltpu.roll(ge, 2 * HEADS, axis=1)
        pre = li - bc
        suf = pre
        for sh in (1, 2, 4, 8, 16, 32):
            pre = jnp.maximum(pre, jnp.where(rowid >= sh, pltpu.roll(pre, sh, axis=0), NEG_INF))
            suf = jnp.maximum(suf, jnp.where(rowid < ML_CHUNK - sh, pltpu.roll(suf, ML_CHUNK - sh, axis=0), NEG_INF))
        blast = jnp.where(bwd, bc[0:1, :], bc[ML_CHUNK - 1:ML_CHUNK, :])
        gb = blast - bc + li
        edge_rows = jnp.concatenate([blast, jnp.max(gb, axis=0, keepdims=True), jnp.zeros((6, LANES), F32)], axis=0)
        stacked = jnp.concatenate([bc, bc + jnp.where(bwd, suf, pre), gb, edge_rows], axis=0)
        for d in range(2):
            ex = _dot_sel_r(stacked, exp_ref[d])
            col_scr[d, rows, :] = ex[0:ML_CHUNK, :]
            mloc_scr[d, rows, :] = ex[ML_CHUNK:2 * ML_CHUNK, :]
            gb_scr[d, rows, :] = ex[2 * ML_CHUNK:3 * ML_CHUNK, :]
            blast_scr[d, pl.ds(ci, 1), :] = ex[3 * ML_CHUNK:3 * ML_CHUNK + 1, :]
            gmax_scr[d, pl.ds(ci, 1), :] = ex[3 * ML_CHUNK + 1:3 * ML_CHUNK + 2, :]
        return carry

    lax.fori_loop(0, nch, pre_chunk, 0, unroll=2)

    for d in range(2):
        c_scr[d] = jnp.zeros((GROUP_W, GROUP_W), F32)
        for h in range(HEADS):
            lo = HEAD_D * h
            c_scr[d, lo:lo + HEAD_D, lo:lo + HEAD_D] = c0_ref[d, h]
    n_scr[...] = n0_ref[...]
    m_scr[...] = m0_ref[...]

    def chunk(ci, d):
        r0 = pl.multiple_of(ci * ML_CHUNK, ML_CHUNK)
        rows = pl.ds(r0, ML_CHUNK)
        qb = q_scr[rows, :]
        kc = k_scr[rows, :]
        vb = p_ref[rows, 2 * GROUP_W:3 * GROUP_W].astype(BF16)
        hm = e_ref[...]
        kbd = jnp.concatenate([kc.astype(BF16)] * HEADS, axis=0) * hm
        vbd = jnp.concatenate([vb] * HEADS, axis=0) * hm
        s = _dot_nt(qb, kbd)
        colb = col_scr[d, rows, :]
        m_b = m_scr[d]
        prev = colb + m_b
        m_t = jnp.maximum(prev, mloc_scr[d, rows, :])
        keep = (ti >= si) if d == 0 else (ti <= si)
        w = jnp.exp(jnp.where(keep, colb + row_scr[d, pl.ds(ci, 1), :], NEG_INF) - m_t) * s
        wp = jnp.exp(prev - m_t)
        cst = c_scr[d]
        nrow = n_scr[d]
        num = wp * _dot(qb, cst.astype(BF16)) + _dot(w.astype(BF16), vbd)
        den = _dot_hilo(wp * qb.astype(F32) * nrow + w, e_ref[...])
        h_scr[d, rows, :] = num / jnp.maximum(jnp.abs(den), jnp.exp(-m_t))
        blast = blast_scr[d, pl.ds(ci, 1), :]
        m_new = jnp.maximum(blast + m_b, gmax_scr[d, pl.ds(ci, 1), :])
        dec = jnp.exp(blast + m_b - m_new)
        kw = kc * jnp.exp(gb_scr[d, rows, :] - m_new)
        c_scr[d] = cst * dec + _dot_tn(kw.astype(BF16), vb) * hm.astype(F32)
        n_scr[d] = nrow * dec + jnp.sum(kw, axis=0, keepdims=True)
        m_scr[d] = m_new

    def body(i, carry):
        chunk(i, 0)
        chunk(nch - 1 - i, 1)
        return carry

    lax.fori_loop(0, nch, body, 0, unroll=2)

    def fin_chunk(ci, carry):
        rows = pl.ds(pl.multiple_of(ci * ML_CHUNK, ML_CHUNK), ML_CHUNK)
        hs = h_scr[0, rows, :] + h_scr[1, rows, :]
        ms = _dot_hilo(hs * hs, e_ref[...]) * (1.0 / HEAD_D)
        po = p_ref[rows, 3 * GROUP_W:4 * GROUP_W]
        out_ref[rows, :] = (hs * lax.rsqrt(ms + EPS) * ng_ref[...] * _sigmoid(po)).astype(out_ref.dtype)
        return carry

    lax.fori_loop(0, nch, fin_chunk, 0, unroll=4)
    if emit_state:
        for d in range(2):
            for h in range(HEADS):
                lo = HEAD_D * h
                cst_ref[d, h] = c_scr[d, lo:lo + HEAD_D, lo:lo + HEAD_D]
        nst_ref[...] = n_scr[...]
        mst_ref[...] = m_scr[...]


def _mlstm(p3, g3, gr4, conv_w3, brow, brows, norm_g, c0, n0, m0, conv_width, emit_state):
    b, seq, _ = p3.shape
    nch = seq // ML_CHUNK
    tri, rowtri, expand, head = _mlstm_consts()
    const2 = lambda i: (0, 0)
    const3 = lambda i: (0, 0, 0)
    out_shape = [jax.ShapeDtypeStruct((b, seq, GROUP_W), BF16)]
    out_specs = [pl.BlockSpec((None, seq, GROUP_W), lambda i: (i, 0, 0))]
    c_spec = pl.BlockSpec((None, 2, HEADS, HEAD_D, HEAD_D), lambda i: (i, 0, 0, 0, 0))
    row_spec = pl.BlockSpec((None, 2, 1, GROUP_W), lambda i: (i, 0, 0, 0))
    if emit_state:
        out_shape += [jax.ShapeDtypeStruct((b, 2, HEADS, HEAD_D, HEAD_D), F32),
                      jax.ShapeDtypeStruct((b, 2, 1, GROUP_W), F32), jax.ShapeDtypeStruct((b, 2, 1, GROUP_W), F32)]
        out_specs += [c_spec, row_spec, row_spec]
    dir_seq = (2, seq, GROUP_W)
    dir_chunks = (2, nch, GROUP_W)
    res = pl.pallas_call(
        functools.partial(_mlstm_kernel, seq=seq, conv_w=conv_width, emit_state=emit_state),
        grid=(b,),
        in_specs=[
            pl.BlockSpec((None, seq, 4 * GROUP_W), lambda i: (i, 0, 0)),
            pl.BlockSpec((None, seq, LANES), lambda i: (i, 0, 0)),
            pl.BlockSpec((None, 4, nch, GROUP_W), lambda i: (i, 0, 0, 0)),
            pl.BlockSpec((3, 2 * GROUP_W), const2),
            pl.BlockSpec((1, LANES), const2),
            pl.BlockSpec((4, GROUP_W), const2),
            pl.BlockSpec((1, GROUP_W), const2),
            c_spec, row_spec, row_spec,
            pl.BlockSpec((2, ML_CHUNK, ML_CHUNK), const3),
            pl.BlockSpec((2, GROUP_W, GROUP_W), const3),
            pl.BlockSpec((2, LANES, GROUP_W), const3),
            pl.BlockSpec((GROUP_W, GROUP_W), const2),
        ],
        out_specs=out_specs,
        out_shape=out_shape,
        scratch_shapes=[
            pltpu.VMEM((seq, GROUP_W), BF16),
            pltpu.VMEM((seq, GROUP_W), F32),
            pltpu.VMEM(dir_seq, F32), pltpu.VMEM(dir_seq, F32), pltpu.VMEM(dir_seq, F32), pltpu.VMEM(dir_seq, F32),
            pltpu.VMEM(dir_chunks, F32), pltpu.VMEM(dir_chunks, F32), pltpu.VMEM(dir_chunks, F32),
            pltpu.VMEM((2, GROUP_W, GROUP_W), F32),
            pltpu.VMEM((2, 1, GROUP_W), F32),
            pltpu.VMEM((2, 1, GROUP_W), F32),
        ],
        name="mlstm",
    )(p3, g3, gr4, conv_w3, brow, brows, norm_g, c0, n0, m0, jnp.asarray(tri, BF16), jnp.asarray(rowtri, BF16),
      jnp.asarray(expand, BF16), jnp.asarray(head, BF16))
    if emit_state:
        return res
    return res[0], None, None, None


def _s5_kernel(*refs, nb, reverse, final):
    if final:
        (u_ref, hre0_ref, him0_ref, are_ref, aim_ref, wb_ref, wc_ref, yin_ref, d_ref, gw_ref, gb_ref,
         y_ref, sre_ref, sim_ref, utm_scr, bu_scr, hb_scr, ytm_scr, hre_scr, him_scr) = refs
    else:
        (u_ref, hre0_ref, him0_ref, are_ref, aim_ref, wb_ref, wc_ref,
         y_ref, sre_ref, sim_ref, utm_scr, bu_scr, hb_scr, ytm_scr, hre_scr, him_scr) = refs
    tc = S5_TC
    step_id = pl.program_id(0)

    @pl.when(step_id == 0)
    def _():
        hre_scr[...] = hre0_ref[...]
        him_scr[...] = him0_ref[...]

    halves = GROUP_W // LANES
    for b in range(nb):
        for hv in range(halves):
            utm_scr[hv, pl.ds(b, tc, stride=nb), :] = u_ref[b, :, hv * LANES:(hv + 1) * LANES]
    u_tm = jnp.concatenate([utm_scr[hv] for hv in range(halves)], axis=1)
    bu_scr[...] = _dot(u_tm.astype(BF16), wb_ref[...])
    are = jnp.broadcast_to(are_ref[...], (nb, S5_STATE))
    aim = jnp.broadcast_to(aim_ref[...], (nb, S5_STATE))

    spi = max(ROW_ALIGN // nb, 1)
    rows_it = spi * nb

    def step(i, carry):
        hre, him = carry
        r0 = pl.multiple_of(((tc // spi - 1 - i) if reverse else i) * rows_it, rows_it)
        res = [None] * spi
        ims = [None] * spi
        for s in (range(spi - 1, -1, -1) if reverse else range(spi)):
            bu = bu_scr[pl.ds(r0 + s * nb, nb), :]
            hre, him = (are * hre - aim * him + bu[:, 0:S5_STATE],
                        are * him + aim * hre + bu[:, S5_STATE:2 * S5_STATE])
            res[s], ims[s] = hre, him
        hb_scr[pl.ds(r0, rows_it), 0:S5_STATE] = jnp.concatenate(res, axis=0).astype(BF16)
        hb_scr[pl.ds(r0, rows_it), S5_STATE:2 * S5_STATE] = jnp.concatenate(ims, axis=0).astype(BF16)
        return hre, him

    hre, him = lax.fori_loop(0, tc // spi, step, (hre_scr[...], him_scr[...]))
    hre_scr[...] = hre
    him_scr[...] = him
    y_tm = _dot(hb_scr[...], wc_ref[...])
    for hv in range(halves):
        ytm_scr[hv] = y_tm[:, hv * LANES:(hv + 1) * LANES]
    for b in range(nb):
        yb = jnp.concatenate([ytm_scr[hv, pl.ds(b, tc, stride=nb), :] for hv in range(halves)], axis=1)
        if final:
            y = yb + yin_ref[b] + d_ref[...] * u_ref[b]
            z = 0.5 * y * (1.0 + jnp.tanh(math.sqrt(2.0 / math.pi) * (y + 0.044715 * (y * y * y))))
            gate = _sigmoid(_dot(z.astype(BF16), gw_ref[...]) + gb_ref[...])
            y_ref[b] = (z * gate).astype(y_ref.dtype)
        else:
            y_ref[b] = yb

    @pl.when(step_id == pl.num_programs(0) - 1)
    def _():
        sre_ref[...] = hre
        sim_ref[...] = him


def _s5_pass(u3, hre0, him0, are, aim, wb, wc, reverse, extra):
    nb, seq, _ = u3.shape
    nch = seq // S5_TC
    final = extra is not None
    chunk = (lambda i: (0, nch - 1 - i, 0)) if reverse else (lambda i: (0, i, 0))
    const2 = lambda i: (0, 0)
    in_specs = [
        pl.BlockSpec((nb, S5_TC, GROUP_W), chunk),
        pl.BlockSpec((nb, S5_STATE), const2),
        pl.BlockSpec((nb, S5_STATE), const2),
        pl.BlockSpec((1, S5_STATE), const2),
        pl.BlockSpec((1, S5_STATE), const2),
        pl.BlockSpec((GROUP_W, 2 * S5_STATE), const2),
        pl.BlockSpec((2 * S5_STATE, GROUP_W), const2),
    ]
    args = [u3, hre0, him0, are, aim, wb, wc]
    if final:
        yin, dvec, gw, gb = extra
        in_specs += [pl.BlockSpec((nb, S5_TC, GROUP_W), chunk), pl.BlockSpec((1, GROUP_W), const2),
                     pl.BlockSpec((GROUP_W, GROUP_W), const2), pl.BlockSpec((1, GROUP_W), const2)]
        args += [yin, dvec, gw, gb]
    return pl.pallas_call(
        functools.partial(_s5_kernel, nb=nb, reverse=reverse, final=final),
        grid=(nch,),
        in_specs=in_specs,
        out_specs=[pl.BlockSpec((nb, S5_TC, GROUP_W), chunk),
                   pl.BlockSpec((nb, S5_STATE), const2), pl.BlockSpec((nb, S5_STATE), const2)],
        out_shape=[jax.ShapeDtypeStruct((nb, seq, GROUP_W), BF16 if final else F32),
                   jax.ShapeDtypeStruct((nb, S5_STATE), F32), jax.ShapeDtypeStruct((nb, S5_STATE), F32)],
        scratch_shapes=[
            pltpu.VMEM((GROUP_W // LANES, nb * S5_TC, LANES), F32),
            pltpu.VMEM((nb * S5_TC, 2 * S5_STATE), F32),
            pltpu.VMEM((nb * S5_TC, 2 * S5_STATE), BF16),
            pltpu.VMEM((GROUP_W // LANES, nb * S5_TC, LANES), F32),
            pltpu.VMEM((nb, S5_STATE), F32),
            pltpu.VMEM((nb, S5_STATE), F32),
        ],
        name="s5_bwd" if reverse else "s5_fwd",
    )(*args)


def _s5_params(a_re, a_im, log_dt, b_re, b_im, c_re, c_im):
    eye = jnp.eye(S5_GROUPS, dtype=F32)
    dt = jnp.exp(log_dt)[..., None]
    mag = jnp.exp(a_re * dt)
    ab_re, ab_im = mag * jnp.cos(a_im * dt), mag * jnp.sin(a_im * dt)
    den = a_re * a_re + a_im * a_im
    g_re = ((ab_re - 1.0) * a_re + ab_im * a_im) / den
    g_im = (ab_im * a_re - (ab_re - 1.0) * a_im) / den
    bb_re = g_re[..., None] * b_re - g_im[..., None] * b_im
    bb_im = g_re[..., None] * b_im + g_im[..., None] * b_re

    def in_mat(bb):
        return jnp.einsum('dgpc,gh->dgchp', bb, eye).reshape(2, GROUP_W, S5_STATE)

    wb = jnp.concatenate([in_mat(bb_re), in_mat(bb_im)], axis=-1).astype(BF16)

    def out_mat(cc):
        return jnp.einsum('gcp,gh->gphc', cc, eye).reshape(S5_STATE, GROUP_W)

    wc = jnp.concatenate([out_mat(c_re), -out_mat(c_im)], axis=0).astype(BF16)
    return ab_re.reshape(2, 1, S5_STATE), ab_im.reshape(2, 1, S5_STATE), wb, wc


def _s5(u3, h0_re, h0_im, prm, d_vec, glu_w, glu_b):
    ab_re, ab_im, wb, wc = prm
    yf, fre, fim = _s5_pass(u3, h0_re[:, 0], h0_im[:, 0], ab_re[0], ab_im[0], wb[0], wc, False, None)
    out, bre, bim = _s5_pass(u3, h0_re[:, 1], h0_im[:, 1], ab_re[1], ab_im[1], wb[1], wc, True,
                             (yf, d_vec, glu_w, glu_b))
    return out, jnp.stack([fre, bre], axis=1), jnp.stack([fim, bim], axis=1)


@functools.lru_cache(maxsize=None)
def _dft_consts(n):
    idx = np.arange(n, dtype=np.int64)
    ang = (np.pi / n) * ((idx[:, None] * idx[None, :]) % (2 * n)).astype(np.float64)
    cos, sin = np.cos(ang), np.sin(ang)
    sign = np.where(idx % 2 == 0, 1.0, -1.0)
    sin_n = sin.copy()
    sin_n[0, :] = sign
    fwd = np.concatenate([cos, sin_n], axis=0)
    inv_c = cos.T / n
    inv_c[:, 0] = 1.0 / (2 * n)
    inv_s = sin.T / n
    inv_s[:, 0] = sign / (2 * n)
    inv = np.concatenate([inv_c, inv_s], axis=1)
    return fwd.astype(np.float32), inv.astype(np.float32), sign.astype(np.float32)[:, None]


@functools.lru_cache(maxsize=None)
def _hyena_feats(n):
    t = np.linspace(0.0, 1.0, n, dtype=np.float64)[:, None]
    w = (2.0 * np.pi / n) * np.arange(n, dtype=np.float64)[:, None]
    bands = np.linspace(1e-4, HY_BANDS - 1, HY_BANDS, dtype=np.float64)[None, :]
    feats = np.concatenate([t, np.cos(w * bands), -np.sin(w * bands)], axis=-1)
    pad = np.zeros((n, LANES - HY_EMB))
    return np.concatenate([feats, pad], axis=-1).astype(np.float32), t.astype(np.float32)


def _hyfilt_kernel(feat_ref, t_ref, sign_ref, w1_ref, b1_ref, w2_ref, b2_ref, w3_ref, fr_ref, dec_ref,
                   fh_ref, fl_ref, ka_ref, kb_ref, ka2_ref, *, n):
    freq = fr_ref[...]
    hdn = jnp.sin(freq * (_dot3(feat_ref[...], w1_ref[...]) + b1_ref[...]))
    hdn = jnp.sin(freq * (_dot3(hdn, w2_ref[...]) + b2_ref[...]))
    h = _dot3(hdn, w3_ref[...]) * (jnp.exp(-t_ref[...] * jnp.abs(dec_ref[...])) + HY_SHIFT)
    half = HY_ORDER * GROUP_W
    row = lax.broadcasted_iota(jnp.int32, (n, 1), 0)
    h0 = h[:, 0:half]
    h1 = jnp.where(row == 0, 0.0, h[:, half:2 * half])
    norm = jnp.sum(jnp.abs(h0), axis=0, keepdims=True) + jnp.sum(jnp.abs(h1), axis=0, keepdims=True)
    hp = (h0 + h1) / norm
    hm = (h0 - h1) / norm

    def dft(lo, x):
        xh = x.astype(BF16)
        xl = (x - xh.astype(F32)).astype(BF16)
        fh = fh_ref[lo:lo + n, :]
        return _dot(fh, xh) + _dot(fl_ref[lo:lo + n, :], xh) + _dot(fh, xl)

    kc = dft(0, hp)
    ks = dft(n, hm)
    kn = jnp.sum(hp * sign_ref[...], axis=0, keepdims=True)
    ka_ref[...] = kc
    kb_ref[...] = jnp.where(row == 0, 0.0, ks)
    ka2_ref[...] = jnp.where(row == 0, kn, kc)


def _hyena_filter(n, w1p, b1, w2, b2, w3, freq, decay):
    fwd, _, sign = _dft_consts(n)
    feats, t = _hyena_feats(n)
    fwd = jnp.asarray(fwd)
    fh = fwd.astype(BF16)
    fl = (fwd - fh.astype(F32)).astype(BF16)
    half = HY_ORDER * GROUP_W
    return pl.pallas_call(
        functools.partial(_hyfilt_kernel, n=n),
        out_shape=[jax.ShapeDtypeStruct((n, half), F32)] * 3,
        name="hyena_filter",
    )(jnp.asarray(feats), jnp.asarray(t), jnp.asarray(sign), w1p, b1, w2, b2, w3, freq, decay, fh, fl)


def _hyena_kernel(p_ref, cw_ref, ka_ref, kb_ref, ka2_ref, bias_ref, f_ref, g_ref, out_ref,
                  z_scr, y_scr, yb_scr, zf_scr, *, seq, conv_w):
    nch = seq // ML_CHUNK
    rowid = lax.broadcasted_iota(jnp.int32, (ML_CHUNK, 1), 0)

    def conv_chunk(ci, carry):
        r0 = pl.multiple_of(ci * ML_CHUNK, ML_CHUNK)
        zc = p_ref[pl.ds(r0, ML_CHUNK), :]
        zp = pltpu.roll(zc, 1, axis=0)
        zn = pltpu.roll(zc, ML_CHUNK - 1, axis=0)
        if conv_w == ML_CHUNK:
            prev_row = jnp.zeros((1, 3 * GROUP_W), F32)
            next_row = prev_row
        else:
            prev_row = p_ref[pl.ds(jnp.maximum(r0 - 1, 0), 1), :]
            prev_row = jnp.where(r0 % conv_w == 0, 0.0, prev_row)
            next_row = p_ref[pl.ds(jnp.minimum(r0 + ML_CHUNK, seq - 1), 1), :]
            next_row = jnp.where((r0 + ML_CHUNK) % conv_w == 0, 0.0, next_row)
        zp = jnp.where(rowid == 0, prev_row, zp)
        zn = jnp.where(rowid == ML_CHUNK - 1, next_row, zn)
        z_scr[pl.ds(r0, ML_CHUNK), :] = cw_ref[0:1, :] * zp + cw_ref[1:2, :] * zc + cw_ref[2:3, :] * zn
        return carry

    lax.fori_loop(0, nch, conv_chunk, 0)
    y_scr[...] = z_scr[:, 0:GROUP_W]
    nblk = seq // HY_BLK
    for o in range(HY_ORDER):
        cols = slice(o * GROUP_W, (o + 1) * GROUP_W)
        yb_scr[...] = y_scr[...].astype(BF16)

        def freq_block(fi, carry):
            f0 = pl.multiple_of(fi * HY_BLK, HY_BLK)
            fr = pl.ds(f0, HY_BLK)
            xc = _dot(f_ref[fr, :], yb_scr[...])
            xs = _dot(f_ref[pl.ds(seq + f0, HY_BLK), :], yb_scr[...])
            kb = kb_ref[fr, cols]
            zf_scr[fr, :] = (xc * ka_ref[fr, cols] - xs * kb).astype(BF16)
            zf_scr[pl.ds(seq + f0, HY_BLK), :] = (xc * kb + xs * ka2_ref[fr, cols]).astype(BF16)
            return carry

        lax.fori_loop(0, nblk, freq_block, 0)

        def time_block(ti, carry):
            tr = pl.ds(pl.multiple_of(ti * HY_BLK, HY_BLK), HY_BLK)
            conv = _dot(g_ref[tr, :], zf_scr[...])
            gate = z_scr[tr, (o + 1) * GROUP_W:(o + 2) * GROUP_W]
            y_scr[tr, :] = gate * (conv + bias_ref[o:o + 1, :] * y_scr[tr, :])
            return carry

        lax.fori_loop(0, nblk, time_block, 0)
    out_ref[...] = y_scr[...].astype(out_ref.dtype)


def _hyena(p3, conv_w3, filt, bias, conv_width):
    b, seq, _ = p3.shape
    ka, kb, ka2 = filt
    fwd, inv, _ = _dft_consts(seq)
    const2 = lambda i: (0, 0)
    half = HY_ORDER * GROUP_W
    return pl.pallas_call(
        functools.partial(_hyena_kernel, seq=seq, conv_w=conv_width),
        grid=(b,),
        in_specs=[
            pl.BlockSpec((None, seq, 3 * GROUP_W), lambda i: (i, 0, 0)),
            pl.BlockSpec((3, 3 * GROUP_W), const2),
            pl.BlockSpec((seq, half), const2),
            pl.BlockSpec((seq, half), const2),
            pl.BlockSpec((seq, half), const2),
            pl.BlockSpec((HY_ORDER, GROUP_W), const2),
            pl.BlockSpec((2 * seq, seq), const2),
            pl.BlockSpec((seq, 2 * seq), const2),
        ],
        out_specs=pl.BlockSpec((None, seq, GROUP_W), lambda i: (i, 0, 0)),
        out_shape=jax.ShapeDtypeStruct((b, seq, GROUP_W), BF16),
        scratch_shapes=[
            pltpu.VMEM((seq, 3 * GROUP_W), F32),
            pltpu.VMEM((seq, GROUP_W), F32),
            pltpu.VMEM((seq, GROUP_W), BF16),
            pltpu.VMEM((2 * seq, GROUP_W), BF16),
        ],
        name="hyena",
    )(p3, conv_w3, ka, kb, ka2, bias, jnp.asarray(fwd, BF16), jnp.asarray(inv, BF16))


def _outproj_kernel(a_ref, b_ref, c_ref, d_ref, x_ref, g1_ref, sh2_ref, sc2_ref, n2_ref, wo_ref,
                    wr_ref, br_ref, x1_ref, h2_ref, gate_ref):
    acc = _dot(a_ref[...], wo_ref[0:GROUP_W, :])
    acc += _dot(b_ref[...], wo_ref[GROUP_W:2 * GROUP_W, :])
    acc += _dot(c_ref[...], wo_ref[2 * GROUP_W:3 * GROUP_W, :])
    acc += _dot(d_ref[...], wo_ref[3 * GROUP_W:4 * GROUP_W, :])
    x1 = x_ref[...] + g1_ref[0] * acc
    x1_ref[...] = x1
    h2 = _rmsnorm_rows(x1, n2_ref[...]) * (1.0 + sc2_ref[0]) + sh2_ref[0]
    h2_ref[...] = h2.astype(BF16)
    logits = _dot3(h2, wr_ref[...]) + br_ref[...]
    lane = lax.broadcasted_iota(jnp.int32, logits.shape, 1).astype(F32)
    big = float(LANES)
    gl = jnp.where(lane < N_EXPERT_GROUPS, logits, NEG_INF)
    gmax = jnp.max(gl, axis=1, keepdims=True)
    gsel = jnp.min(jnp.where(gl == gmax, lane, big), axis=1, keepdims=True)
    psel = 1.0 / jnp.sum(jnp.exp(gl - gmax), axis=1, keepdims=True)
    lo = N_EXPERT_GROUPS + EXPERTS_PER_GROUP * gsel
    el = jnp.where((lane >= lo) & (lane < lo + EXPERTS_PER_GROUP), logits, NEG_INF)
    v1 = jnp.max(el, axis=1, keepdims=True)
    i1 = jnp.min(jnp.where(el == v1, lane, big), axis=1, keepdims=True)
    el2 = jnp.where(lane == i1, NEG_INF, el)
    v2 = jnp.max(el2, axis=1, keepdims=True)
    i2 = jnp.min(jnp.where(el2 == v2, lane, big), axis=1, keepdims=True)
    e2 = jnp.exp(v2 - v1)
    w1 = psel / (1.0 + e2)
    w2 = psel * e2 / (1.0 + e2)
    gate_ref[...] = (jnp.where(lane == 0.0, gsel, 0.0) + jnp.where(lane == i1 - lo + 1.0, w1, 0.0)
                     + jnp.where(lane == i2 - lo + 1.0, w2, 0.0))


def _outproj(mix, x2d, mod3, row_of_tile, norm_g, w_out, w_r, b_r):
    t = x2d.shape[0]
    tm = TOK_TILE
    const = lambda i: (0, 0)
    tile = lambda i: (i, 0)
    return pl.pallas_call(
        _outproj_kernel,
        grid=(t // tm,),
        in_specs=[pl.BlockSpec((tm, GROUP_W), tile)] * 4 + [
            pl.BlockSpec((tm, D_MODEL), tile),
            _mod_spec(2, row_of_tile),
            _mod_spec(3, row_of_tile),
            _mod_spec(4, row_of_tile),
            pl.BlockSpec((1, D_MODEL), const),
            pl.BlockSpec((D_MODEL, D_MODEL), const),
            pl.BlockSpec((D_MODEL, LANES), const),
            pl.BlockSpec((1, LANES), const),
        ],
        out_specs=[pl.BlockSpec((tm, D_MODEL), tile), pl.BlockSpec((tm, D_MODEL), tile),
                   pl.BlockSpec((tm, LANES), tile)],
        out_shape=[jax.ShapeDtypeStruct((t, D_MODEL), F32), jax.ShapeDtypeStruct((t, D_MODEL), BF16),
                   jax.ShapeDtypeStruct((t, LANES), F32)],
        name="outproj_router",
    )(*mix, x2d, mod3, mod3, mod3, norm_g, w_out, w_r, b_r)


def _tile_positions(gates, tri_ref, extra):
    lane = lax.broadcasted_iota(jnp.int32, gates.shape, 1).astype(F32)
    gsel = gates[:, 0:1]
    member = jnp.where((lane == gsel) & (lane < N_EXPERT_GROUPS), 1.0, 0.0)
    before = _dot(tri_ref[...], member.astype(BF16))
    pos = gsel * MOE_TOK + jnp.sum(member * (before + extra), axis=1, keepdims=True)
    slot = lax.broadcasted_iota(jnp.int32, (MOE_TOK, N_EXPERT_GROUPS * MOE_TOK), 1).astype(F32)
    return jnp.where(slot == pos, 1.0, 0.0).astype(BF16)


def _moe_pack_kernel(off_ref, cnt_ref, tot_ref, h_ref, gate_ref, tri_ref, xg_ref, gg_ref,
                     xs_scr, gs_scr, zx_scr, zg_scr, sem):
    i = pl.program_id(0)
    slot = i % 2
    gates = gate_ref[...]
    lane = lax.broadcasted_iota(jnp.int32, gates.shape, 1).astype(F32)
    member = jnp.where((lane == gates[:, 0:1]) & (lane < N_EXPERT_GROUPS), 1.0, 0.0).astype(BF16)
    grp = lax.broadcasted_iota(jnp.int32, (8, LANES), 0)
    pick = jnp.where(grp == lax.broadcasted_iota(jnp.int32, (8, LANES), 1), 1.0, 0.0).astype(BF16)
    member_t = _dot_nt(pick, member)
    before_t = _dot_nt(member_t.astype(BF16), tri_ref[...])
    gid = lax.broadcasted_iota(jnp.int32, member_t.shape, 0).astype(F32)
    pos_t = jnp.sum(member_t * (gid * MOE_TOK + before_t), axis=0, keepdims=True)
    row_id = lax.broadcasted_iota(jnp.int32, (N_EXPERT_GROUPS * MOE_TOK, MOE_TOK), 0).astype(F32)
    p = jnp.where(row_id == pos_t, 1.0, 0.0).astype(BF16)
    xs_scr[slot] = _dot(p, h_ref[...]).astype(BF16)
    hi, mid, lo = _split3(gates)
    gs_scr[slot] = _dot(p, hi) + _dot(p, mid) + _dot(p, lo)

    def for_each_piece(tile, buf, action):
        for g in range(N_EXPERT_GROUPS):
            off = pl.multiple_of(off_ref[tile * N_EXPERT_GROUPS + g], ROW_ALIGN)
            for j in range(MOE_TOK // MOE_PIECE):
                src = pl.ds(g * MOE_TOK + j * MOE_PIECE, MOE_PIECE)
                dst = pl.ds(off + j * MOE_PIECE, MOE_PIECE)

                @pl.when(cnt_ref[tile * N_EXPERT_GROUPS + g] > j * MOE_PIECE)
                def _():
                    action(pltpu.make_async_copy(xs_scr.at[buf, src], xg_ref.at[g, dst], sem.at[buf, 0, g, j]))
                    action(pltpu.make_async_copy(gs_scr.at[buf, src], gg_ref.at[g, dst], sem.at[buf, 1, g, j]))

    @pl.when(i > 0)
    def _():
        for_each_piece(i - 1, 1 - slot, lambda cp: cp.wait())

    for_each_piece(i, slot, lambda cp: cp.start())

    @pl.when(i == pl.num_programs(0) - 1)
    def _():
        for_each_piece(i, slot, lambda cp: cp.wait())
        zx_scr[...] = jnp.zeros_like(zx_scr)
        zg_scr[...] = jnp.zeros_like(zg_scr)

        def tail_copies(g):
            tot = pl.multiple_of(tot_ref[g], ROW_ALIGN)
            return (pltpu.make_async_copy(zx_scr, xg_ref.at[g, pl.ds(tot, MOE_ROWS)], sem.at[0, 0, g, 0]),
                    pltpu.make_async_copy(zg_scr, gg_ref.at[g, pl.ds(tot, MOE_ROWS)], sem.at[0, 1, g, 0]))

        for g in range(N_EXPERT_GROUPS):
            for cp in tail_copies(g):
                cp.start()
        for g in range(N_EXPERT_GROUPS):
            for cp in tail_copies(g):
                cp.wait()


def _moe_expert_kernel(grp_ref, chk_ref, meta_ref, x_ref, g_ref, wg_ref, wu_ref, wd_ref, y_ref):
    k = pl.program_id(0)
    half = MOE_ROWS // 2

    def experts(rows):
        x = x_ref[0:rows, :]
        gates = g_ref[0:rows, :]
        acc = None
        for e in range(EXPERTS_PER_GROUP):
            a = _dot(x, wg_ref[e])
            u = _dot(x, wu_ref[e])
            hid = (_silu(a) * u * gates[:, 1 + e:2 + e]).astype(BF16)
            part = _dot(hid, wd_ref[e])
            acc = part if acc is None else acc + part
        return acc.astype(y_ref.dtype)

    def whole_chunk():
        y_ref[...] = experts(MOE_ROWS)

    def lower_half():
        y_ref[0:half, :] = experts(half)
        y_ref[half:MOE_ROWS, :] = jnp.zeros((MOE_ROWS - half, D_MODEL), y_ref.dtype)

    @pl.when(k < meta_ref[0])
    def _():
        lax.cond(meta_ref[1 + k] > half, whole_chunk, lower_half)


def _moe_unpack_kernel(st_ref, dl_ref, cnt_ref, gate_ref, tri_ref, x1_ref, g2_ref, fn_ref, yg_ref, out_ref,
                       ys_scr, sem, *, final):
    i = pl.program_id(0)
    slot = i % 2

    def for_each_piece(tile, buf, action):
        for g in range(N_EXPERT_GROUPS):
            k = tile * N_EXPERT_GROUPS + g
            start = pl.multiple_of(st_ref[k], ROW_ALIGN)
            for j in range(MOE_TOK // MOE_PIECE):
                lo, hi = j * MOE_PIECE, (j + 1) * MOE_PIECE

                @pl.when((dl_ref[k] < hi) & (dl_ref[k] + cnt_ref[k] > lo))
                def _():
                    action(pltpu.make_async_copy(yg_ref.at[g, pl.ds(start + lo, MOE_PIECE)],
                                                 ys_scr.at[buf, pl.ds(g * MOE_TOK + lo, MOE_PIECE)],
                                                 sem.at[buf, g, j]))

    @pl.when(i == 0)
    def _():
        ys_scr[...] = jnp.zeros_like(ys_scr)
        for_each_piece(0, 0, lambda cp: cp.start())

    @pl.when(i + 1 < pl.num_programs(0))
    def _():
        for_each_piece(i + 1, 1 - slot, lambda cp: cp.start())

    gates = gate_ref[...]
    lane = lax.broadcasted_iota(jnp.int32, (1, LANES), 1)
    delta = jnp.zeros((1, LANES), F32)
    for g in range(N_EXPERT_GROUPS):
        delta = jnp.where(lane == g, dl_ref[i * N_EXPERT_GROUPS + g].astype(F32), delta)
    pt = _tile_positions(gates, tri_ref, delta)
    for_each_piece(i, slot, lambda cp: cp.wait())
    x2 = x1_ref[...] + g2_ref[0] * _dot(pt, ys_scr[slot])
    if final:
        x2 = _rmsnorm_rows(x2, fn_ref[...])
    out_ref[...] = x2


def _moe_schedule(gsel, n_tiles, n_steps):
    ng = N_EXPERT_GROUPS
    member = gsel.reshape(n_tiles, MOE_TOK, 1) == jnp.arange(ng, dtype=jnp.int32)
    cnt = jnp.sum(member.astype(jnp.int32), axis=1)
    padded = (cnt + ROW_ALIGN - 1) // ROW_ALIGN * ROW_ALIGN
    off = jnp.cumsum(padded, axis=0) - padded
    tot = jnp.sum(padded, axis=0)
    n_chunks = jnp.maximum((tot + MOE_ROWS - 1) // MOE_ROWS, 1)
    start = jnp.minimum(off, n_chunks * MOE_ROWS - MOE_TOK)
    ends = jnp.cumsum(n_chunks)
    n_active = ends[-1]
    step = jnp.minimum(jnp.arange(n_steps, dtype=jnp.int32), n_active - 1)
    grp = jnp.sum((step[:, None] >= ends[None, :]).astype(jnp.int32), axis=1)
    chk = step - (ends - n_chunks)[grp]
    i32 = lambda a: a.astype(jnp.int32)
    populated = jnp.clip(tot[grp] - chk * MOE_ROWS, 0, MOE_ROWS)
    meta = jnp.concatenate([n_active.reshape(1), populated])
    return (i32(off).reshape(-1), i32(padded).reshape(-1), i32(tot), i32(start).reshape(-1),
            i32(off - start).reshape(-1), i32(grp), i32(chk), i32(meta))


def _moe(h2, gates, layer, w_gate, w_up, w_down, x1, mod3, row_of_tile, final_g, final):
    t = h2.shape[0]
    ng = N_EXPERT_GROUPS
    n_tiles = t // MOE_TOK
    cap = -(-(t + ROW_ALIGN * n_tiles + MOE_ROWS) // MOE_ROWS) * MOE_ROWS
    n_steps = (t + ROW_ALIGN * n_tiles) // MOE_ROWS + ng
    off, cnt, tot, start, delta, grp, chk, meta = _moe_schedule(gates[:, 0].astype(jnp.int32), n_tiles,
                                                                    n_steps)
    pieces = MOE_TOK // MOE_PIECE
    r = np.arange(MOE_TOK)
    tri = jnp.asarray((r[:, None] > r[None, :]).astype(np.float32), BF16)
    any_spec = pl.BlockSpec(memory_space=pl.ANY)

    xg, gg = pl.pallas_call(
        _moe_pack_kernel,
        grid_spec=pltpu.PrefetchScalarGridSpec(
            num_scalar_prefetch=3,
            grid=(n_tiles,),
            in_specs=[
                pl.BlockSpec((MOE_TOK, D_MODEL), lambda i, *_: (i, 0)),
                pl.BlockSpec((MOE_TOK, LANES), lambda i, *_: (i, 0)),
                pl.BlockSpec((MOE_TOK, MOE_TOK), lambda i, *_: (0, 0)),
            ],
            out_specs=[any_spec, any_spec],
            scratch_shapes=[
                pltpu.VMEM((2, ng * MOE_TOK, D_MODEL), BF16),
                pltpu.VMEM((2, ng * MOE_TOK, LANES), F32),
                pltpu.VMEM((MOE_ROWS, D_MODEL), BF16),
                pltpu.VMEM((MOE_ROWS, LANES), F32),
                pltpu.SemaphoreType.DMA((2, 2, ng, pieces)),
            ],
        ),
        out_shape=[jax.ShapeDtypeStruct((ng, cap, D_MODEL), BF16), jax.ShapeDtypeStruct((ng, cap, LANES), F32)],
        name="moe_pack",
    )(off, cnt, tot, h2, gates, tri)

    group_w = lambda shape: pl.BlockSpec((None, None, EXPERTS_PER_GROUP) + shape,
                                         lambda k, grp, chk, na: (layer, grp[k], 0, 0, 0))
    rows = lambda width: pl.BlockSpec((None, MOE_ROWS, width), lambda k, grp, chk, na: (grp[k], chk[k], 0))
    yg = pl.pallas_call(
        _moe_expert_kernel,
        grid_spec=pltpu.PrefetchScalarGridSpec(
            num_scalar_prefetch=3,
            grid=(n_steps,),
            in_specs=[rows(D_MODEL), rows(LANES), group_w((D_MODEL, D_EXPERT)), group_w((D_MODEL, D_EXPERT)),
                      group_w((D_EXPERT, D_MODEL))],
            out_specs=rows(D_MODEL),
        ),
        out_shape=jax.ShapeDtypeStruct((ng, cap, D_MODEL), BF16),
        name="moe_experts",
    )(grp, chk, meta, xg, gg,
      w_gate.reshape(DEPTH, ng, EXPERTS_PER_GROUP, D_MODEL, D_EXPERT),
      w_up.reshape(DEPTH, ng, EXPERTS_PER_GROUP, D_MODEL, D_EXPERT),
      w_down.reshape(DEPTH, ng, EXPERTS_PER_GROUP, D_EXPERT, D_MODEL))

    return pl.pallas_call(
        functools.partial(_moe_unpack_kernel, final=final),
        grid_spec=pltpu.PrefetchScalarGridSpec(
            num_scalar_prefetch=3,
            grid=(n_tiles,),
            in_specs=[
                pl.BlockSpec((MOE_TOK, LANES), lambda i, *_: (i, 0)),
                pl.BlockSpec((MOE_TOK, MOE_TOK), lambda i, *_: (0, 0)),
                pl.BlockSpec((MOE_TOK, D_MODEL), lambda i, *_: (i, 0)),
                pl.BlockSpec((1, 1, D_MODEL), lambda i, *_: (row_of_tile(i), 0, 5)),
                pl.BlockSpec((1, D_MODEL), lambda i, *_: (0, 0)),
                any_spec,
            ],
            out_specs=pl.BlockSpec((MOE_TOK, D_MODEL), lambda i, *_: (i, 0)),
            scratch_shapes=[pltpu.VMEM((2, ng * MOE_TOK, D_MODEL), BF16),
                            pltpu.SemaphoreType.DMA((2, ng, pieces))],
        ),
        out_shape=jax.ShapeDtypeStruct((t, D_MODEL), F32),
        name="moe_unpack",
    )(start, delta, cnt, gates, tri, x1, mod3, final_g, yg)


def _prep_layer(l, w):
    gw = GROUP_W
    w_in = w['w_in'][l]
    w_g = w_in[:, 9 * gw:9 * gw + N_GATES]
    lb_all = jnp.cumsum(jax.nn.softmax(w['hg_lb'].astype(F32), axis=0), axis=0)
    lb = lb_all[l] - lb_all[0]
    lb_rows = jnp.zeros((8, gw), F32).at[0].set(jnp.log(lb)).at[1].set(jnp.log1p(-lb)).at[2].set(1.0 - lb)
    w_r = jnp.zeros((D_MODEL, LANES), F32)
    w_r = w_r.at[:, 0:N_EXPERT_GROUPS].set(w['moe_wg'][l])
    w_r = w_r.at[:, N_EXPERT_GROUPS:N_EXPERT_GROUPS + N_EXPERTS].set(w['moe_we'][l])
    b_r = jnp.zeros((1, LANES), F32)
    b_r = b_r.at[0, 0:N_EXPERT_GROUPS].set(w['moe_bg'][l])
    b_r = b_r.at[0, N_EXPERT_GROUPS:N_EXPERT_GROUPS + N_EXPERTS].set(w['moe_be'][l])
    gate_bias = jnp.concatenate([w['ml_i_bias'][l], w['ml_f_bias'][l]])
    w1p = jnp.zeros((LANES, HY_HIDDEN), F32).at[0:HY_EMB].set(w['hy_w1'][l])
    return dict(
        norm1=w['norm1'][l][None, :], norm2=w['norm2'][l][None, :],
        w_a=w_in[:, :9 * gw].astype(BF16), w_b=w_in[:, 9 * gw + N_GATES:].astype(BF16),
        w_g=jnp.zeros((D_MODEL, LANES), BF16).at[:, 0:N_GATES].set(w_g.astype(BF16)), w_gt=w_g.T.astype(BF16),
        w_out=w['w_out'][l].astype(BF16),
        lb_rows=lb_rows, hg_norm=w['hg_norm'][l][None, :],
        ml_conv=w['ml_conv'][l], ml_brow=jnp.zeros((1, LANES), F32).at[0, 0:N_GATES].set(gate_bias),
        ml_brows=jnp.repeat(gate_bias.reshape(4, HEADS), HEAD_D, axis=1),
        ml_norm=w['ml_norm'][l][None, :],
        s5=_s5_params(w['s5_a_re'][l], w['s5_a_im'][l], w['s5_log_dt'][l], w['s5_b_re'][l], w['s5_b_im'][l],
                      w['s5_c_re'][l], w['s5_c_im'][l]),
        s5_d=w['s5_d'][l][None, :], s5_glu_w=w['s5_glu_w'][l].astype(BF16), s5_glu_b=w['s5_glu_b'][l][None, :],
        hy_conv=w['hy_conv'][l], hy_bias=w['hy_bias'][l],
        hy_mlp=(w1p, w['hy_b1'][l][None, :], w['hy_w2'][l], w['hy_b2'][l][None, :], w['hy_w3'][l],
                w['hy_freq'][l][None, :], w['hy_decay'][l][None, :]),
        w_r=w_r, b_r=b_r,
        layer=l,
    )


def _trunk_layer(x2d, nb, seq, mod3, cond_row, st, lw, moe_w, filt, conv_width, emit_state, final_g, final):
    if cond_row is None:
        row_of_tile = lambda i: i // (seq // TOK_TILE)
        in_row_of_tile = lambda i: i // (seq // IN_TILE)
    else:
        row_of_tile = in_row_of_tile = lambda i: cond_row
    hg, ml, su, hy, gt, gtt = _inproj(x2d, mod3, in_row_of_tile, lw['norm1'], lw['w_a'], lw['w_b'], lw['w_g'],
                                       lw['w_gt'])
    nch = seq // ML_CHUNK
    a_out, hg_st = _hgrn(hg.reshape(nb, seq, -1), st['hgrn'], lw['lb_rows'], lw['hg_norm'], emit_state)
    gr4 = jnp.transpose(gtt.reshape(4, HEADS, nb, nch, ML_CHUNK), (2, 0, 3, 1, 4)).reshape(nb, 4, nch, GROUP_W)
    b_out, mlc, mln, mlm = _mlstm(ml.reshape(nb, seq, -1), gt.reshape(nb, seq, LANES), gr4, lw['ml_conv'],
                                  lw['ml_brow'], lw['ml_brows'], lw['ml_norm'], st['ml_c'], st['ml_n'],
                                  st['ml_m'], conv_width, emit_state)
    c_out, s5re, s5im = _s5(su.reshape(nb, seq, -1), st['s5_re'], st['s5_im'], lw['s5'], lw['s5_d'],
                            lw['s5_glu_w'], lw['s5_glu_b'])
    d_out = _hyena(hy.reshape(nb, seq, -1), lw['hy_conv'], filt, lw['hy_bias'], conv_width)
    mix = [m.reshape(nb * seq, GROUP_W) for m in (a_out, b_out, c_out, d_out)]
    x1, h2, gates = _outproj(mix, x2d, mod3, row_of_tile, lw['norm2'], lw['w_out'], lw['w_r'], lw['b_r'])
    x2 = _moe(h2, gates, lw['layer'], *moe_w, x1, mod3, row_of_tile, final_g, final)
    new_st = dict(hgrn=hg_st, ml_c=mlc, ml_n=mln, ml_m=mlm, s5_re=s5re, s5_im=s5im)
    return x2, new_st


def kernel(x_prompt, x_sample, c, state_hgrn, state_mlstm_c, state_mlstm_n, state_mlstm_m, state_s5_re, state_s5_im, c_ctx, w_ada, b_ada, norm1, norm2, w_in, w_out, hg_lb, hg_norm, ml_conv, ml_i_bias, ml_f_bias, ml_norm, s5_a_re, s5_a_im, s5_log_dt, s5_b_re, s5_b_im, s5_c_re, s5_c_im, s5_d, s5_glu_w, s5_glu_b, hy_conv, hy_w1, hy_b1, hy_w2, hy_b2, hy_w3, hy_freq, hy_decay, hy_bias, moe_wg, moe_bg, moe_we, moe_be, moe_w_gate, moe_w_up, moe_w_down, final_norm):
    w = dict(w_in=w_in, w_out=w_out, norm1=norm1, norm2=norm2, hg_lb=hg_lb, hg_norm=hg_norm, ml_conv=ml_conv,
             ml_i_bias=ml_i_bias, ml_f_bias=ml_f_bias, ml_norm=ml_norm, s5_a_re=s5_a_re, s5_a_im=s5_a_im,
             s5_log_dt=s5_log_dt, s5_b_re=s5_b_re, s5_b_im=s5_b_im, s5_c_re=s5_c_re, s5_c_im=s5_c_im,
             s5_d=s5_d, s5_glu_w=s5_glu_w, s5_glu_b=s5_glu_b, hy_conv=hy_conv, hy_w1=hy_w1, hy_b1=hy_b1,
             hy_w2=hy_w2, hy_b2=hy_b2, hy_w3=hy_w3, hy_freq=hy_freq, hy_decay=hy_decay, hy_bias=hy_bias,
             moe_wg=moe_wg, moe_bg=moe_bg, moe_we=moe_we, moe_be=moe_be, moe_w_gate=moe_w_gate,
             moe_w_up=moe_w_up, moe_w_down=moe_w_down)
    bp, lp, _ = x_prompt.shape
    bs, ls, _ = x_sample.shape
    ctx_row = bs
    cond = jnp.zeros((MOD_ROWS, D_MODEL), F32).at[0:bs].set(c).at[ctx_row].set(c_ctx)
    mod = _ada_mod(cond, w_ada, b_ada)
    final_g = final_norm[None, :]

    zero_st = dict(
        hgrn=jnp.zeros((bp, 2, HEADS, HEAD_D, HEAD_D), F32),
        ml_c=jnp.zeros((bp, 2, HEADS, HEAD_D, HEAD_D), F32),
        ml_n=jnp.zeros((bp, 2, 1, GROUP_W), F32),
        ml_m=jnp.zeros((bp, 2, 1, GROUP_W), F32),
        s5_re=jnp.zeros((bp, 2, S5_STATE), F32),
        s5_im=jnp.zeros((bp, 2, S5_STATE), F32))

    xp = x_prompt.reshape(bp * lp, D_MODEL)
    xs = x_sample.reshape(bs * ls, D_MODEL)
    moe_w = (moe_w_gate.astype(BF16), moe_w_up.astype(BF16), moe_w_down.astype(BF16))
    per_layer = []
    for l in range(DEPTH):
        lw = _prep_layer(l, w)
        mod3 = mod[l].reshape(MOD_ROWS, 1, 6 * D_MODEL)
        last = l == DEPTH - 1
        filt_p = _hyena_filter(lp, *lw['hy_mlp'])
        filt_s = filt_p if ls == lp else _hyena_filter(ls, *lw['hy_mlp'])
        xp, st = _trunk_layer(xp, bp, lp, mod3, ctx_row, zero_st, lw, moe_w, filt_p, lp, True, final_g, last)
        per_layer.append(st)
        st_in = dict(
            hgrn=state_hgrn[:, l].astype(F32),
            ml_c=state_mlstm_c[:, l].astype(F32),
            ml_n=state_mlstm_n[:, l].astype(F32).reshape(bs, 2, 1, GROUP_W),
            ml_m=jnp.repeat(state_mlstm_m[:, l].astype(F32), HEAD_D, axis=-1).reshape(bs, 2, 1, GROUP_W),
            s5_re=state_s5_re[:, l].astype(F32).reshape(bs, 2, S5_STATE),
            s5_im=state_s5_im[:, l].astype(F32).reshape(bs, 2, S5_STATE))
        xs, _ = _trunk_layer(xs, bs, ls, mod3, None, st_in, lw, moe_w, filt_s, GRID_W, False, final_g, last)

    new_hgrn = jnp.stack([s['hgrn'] for s in per_layer], axis=1)
    new_ml_c = jnp.stack([s['ml_c'] for s in per_layer], axis=1)
    new_ml_n = jnp.stack([s['ml_n'].reshape(bp, 2, HEADS, HEAD_D) for s in per_layer], axis=1)
    new_ml_m = jnp.stack([s['ml_m'][:, :, 0, ::HEAD_D] for s in per_layer], axis=1)
    new_s5_re = jnp.stack([s['s5_re'].reshape(bp, 2, S5_GROUPS, S5_P) for s in per_layer], axis=1)
    new_s5_im = jnp.stack([s['s5_im'].reshape(bp, 2, S5_GROUPS, S5_P) for s in per_layer], axis=1)
    return (xp.reshape(bp, lp, D_MODEL), xs.reshape(bs, ls, D_MODEL),
            new_hgrn, new_ml_c, new_ml_n, new_ml_m, new_s5_re, new_s5_im)
```

```python
import functools
import math

import numpy as np
import jax
import jax.numpy as jnp
from jax import lax
from jax.experimental import pallas as pl
from jax.experimental.pallas import tpu as pltpu

F32 = jnp.float32
BF16 = jnp.bfloat16

D_MODEL = 1024
DEPTH = 2
GRID_W = 64
GROUP_W = D_MODEL // 4
HEADS = 4
HEAD_D = GROUP_W // HEADS
S5_CH = 16
S5_GROUPS = GROUP_W // S5_CH
S5_P = 64
S5_STATE = S5_GROUPS * S5_P
HY_ORDER = 2
HY_BANDS = 16
HY_EMB = 1 + 2 * HY_BANDS
HY_HIDDEN = 64
HY_SHIFT = 0.05
N_EXPERT_GROUPS = 4
EXPERTS_PER_GROUP = 4
N_EXPERTS = N_EXPERT_GROUPS * EXPERTS_PER_GROUP
D_EXPERT = D_MODEL // 2
EPS = 1e-6
N_GATES = 4 * HEADS

LANES = 128
MOD_ROWS = 16
TOK_TILE = 256
IN_TILE = 512
MOE_TOK = TOK_TILE
MOE_PIECE = 64
MOE_ROWS = 512
ROW_ALIGN = 16
HG_BLK = 32
HG_SB = 128
HG_SAFE_LOG = -75.0
HG_SAFE_Q = 1e3
ML_CHUNK = 64
S5_TC = 64
HY_BLK = 256
NEG_INF = float("-inf")


def _dot(a, b):
    return jnp.dot(a, b, preferred_element_type=F32)


def _dot_nt(a, b):
    return lax.dot_general(a, b, (((1,), (1,)), ((), ())), preferred_element_type=F32)


def _dot_tn(a, b):
    return lax.dot_general(a, b, (((0,), (0,)), ((), ())), preferred_element_type=F32)


def _split3(x):
    hi = x.astype(BF16)
    r1 = x - hi.astype(F32)
    mid = r1.astype(BF16)
    lo = (r1 - mid.astype(F32)).astype(BF16)
    return hi, mid, lo


def _dot_sel_l(m, x):
    hi, mid, lo = _split3(x)
    return _dot(m, hi) + _dot(m, mid) + _dot(m, lo)


def _dot_sel_r(x, m):
    hi, mid, lo = _split3(x)
    return _dot(hi, m) + _dot(mid, m) + _dot(lo, m)


def _dot3(a, b):
    ah = a.astype(BF16)
    al = (a - ah.astype(F32)).astype(BF16)
    bh = b.astype(BF16)
    bl = (b - bh.astype(F32)).astype(BF16)
    return _dot(ah, bh) + _dot(al, bh) + _dot(ah, bl)


def _sigmoid(x):
    return 1.0 / (1.0 + jnp.exp(-x))


def _silu(x):
    return x * _sigmoid(x)


def _log_sigmoid(x):
    return jnp.minimum(x, 0.0) - jnp.log(1.0 + jnp.exp(-jnp.abs(x)))


def _rmsnorm_rows(x, g):
    return x * lax.rsqrt(jnp.mean(x * x, axis=-1, keepdims=True) + EPS) * g


def _ada_kernel(c_ref, w_ref, b_ref, o_ref):
    s = _silu(c_ref[...]).astype(BF16)
    o_ref[0] = _dot(s, w_ref[0].astype(BF16)) + b_ref[0]


def _ada_mod(cond, w_ada, b_ada):
    tn = 1536
    n = w_ada.shape[-1]
    return pl.pallas_call(
        _ada_kernel,
        grid=(DEPTH, n // tn),
        in_specs=[
            pl.BlockSpec((MOD_ROWS, D_MODEL), lambda l, j: (0, 0)),
            pl.BlockSpec((1, D_MODEL, tn), lambda l, j: (l, 0, j)),
            pl.BlockSpec((1, 1, tn), lambda l, j: (l, 0, j)),
        ],
        out_specs=pl.BlockSpec((1, MOD_ROWS, tn), lambda l, j: (l, 0, j)),
        out_shape=jax.ShapeDtypeStruct((DEPTH, MOD_ROWS, n), F32),
        name="ada_mod",
    )(cond, w_ada, b_ada.reshape(DEPTH, 1, n))


def _inproj_kernel(x_ref, sh_ref, sc_ref, g_ref, wa_ref, wb_ref, wg_ref, wgt_ref,
                   hg_ref, ml_ref, su_ref, hy_ref, gt_ref, gtt_ref):
    h = _rmsnorm_rows(x_ref[...], g_ref[...])
    h = h * (1.0 + sc_ref[0]) + sh_ref[0]
    hb = h.astype(BF16)
    hg_ref[...] = _dot(hb, wa_ref[:, 0:5 * GROUP_W])
    ml_ref[...] = _dot(hb, wa_ref[:, 5 * GROUP_W:9 * GROUP_W])
    su_ref[...] = _dot(hb, wb_ref[:, 0:GROUP_W])
    hy_ref[...] = _dot(hb, wb_ref[:, GROUP_W:4 * GROUP_W])
    gt_ref[...] = _dot(hb, wg_ref[...])
    gtt_ref[...] = _dot_nt(wgt_ref[...], hb)


def _mod_spec(k, row_of_tile):
    return pl.BlockSpec((1, 1, D_MODEL), lambda i: (row_of_tile(i), 0, k))


def _inproj(x2d, mod3, row_of_tile, norm_g, w_a, w_b, w_g, w_gt):
    t = x2d.shape[0]
    tm = IN_TILE
    const = lambda i: (0, 0)
    tile = lambda i: (i, 0)
    widths = (5 * GROUP_W, 4 * GROUP_W, GROUP_W, 3 * GROUP_W)
    return pl.pallas_call(
        _inproj_kernel,
        grid=(t // tm,),
        in_specs=[
            pl.BlockSpec((tm, D_MODEL), tile),
            _mod_spec(0, row_of_tile),
            _mod_spec(1, row_of_tile),
            pl.BlockSpec((1, D_MODEL), const),
            pl.BlockSpec(w_a.shape, const),
            pl.BlockSpec(w_b.shape, const),
            pl.BlockSpec(w_g.shape, const),
            pl.BlockSpec(w_gt.shape, const),
        ],
        out_specs=[pl.BlockSpec((tm, w), tile) for w in widths]
        + [pl.BlockSpec((tm, LANES), tile), pl.BlockSpec((N_GATES, tm), lambda i: (0, i))],
        out_shape=[jax.ShapeDtypeStruct((t, w), F32) for w in widths]
        + [jax.ShapeDtypeStruct((t, LANES), F32), jax.ShapeDtypeStruct((N_GATES, t), F32)],
        name="inproj",
    )(x2d, mod3, mod3, norm_g, w_a, w_b, w_g, w_gt)


@functools.lru_cache(maxsize=None)
def _hgrn_consts():
    r = np.arange(HG_SB)
    same = (r[:, None] // HG_BLK) == (r[None, :] // HG_BLK)
    tri_f = same & (r[:, None] >= r[None, :])
    tri_b = same & (r[:, None] <= r[None, :])
    c = np.arange(GROUP_W)
    head = (c[:, None] // HEAD_D) == (c[None, :] // HEAD_D)
    rr = np.arange(HEADS * HG_SB)
    head4 = (rr[:, None] // HG_SB) == (c[None, :] // HEAD_D)
    pairs = np.stack([np.tile(tri_f, (1, HEADS)), np.tile(tri_b, (1, HEADS))])
    return (np.stack([tri_f, tri_b]).astype(np.float32), same.astype(np.float32),
            head.astype(np.float32), head4.astype(np.float32), pairs.astype(np.float32))


def _hgrn_kernel(p_ref, s0_ref, lb_ref, ng_ref, tri_ref, bones_ref, e_ref, hmask_ref, hmask4_ref,
                 smask_ref, out_ref, *rest, seq, emit_state):
    if emit_state:
        st_ref = rest[0]
        rest = rest[1:]
    o_scr, st_scr, q_scr, k_scr, bc_scr, gq_scr, gk_scr, ks_scr, dec_scr, oi_scr = rest
    nsb = seq // HG_SB
    nblk = HG_SB // HG_BLK
    log_lb = lb_ref[0:1, :]
    log_1mlb = lb_ref[1:2, :]
    one_m_lb = lb_ref[2:3, :]
    row = lax.broadcasted_iota(jnp.int32, (HG_BLK, 1), 0)

    def stage(r0, d):
        rows = pl.ds(r0, HG_SB)
        pq = p_ref[rows, 0:GROUP_W]
        pf = p_ref[rows, (1 + d) * GROUP_W:(2 + d) * GROUP_W]
        q = _silu(pq)
        e = jnp.exp(-jnp.abs(pf))
        b2 = log_1mlb + jnp.minimum(pf, 0.0) - jnp.log(1.0 + e)
        mx = jnp.maximum(log_lb, b2)
        mn = jnp.minimum(log_lb, b2)
        lf = mx + jnp.log(1.0 + jnp.exp(mn - mx))
        k = one_m_lb * jnp.where(pf >= 0.0, e, 1.0) / (1.0 + e)
        hi, mid, lo = _split3(lf)
        bc = _dot(tri_ref[d], hi) + _dot(tri_ref[d], mid) + _dot(tri_ref[d], lo)
        bt = _dot(bones_ref[...], hi) + _dot(bones_ref[...], mid) + _dot(bones_ref[...], lo)
        gq_scr[d] = (q * jnp.exp(bc)).astype(BF16)
        gk_scr[d] = (k * jnp.exp(bt - bc)).astype(BF16)
        dec_scr[d] = jnp.exp(bt)
        ks_scr[d] = (k * jnp.exp(-bc)).astype(BF16)
        q_scr[d] = q
        k_scr[d] = k
        bc_scr[d] = bc
        return (jnp.min(bc) >= HG_SAFE_LOG) & (jnp.max(jnp.abs(q)) <= HG_SAFE_Q)

    def block_edge_path(r0, d):
        vb = p_ref[pl.ds(r0, HG_SB), 3 * GROUP_W:4 * GROUP_W].astype(BF16)
        ksbd = jnp.concatenate([ks_scr[d]] * HEADS, axis=0) * hmask4_ref[...]
        vbd = jnp.concatenate([vb] * HEADS, axis=0) * hmask4_ref[...]
        s = _dot_nt(gq_scr[d], ksbd) * smask_ref[d]
        oi_scr[d] = _dot(s.astype(BF16), vbd)

    def exact_path(r0, d):
        for j in range(nblk):
            sl = slice(HG_BLK * j, HG_BLK * (j + 1))
            qj = q_scr[d, sl, :]
            kj = k_scr[d, sl, :]
            vj = p_ref[pl.ds(r0 + HG_BLK * j, HG_BLK), 3 * GROUP_W:4 * GROUP_W]
            bcj = bc_scr[d, sl, :]
            parts = []
            for s in range(HG_BLK):
                keep = (row >= s) if d == 0 else (row <= s)
                dd = jnp.exp(jnp.where(keep, bcj - bcj[s:s + 1, :], NEG_INF))
                parts.append((qj * dd * kj[s:s + 1, :]).astype(BF16))
            r_all = _dot(jnp.concatenate(parts, axis=0), e_ref[...])
            o = r_all[0:HG_BLK, :] * vj[0:1, :]
            for s in range(1, HG_BLK):
                o = o + r_all[HG_BLK * s:HG_BLK * (s + 1), :] * vj[s:s + 1, :]
            oi_scr[d, sl, :] = o

    def chain(r0, d):
        order = range(nblk) if d == 0 else range(nblk - 1, -1, -1)
        for j in order:
            sl = slice(HG_BLK * j, HG_BLK * (j + 1))
            vb = p_ref[pl.ds(r0 + HG_BLK * j, HG_BLK), 3 * GROUP_W:4 * GROUP_W].astype(BF16)
            upd = _dot_tn(vb, gk_scr[d, sl, :]) * hmask_ref[...]
            st = st_scr[d]
            o_scr[d, pl.ds(r0 + HG_BLK * j, HG_BLK), :] = oi_scr[d, sl, :] + _dot_nt(gq_scr[d, sl, :], st.astype(BF16))
            st_scr[d] = st * dec_scr[d, HG_BLK * j:HG_BLK * j + 1, :] + upd

    eye_h = jnp.where(lax.broadcasted_iota(jnp.int32, (HEAD_D, HEAD_D), 0)
                      == lax.broadcasted_iota(jnp.int32, (HEAD_D, HEAD_D), 1), 1.0, 0.0).astype(BF16)
    st_scr[...] = jnp.zeros_like(st_scr)
    for d in range(2):
        for h in range(HEADS):
            hi, mid, lo = _split3(s0_ref[d, h])
            lanes = slice(HEAD_D * h, HEAD_D * (h + 1))
            st_scr[d, lanes, lanes] = _dot_tn(hi, eye_h) + _dot_tn(mid, eye_h) + _dot_tn(lo, eye_h)

    def body(i, carry):
        rf = pl.multiple_of(i * HG_SB, HG_SB)
        rb = pl.multiple_of((nsb - 1 - i) * HG_SB, HG_SB)
        safe = stage(rf, 0) & stage(rb, 1)

        def both_block_edge():
            block_edge_path(rf, 0)
            block_edge_path(rb, 1)

        def both_exact():
            exact_path(rf, 0)
            exact_path(rb, 1)

        lax.cond(safe, both_block_edge, both_exact)
        chain(rf, 0)
        chain(rb, 1)
        return carry

    lax.fori_loop(0, nsb, body, 0)

    def finish(i, carry):
        rows = pl.ds(pl.multiple_of(i * HG_SB, HG_SB), HG_SB)
        o = o_scr[0, rows, :] + o_scr[1, rows, :]
        ms = _dot_hilo(o * o, e_ref[...]) * (1.0 / HEAD_D)
        pg = p_ref[rows, 4 * GROUP_W:5 * GROUP_W]
        out_ref[rows, :] = (o * lax.rsqrt(ms + EPS) * ng_ref[...] * _silu(pg)).astype(out_ref.dtype)
        return carry

    lax.fori_loop(0, nsb, finish, 0, unroll=2)
    if emit_state:
        ri = lax.broadcasted_iota(jnp.int32, (GROUP_W, GROUP_W), 0)
        ci = lax.broadcasted_iota(jnp.int32, (GROUP_W, GROUP_W), 1)
        eye = jnp.where(ri == ci, 1.0, 0.0).astype(BF16)
        for d in range(2):
            hi, mid, lo = _split3(st_scr[d])
            s_t = _dot_tn(hi, eye) + _dot_tn(mid, eye) + _dot_tn(lo, eye)
            for h in range(HEADS):
                st_ref[d, h] = s_t[HEAD_D * h:HEAD_D * (h + 1), HEAD_D * h:HEAD_D * (h + 1)]


def _hgrn(p3, s0_bd, lb_rows, norm_g, emit_state):
    b, seq, _ = p3.shape
    tri, bones, head, head4, pairs = _hgrn_consts()
    const2 = lambda i: (0, 0)
    const3 = lambda i: (0, 0, 0)
    out_shape = [jax.ShapeDtypeStruct((b, seq, GROUP_W), BF16)]
    out_specs = [pl.BlockSpec((None, seq, GROUP_W), lambda i: (i, 0, 0))]
    if emit_state:
        out_shape.append(jax.ShapeDtypeStruct((b, 2, HEADS, HEAD_D, HEAD_D), F32))
        out_specs.append(pl.BlockSpec((None, 2, HEADS, HEAD_D, HEAD_D), lambda i: (i, 0, 0, 0, 0)))
    sb_f32 = pltpu.VMEM((2, HG_SB, GROUP_W), F32)
    sb_bf16 = pltpu.VMEM((2, HG_SB, GROUP_W), BF16)
    res = pl.pallas_call(
        functools.partial(_hgrn_kernel, seq=seq, emit_state=emit_state),
        grid=(b,),
        in_specs=[
            pl.BlockSpec((None, seq, 5 * GROUP_W), lambda i: (i, 0, 0)),
            pl.BlockSpec((None, 2, HEADS, HEAD_D, HEAD_D), lambda i: (i, 0, 0, 0, 0)),
            pl.BlockSpec((8, GROUP_W), const2),
            pl.BlockSpec((1, GROUP_W), const2),
            pl.BlockSpec((2, HG_SB, HG_SB), const3),
            pl.BlockSpec((HG_SB, HG_SB), const2),
            pl.BlockSpec((GROUP_W, GROUP_W), const2),
            pl.BlockSpec((GROUP_W, GROUP_W), const2),
            pl.BlockSpec((HEADS * HG_SB, GROUP_W), const2),
            pl.BlockSpec((2, HG_SB, HEADS * HG_SB), const3),
        ],
        out_specs=out_specs,
        out_shape=out_shape,
        scratch_shapes=[
            pltpu.VMEM((2, seq, GROUP_W), F32),
            pltpu.VMEM((2, GROUP_W, GROUP_W), F32),
            sb_f32, sb_f32, sb_f32, sb_bf16, sb_bf16, sb_bf16, sb_f32, sb_f32,
        ],
        name="hgrn2",
    )(p3, s0_bd, lb_rows, norm_g, jnp.asarray(tri, BF16), jnp.asarray(bones, BF16),
      jnp.asarray(head, BF16), jnp.asarray(head, F32), jnp.asarray(head4, BF16), jnp.asarray(pairs, F32))
    return res if emit_state else (res[0], None)


@functools.lru_cache(maxsize=None)
def _mlstm_consts():
    r = np.arange(ML_CHUNK)
    tri_f = (r[:, None] >= r[None, :]).astype(np.float32)
    tri = np.stack([tri_f, tri_f.T])
    c = np.arange(GROUP_W)
    head = (c[:, None] // HEAD_D) == (c[None, :] // HEAD_D)
    pos = c % HEAD_D
    row_f = head & (pos[:, None] <= pos[None, :])
    row_b = head & (pos[:, None] >= pos[None, :])
    rowtri = np.stack([row_f, row_b]).astype(np.float32)
    expand = np.zeros((2, LANES, GROUP_W), np.float32)
    for d in range(2):
        for h in range(HEADS):
            expand[d, 2 * HEADS + d * HEADS + h, h * HEAD_D:(h + 1) * HEAD_D] = 1.0
    return tri, rowtri, expand, head.astype(np.float32)


def _dot_hilo(x, m):
    hi = x.astype(BF16)
    lo = (x - hi.astype(F32)).astype(BF16)
    return _dot(hi, m) + _dot(lo, m)


def _mlstm_kernel(p_ref, g_ref, gr_ref, conv_ref, brow_ref, brows_ref, ng_ref, c0_ref, n0_ref, m0_ref,
                  tri_ref, rowtri_ref, exp_ref, e_ref, out_ref, *rest, seq, conv_w, emit_state):
    if emit_state:
        cst_ref, nst_ref, mst_ref = rest[:3]
        rest = rest[3:]
    (q_scr, k_scr, col_scr, mloc_scr, gb_scr, h_scr, row_scr, blast_scr, gmax_scr,
     c_scr, n_scr, m_scr) = rest
    nch = seq // ML_CHUNK
    rowid = lax.broadcasted_iota(jnp.int32, (ML_CHUNK, 1), 0)
    ti = lax.broadcasted_iota(jnp.int32, (ML_CHUNK, GROUP_W), 0)
    si = lax.broadcasted_iota(jnp.int32, (ML_CHUNK, GROUP_W), 1) % HEAD_D

    for d in range(2):
        li_row = gr_ref[d] + brows_ref[d:d + 1, :]
        lf_row = _log_sigmoid(gr_ref[2 + d] + brows_ref[2 + d:3 + d, :])
        row_scr[d] = li_row - _dot_sel_r(lf_row, rowtri_ref[d])

    def pre_chunk(ci, carry):
        r0 = pl.multiple_of(ci * ML_CHUNK, ML_CHUNK)
        rows = pl.ds(r0, ML_CHUNK)
        zc = p_ref[rows, 0:2 * GROUP_W]
        zp = pltpu.roll(zc, 1, axis=0)
        zn = pltpu.roll(zc, ML_CHUNK - 1, axis=0)
        if conv_w == ML_CHUNK:
            prev_row = jnp.zeros((1, 2 * GROUP_W), F32)
            next_row = prev_row
        else:
            prev_row = p_ref[pl.ds(jnp.maximum(r0 - 1, 0), 1), 0:2 * GROUP_W]
            prev_row = jnp.where(r0 % conv_w == 0, 0.0, prev_row)
            next_row = p_ref[pl.ds(jnp.minimum(r0 + ML_CHUNK, seq - 1), 1), 0:2 * GROUP_W]
            next_row = jnp.where((r0 + ML_CHUNK) % conv_w == 0, 0.0, next_row)
        zp = jnp.where(rowid == 0, prev_row, zp)
        zn = jnp.where(rowid == ML_CHUNK - 1, next_row, zn)
        qk = _silu(conv_ref[0:1, :] * zp + conv_ref[1:2, :] * zc + conv_ref[2:3, :] * zn)
        q_scr[rows, :] = qk[:, 0:GROUP_W].astype(BF16)
        k_scr[rows, :] = qk[:, GROUP_W:2 * GROUP_W] * (HEAD_D ** -0.5)
        ge = g_ref[rows, :] + brow_ref[...]
        lane = lax.broadcasted_iota(jnp.int32, (1, LANES), 1)
        bwd = lane >= 3 * HEADS
        lf = _log_sigmoid(ge)
        bc = jnp.where(bwd, _dot_sel_l(tri_ref[1], lf), _dot_sel_l(tri_ref[0], lf))
        li = pltpu.roll(ge, 2 * HEADS, axis=1)
        pre = li - bc
        suf = pre
        for sh in (1, 2, 4, 8, 16, 32):
            pre = jnp.maximum(pre, jnp.where(rowid >= sh, pltpu.roll(pre, sh, axis=0), NEG_INF))
            suf = jnp.maximum(suf, jnp.where(rowid < ML_CHUNK - sh, pltpu.roll(suf, ML_CHUNK - sh, axis=0), NEG_INF))
        blast = jnp.where(bwd, bc[0:1, :], bc[ML_CHUNK - 1:ML_CHUNK, :])
        gb = blast - bc + li
        edge_rows = jnp.concatenate([blast, jnp.max(gb, axis=0, keepdims=True), jnp.zeros((6, LANES), F32)], axis=0)
        stacked = jnp.concatenate([bc, bc + jnp.where(bwd, suf, pre), gb, edge_rows], axis=0)
        for d in range(2):
            ex = _dot_sel_r(stacked, exp_ref[d])
            col_scr[d, rows, :] = ex[0:ML_CHUNK, :]
            mloc_scr[d, rows, :] = ex[ML_CHUNK:2 * ML_CHUNK, :]
            gb_scr[d, rows, :] = ex[2 * ML_CHUNK:3 * ML_CHUNK, :]
            blast_scr[d, pl.ds(ci, 1), :] = ex[3 * ML_CHUNK:3 * ML_CHUNK + 1, :]
            gmax_scr[d, pl.ds(ci, 1), :] = ex[3 * ML_CHUNK + 1:3 * ML_CHUNK + 2, :]
        return carry

    lax.fori_loop(0, nch, pre_chunk, 0, unroll=2)

    for d in range(2):
        c_scr[d] = jnp.zeros((GROUP_W, GROUP_W), F32)
        for h in range(HEADS):
            lo = HEAD_D * h
            c_scr[d, lo:lo + HEAD_D, lo:lo + HEAD_D] = c0_ref[d, h]
    n_scr[...] = n0_ref[...]
    m_scr[...] = m0_ref[...]

    def chunk(ci, d):
        r0 = pl.multiple_of(ci * ML_CHUNK, ML_CHUNK)
        rows = pl.ds(r0, ML_CHUNK)
        qb = q_scr[rows, :]
        kc = k_scr[rows, :]
        vb = p_ref[rows, 2 * GROUP_W:3 * GROUP_W].astype(BF16)
        hm = e_ref[...]
        kbd = jnp.concatenate([kc.astype(BF16)] * HEADS, axis=0) * hm
        vbd = jnp.concatenate([vb] * HEADS, axis=0) * hm
        s = _dot_nt(qb, kbd)
        colb = col_scr[d, rows, :]
        m_b = m_scr[d]
        prev = colb + m_b
        m_t = jnp.maximum(prev, mloc_scr[d, rows, :])
        keep = (ti >= si) if d == 0 else (ti <= si)
        w = jnp.exp(jnp.where(keep, colb + row_scr[d, pl.ds(ci, 1), :], NEG_INF) - m_t) * s
        wp = jnp.exp(prev - m_t)
        cst = c_scr[d]
        nrow = n_scr[d]
        num = wp * _dot(qb, cst.astype(BF16)) + _dot(w.astype(BF16), vbd)
        den = _dot_hilo(wp * qb.astype(F32) * nrow + w, e_ref[...])
        h_scr[d, rows, :] = num / jnp.maximum(jnp.abs(den), jnp.exp(-m_t))
        blast = blast_scr[d, pl.ds(ci, 1), :]
        m_new = jnp.maximum(blast + m_b, gmax_scr[d, pl.ds(ci, 1), :])
        dec = jnp.exp(blast + m_b - m_new)
        kw = kc * jnp.exp(gb_scr[d, rows, :] - m_new)
        c_scr[d] = cst * dec + _dot_tn(kw.astype(BF16), vb) * hm.astype(F32)
        n_scr[d] = nrow * dec + jnp.sum(kw, axis=0, keepdims=True)
        m_scr[d] = m_new

    def body(i, carry):
        chunk(i, 0)
        chunk(nch - 1 - i, 1)
        return carry

    lax.fori_loop(0, nch, body, 0, unroll=2)

    def fin_chunk(ci, carry):
        rows = pl.ds(pl.multiple_of(ci * ML_CHUNK, ML_CHUNK), ML_CHUNK)
        hs = h_scr[0, rows, :] + h_scr[1, rows, :]
        ms = _dot_hilo(hs * hs, e_ref[...]) * (1.0 / HEAD_D)
        po = p_ref[rows, 3 * GROUP_W:4 * GROUP_W]
        out_ref[rows, :] = (hs * lax.rsqrt(ms + EPS) * ng_ref[...] * _sigmoid(po)).astype(out_ref.dtype)
        return carry

    lax.fori_loop(0, nch, fin_chunk, 0, unroll=4)
    if emit_state:
        for d in range(2):
            for h in range(HEADS):
                lo = HEAD_D * h
                cst_ref[d, h] = c_scr[d, lo:lo + HEAD_D, lo:lo + HEAD_D]
        nst_ref[...] = n_scr[...]
        mst_ref[...] = m_scr[...]


def _mlstm(p3, g3, gr4, conv_w3, brow, brows, norm_g, c0, n0, m0, conv_width, emit_state):
    b, seq, _ = p3.shape
    nch = seq // ML_CHUNK
    tri, rowtri, expand, head = _mlstm_consts()
    const2 = lambda i: (0, 0)
    const3 = lambda i: (0, 0, 0)
    out_shape = [jax.ShapeDtypeStruct((b, seq, GROUP_W), BF16)]
    out_specs = [pl.BlockSpec((None, seq, GROUP_W), lambda i: (i, 0, 0))]
    c_spec = pl.BlockSpec((None, 2, HEADS, HEAD_D, HEAD_D), lambda i: (i, 0, 0, 0, 0))
    row_spec = pl.BlockSpec((None, 2, 1, GROUP_W), lambda i: (i, 0, 0, 0))
    if emit_state:
        out_shape += [jax.ShapeDtypeStruct((b, 2, HEADS, HEAD_D, HEAD_D), F32),
                      jax.ShapeDtypeStruct((b, 2, 1, GROUP_W), F32), jax.ShapeDtypeStruct((b, 2, 1, GROUP_W), F32)]
        out_specs += [c_spec, row_spec, row_spec]
    dir_seq = (2, seq, GROUP_W)
    dir_chunks = (2, nch, GROUP_W)
    res = pl.pallas_call(
        functools.partial(_mlstm_kernel, seq=seq, conv_w=conv_width, emit_state=emit_state),
        grid=(b,),
        in_specs=[
            pl.BlockSpec((None, seq, 4 * GROUP_W), lambda i: (i, 0, 0)),
            pl.BlockSpec((None, seq, LANES), lambda i: (i, 0, 0)),
            pl.BlockSpec((None, 4, nch, GROUP_W), lambda i: (i, 0, 0, 0)),
            pl.BlockSpec((3, 2 * GROUP_W), const2),
            pl.BlockSpec((1, LANES), const2),
            pl.BlockSpec((4, GROUP_W), const2),
            pl.BlockSpec((1, GROUP_W), const2),
            c_spec, row_spec, row_spec,
            pl.BlockSpec((2, ML_CHUNK, ML_CHUNK), const3),
            pl.BlockSpec((2, GROUP_W, GROUP_W), const3),
            pl.BlockSpec((2, LANES, GROUP_W), const3),
            pl.BlockSpec((GROUP_W, GROUP_W), const2),
        ],
        out_specs=out_specs,
        out_shape=out_shape,
        scratch_shapes=[
            pltpu.VMEM((seq, GROUP_W), BF16),
            pltpu.VMEM((seq, GROUP_W), F32),
            pltpu.VMEM(dir_seq, F32), pltpu.VMEM(dir_seq, F32), pltpu.VMEM(dir_seq, F32), pltpu.VMEM(dir_seq, F32),
            pltpu.VMEM(dir_chunks, F32), pltpu.VMEM(dir_chunks, F32), pltpu.VMEM(dir_chunks, F32),
            pltpu.VMEM((2, GROUP_W, GROUP_W), F32),
            pltpu.VMEM((2, 1, GROUP_W), F32),
            pltpu.VMEM((2, 1, GROUP_W), F32),
        ],
        name="mlstm",
    )(p3, g3, gr4, conv_w3, brow, brows, norm_g, c0, n0, m0, jnp.asarray(tri, BF16), jnp.asarray(rowtri, BF16),
      jnp.asarray(expand, BF16), jnp.asarray(head, BF16))
    if emit_state:
        return res
    return res[0], None, None, None


def _s5_kernel(*refs, nb, reverse, final):
    if final:
        (u_ref, hre0_ref, him0_ref, are_ref, aim_ref, wb_ref, wc_ref, yin_ref, d_ref, gw_ref, gb_ref,
         y_ref, sre_ref, sim_ref, utm_scr, bu_scr, hb_scr, ytm_scr, hre_scr, him_scr) = refs
    else:
        (u_ref, hre0_ref, him0_ref, are_ref, aim_ref, wb_ref, wc_ref,
         y_ref, sre_ref, sim_ref, utm_scr, bu_scr, hb_scr, ytm_scr, hre_scr, him_scr) = refs
    tc = S5_TC
    step_id = pl.program_id(0)

    @pl.when(step_id == 0)
    def _():
        hre_scr[...] = hre0_ref[...]
        him_scr[...] = him0_ref[...]

    halves = GROUP_W // LANES
    for b in range(nb):
        for hv in range(halves):
            utm_scr[hv, pl.ds(b, tc, stride=nb), :] = u_ref[b, :, hv * LANES:(hv + 1) * LANES]
    u_tm = jnp.concatenate([utm_scr[hv] for hv in range(halves)], axis=1)
    bu_scr[...] = _dot(u_tm.astype(BF16), wb_ref[...])
    are = jnp.broadcast_to(are_ref[...], (nb, S5_STATE))
    aim = jnp.broadcast_to(aim_ref[...], (nb, S5_STATE))

    spi = max(ROW_ALIGN // nb, 1)
    rows_it = spi * nb

    def step(i, carry):
        hre, him = carry
        r0 = pl.multiple_of(((tc // spi - 1 - i) if reverse else i) * rows_it, rows_it)
        res = [None] * spi
        ims = [None] * spi
        for s in (range(spi - 1, -1, -1) if reverse else range(spi)):
            bu = bu_scr[pl.ds(r0 + s * nb, nb), :]
            hre, him = (are * hre - aim * him + bu[:, 0:S5_STATE],
                        are * him + aim * hre + bu[:, S5_STATE:2 * S5_STATE])
            res[s], ims[s] = hre, him
        hb_scr[pl.ds(r0, rows_it), 0:S5_STATE] = jnp.concatenate(res, axis=0).astype(BF16)
        hb_scr[pl.ds(r0, rows_it), S5_STATE:2 * S5_STATE] = jnp.concatenate(ims, axis=0).astype(BF16)
        return hre, him

    hre, him = lax.fori_loop(0, tc // spi, step, (hre_scr[...], him_scr[...]))
    hre_scr[...] = hre
    him_scr[...] = him
    y_tm = _dot(hb_scr[...], wc_ref[...])
    for hv in range(halves):
        ytm_scr[hv] = y_tm[:, hv * LANES:(hv + 1) * LANES]
    for b in range(nb):
        yb = jnp.concatenate([ytm_scr[hv, pl.ds(b, tc, stride=nb), :] for hv in range(halves)], axis=1)
        if final:
            y = yb + yin_ref[b] + d_ref[...] * u_ref[b]
            z = 0.5 * y * (1.0 + jnp.tanh(math.sqrt(2.0 / math.pi) * (y + 0.044715 * (y * y * y))))
            gate = _sigmoid(_dot(z.astype(BF16), gw_ref[...]) + gb_ref[...])
            y_ref[b] = (z * gate).astype(y_ref.dtype)
        else:
            y_ref[b] = yb

    @pl.when(step_id == pl.num_programs(0) - 1)
    def _():
        sre_ref[...] = hre
        sim_ref[...] = him


def _s5_pass(u3, hre0, him0, are, aim, wb, wc, reverse, extra):
    nb, seq, _ = u3.shape
    nch = seq // S5_TC
    final = extra is not None
    chunk = (lambda i: (0, nch - 1 - i, 0)) if reverse else (lambda i: (0, i, 0))
    const2 = lambda i: (0, 0)
    in_specs = [
        pl.BlockSpec((nb, S5_TC, GROUP_W), chunk),
        pl.BlockSpec((nb, S5_STATE), const2),
        pl.BlockSpec((nb, S5_STATE), const2),
        pl.BlockSpec((1, S5_STATE), const2),
        pl.BlockSpec((1, S5_STATE), const2),
        pl.BlockSpec((GROUP_W, 2 * S5_STATE), const2),
        pl.BlockSpec((2 * S5_STATE, GROUP_W), const2),
    ]
    args = [u3, hre0, him0, are, aim, wb, wc]
    if final:
        yin, dvec, gw, gb = extra
        in_specs += [pl.BlockSpec((nb, S5_TC, GROUP_W), chunk), pl.BlockSpec((1, GROUP_W), const2),
                     pl.BlockSpec((GROUP_W, GROUP_W), const2), pl.BlockSpec((1, GROUP_W), const2)]
        args += [yin, dvec, gw, gb]
    return pl.pallas_call(
        functools.partial(_s5_kernel, nb=nb, reverse=reverse, final=final),
        grid=(nch,),
        in_specs=in_specs,
        out_specs=[pl.BlockSpec((nb, S5_TC, GROUP_W), chunk),
                   pl.BlockSpec((nb, S5_STATE), const2), pl.BlockSpec((nb, S5_STATE), const2)],
        out_shape=[jax.ShapeDtypeStruct((nb, seq, GROUP_W), BF16 if final else F32),
                   jax.ShapeDtypeStruct((nb, S5_STATE), F32), jax.ShapeDtypeStruct((nb, S5_STATE), F32)],
        scratch_shapes=[
            pltpu.VMEM((GROUP_W // LANES, nb * S5_TC, LANES), F32),
            pltpu.VMEM((nb * S5_TC, 2 * S5_STATE), F32),
            pltpu.VMEM((nb * S5_TC, 2 * S5_STATE), BF16),
            pltpu.VMEM((GROUP_W // LANES, nb * S5_TC, LANES), F32),
            pltpu.VMEM((nb, S5_STATE), F32),
            pltpu.VMEM((nb, S5_STATE), F32),
        ],
        name="s5_bwd" if reverse else "s5_fwd",
    )(*args)


def _s5_params(a_re, a_im, log_dt, b_re, b_im, c_re, c_im):
    eye = jnp.eye(S5_GROUPS, dtype=F32)
    dt = jnp.exp(log_dt)[..., None]
    mag = jnp.exp(a_re * dt)
    ab_re, ab_im = mag * jnp.cos(a_im * dt), mag * jnp.sin(a_im * dt)
    den = a_re * a_re + a_im * a_im
    g_re = ((ab_re - 1.0) * a_re + ab_im * a_im) / den
    g_im = (ab_im * a_re - (ab_re - 1.0) * a_im) / den
    bb_re = g_re[..., None] * b_re - g_im[..., None] * b_im
    bb_im = g_re[..., None] * b_im + g_im[..., None] * b_re

    def in_mat(bb):
        return jnp.einsum('dgpc,gh->dgchp', bb, eye).reshape(2, GROUP_W, S5_STATE)

    wb = jnp.concatenate([in_mat(bb_re), in_mat(bb_im)], axis=-1).astype(BF16)

    def out_mat(cc):
        return jnp.einsum('gcp,gh->gphc', cc, eye).reshape(S5_STATE, GROUP_W)

    wc = jnp.concatenate([out_mat(c_re), -out_mat(c_im)], axis=0).astype(BF16)
    return ab_re.reshape(2, 1, S5_STATE), ab_im.reshape(2, 1, S5_STATE), wb, wc


def _s5(u3, h0_re, h0_im, prm, d_vec, glu_w, glu_b):
    ab_re, ab_im, wb, wc = prm
    yf, fre, fim = _s5_pass(u3, h0_re[:, 0], h0_im[:, 0], ab_re[0], ab_im[0], wb[0], wc, False, None)
    out, bre, bim = _s5_pass(u3, h0_re[:, 1], h0_im[:, 1], ab_re[1], ab_im[1], wb[1], wc, True,
                             (yf, d_vec, glu_w, glu_b))
    return out, jnp.stack([fre, bre], axis=1), jnp.stack([fim, bim], axis=1)


@functools.lru_cache(maxsize=None)
def _dft_consts(n):
    idx = np.arange(n, dtype=np.int64)
    ang = (np.pi / n) * ((idx[:, None] * idx[None, :]) % (2 * n)).astype(np.float64)
    cos, sin = np.cos(ang), np.sin(ang)
    sign = np.where(idx % 2 == 0, 1.0, -1.0)
    sin_n = sin.copy()
    sin_n[0, :] = sign
    fwd = np.concatenate([cos, sin_n], axis=0)
    inv_c = cos.T / n
    inv_c[:, 0] = 1.0 / (2 * n)
    inv_s = sin.T / n
    inv_s[:, 0] = sign / (2 * n)
    inv = np.concatenate([inv_c, inv_s], axis=1)
    return fwd.astype(np.float32), inv.astype(np.float32), sign.astype(np.float32)[:, None]


@functools.lru_cache(maxsize=None)
def _hyena_feats(n):
    t = np.linspace(0.0, 1.0, n, dtype=np.float64)[:, None]
    w = (2.0 * np.pi / n) * np.arange(n, dtype=np.float64)[:, None]
    bands = np.linspace(1e-4, HY_BANDS - 1, HY_BANDS, dtype=np.float64)[None, :]
    feats = np.concatenate([t, np.cos(w * bands), -np.sin(w * bands)], axis=-1)
    pad = np.zeros((n, LANES - HY_EMB))
    return np.concatenate([feats, pad], axis=-1).astype(np.float32), t.astype(np.float32)


def _hyfilt_kernel(feat_ref, t_ref, sign_ref, w1_ref, b1_ref, w2_ref, b2_ref, w3_ref, fr_ref, dec_ref,
                   fh_ref, fl_ref, ka_ref, kb_ref, ka2_ref, *, n):
    freq = fr_ref[...]
    hdn = jnp.sin(freq * (_dot3(feat_ref[...], w1_ref[...]) + b1_ref[...]))
    hdn = jnp.sin(freq * (_dot3(hdn, w2_ref[...]) + b2_ref[...]))
    h = _dot3(hdn, w3_ref[...]) * (jnp.exp(-t_ref[...] * jnp.abs(dec_ref[...])) + HY_SHIFT)
    half = HY_ORDER * GROUP_W
    row = lax.broadcasted_iota(jnp.int32, (n, 1), 0)
    h0 = h[:, 0:half]
    h1 = jnp.where(row == 0, 0.0, h[:, half:2 * half])
    norm = jnp.sum(jnp.abs(h0), axis=0, keepdims=True) + jnp.sum(jnp.abs(h1), axis=0, keepdims=True)
    hp = (h0 + h1) / norm
    hm = (h0 - h1) / norm

    def dft(lo, x):
        xh = x.astype(BF16)
        xl = (x - xh.astype(F32)).astype(BF16)
        fh = fh_ref[lo:lo + n, :]
        return _dot(fh, xh) + _dot(fl_ref[lo:lo + n, :], xh) + _dot(fh, xl)

    kc = dft(0, hp)
    ks = dft(n, hm)
    kn = jnp.sum(hp * sign_ref[...], axis=0, keepdims=True)
    ka_ref[...] = kc
    kb_ref[...] = jnp.where(row == 0, 0.0, ks)
    ka2_ref[...] = jnp.where(row == 0, kn, kc)


def _hyena_filter(n, w1p, b1, w2, b2, w3, freq, decay):
    fwd, _, sign = _dft_consts(n)
    feats, t = _hyena_feats(n)
    fwd = jnp.asarray(fwd)
    fh = fwd.astype(BF16)
    fl = (fwd - fh.astype(F32)).astype(BF16)
    half = HY_ORDER * GROUP_W
    return pl.pallas_call(
        functools.partial(_hyfilt_kernel, n=n),
        out_shape=[jax.ShapeDtypeStruct((n, half), F32)] * 3,
        name="hyena_filter",
    )(jnp.asarray(feats), jnp.asarray(t), jnp.asarray(sign), w1p, b1, w2, b2, w3, freq, decay, fh, fl)


def _hyena_kernel(p_ref, cw_ref, ka_ref, kb_ref, ka2_ref, bias_ref, f_ref, g_ref, out_ref,
                  z_scr, y_scr, yb_scr, zf_scr, *, seq, conv_w):
    nch = seq // ML_CHUNK
    rowid = lax.broadcasted_iota(jnp.int32, (ML_CHUNK, 1), 0)

    def conv_chunk(ci, carry):
        r0 = pl.multiple_of(ci * ML_CHUNK, ML_CHUNK)
        zc = p_ref[pl.ds(r0, ML_CHUNK), :]
        zp = pltpu.roll(zc, 1, axis=0)
        zn = pltpu.roll(zc, ML_CHUNK - 1, axis=0)
        if conv_w == ML_CHUNK:
            prev_row = jnp.zeros((1, 3 * GROUP_W), F32)
            next_row = prev_row
        else:
            prev_row = p_ref[pl.ds(jnp.maximum(r0 - 1, 0), 1), :]
            prev_row = jnp.where(r0 % conv_w == 0, 0.0, prev_row)
            next_row = p_ref[pl.ds(jnp.minimum(r0 + ML_CHUNK, seq - 1), 1), :]
            next_row = jnp.where((r0 + ML_CHUNK) % conv_w == 0, 0.0, next_row)
        zp = jnp.where(rowid == 0, prev_row, zp)
        zn = jnp.where(rowid == ML_CHUNK - 1, next_row, zn)
        z_scr[pl.ds(r0, ML_CHUNK), :] = cw_ref[0:1, :] * zp + cw_ref[1:2, :] * zc + cw_ref[2:3, :] * zn
        return carry

    lax.fori_loop(0, nch, conv_chunk, 0)
    y_scr[...] = z_scr[:, 0:GROUP_W]
    nblk = seq // HY_BLK
    for o in range(HY_ORDER):
        cols = slice(o * GROUP_W, (o + 1) * GROUP_W)
        yb_scr[...] = y_scr[...].astype(BF16)

        def freq_block(fi, carry):
            f0 = pl.multiple_of(fi * HY_BLK, HY_BLK)
            fr = pl.ds(f0, HY_BLK)
            xc = _dot(f_ref[fr, :], yb_scr[...])
            xs = _dot(f_ref[pl.ds(seq + f0, HY_BLK), :], yb_scr[...])
            kb = kb_ref[fr, cols]
            zf_scr[fr, :] = (xc * ka_ref[fr, cols] - xs * kb).astype(BF16)
            zf_scr[pl.ds(seq + f0, HY_BLK), :] = (xc * kb + xs * ka2_ref[fr, cols]).astype(BF16)
            return carry

        lax.fori_loop(0, nblk, freq_block, 0)

        def time_block(ti, carry):
            tr = pl.ds(pl.multiple_of(ti * HY_BLK, HY_BLK), HY_BLK)
            conv = _dot(g_ref[tr, :], zf_scr[...])
            gate = z_scr[tr, (o + 1) * GROUP_W:(o + 2) * GROUP_W]
            y_scr[tr, :] = gate * (conv + bias_ref[o:o + 1, :] * y_scr[tr, :])
            return carry

        lax.fori_loop(0, nblk, time_block, 0)
    out_ref[...] = y_scr[...].astype(out_ref.dtype)


def _hyena(p3, conv_w3, filt, bias, conv_width):
    b, seq, _ = p3.shape
    ka, kb, ka2 = filt
    fwd, inv, _ = _dft_consts(seq)
    const2 = lambda i: (0, 0)
    half = HY_ORDER * GROUP_W
    return pl.pallas_call(
        functools.partial(_hyena_kernel, seq=seq, conv_w=conv_width),
        grid=(b,),
        in_specs=[
            pl.BlockSpec((None, seq, 3 * GROUP_W), lambda i: (i, 0, 0)),
            pl.BlockSpec((3, 3 * GROUP_W), const2),
            pl.BlockSpec((seq, half), const2),
            pl.BlockSpec((seq, half), const2),
            pl.BlockSpec((seq, half), const2),
            pl.BlockSpec((HY_ORDER, GROUP_W), const2),
            pl.BlockSpec((2 * seq, seq), const2),
            pl.BlockSpec((seq, 2 * seq), const2),
        ],
        out_specs=pl.BlockSpec((None, seq, GROUP_W), lambda i: (i, 0, 0)),
        out_shape=jax.ShapeDtypeStruct((b, seq, GROUP_W), BF16),
        scratch_shapes=[
            pltpu.VMEM((seq, 3 * GROUP_W), F32),
            pltpu.VMEM((seq, GROUP_W), F32),
            pltpu.VMEM((seq, GROUP_W), BF16),
            pltpu.VMEM((2 * seq, GROUP_W), BF16),
        ],
        name="hyena",
    )(p3, conv_w3, ka, kb, ka2, bias, jnp.asarray(fwd, BF16), jnp.asarray(inv, BF16))


def _outproj_kernel(a_ref, b_ref, c_ref, d_ref, x_ref, g1_ref, sh2_ref, sc2_ref, n2_ref, wo_ref,
                    wrh_ref, wrl_ref, br_ref, x1_ref, h2_ref, gate_ref):
    acc = _dot(a_ref[...], wo_ref[0:GROUP_W, :])
    acc += _dot(b_ref[...], wo_ref[GROUP_W:2 * GROUP_W, :])
    acc += _dot(c_ref[...], wo_ref[2 * GROUP_W:3 * GROUP_W, :])
    acc += _dot(d_ref[...], wo_ref[3 * GROUP_W:4 * GROUP_W, :])
    x1 = x_ref[...] + g1_ref[0] * acc
    x1_ref[...] = x1
    h2 = _rmsnorm_rows(x1, n2_ref[...]) * (1.0 + sc2_ref[0]) + sh2_ref[0]
    h2_ref[...] = h2.astype(BF16)
    hh = h2.astype(BF16)
    hl = (h2 - hh.astype(F32)).astype(BF16)
    logits = _dot(hh, wrh_ref[...]) + _dot(hl, wrh_ref[...]) + _dot(hh, wrl_ref[...]) + br_ref[...]
    lane = lax.broadcasted_iota(jnp.int32, logits.shape, 1).astype(F32)
    big = float(LANES)
    gl = jnp.where(lane < N_EXPERT_GROUPS, logits, NEG_INF)
    gmax = jnp.max(gl, axis=1, keepdims=True)
    gsel = jnp.min(jnp.where(gl == gmax, lane, big), axis=1, keepdims=True)
    psel = 1.0 / jnp.sum(jnp.exp(gl - gmax), axis=1, keepdims=True)
    lo = N_EXPERT_GROUPS + EXPERTS_PER_GROUP * gsel
    el = jnp.where((lane >= lo) & (lane < lo + EXPERTS_PER_GROUP), logits, NEG_INF)
    v1 = jnp.max(el, axis=1, keepdims=True)
    i1 = jnp.min(jnp.where(el == v1, lane, big), axis=1, keepdims=True)
    el2 = jnp.where(lane == i1, NEG_INF, el)
    v2 = jnp.max(el2, axis=1, keepdims=True)
    i2 = jnp.min(jnp.where(el2 == v2, lane, big), axis=1, keepdims=True)
    e2 = jnp.exp(v2 - v1)
    w1 = psel / (1.0 + e2)
    w2 = psel * e2 / (1.0 + e2)
    gate_ref[...] = (jnp.where(lane == 0.0, gsel, 0.0) + jnp.where(lane == i1 - lo + 1.0, w1, 0.0)
                     + jnp.where(lane == i2 - lo + 1.0, w2, 0.0))


def _outproj(mix, x2d, mod3, row_of_tile, norm_g, w_out, w_r, b_r):
    t = x2d.shape[0]
    tm = TOK_TILE
    w_r_hi = w_r.astype(BF16)
    const = lambda i: (0, 0)
    tile = lambda i: (i, 0)
    return pl.pallas_call(
        _outproj_kernel,
        grid=(t // tm,),
        in_specs=[pl.BlockSpec((tm, GROUP_W), tile)] * 4 + [
            pl.BlockSpec((tm, D_MODEL), tile),
            _mod_spec(2, row_of_tile),
            _mod_spec(3, row_of_tile),
            _mod_spec(4, row_of_tile),
            pl.BlockSpec((1, D_MODEL), const),
            pl.BlockSpec((D_MODEL, D_MODEL), const),
            pl.BlockSpec((D_MODEL, LANES), const),
            pl.BlockSpec((D_MODEL, LANES), const),
            pl.BlockSpec((1, LANES), const),
        ],
        out_specs=[pl.BlockSpec((tm, D_MODEL), tile), pl.BlockSpec((tm, D_MODEL), tile),
                   pl.BlockSpec((tm, LANES), tile)],
        out_shape=[jax.ShapeDtypeStruct((t, D_MODEL), F32), jax.ShapeDtypeStruct((t, D_MODEL), BF16),
                   jax.ShapeDtypeStruct((t, LANES), F32)],
        name="outproj_router",
    )(*mix, x2d, mod3, mod3, mod3, norm_g, w_out, w_r_hi, (w_r - w_r_hi.astype(F32)).astype(BF16), b_r)


def _tile_positions(gates, tri_ref, extra):
    lane = lax.broadcasted_iota(jnp.int32, gates.shape, 1).astype(F32)
    gsel = gates[:, 0:1]
    member = jnp.where((lane == gsel) & (lane < N_EXPERT_GROUPS), 1.0, 0.0)
    before = _dot(tri_ref[...], member.astype(BF16))
    pos = gsel * MOE_TOK + jnp.sum(member * (before + extra), axis=1, keepdims=True)
    slot = lax.broadcasted_iota(jnp.int32, (MOE_TOK, N_EXPERT_GROUPS * MOE_TOK), 1).astype(F32)
    return jnp.where(slot == pos, 1.0, 0.0).astype(BF16)


def _moe_pack_kernel(off_ref, cnt_ref, tot_ref, h_ref, gate_ref, tri_ref, xg_ref, gg_ref,
                     xs_scr, gs_scr, zx_scr, zg_scr, sem):
    i = pl.program_id(0)
    slot = i % 2
    gates = gate_ref[...]
    lane = lax.broadcasted_iota(jnp.int32, gates.shape, 1).astype(F32)
    member = jnp.where((lane == gates[:, 0:1]) & (lane < N_EXPERT_GROUPS), 1.0, 0.0).astype(BF16)
    grp = lax.broadcasted_iota(jnp.int32, (8, LANES), 0)
    pick = jnp.where(grp == lax.broadcasted_iota(jnp.int32, (8, LANES), 1), 1.0, 0.0).astype(BF16)
    member_t = _dot_nt(pick, member)
    before_t = _dot_nt(member_t.astype(BF16), tri_ref[...])
    gid = lax.broadcasted_iota(jnp.int32, member_t.shape, 0).astype(F32)
    pos_t = jnp.sum(member_t * (gid * MOE_TOK + before_t), axis=0, keepdims=True)
    hi, mid, lo = _split3(gates)

    def compact(row0, nrows):
        row_id = (lax.broadcasted_iota(jnp.int32, (nrows, MOE_TOK), 0) + row0).astype(F32)
        p = jnp.where(row_id == pos_t, 1.0, 0.0).astype(BF16)
        xs_scr[slot, row0:row0 + nrows, :] = _dot(p, h_ref[...]).astype(BF16)
        gs_scr[slot, row0:row0 + nrows, :] = _dot(p, hi) + _dot(p, mid) + _dot(p, lo)

    def all_slots():
        compact(0, N_EXPERT_GROUPS * MOE_TOK)

    def first_halves():
        for g in range(N_EXPERT_GROUPS):
            compact(g * MOE_TOK, MOE_TOK // 2)

    small = cnt_ref[i * N_EXPERT_GROUPS] <= MOE_TOK // 2
    for g in range(1, N_EXPERT_GROUPS):
        small = small & (cnt_ref[i * N_EXPERT_GROUPS + g] <= MOE_TOK // 2)
    lax.cond(small, first_halves, all_slots)

    def for_each_piece(tile, buf, action):
        for g in range(N_EXPERT_GROUPS):
            off = pl.multiple_of(off_ref[tile * N_EXPERT_GROUPS + g], ROW_ALIGN)
            for j in range(MOE_TOK // MOE_PIECE):
                src = pl.ds(g * MOE_TOK + j * MOE_PIECE, MOE_PIECE)
                dst = pl.ds(off + j * MOE_PIECE, MOE_PIECE)

                @pl.when(cnt_ref[tile * N_EXPERT_GROUPS + g] > j * MOE_PIECE)
                def _():
                    action(pltpu.make_async_copy(xs_scr.at[buf, src], xg_ref.at[g, dst], sem.at[buf, 0, g, j]))
                    action(pltpu.make_async_copy(gs_scr.at[buf, src], gg_ref.at[g, dst], sem.at[buf, 1, g, j]))

    @pl.when(i > 0)
    def _():
        for_each_piece(i - 1, 1 - slot, lambda cp: cp.wait())

    for_each_piece(i, slot, lambda cp: cp.start())

    @pl.when(i == pl.num_programs(0) - 1)
    def _():
        for_each_piece(i, slot, lambda cp: cp.wait())
        zx_scr[...] = jnp.zeros_like(zx_scr)
        zg_scr[...] = jnp.zeros_like(zg_scr)

        def tail_copies(g):
            tot = pl.multiple_of(tot_ref[g], ROW_ALIGN)
            return (pltpu.make_async_copy(zx_scr, xg_ref.at[g, pl.ds(tot, MOE_ROWS)], sem.at[0, 0, g, 0]),
                    pltpu.make_async_copy(zg_scr, gg_ref.at[g, pl.ds(tot, MOE_ROWS)], sem.at[0, 1, g, 0]))

        for g in range(N_EXPERT_GROUPS):
            for cp in tail_copies(g):
                cp.start()
        for g in range(N_EXPERT_GROUPS):
            for cp in tail_copies(g):
                cp.wait()


def _moe_expert_kernel(grp_ref, chk_ref, meta_ref, x_ref, g_ref, wg_ref, wu_ref, wd_ref, y_ref):
    k = pl.program_id(0)
    half = MOE_ROWS // 2

    def experts(rows):
        x = x_ref[0:rows, :]
        gates = g_ref[0:rows, :]
        acc = None
        for e in range(EXPERTS_PER_GROUP):
            a = _dot(x, wg_ref[e])
            u = _dot(x, wu_ref[e])
            hid = (_silu(a) * u * gates[:, 1 + e:2 + e]).astype(BF16)
            part = _dot(hid, wd_ref[e])
            acc = part if acc is None else acc + part
        return acc.astype(y_ref.dtype)

    def whole_chunk():
        y_ref[...] = experts(MOE_ROWS)

    def lower_half():
        y_ref[0:half, :] = experts(half)
        y_ref[half:MOE_ROWS, :] = jnp.zeros((MOE_ROWS - half, D_MODEL), y_ref.dtype)

    @pl.when(k < meta_ref[0])
    def _():
        lax.cond(meta_ref[1 + k] > half, whole_chunk, lower_half)


def _moe_unpack_kernel(st_ref, dl_ref, cnt_ref, gate_ref, tri_ref, x1_ref, g2_ref, fn_ref, yg_ref, out_ref,
                       ys_scr, sem, *, final):
    i = pl.program_id(0)
    slot = i % 2

    def for_each_piece(tile, buf, action):
        for g in range(N_EXPERT_GROUPS):
            k = tile * N_EXPERT_GROUPS + g
            start = pl.multiple_of(st_ref[k], ROW_ALIGN)
            for j in range(MOE_TOK // MOE_PIECE):
                lo, hi = j * MOE_PIECE, (j + 1) * MOE_PIECE

                @pl.when((dl_ref[k] < hi) & (dl_ref[k] + cnt_ref[k] > lo))
                def _():
                    action(pltpu.make_async_copy(yg_ref.at[g, pl.ds(start + lo, MOE_PIECE)],
                                                 ys_scr.at[buf, pl.ds(g * MOE_TOK + lo, MOE_PIECE)],
                                                 sem.at[buf, g, j]))

    @pl.when(i == 0)
    def _():
        ys_scr[...] = jnp.zeros_like(ys_scr)
        for_each_piece(0, 0, lambda cp: cp.start())

    @pl.when(i + 1 < pl.num_programs(0))
    def _():
        for_each_piece(i + 1, 1 - slot, lambda cp: cp.start())

    gates = gate_ref[...]
    lane = lax.broadcasted_iota(jnp.int32, (1, LANES), 1)
    delta = jnp.zeros((1, LANES), F32)
    for g in range(N_EXPERT_GROUPS):
        delta = jnp.where(lane == g, dl_ref[i * N_EXPERT_GROUPS + g].astype(F32), delta)
    pt = _tile_positions(gates, tri_ref, delta)
    for_each_piece(i, slot, lambda cp: cp.wait())
    x2 = x1_ref[...] + g2_ref[0] * _dot(pt, ys_scr[slot])
    if final:
        x2 = _rmsnorm_rows(x2, fn_ref[...])
    out_ref[...] = x2


def _moe_schedule(gsel, n_tiles, n_steps):
    ng = N_EXPERT_GROUPS
    member = gsel.reshape(n_tiles, MOE_TOK, 1) == jnp.arange(ng, dtype=jnp.int32)
    cnt = jnp.sum(member.astype(jnp.int32), axis=1)
    padded = (cnt + ROW_ALIGN - 1) // ROW_ALIGN * ROW_ALIGN
    off = jnp.cumsum(padded, axis=0) - padded
    tot = jnp.sum(padded, axis=0)
    n_chunks = jnp.maximum((tot + MOE_ROWS - 1) // MOE_ROWS, 1)
    start = jnp.minimum(off, n_chunks * MOE_ROWS - MOE_TOK)
    ends = jnp.cumsum(n_chunks)
    n_active = ends[-1]
    step = jnp.minimum(jnp.arange(n_steps, dtype=jnp.int32), n_active - 1)
    grp = jnp.sum((step[:, None] >= ends[None, :]).astype(jnp.int32), axis=1)
    chk = step - (ends - n_chunks)[grp]
    i32 = lambda a: a.astype(jnp.int32)
    populated = jnp.clip(tot[grp] - chk * MOE_ROWS, 0, MOE_ROWS)
    meta = jnp.concatenate([n_active.reshape(1), populated])
    return (i32(off).reshape(-1), i32(padded).reshape(-1), i32(tot), i32(start).reshape(-1),
            i32(off - start).reshape(-1), i32(grp), i32(chk), i32(meta))


def _moe(h2, gates, layer, w_gate, w_up, w_down, x1, mod3, row_of_tile, final_g, final):
    t = h2.shape[0]
    ng = N_EXPERT_GROUPS
    n_tiles = t // MOE_TOK
    cap = -(-(t + ROW_ALIGN * n_tiles + MOE_ROWS) // MOE_ROWS) * MOE_ROWS
    n_steps = (t + ROW_ALIGN * n_tiles) // MOE_ROWS + ng
    off, cnt, tot, start, delta, grp, chk, meta = _moe_schedule(gates[:, 0].astype(jnp.int32), n_tiles,
                                                                    n_steps)
    pieces = MOE_TOK // MOE_PIECE
    r = np.arange(MOE_TOK)
    tri = jnp.asarray((r[:, None] > r[None, :]).astype(np.float32), BF16)
    any_spec = pl.BlockSpec(memory_space=pl.ANY)

    xg, gg = pl.pallas_call(
        _moe_pack_kernel,
        grid_spec=pltpu.PrefetchScalarGridSpec(
            num_scalar_prefetch=3,
            grid=(n_tiles,),
            in_specs=[
                pl.BlockSpec((MOE_TOK, D_MODEL), lambda i, *_: (i, 0)),
                pl.BlockSpec((MOE_TOK, LANES), lambda i, *_: (i, 0)),
                pl.BlockSpec((MOE_TOK, MOE_TOK), lambda i, *_: (0, 0)),
            ],
            out_specs=[any_spec, any_spec],
            scratch_shapes=[
                pltpu.VMEM((2, ng * MOE_TOK, D_MODEL), BF16),
                pltpu.VMEM((2, ng * MOE_TOK, LANES), F32),
                pltpu.VMEM((MOE_ROWS, D_MODEL), BF16),
                pltpu.VMEM((MOE_ROWS, LANES), F32),
                pltpu.SemaphoreType.DMA((2, 2, ng, pieces)),
            ],
        ),
        out_shape=[jax.ShapeDtypeStruct((ng, cap, D_MODEL), BF16), jax.ShapeDtypeStruct((ng, cap, LANES), F32)],
        name="moe_pack",
    )(off, cnt, tot, h2, gates, tri)

    group_w = lambda shape: pl.BlockSpec((None, None, EXPERTS_PER_GROUP) + shape,
                                         lambda k, grp, chk, na: (layer, grp[k], 0, 0, 0))
    rows = lambda width: pl.BlockSpec((None, MOE_ROWS, width), lambda k, grp, chk, na: (grp[k], chk[k], 0))
    yg = pl.pallas_call(
        _moe_expert_kernel,
        grid_spec=pltpu.PrefetchScalarGridSpec(
            num_scalar_prefetch=3,
            grid=(n_steps,),
            in_specs=[rows(D_MODEL), rows(LANES), group_w((D_MODEL, D_EXPERT)), group_w((D_MODEL, D_EXPERT)),
                      group_w((D_EXPERT, D_MODEL))],
            out_specs=rows(D_MODEL),
        ),
        out_shape=jax.ShapeDtypeStruct((ng, cap, D_MODEL), BF16),
        name="moe_experts",
    )(grp, chk, meta, xg, gg,
      w_gate.reshape(DEPTH, ng, EXPERTS_PER_GROUP, D_MODEL, D_EXPERT),
      w_up.reshape(DEPTH, ng, EXPERTS_PER_GROUP, D_MODEL, D_EXPERT),
      w_down.reshape(DEPTH, ng, EXPERTS_PER_GROUP, D_EXPERT, D_MODEL))

    return pl.pallas_call(
        functools.partial(_moe_unpack_kernel, final=final),
        grid_spec=pltpu.PrefetchScalarGridSpec(
            num_scalar_prefetch=3,
            grid=(n_tiles,),
            in_specs=[
                pl.BlockSpec((MOE_TOK, LANES), lambda i, *_: (i, 0)),
                pl.BlockSpec((MOE_TOK, MOE_TOK), lambda i, *_: (0, 0)),
                pl.BlockSpec((MOE_TOK, D_MODEL), lambda i, *_: (i, 0)),
                pl.BlockSpec((1, 1, D_MODEL), lambda i, *_: (row_of_tile(i), 0, 5)),
                pl.BlockSpec((1, D_MODEL), lambda i, *_: (0, 0)),
                any_spec,
            ],
            out_specs=pl.BlockSpec((MOE_TOK, D_MODEL), lambda i, *_: (i, 0)),
            scratch_shapes=[pltpu.VMEM((2, ng * MOE_TOK, D_MODEL), BF16),
                            pltpu.SemaphoreType.DMA((2, ng, pieces))],
        ),
        out_shape=jax.ShapeDtypeStruct((t, D_MODEL), F32),
        name="moe_unpack",
    )(start, delta, cnt, gates, tri, x1, mod3, final_g, yg)


def _prep_layer(l, w):
    gw = GROUP_W
    w_in = w['w_in'][l]
    w_g = w_in[:, 9 * gw:9 * gw + N_GATES]
    lb_all = jnp.cumsum(jax.nn.softmax(w['hg_lb'].astype(F32), axis=0), axis=0)
    lb = lb_all[l] - lb_all[0]
    lb_rows = jnp.zeros((8, gw), F32).at[0].set(jnp.log(lb)).at[1].set(jnp.log1p(-lb)).at[2].set(1.0 - lb)
    w_r = jnp.zeros((D_MODEL, LANES), F32)
    w_r = w_r.at[:, 0:N_EXPERT_GROUPS].set(w['moe_wg'][l])
    w_r = w_r.at[:, N_EXPERT_GROUPS:N_EXPERT_GROUPS + N_EXPERTS].set(w['moe_we'][l])
    b_r = jnp.zeros((1, LANES), F32)
    b_r = b_r.at[0, 0:N_EXPERT_GROUPS].set(w['moe_bg'][l])
    b_r = b_r.at[0, N_EXPERT_GROUPS:N_EXPERT_GROUPS + N_EXPERTS].set(w['moe_be'][l])
    gate_bias = jnp.concatenate([w['ml_i_bias'][l], w['ml_f_bias'][l]])
    w1p = jnp.zeros((LANES, HY_HIDDEN), F32).at[0:HY_EMB].set(w['hy_w1'][l])
    return dict(
        norm1=w['norm1'][l][None, :], norm2=w['norm2'][l][None, :],
        w_a=w_in[:, :9 * gw].astype(BF16), w_b=w_in[:, 9 * gw + N_GATES:].astype(BF16),
        w_g=jnp.zeros((D_MODEL, LANES), BF16).at[:, 0:N_GATES].set(w_g.astype(BF16)), w_gt=w_g.T.astype(BF16),
        w_out=w['w_out'][l].astype(BF16),
        lb_rows=lb_rows, hg_norm=w['hg_norm'][l][None, :],
        ml_conv=w['ml_conv'][l], ml_brow=jnp.zeros((1, LANES), F32).at[0, 0:N_GATES].set(gate_bias),
        ml_brows=jnp.repeat(gate_bias.reshape(4, HEADS), HEAD_D, axis=1),
        ml_norm=w['ml_norm'][l][None, :],
        s5=_s5_params(w['s5_a_re'][l], w['s5_a_im'][l], w['s5_log_dt'][l], w['s5_b_re'][l], w['s5_b_im'][l],
                      w['s5_c_re'][l], w['s5_c_im'][l]),
        s5_d=w['s5_d'][l][None, :], s5_glu_w=w['s5_glu_w'][l].astype(BF16), s5_glu_b=w['s5_glu_b'][l][None, :],
        hy_conv=w['hy_conv'][l], hy_bias=w['hy_bias'][l],
        hy_mlp=(w1p, w['hy_b1'][l][None, :], w['hy_w2'][l], w['hy_b2'][l][None, :], w['hy_w3'][l],
                w['hy_freq'][l][None, :], w['hy_decay'][l][None, :]),
        w_r=w_r, b_r=b_r,
        layer=l,
    )


def _trunk_layer(x2d, nb, seq, mod3, cond_row, st, lw, moe_w, filt, conv_width, emit_state, final_g, final):
    if cond_row is None:
        row_of_tile = lambda i: i // (seq // TOK_TILE)
        in_row_of_tile = lambda i: i // (seq // IN_TILE)
    else:
        row_of_tile = in_row_of_tile = lambda i: cond_row
    hg, ml, su, hy, gt, gtt = _inproj(x2d, mod3, in_row_of_tile, lw['norm1'], lw['w_a'], lw['w_b'], lw['w_g'],
                                       lw['w_gt'])
    nch = seq // ML_CHUNK
    a_out, hg_st = _hgrn(hg.reshape(nb, seq, -1), st['hgrn'], lw['lb_rows'], lw['hg_norm'], emit_state)
    gr4 = jnp.transpose(gtt.reshape(4, HEADS, nb, nch, ML_CHUNK), (2, 0, 3, 1, 4)).reshape(nb, 4, nch, GROUP_W)
    b_out, mlc, mln, mlm = _mlstm(ml.reshape(nb, seq, -1), gt.reshape(nb, seq, LANES), gr4, lw['ml_conv'],
                                  lw['ml_brow'], lw['ml_brows'], lw['ml_norm'], st['ml_c'], st['ml_n'],
                                  st['ml_m'], conv_width, emit_state)
    c_out, s5re, s5im = _s5(su.reshape(nb, seq, -1), st['s5_re'], st['s5_im'], lw['s5'], lw['s5_d'],
                            lw['s5_glu_w'], lw['s5_glu_b'])
    d_out = _hyena(hy.reshape(nb, seq, -1), lw['hy_conv'], filt, lw['hy_bias'], conv_width)
    mix = [m.reshape(nb * seq, GROUP_W) for m in (a_out, b_out, c_out, d_out)]
    x1, h2, gates = _outproj(mix, x2d, mod3, row_of_tile, lw['norm2'], lw['w_out'], lw['w_r'], lw['b_r'])
    x2 = _moe(h2, gates, lw['layer'], *moe_w, x1, mod3, row_of_tile, final_g, final)
    new_st = dict(hgrn=hg_st, ml_c=mlc, ml_n=mln, ml_m=mlm, s5_re=s5re, s5_im=s5im)
    return x2, new_st


def kernel(x_prompt, x_sample, c, state_hgrn, state_mlstm_c, state_mlstm_n, state_mlstm_m, state_s5_re, state_s5_im, c_ctx, w_ada, b_ada, norm1, norm2, w_in, w_out, hg_lb, hg_norm, ml_conv, ml_i_bias, ml_f_bias, ml_norm, s5_a_re, s5_a_im, s5_log_dt, s5_b_re, s5_b_im, s5_c_re, s5_c_im, s5_d, s5_glu_w, s5_glu_b, hy_conv, hy_w1, hy_b1, hy_w2, hy_b2, hy_w3, hy_freq, hy_decay, hy_bias, moe_wg, moe_bg, moe_we, moe_be, moe_w_gate, moe_w_up, moe_w_down, final_norm):
    w = dict(w_in=w_in, w_out=w_out, norm1=norm1, norm2=norm2, hg_lb=hg_lb, hg_norm=hg_norm, ml_conv=ml_conv,
             ml_i_bias=ml_i_bias, ml_f_bias=ml_f_bias, ml_norm=ml_norm, s5_a_re=s5_a_re, s5_a_im=s5_a_im,
             s5_log_dt=s5_log_dt, s5_b_re=s5_b_re, s5_b_im=s5_b_im, s5_c_re=s5_c_re, s5_c_im=s5_c_im,
             s5_d=s5_d, s5_glu_w=s5_glu_w, s5_glu_b=s5_glu_b, hy_conv=hy_conv, hy_w1=hy_w1, hy_b1=hy_b1,
             hy_w2=hy_w2, hy_b2=hy_b2, hy_w3=hy_w3, hy_freq=hy_freq, hy_decay=hy_decay, hy_bias=hy_bias,
             moe_wg=moe_wg, moe_bg=moe_bg, moe_we=moe_we, moe_be=moe_be, moe_w_gate=moe_w_gate,
             moe_w_up=moe_w_up, moe_w_down=moe_w_down)
    bp, lp, _ = x_prompt.shape
    bs, ls, _ = x_sample.shape
    ctx_row = bs
    cond = jnp.zeros((MOD_ROWS, D_MODEL), F32).at[0:bs].set(c).at[ctx_row].set(c_ctx)
    mod = _ada_mod(cond, w_ada, b_ada)
    final_g = final_norm[None, :]

    zero_st = dict(
        hgrn=jnp.zeros((bp, 2, HEADS, HEAD_D, HEAD_D), F32),
        ml_c=jnp.zeros((bp, 2, HEADS, HEAD_D, HEAD_D), F32),
        ml_n=jnp.zeros((bp, 2, 1, GROUP_W), F32),
        ml_m=jnp.zeros((bp, 2, 1, GROUP_W), F32),
        s5_re=jnp.zeros((bp, 2, S5_STATE), F32),
        s5_im=jnp.zeros((bp, 2, S5_STATE), F32))

    xp = x_prompt.reshape(bp * lp, D_MODEL)
    xs = x_sample.reshape(bs * ls, D_MODEL)
    moe_w = (moe_w_gate.astype(BF16), moe_w_up.astype(BF16), moe_w_down.astype(BF16))
    per_layer = []
    for l in range(DEPTH):
        lw = _prep_layer(l, w)
        mod3 = mod[l].reshape(MOD_ROWS, 1, 6 * D_MODEL)
        last = l == DEPTH - 1
        filt_p = _hyena_filter(lp, *lw['hy_mlp'])
        filt_s = filt_p if ls == lp else _hyena_filter(ls, *lw['hy_mlp'])
        xp, st = _trunk_layer(xp, bp, lp, mod3, ctx_row, zero_st, lw, moe_w, filt_p, lp, True, final_g, last)
        per_layer.append(st)
        st_in = dict(
            hgrn=state_hgrn[:, l].astype(F32),
            ml_c=state_mlstm_c[:, l].astype(F32),
            ml_n=state_mlstm_n[:, l].astype(F32).reshape(bs, 2, 1, GROUP_W),
            ml_m=jnp.repeat(state_mlstm_m[:, l].astype(F32), HEAD_D, axis=-1).reshape(bs, 2, 1, GROUP_W),
            s5_re=state_s5_re[:, l].astype(F32).reshape(bs, 2, S5_STATE),
            s5_im=state_s5_im[:, l].astype(F32).reshape(bs, 2, S5_STATE))
        xs, _ = _trunk_layer(xs, bs, ls, mod3, None, st_in, lw, moe_w, filt_s, GRID_W, False, final_g, last)

    new_hgrn = jnp.stack([s['hgrn'] for s in per_layer], axis=1)
    new_ml_c = jnp.stack([s['ml_c'] for s in per_layer], axis=1)
    new_ml_n = jnp.stack([s['ml_n'].reshape(bp, 2, HEADS, HEAD_D) for s in per_layer], axis=1)
    new_ml_m = jnp.stack([s['ml_m'][:, :, 0, ::HEAD_D] for s in per_layer], axis=1)
    new_s5_re = jnp.stack([s['s5_re'].reshape(bp, 2, S5_GROUPS, S5_P) for s in per_layer], axis=1)
    new_s5_im = jnp.stack([s['s5_im'].reshape(bp, 2, S5_GROUPS, S5_P) for s in per_layer], axis=1)
    return (xp.reshape(bp, lp, D_MODEL), xs.reshape(bs, ls, D_MODEL),
            new_hgrn, new_ml_c, new_ml_n, new_ml_m, new_s5_re, new_s5_im)
```

```python
import functools
import math

import numpy as np
import jax
import jax.numpy as jnp
from jax import lax
from jax.experimental import pallas as pl
from jax.experimental.pallas import tpu as pltpu

F32 = jnp.float32
BF16 = jnp.bfloat16

D_MODEL = 1024
DEPTH = 2
GRID_W = 64
GROUP_W = D_MODEL // 4
HEADS = 4
HEAD_D = GROUP_W // HEADS
S5_CH = 16
S5_GROUPS = GROUP_W // S5_CH
S5_P = 64
S5_STATE = S5_GROUPS * S5_P
HY_ORDER = 2
HY_BANDS = 16
HY_EMB = 1 + 2 * HY_BANDS
HY_HIDDEN = 64
HY_SHIFT = 0.05
N_EXPERT_GROUPS = 4
EXPERTS_PER_GROUP = 4
N_EXPERTS = N_EXPERT_GROUPS * EXPERTS_PER_GROUP
D_EXPERT = D_MODEL // 2
EPS = 1e-6
N_GATES = 4 * HEADS

LANES = 128
MOD_ROWS = 16
TOK_TILE = 256
IN_TILE = 512
MOE_TOK = TOK_TILE
MOE_PIECE = 64
MOE_ROWS = 512
ROW_ALIGN = 16
HG_BLK = 32
HG_SB = 128
HG_SAFE_LOG = -75.0
HG_SAFE_Q = 1e3
ML_CHUNK = 64
S5_TC = 64
HY_BLK = 256
NEG_INF = float("-inf")


def _dot(a, b):
    return jnp.dot(a, b, preferred_element_type=F32)


def _dot_nt(a, b):
    return lax.dot_general(a, b, (((1,), (1,)), ((), ())), preferred_element_type=F32)


def _dot_tn(a, b):
    return lax.dot_general(a, b, (((0,), (0,)), ((), ())), preferred_element_type=F32)


def _split3(x):
    hi = x.astype(BF16)
    r1 = x - hi.astype(F32)
    mid = r1.astype(BF16)
    lo = (r1 - mid.astype(F32)).astype(BF16)
    return hi, mid, lo


def _dot_sel_l(m, x):
    hi, mid, lo = _split3(x)
    return _dot(m, hi) + _dot(m, mid) + _dot(m, lo)


def _dot_sel_r(x, m):
    hi, mid, lo = _split3(x)
    return _dot(hi, m) + _dot(mid, m) + _dot(lo, m)


def _dot3(a, b):
    ah = a.astype(BF16)
    al = (a - ah.astype(F32)).astype(BF16)
    bh = b.astype(BF16)
    bl = (b - bh.astype(F32)).astype(BF16)
    return _dot(ah, bh) + _dot(al, bh) + _dot(ah, bl)


def _sigmoid(x):
    return 1.0 / (1.0 + jnp.exp(-x))


def _silu(x):
    return x * _sigmoid(x)


def _log_sigmoid(x):
    return jnp.minimum(x, 0.0) - jnp.log(1.0 + jnp.exp(-jnp.abs(x)))


def _rmsnorm_rows(x, g):
    return x * lax.rsqrt(jnp.mean(x * x, axis=-1, keepdims=True) + EPS) * g


def _ada_kernel(c_ref, w_ref, b_ref, o_ref):
    s = _silu(c_ref[...]).astype(BF16)
    o_ref[0] = _dot(s, w_ref[0].astype(BF16)) + b_ref[0]


def _ada_mod(cond, w_ada, b_ada):
    tn = 1536
    n = w_ada.shape[-1]
    return pl.pallas_call(
        _ada_kernel,
        grid=(DEPTH, n // tn),
        in_specs=[
            pl.BlockSpec((MOD_ROWS, D_MODEL), lambda l, j: (0, 0)),
            pl.BlockSpec((1, D_MODEL, tn), lambda l, j: (l, 0, j)),
            pl.BlockSpec((1, 1, tn), lambda l, j: (l, 0, j)),
        ],
        out_specs=pl.BlockSpec((1, MOD_ROWS, tn), lambda l, j: (l, 0, j)),
        out_shape=jax.ShapeDtypeStruct((DEPTH, MOD_ROWS, n), F32),
        name="ada_mod",
    )(cond, w_ada, b_ada.reshape(DEPTH, 1, n))


def _inproj_kernel(x_ref, sh_ref, sc_ref, g_ref, wa_ref, wb_ref, wg_ref, wgt_ref,
                   hg_ref, ml_ref, su_ref, hy_ref, gt_ref, gtt_ref):
    h = _rmsnorm_rows(x_ref[...], g_ref[...])
    h = h * (1.0 + sc_ref[0]) + sh_ref[0]
    hb = h.astype(BF16)
    hg_ref[...] = _dot(hb, wa_ref[:, 0:5 * GROUP_W])
    ml_ref[...] = _dot(hb, wa_ref[:, 5 * GROUP_W:9 * GROUP_W])
    su_ref[...] = _dot(hb, wb_ref[:, 0:GROUP_W])
    hy_ref[...] = _dot(hb, wb_ref[:, GROUP_W:4 * GROUP_W])
    gt_ref[...] = _dot(hb, wg_ref[...])
    gtt_ref[...] = _dot_nt(wgt_ref[...], hb)


def _mod_spec(k, row_of_tile):
    return pl.BlockSpec((1, 1, D_MODEL), lambda i: (row_of_tile(i), 0, k))


def _inproj(x2d, mod3, row_of_tile, norm_g, w_a, w_b, w_g, w_gt):
    t = x2d.shape[0]
    tm = IN_TILE
    const = lambda i: (0, 0)
    tile = lambda i: (i, 0)
    widths = (5 * GROUP_W, 4 * GROUP_W, GROUP_W, 3 * GROUP_W)
    return pl.pallas_call(
        _inproj_kernel,
        grid=(t // tm,),
        in_specs=[
            pl.BlockSpec((tm, D_MODEL), tile),
            _mod_spec(0, row_of_tile),
            _mod_spec(1, row_of_tile),
            pl.BlockSpec((1, D_MODEL), const),
            pl.BlockSpec(w_a.shape, const),
            pl.BlockSpec(w_b.shape, const),
            pl.BlockSpec(w_g.shape, const),
            pl.BlockSpec(w_gt.shape, const),
        ],
        out_specs=[pl.BlockSpec((tm, w), tile) for w in widths]
        + [pl.BlockSpec((tm, LANES), tile), pl.BlockSpec((N_GATES, tm), lambda i: (0, i))],
        out_shape=[jax.ShapeDtypeStruct((t, w), F32) for w in widths]
        + [jax.ShapeDtypeStruct((t, LANES), F32), jax.ShapeDtypeStruct((N_GATES, t), F32)],
        name="inproj",
    )(x2d, mod3, mod3, norm_g, w_a, w_b, w_g, w_gt)


@functools.lru_cache(maxsize=None)
def _hgrn_consts():
    r = np.arange(HG_SB)
    same = (r[:, None] // HG_BLK) == (r[None, :] // HG_BLK)
    tri_f = same & (r[:, None] >= r[None, :])
    tri_b = same & (r[:, None] <= r[None, :])
    c = np.arange(GROUP_W)
    head = (c[:, None] // HEAD_D) == (c[None, :] // HEAD_D)
    rr = np.arange(HEADS * HG_SB)
    head4 = (rr[:, None] // HG_SB) == (c[None, :] // HEAD_D)
    pairs = np.stack([np.tile(tri_f, (1, HEADS)), np.tile(tri_b, (1, HEADS))])
    return (np.stack([tri_f, tri_b]).astype(np.float32), same.astype(np.float32),
            head.astype(np.float32), head4.astype(np.float32), pairs.astype(np.float32))


def _hgrn_kernel(p_ref, s0_ref, lb_ref, ng_ref, tri_ref, bones_ref, e_ref, hmask_ref, hmask4_ref,
                 smask_ref, out_ref, *rest, seq, emit_state):
    if emit_state:
        st_ref = rest[0]
        rest = rest[1:]
    o_scr, st_scr, q_scr, k_scr, bc_scr, gq_scr, gk_scr, ks_scr, dec_scr, oi_scr = rest
    nsb = seq // HG_SB
    nblk = HG_SB // HG_BLK
    log_lb = lb_ref[0:1, :]
    log_1mlb = lb_ref[1:2, :]
    one_m_lb = lb_ref[2:3, :]
    row = lax.broadcasted_iota(jnp.int32, (HG_BLK, 1), 0)

    def stage(r0, d):
        rows = pl.ds(r0, HG_SB)
        pq = p_ref[rows, 0:GROUP_W]
        pf = p_ref[rows, (1 + d) * GROUP_W:(2 + d) * GROUP_W]
        q = _silu(pq)
        e = jnp.exp(-jnp.abs(pf))
        b2 = log_1mlb + jnp.minimum(pf, 0.0) - jnp.log(1.0 + e)
        mx = jnp.maximum(log_lb, b2)
        mn = jnp.minimum(log_lb, b2)
        lf = mx + jnp.log(1.0 + jnp.exp(mn - mx))
        k = one_m_lb * jnp.where(pf >= 0.0, e, 1.0) / (1.0 + e)
        hi, mid, lo = _split3(lf)
        bc = _dot(tri_ref[d], hi) + _dot(tri_ref[d], mid) + _dot(tri_ref[d], lo)
        bt = _dot(bones_ref[...], hi) + _dot(bones_ref[...], mid) + _dot(bones_ref[...], lo)
        gq_scr[d] = (q * jnp.exp(bc)).astype(BF16)
        gk_scr[d] = (k * jnp.exp(bt - bc)).astype(BF16)
        dec_scr[d] = jnp.exp(bt)
        ks_scr[d] = (k * jnp.exp(-bc)).astype(BF16)
        q_scr[d] = q
        k_scr[d] = k
        bc_scr[d] = bc
        return (jnp.min(bc) >= HG_SAFE_LOG) & (jnp.max(jnp.abs(q)) <= HG_SAFE_Q)

    def block_edge_path(r0, d):
        vb = p_ref[pl.ds(r0, HG_SB), 3 * GROUP_W:4 * GROUP_W].astype(BF16)
        ksbd = jnp.concatenate([ks_scr[d]] * HEADS, axis=0) * hmask4_ref[...]
        vbd = jnp.concatenate([vb] * HEADS, axis=0) * hmask4_ref[...]
        s = _dot_nt(gq_scr[d], ksbd) * smask_ref[d]
        oi_scr[d] = _dot(s.astype(BF16), vbd)

    def exact_path(r0, d):
        for j in range(nblk):
            sl = slice(HG_BLK * j, HG_BLK * (j + 1))
            qj = q_scr[d, sl, :]
            kj = k_scr[d, sl, :]
            vj = p_ref[pl.ds(r0 + HG_BLK * j, HG_BLK), 3 * GROUP_W:4 * GROUP_W]
            bcj = bc_scr[d, sl, :]
            parts = []
            for s in range(HG_BLK):
                keep = (row >= s) if d == 0 else (row <= s)
                dd = jnp.exp(jnp.where(keep, bcj - bcj[s:s + 1, :], NEG_INF))
                parts.append((qj * dd * kj[s:s + 1, :]).astype(BF16))
            r_all = _dot(jnp.concatenate(parts, axis=0), e_ref[...])
            o = r_all[0:HG_BLK, :] * vj[0:1, :]
            for s in range(1, HG_BLK):
                o = o + r_all[HG_BLK * s:HG_BLK * (s + 1), :] * vj[s:s + 1, :]
            oi_scr[d, sl, :] = o

    def chain(r0, d):
        order = range(nblk) if d == 0 else range(nblk - 1, -1, -1)
        for j in order:
            sl = slice(HG_BLK * j, HG_BLK * (j + 1))
            vb = p_ref[pl.ds(r0 + HG_BLK * j, HG_BLK), 3 * GROUP_W:4 * GROUP_W].astype(BF16)
            upd = _dot_tn(vb, gk_scr[d, sl, :]) * hmask_ref[...]
            st = st_scr[d]
            o_scr[d, pl.ds(r0 + HG_BLK * j, HG_BLK), :] = oi_scr[d, sl, :] + _dot_nt(gq_scr[d, sl, :], st.astype(BF16))
            st_scr[d] = st * dec_scr[d, HG_BLK * j:HG_BLK * j + 1, :] + upd

    eye_h = jnp.where(lax.broadcasted_iota(jnp.int32, (HEAD_D, HEAD_D), 0)
                      == lax.broadcasted_iota(jnp.int32, (HEAD_D, HEAD_D), 1), 1.0, 0.0).astype(BF16)
    st_scr[...] = jnp.zeros_like(st_scr)
    for d in range(2):
        for h in range(HEADS):
            hi, mid, lo = _split3(s0_ref[d, h])
            lanes = slice(HEAD_D * h, HEAD_D * (h + 1))
            st_scr[d, lanes, lanes] = _dot_tn(hi, eye_h) + _dot_tn(mid, eye_h) + _dot_tn(lo, eye_h)

    def body(i, carry):
        rf = pl.multiple_of(i * HG_SB, HG_SB)
        rb = pl.multiple_of((nsb - 1 - i) * HG_SB, HG_SB)
        safe = stage(rf, 0) & stage(rb, 1)

        def both_block_edge():
            block_edge_path(rf, 0)
            block_edge_path(rb, 1)

        def both_exact():
            exact_path(rf, 0)
            exact_path(rb, 1)

        lax.cond(safe, both_block_edge, both_exact)
        chain(rf, 0)
        chain(rb, 1)
        return carry

    lax.fori_loop(0, nsb, body, 0, unroll=2)

    def finish(i, carry):
        rows = pl.ds(pl.multiple_of(i * HG_SB, HG_SB), HG_SB)
        o = o_scr[0, rows, :] + o_scr[1, rows, :]
        ms = _dot_hilo(o * o, e_ref[...]) * (1.0 / HEAD_D)
        pg = p_ref[rows, 4 * GROUP_W:5 * GROUP_W]
        out_ref[rows, :] = (o * lax.rsqrt(ms + EPS) * ng_ref[...] * _silu(pg)).astype(out_ref.dtype)
        return carry

    lax.fori_loop(0, nsb, finish, 0, unroll=2)
    if emit_state:
        ri = lax.broadcasted_iota(jnp.int32, (GROUP_W, GROUP_W), 0)
        ci = lax.broadcasted_iota(jnp.int32, (GROUP_W, GROUP_W), 1)
        eye = jnp.where(ri == ci, 1.0, 0.0).astype(BF16)
        for d in range(2):
            hi, mid, lo = _split3(st_scr[d])
            s_t = _dot_tn(hi, eye) + _dot_tn(mid, eye) + _dot_tn(lo, eye)
            for h in range(HEADS):
                st_ref[d, h] = s_t[HEAD_D * h:HEAD_D * (h + 1), HEAD_D * h:HEAD_D * (h + 1)]


def _hgrn(p3, s0_bd, lb_rows, norm_g, emit_state):
    b, seq, _ = p3.shape
    tri, bones, head, head4, pairs = _hgrn_consts()
    const2 = lambda i: (0, 0)
    const3 = lambda i: (0, 0, 0)
    out_shape = [jax.ShapeDtypeStruct((b, seq, GROUP_W), BF16)]
    out_specs = [pl.BlockSpec((None, seq, GROUP_W), lambda i: (i, 0, 0))]
    if emit_state:
        out_shape.append(jax.ShapeDtypeStruct((b, 2, HEADS, HEAD_D, HEAD_D), F32))
        out_specs.append(pl.BlockSpec((None, 2, HEADS, HEAD_D, HEAD_D), lambda i: (i, 0, 0, 0, 0)))
    sb_f32 = pltpu.VMEM((2, HG_SB, GROUP_W), F32)
    sb_bf16 = pltpu.VMEM((2, HG_SB, GROUP_W), BF16)
    res = pl.pallas_call(
        functools.partial(_hgrn_kernel, seq=seq, emit_state=emit_state),
        grid=(b,),
        in_specs=[
            pl.BlockSpec((None, seq, 5 * GROUP_W), lambda i: (i, 0, 0)),
            pl.BlockSpec((None, 2, HEADS, HEAD_D, HEAD_D), lambda i: (i, 0, 0, 0, 0)),
            pl.BlockSpec((8, GROUP_W), const2),
            pl.BlockSpec((1, GROUP_W), const2),
            pl.BlockSpec((2, HG_SB, HG_SB), const3),
            pl.BlockSpec((HG_SB, HG_SB), const2),
            pl.BlockSpec((GROUP_W, GROUP_W), const2),
            pl.BlockSpec((GROUP_W, GROUP_W), const2),
            pl.BlockSpec((HEADS * HG_SB, GROUP_W), const2),
            pl.BlockSpec((2, HG_SB, HEADS * HG_SB), const3),
        ],
        out_specs=out_specs,
        out_shape=out_shape,
        scratch_shapes=[
            pltpu.VMEM((2, seq, GROUP_W), F32),
            pltpu.VMEM((2, GROUP_W, GROUP_W), F32),
            sb_f32, sb_f32, sb_f32, sb_bf16, sb_bf16, sb_bf16, sb_f32, sb_f32,
        ],
        name="hgrn2",
    )(p3, s0_bd, lb_rows, norm_g, jnp.asarray(tri, BF16), jnp.asarray(bones, BF16),
      jnp.asarray(head, BF16), jnp.asarray(head, F32), jnp.asarray(head4, BF16), jnp.asarray(pairs, F32))
    return res if emit_state else (res[0], None)


@functools.lru_cache(maxsize=None)
def _mlstm_consts():
    r = np.arange(ML_CHUNK)
    tri_f = (r[:, None] >= r[None, :]).astype(np.float32)
    tri = np.stack([tri_f, tri_f.T])
    c = np.arange(GROUP_W)
    head = (c[:, None] // HEAD_D) == (c[None, :] // HEAD_D)
    pos = c % HEAD_D
    row_f = head & (pos[:, None] <= pos[None, :])
    row_b = head & (pos[:, None] >= pos[None, :])
    rowtri = np.stack([row_f, row_b]).astype(np.float32)
    expand = np.zeros((2, LANES, GROUP_W), np.float32)
    for d in range(2):
        for h in range(HEADS):
            expand[d, 2 * HEADS + d * HEADS + h, h * HEAD_D:(h + 1) * HEAD_D] = 1.0
    return tri, rowtri, expand, head.astype(np.float32)


def _dot_hilo(x, m):
    hi = x.astype(BF16)
    lo = (x - hi.astype(F32)).astype(BF16)
    return _dot(hi, m) + _dot(lo, m)


def _mlstm_kernel(p_ref, g_ref, gr_ref, conv_ref, brow_ref, brows_ref, ng_ref, c0_ref, n0_ref, m0_ref,
                  tri_ref, rowtri_ref, exp_ref, e_ref, out_ref, *rest, seq, conv_w, emit_state):
    if emit_state:
        cst_ref, nst_ref, mst_ref = rest[:3]
        rest = rest[3:]
    (q_scr, k_scr, col_scr, mloc_scr, gb_scr, h_scr, row_scr, blast_scr, gmax_scr,
     c_scr, n_scr, m_scr) = rest
    nch = seq // ML_CHUNK
    rowid = lax.broadcasted_iota(jnp.int32, (ML_CHUNK, 1), 0)
    ti = lax.broadcasted_iota(jnp.int32, (ML_CHUNK, GROUP_W), 0)
    si = lax.broadcasted_iota(jnp.int32, (ML_CHUNK, GROUP_W), 1) % HEAD_D

    for d in range(2):
        li_row = gr_ref[d] + brows_ref[d:d + 1, :]
        lf_row = _log_sigmoid(gr_ref[2 + d] + brows_ref[2 + d:3 + d, :])
        row_scr[d] = li_row - _dot_sel_r(lf_row, rowtri_ref[d])

    def pre_chunk(ci, carry):
        r0 = pl.multiple_of(ci * ML_CHUNK, ML_CHUNK)
        rows = pl.ds(r0, ML_CHUNK)
        zc = p_ref[rows, 0:2 * GROUP_W]
        zp = pltpu.roll(zc, 1, axis=0)
        zn = pltpu.roll(zc, ML_CHUNK - 1, axis=0)
        if conv_w == ML_CHUNK:
            prev_row = jnp.zeros((1, 2 * GROUP_W), F32)
            next_row = prev_row
        else:
            prev_row = p_ref[pl.ds(jnp.maximum(r0 - 1, 0), 1), 0:2 * GROUP_W]
            prev_row = jnp.where(r0 % conv_w == 0, 0.0, prev_row)
            next_row = p_ref[pl.ds(jnp.minimum(r0 + ML_CHUNK, seq - 1), 1), 0:2 * GROUP_W]
            next_row = jnp.where((r0 + ML_CHUNK) % conv_w == 0, 0.0, next_row)
        zp = jnp.where(rowid == 0, prev_row, zp)
        zn = jnp.where(rowid == ML_CHUNK - 1, next_row, zn)
        qk = _silu(conv_ref[0:1, :] * zp + conv_ref[1:2, :] * zc + conv_ref[2:3, :] * zn)
        q_scr[rows, :] = qk[:, 0:GROUP_W].astype(BF16)
        k_scr[rows, :] = qk[:, GROUP_W:2 * GROUP_W] * (HEAD_D ** -0.5)
        ge = g_ref[rows, :] + brow_ref[...]
        lane = lax.broadcasted_iota(jnp.int32, (1, LANES), 1)
        bwd = lane >= 3 * HEADS
        lf = _log_sigmoid(ge)
        bc = jnp.where(bwd, _dot_sel_l(tri_ref[1], lf), _dot_sel_l(tri_ref[0], lf))
        li = pltpu.roll(ge, 2 * HEADS, axis=1)
        pre = li - bc
        suf = pre
        for sh in (1, 2, 4, 8, 16, 32):
            pre = jnp.maximum(pre, jnp.where(rowid >= sh, pltpu.roll(pre, sh, axis=0), NEG_INF))
            suf = jnp.maximum(suf, jnp.where(rowid < ML_CHUNK - sh, pltpu.roll(suf, ML_CHUNK - sh, axis=0), NEG_INF))
        blast = jnp.where(bwd, bc[0:1, :], bc[ML_CHUNK - 1:ML_CHUNK, :])
        gb = blast - bc + li
        edge_rows = jnp.concatenate([blast, jnp.max(gb, axis=0, keepdims=True), jnp.zeros((6, LANES), F32)], axis=0)
        stacked = jnp.concatenate([bc, bc + jnp.where(bwd, suf, pre), gb, edge_rows], axis=0)
        for d in range(2):
            ex = _dot_sel_r(stacked, exp_ref[d])
            col_scr[d, rows, :] = ex[0:ML_CHUNK, :]
            mloc_scr[d, rows, :] = ex[ML_CHUNK:2 * ML_CHUNK, :]
            gb_scr[d, rows, :] = ex[2 * ML_CHUNK:3 * ML_CHUNK, :]
            blast_scr[d, pl.ds(ci, 1), :] = ex[3 * ML_CHUNK:3 * ML_CHUNK + 1, :]
            gmax_scr[d, pl.ds(ci, 1), :] = ex[3 * ML_CHUNK + 1:3 * ML_CHUNK + 2, :]
        return carry

    lax.fori_loop(0, nch, pre_chunk, 0, unroll=2)

    for d in range(2):
        c_scr[d] = jnp.zeros((GROUP_W, GROUP_W), F32)
        for h in range(HEADS):
            lo = HEAD_D * h
            c_scr[d, lo:lo + HEAD_D, lo:lo + HEAD_D] = c0_ref[d, h]
    n_scr[...] = n0_ref[...]
    m_scr[...] = m0_ref[...]

    def chunk(ci, d):
        r0 = pl.multiple_of(ci * ML_CHUNK, ML_CHUNK)
        rows = pl.ds(r0, ML_CHUNK)
        qb = q_scr[rows, :]
        kc = k_scr[rows, :]
        vb = p_ref[rows, 2 * GROUP_W:3 * GROUP_W].astype(BF16)
        hm = e_ref[...]
        kbd = jnp.concatenate([kc.astype(BF16)] * HEADS, axis=0) * hm
        vbd = jnp.concatenate([vb] * HEADS, axis=0) * hm
        s = _dot_nt(qb, kbd)
        colb = col_scr[d, rows, :]
        m_b = m_scr[d]
        prev = colb + m_b
        m_t = jnp.maximum(prev, mloc_scr[d, rows, :])
        keep = (ti >= si) if d == 0 else (ti <= si)
        w = jnp.exp(jnp.where(keep, colb + row_scr[d, pl.ds(ci, 1), :], NEG_INF) - m_t) * s
        wp = jnp.exp(prev - m_t)
        cst = c_scr[d]
        nrow = n_scr[d]
        num = wp * _dot(qb, cst.astype(BF16)) + _dot(w.astype(BF16), vbd)
        den = _dot_hilo(wp * qb.astype(F32) * nrow + w, e_ref[...])
        h_scr[d, rows, :] = num / jnp.maximum(jnp.abs(den), jnp.exp(-m_t))
        blast = blast_scr[d, pl.ds(ci, 1), :]
        m_new = jnp.maximum(blast + m_b, gmax_scr[d, pl.ds(ci, 1), :])
        dec = jnp.exp(blast + m_b - m_new)
        kw = kc * jnp.exp(gb_scr[d, rows, :] - m_new)
        c_scr[d] = cst * dec + _dot_tn(kw.astype(BF16), vb) * hm.astype(F32)
        n_scr[d] = nrow * dec + jnp.sum(kw, axis=0, keepdims=True)
        m_scr[d] = m_new

    def body(i, carry):
        chunk(i, 0)
        chunk(nch - 1 - i, 1)
        return carry

    lax.fori_loop(0, nch, body, 0, unroll=2)

    def fin_chunk(ci, carry):
        rows = pl.ds(pl.multiple_of(ci * ML_CHUNK, ML_CHUNK), ML_CHUNK)
        hs = h_scr[0, rows, :] + h_scr[1, rows, :]
        ms = _dot_hilo(hs * hs, e_ref[...]) * (1.0 / HEAD_D)
        po = p_ref[rows, 3 * GROUP_W:4 * GROUP_W]
        out_ref[rows, :] = (hs * lax.rsqrt(ms + EPS) * ng_ref[...] * _sigmoid(po)).astype(out_ref.dtype)
        return carry

    lax.fori_loop(0, nch, fin_chunk, 0, unroll=4)
    if emit_state:
        for d in range(2):
            for h in range(HEADS):
                lo = HEAD_D * h
                cst_ref[d, h] = c_scr[d, lo:lo + HEAD_D, lo:lo + HEAD_D]
        nst_ref[...] = n_scr[...]
        mst_ref[...] = m_scr[...]


def _mlstm(p3, g3, gr4, conv_w3, brow, brows, norm_g, c0, n0, m0, conv_width, emit_state):
    b, seq, _ = p3.shape
    nch = seq // ML_CHUNK
    tri, rowtri, expand, head = _mlstm_consts()
    const2 = lambda i: (0, 0)
    const3 = lambda i: (0, 0, 0)
    out_shape = [jax.ShapeDtypeStruct((b, seq, GROUP_W), BF16)]
    out_specs = [pl.BlockSpec((None, seq, GROUP_W), lambda i: (i, 0, 0))]
    c_spec = pl.BlockSpec((None, 2, HEADS, HEAD_D, HEAD_D), lambda i: (i, 0, 0, 0, 0))
    row_spec = pl.BlockSpec((None, 2, 1, GROUP_W), lambda i: (i, 0, 0, 0))
    if emit_state:
        out_shape += [jax.ShapeDtypeStruct((b, 2, HEADS, HEAD_D, HEAD_D), F32),
                      jax.ShapeDtypeStruct((b, 2, 1, GROUP_W), F32), jax.ShapeDtypeStruct((b, 2, 1, GROUP_W), F32)]
        out_specs += [c_spec, row_spec, row_spec]
    dir_seq = (2, seq, GROUP_W)
    dir_chunks = (2, nch, GROUP_W)
    res = pl.pallas_call(
        functools.partial(_mlstm_kernel, seq=seq, conv_w=conv_width, emit_state=emit_state),
        grid=(b,),
        in_specs=[
            pl.BlockSpec((None, seq, 4 * GROUP_W), lambda i: (i, 0, 0)),
            pl.BlockSpec((None, seq, LANES), lambda i: (i, 0, 0)),
            pl.BlockSpec((None, 4, nch, GROUP_W), lambda i: (i, 0, 0, 0)),
            pl.BlockSpec((3, 2 * GROUP_W), const2),
            pl.BlockSpec((1, LANES), const2),
            pl.BlockSpec((4, GROUP_W), const2),
            pl.BlockSpec((1, GROUP_W), const2),
            c_spec, row_spec, row_spec,
            pl.BlockSpec((2, ML_CHUNK, ML_CHUNK), const3),
            pl.BlockSpec((2, GROUP_W, GROUP_W), const3),
            pl.BlockSpec((2, LANES, GROUP_W), const3),
            pl.BlockSpec((GROUP_W, GROUP_W), const2),
        ],
        out_specs=out_specs,
        out_shape=out_shape,
        scratch_shapes=[
            pltpu.VMEM((seq, GROUP_W), BF16),
            pltpu.VMEM((seq, GROUP_W), F32),
            pltpu.VMEM(dir_seq, F32), pltpu.VMEM(dir_seq, F32), pltpu.VMEM(dir_seq, F32), pltpu.VMEM(dir_seq, F32),
            pltpu.VMEM(dir_chunks, F32), pltpu.VMEM(dir_chunks, F32), pltpu.VMEM(dir_chunks, F32),
            pltpu.VMEM((2, GROUP_W, GROUP_W), F32),
            pltpu.VMEM((2, 1, GROUP_W), F32),
            pltpu.VMEM((2, 1, GROUP_W), F32),
        ],
        name="mlstm",
    )(p3, g3, gr4, conv_w3, brow, brows, norm_g, c0, n0, m0, jnp.asarray(tri, BF16), jnp.asarray(rowtri, BF16),
      jnp.asarray(expand, BF16), jnp.asarray(head, BF16))
    if emit_state:
        return res
    return res[0], None, None, None


def _s5_kernel(*refs, nb, reverse, final):
    if final:
        (u_ref, hre0_ref, him0_ref, are_ref, aim_ref, wb_ref, wc_ref, yin_ref, d_ref, gw_ref, gb_ref,
         y_ref, sre_ref, sim_ref, utm_scr, bu_scr, hb_scr, ytm_scr, hre_scr, him_scr) = refs
    else:
        (u_ref, hre0_ref, him0_ref, are_ref, aim_ref, wb_ref, wc_ref,
         y_ref, sre_ref, sim_ref, utm_scr, bu_scr, hb_scr, ytm_scr, hre_scr, him_scr) = refs
    tc = S5_TC
    step_id = pl.program_id(0)

    @pl.when(step_id == 0)
    def _():
        hre_scr[...] = hre0_ref[...]
        him_scr[...] = him0_ref[...]

    halves = GROUP_W // LANES
    for b in range(nb):
        for hv in range(halves):
            utm_scr[hv, pl.ds(b, tc, stride=nb), :] = u_ref[b, :, hv * LANES:(hv + 1) * LANES]
    u_tm = jnp.concatenate([utm_scr[hv] for hv in range(halves)], axis=1)
    bu_scr[...] = _dot(u_tm.astype(BF16), wb_ref[...])
    are = jnp.broadcast_to(are_ref[...], (nb, S5_STATE))
    aim = jnp.broadcast_to(aim_ref[...], (nb, S5_STATE))

    spi = max(ROW_ALIGN // nb, 1)
    rows_it = spi * nb

    def step(i, carry):
        hre, him = carry
        r0 = pl.multiple_of(((tc // spi - 1 - i) if reverse else i) * rows_it, rows_it)
        res = [None] * spi
        ims = [None] * spi
        for s in (range(spi - 1, -1, -1) if reverse else range(spi)):
            bu = bu_scr[pl.ds(r0 + s * nb, nb), :]
            hre, him = (are * hre - aim * him + bu[:, 0:S5_STATE],
                        are * him + aim * hre + bu[:, S5_STATE:2 * S5_STATE])
            res[s], ims[s] = hre, him
        hb_scr[pl.ds(r0, rows_it), 0:S5_STATE] = jnp.concatenate(res, axis=0).astype(BF16)
        hb_scr[pl.ds(r0, rows_it), S5_STATE:2 * S5_STATE] = jnp.concatenate(ims, axis=0).astype(BF16)
        return hre, him

    hre, him = lax.fori_loop(0, tc // spi, step, (hre_scr[...], him_scr[...]))
    hre_scr[...] = hre
    him_scr[...] = him
    y_tm = _dot(hb_scr[...], wc_ref[...])
    for hv in range(halves):
        ytm_scr[hv] = y_tm[:, hv * LANES:(hv + 1) * LANES]
    for b in range(nb):
        yb = jnp.concatenate([ytm_scr[hv, pl.ds(b, tc, stride=nb), :] for hv in range(halves)], axis=1)
        if final:
            y = yb + yin_ref[b] + d_ref[...] * u_ref[b]
            z = 0.5 * y * (1.0 + jnp.tanh(math.sqrt(2.0 / math.pi) * (y + 0.044715 * (y * y * y))))
            gate = _sigmoid(_dot(z.astype(BF16), gw_ref[...]) + gb_ref[...])
            y_ref[b] = (z * gate).astype(y_ref.dtype)
        else:
            y_ref[b] = yb

    @pl.when(step_id == pl.num_programs(0) - 1)
    def _():
        sre_ref[...] = hre
        sim_ref[...] = him


def _s5_pass(u3, hre0, him0, are, aim, wb, wc, reverse, extra):
    nb, seq, _ = u3.shape
    nch = seq // S5_TC
    final = extra is not None
    chunk = (lambda i: (0, nch - 1 - i, 0)) if reverse else (lambda i: (0, i, 0))
    const2 = lambda i: (0, 0)
    in_specs = [
        pl.BlockSpec((nb, S5_TC, GROUP_W), chunk),
        pl.BlockSpec((nb, S5_STATE), const2),
        pl.BlockSpec((nb, S5_STATE), const2),
        pl.BlockSpec((1, S5_STATE), const2),
        pl.BlockSpec((1, S5_STATE), const2),
        pl.BlockSpec((GROUP_W, 2 * S5_STATE), const2),
        pl.BlockSpec((2 * S5_STATE, GROUP_W), const2),
    ]
    args = [u3, hre0, him0, are, aim, wb, wc]
    if final:
        yin, dvec, gw, gb = extra
        in_specs += [pl.BlockSpec((nb, S5_TC, GROUP_W), chunk), pl.BlockSpec((1, GROUP_W), const2),
                     pl.BlockSpec((GROUP_W, GROUP_W), const2), pl.BlockSpec((1, GROUP_W), const2)]
        args += [yin, dvec, gw, gb]
    return pl.pallas_call(
        functools.partial(_s5_kernel, nb=nb, reverse=reverse, final=final),
        grid=(nch,),
        in_specs=in_specs,
        out_specs=[pl.BlockSpec((nb, S5_TC, GROUP_W), chunk),
                   pl.BlockSpec((nb, S5_STATE), const2), pl.BlockSpec((nb, S5_STATE), const2)],
        out_shape=[jax.ShapeDtypeStruct((nb, seq, GROUP_W), BF16 if final else F32),
                   jax.ShapeDtypeStruct((nb, S5_STATE), F32), jax.ShapeDtypeStruct((nb, S5_STATE), F32)],
        scratch_shapes=[
            pltpu.VMEM((GROUP_W // LANES, nb * S5_TC, LANES), F32),
            pltpu.VMEM((nb * S5_TC, 2 * S5_STATE), F32),
            pltpu.VMEM((nb * S5_TC, 2 * S5_STATE), BF16),
            pltpu.VMEM((GROUP_W // LANES, nb * S5_TC, LANES), F32),
            pltpu.VMEM((nb, S5_STATE), F32),
            pltpu.VMEM((nb, S5_STATE), F32),
        ],
        name="s5_bwd" if reverse else "s5_fwd",
    )(*args)


def _s5_params(a_re, a_im, log_dt, b_re, b_im, c_re, c_im):
    eye = jnp.eye(S5_GROUPS, dtype=F32)
    dt = jnp.exp(log_dt)[..., None]
    mag = jnp.exp(a_re * dt)
    ab_re, ab_im = mag * jnp.cos(a_im * dt), mag * jnp.sin(a_im * dt)
    den = a_re * a_re + a_im * a_im
    g_re = ((ab_re - 1.0) * a_re + ab_im * a_im) / den
    g_im = (ab_im * a_re - (ab_re - 1.0) * a_im) / den
    bb_re = g_re[..., None] * b_re - g_im[..., None] * b_im
    bb_im = g_re[..., None] * b_im + g_im[..., None] * b_re

    def in_mat(bb):
        return jnp.einsum('dgpc,gh->dgchp', bb, eye).reshape(2, GROUP_W, S5_STATE)

    wb = jnp.concatenate([in_mat(bb_re), in_mat(bb_im)], axis=-1).astype(BF16)

    def out_mat(cc):
        return jnp.einsum('gcp,gh->gphc', cc, eye).reshape(S5_STATE, GROUP_W)

    wc = jnp.concatenate([out_mat(c_re), -out_mat(c_im)], axis=0).astype(BF16)
    return ab_re.reshape(2, 1, S5_STATE), ab_im.reshape(2, 1, S5_STATE), wb, wc


def _s5(u3, h0_re, h0_im, prm, d_vec, glu_w, glu_b):
    ab_re, ab_im, wb, wc = prm
    yf, fre, fim = _s5_pass(u3, h0_re[:, 0], h0_im[:, 0], ab_re[0], ab_im[0], wb[0], wc, False, None)
    out, bre, bim = _s5_pass(u3, h0_re[:, 1], h0_im[:, 1], ab_re[1], ab_im[1], wb[1], wc, True,
                             (yf, d_vec, glu_w, glu_b))
    return out, jnp.stack([fre, bre], axis=1), jnp.stack([fim, bim], axis=1)


@functools.lru_cache(maxsize=None)
def _dft_consts(n):
    idx = np.arange(n, dtype=np.int64)
    ang = (np.pi / n) * ((idx[:, None] * idx[None, :]) % (2 * n)).astype(np.float64)
    cos, sin = np.cos(ang), np.sin(ang)
    sign = np.where(idx % 2 == 0, 1.0, -1.0)
    sin_n = sin.copy()
    sin_n[0, :] = sign
    fwd = np.concatenate([cos, sin_n], axis=0)
    inv_c = cos.T / n
    inv_c[:, 0] = 1.0 / (2 * n)
    inv_s = sin.T / n
    inv_s[:, 0] = sign / (2 * n)
    inv = np.concatenate([inv_c, inv_s], axis=1)
    return fwd.astype(np.float32), inv.astype(np.float32), sign.astype(np.float32)[:, None]


@functools.lru_cache(maxsize=None)
def _hyena_feats(n):
    t = np.linspace(0.0, 1.0, n, dtype=np.float64)[:, None]
    w = (2.0 * np.pi / n) * np.arange(n, dtype=np.float64)[:, None]
    bands = np.linspace(1e-4, HY_BANDS - 1, HY_BANDS, dtype=np.float64)[None, :]
    feats = np.concatenate([t, np.cos(w * bands), -np.sin(w * bands)], axis=-1)
    pad = np.zeros((n, LANES - HY_EMB))
    return np.concatenate([feats, pad], axis=-1).astype(np.float32), t.astype(np.float32)


def _hyfilt_kernel(feat_ref, t_ref, sign_ref, w1_ref, b1_ref, w2_ref, b2_ref, w3_ref, fr_ref, dec_ref,
                   fh_ref, fl_ref, ka_ref, kb_ref, ka2_ref, *, n):
    freq = fr_ref[...]
    hdn = jnp.sin(freq * (_dot3(feat_ref[...], w1_ref[...]) + b1_ref[...]))
    hdn = jnp.sin(freq * (_dot3(hdn, w2_ref[...]) + b2_ref[...]))
    h = _dot3(hdn, w3_ref[...]) * (jnp.exp(-t_ref[...] * jnp.abs(dec_ref[...])) + HY_SHIFT)
    half = HY_ORDER * GROUP_W
    row = lax.broadcasted_iota(jnp.int32, (n, 1), 0)
    h0 = h[:, 0:half]
    h1 = jnp.where(row == 0, 0.0, h[:, half:2 * half])
    norm = jnp.sum(jnp.abs(h0), axis=0, keepdims=True) + jnp.sum(jnp.abs(h1), axis=0, keepdims=True)
    hp = (h0 + h1) / norm
    hm = (h0 - h1) / norm

    def dft(lo, x):
        xh = x.astype(BF16)
        xl = (x - xh.astype(F32)).astype(BF16)
        fh = fh_ref[lo:lo + n, :]
        return _dot(fh, xh) + _dot(fl_ref[lo:lo + n, :], xh) + _dot(fh, xl)

    kc = dft(0, hp)
    ks = dft(n, hm)
    kn = jnp.sum(hp * sign_ref[...], axis=0, keepdims=True)
    ka_ref[...] = kc
    kb_ref[...] = jnp.where(row == 0, 0.0, ks)
    ka2_ref[...] = jnp.where(row == 0, kn, kc)


def _hyena_filter(n, w1p, b1, w2, b2, w3, freq, decay):
    fwd, _, sign = _dft_consts(n)
    feats, t = _hyena_feats(n)
    fwd = jnp.asarray(fwd)
    fh = fwd.astype(BF16)
    fl = (fwd - fh.astype(F32)).astype(BF16)
    half = HY_ORDER * GROUP_W
    return pl.pallas_call(
        functools.partial(_hyfilt_kernel, n=n),
        out_shape=[jax.ShapeDtypeStruct((n, half), F32)] * 3,
        name="hyena_filter",
    )(jnp.asarray(feats), jnp.asarray(t), jnp.asarray(sign), w1p, b1, w2, b2, w3, freq, decay, fh, fl)


def _hyena_kernel(p_ref, cw_ref, ka_ref, kb_ref, ka2_ref, bias_ref, f_ref, g_ref, out_ref,
                  z_scr, y_scr, yb_scr, zf_scr, *, seq, conv_w):
    nch = seq // ML_CHUNK
    rowid = lax.broadcasted_iota(jnp.int32, (ML_CHUNK, 1), 0)

    def conv_chunk(ci, carry):
        r0 = pl.multiple_of(ci * ML_CHUNK, ML_CHUNK)
        zc = p_ref[pl.ds(r0, ML_CHUNK), :]
        zp = pltpu.roll(zc, 1, axis=0)
        zn = pltpu.roll(zc, ML_CHUNK - 1, axis=0)
        if conv_w == ML_CHUNK:
            prev_row = jnp.zeros((1, 3 * GROUP_W), F32)
            next_row = prev_row
        else:
            prev_row = p_ref[pl.ds(jnp.maximum(r0 - 1, 0), 1), :]
            prev_row = jnp.where(r0 % conv_w == 0, 0.0, prev_row)
            next_row = p_ref[pl.ds(jnp.minimum(r0 + ML_CHUNK, seq - 1), 1), :]
            next_row = jnp.where((r0 + ML_CHUNK) % conv_w == 0, 0.0, next_row)
        zp = jnp.where(rowid == 0, prev_row, zp)
        zn = jnp.where(rowid == ML_CHUNK - 1, next_row, zn)
        z_scr[pl.ds(r0, ML_CHUNK), :] = cw_ref[0:1, :] * zp + cw_ref[1:2, :] * zc + cw_ref[2:3, :] * zn
        return carry

    lax.fori_loop(0, nch, conv_chunk, 0)
    y_scr[...] = z_scr[:, 0:GROUP_W]
    nblk = seq // HY_BLK
    for o in range(HY_ORDER):
        cols = slice(o * GROUP_W, (o + 1) * GROUP_W)
        yb_scr[...] = y_scr[...].astype(BF16)

        def freq_block(fi, carry):
            f0 = pl.multiple_of(fi * HY_BLK, HY_BLK)
            fr = pl.ds(f0, HY_BLK)
            xc = _dot(f_ref[fr, :], yb_scr[...])
            xs = _dot(f_ref[pl.ds(seq + f0, HY_BLK), :], yb_scr[...])
            kb = kb_ref[fr, cols]
            zf_scr[fr, :] = (xc * ka_ref[fr, cols] - xs * kb).astype(BF16)
            zf_scr[pl.ds(seq + f0, HY_BLK), :] = (xc * kb + xs * ka2_ref[fr, cols]).astype(BF16)
            return carry

        lax.fori_loop(0, nblk, freq_block, 0)

        def time_block(ti, carry):
            tr = pl.ds(pl.multiple_of(ti * HY_BLK, HY_BLK), HY_BLK)
            conv = _dot(g_ref[tr, :], zf_scr[...])
            gate = z_scr[tr, (o + 1) * GROUP_W:(o + 2) * GROUP_W]
            y_scr[tr, :] = gate * (conv + bias_ref[o:o + 1, :] * y_scr[tr, :])
            return carry

        lax.fori_loop(0, nblk, time_block, 0)
    out_ref[...] = y_scr[...].astype(out_ref.dtype)


def _hyena(p3, conv_w3, filt, bias, conv_width):
    b, seq, _ = p3.shape
    ka, kb, ka2 = filt
    fwd, inv, _ = _dft_consts(seq)
    const2 = lambda i: (0, 0)
    half = HY_ORDER * GROUP_W
    return pl.pallas_call(
        functools.partial(_hyena_kernel, seq=seq, conv_w=conv_width),
        grid=(b,),
        in_specs=[
            pl.BlockSpec((None, seq, 3 * GROUP_W), lambda i: (i, 0, 0)),
            pl.BlockSpec((3, 3 * GROUP_W), const2),
            pl.BlockSpec((seq, half), const2),
            pl.BlockSpec((seq, half), const2),
            pl.BlockSpec((seq, half), const2),
            pl.BlockSpec((HY_ORDER, GROUP_W), const2),
            pl.BlockSpec((2 * seq, seq), const2),
            pl.BlockSpec((seq, 2 * seq), const2),
        ],
        out_specs=pl.BlockSpec((None, seq, GROUP_W), lambda i: (i, 0, 0)),
        out_shape=jax.ShapeDtypeStruct((b, seq, GROUP_W), BF16),
        scratch_shapes=[
            pltpu.VMEM((seq, 3 * GROUP_W), F32),
            pltpu.VMEM((seq, GROUP_W), F32),
            pltpu.VMEM((seq, GROUP_W), BF16),
            pltpu.VMEM((2 * seq, GROUP_W), BF16),
        ],
        name="hyena",
    )(p3, conv_w3, ka, kb, ka2, bias, jnp.asarray(fwd, BF16), jnp.asarray(inv, BF16))


def _outproj_kernel(a_ref, b_ref, c_ref, d_ref, x_ref, g1_ref, sh2_ref, sc2_ref, n2_ref, wo_ref,
                    wrh_ref, wrl_ref, br_ref, x1_ref, h2_ref, gate_ref):
    acc = _dot(a_ref[...], wo_ref[0:GROUP_W, :])
    acc += _dot(b_ref[...], wo_ref[GROUP_W:2 * GROUP_W, :])
    acc += _dot(c_ref[...], wo_ref[2 * GROUP_W:3 * GROUP_W, :])
    acc += _dot(d_ref[...], wo_ref[3 * GROUP_W:4 * GROUP_W, :])
    x1 = x_ref[...] + g1_ref[0] * acc
    x1_ref[...] = x1
    h2 = _rmsnorm_rows(x1, n2_ref[...]) * (1.0 + sc2_ref[0]) + sh2_ref[0]
    h2_ref[...] = h2.astype(BF16)
    hh = h2.astype(BF16)
    hl = (h2 - hh.astype(F32)).astype(BF16)
    logits = _dot(hh, wrh_ref[...]) + _dot(hl, wrh_ref[...]) + _dot(hh, wrl_ref[...]) + br_ref[...]
    lane = lax.broadcasted_iota(jnp.int32, logits.shape, 1).astype(F32)
    big = float(LANES)
    gl = jnp.where(lane < N_EXPERT_GROUPS, logits, NEG_INF)
    gmax = jnp.max(gl, axis=1, keepdims=True)
    gsel = jnp.min(jnp.where(gl == gmax, lane, big), axis=1, keepdims=True)
    psel = 1.0 / jnp.sum(jnp.exp(gl - gmax), axis=1, keepdims=True)
    lo = N_EXPERT_GROUPS + EXPERTS_PER_GROUP * gsel
    el = jnp.where((lane >= lo) & (lane < lo + EXPERTS_PER_GROUP), logits, NEG_INF)
    v1 = jnp.max(el, axis=1, keepdims=True)
    i1 = jnp.min(jnp.where(el == v1, lane, big), axis=1, keepdims=True)
    el2 = jnp.where(lane == i1, NEG_INF, el)
    v2 = jnp.max(el2, axis=1, keepdims=True)
    i2 = jnp.min(jnp.where(el2 == v2, lane, big), axis=1, keepdims=True)
    e2 = jnp.exp(v2 - v1)
    w1 = psel / (1.0 + e2)
    w2 = psel * e2 / (1.0 + e2)
    gate_ref[...] = (jnp.where(lane == 0.0, gsel, 0.0) + jnp.where(lane == i1 - lo + 1.0, w1, 0.0)
                     + jnp.where(lane == i2 - lo + 1.0, w2, 0.0))


def _outproj(mix, x2d, mod3, row_of_tile, norm_g, w_out, w_r, b_r):
    t = x2d.shape[0]
    tm = TOK_TILE
    w_r_hi = w_r.astype(BF16)
    const = lambda i: (0, 0)
    tile = lambda i: (i, 0)
    return pl.pallas_call(
        _outproj_kernel,
        grid=(t // tm,),
        in_specs=[pl.BlockSpec((tm, GROUP_W), tile)] * 4 + [
            pl.BlockSpec((tm, D_MODEL), tile),
            _mod_spec(2, row_of_tile),
            _mod_spec(3, row_of_tile),
            _mod_spec(4, row_of_tile),
            pl.BlockSpec((1, D_MODEL), const),
            pl.BlockSpec((D_MODEL, D_MODEL), const),
            pl.BlockSpec((D_MODEL, LANES), const),
            pl.BlockSpec((D_MODEL, LANES), const),
            pl.BlockSpec((1, LANES), const),
        ],
        out_specs=[pl.BlockSpec((tm, D_MODEL), tile), pl.BlockSpec((tm, D_MODEL), tile),
                   pl.BlockSpec((tm, LANES), tile)],
        out_shape=[jax.ShapeDtypeStruct((t, D_MODEL), F32), jax.ShapeDtypeStruct((t, D_MODEL), BF16),
                   jax.ShapeDtypeStruct((t, LANES), F32)],
        name="outproj_router",
    )(*mix, x2d, mod3, mod3, mod3, norm_g, w_out, w_r_hi, (w_r - w_r_hi.astype(F32)).astype(BF16), b_r)


def _tile_positions(gates, tri_ref, extra):
    lane = lax.broadcasted_iota(jnp.int32, gates.shape, 1).astype(F32)
    gsel = gates[:, 0:1]
    member = jnp.where((lane == gsel) & (lane < N_EXPERT_GROUPS), 1.0, 0.0)
    before = _dot(tri_ref[...], member.astype(BF16))
    pos = gsel * MOE_TOK + jnp.sum(member * (before + extra), axis=1, keepdims=True)
    slot = lax.broadcasted_iota(jnp.int32, (MOE_TOK, N_EXPERT_GROUPS * MOE_TOK), 1).astype(F32)
    return jnp.where(slot == pos, 1.0, 0.0).astype(BF16)


def _moe_pack_kernel(off_ref, cnt_ref, tot_ref, h_ref, gate_ref, tri_ref, xg_ref, gg_ref,
                     xs_scr, gs_scr, zx_scr, zg_scr, sem):
    i = pl.program_id(0)
    slot = i % 2
    gates = gate_ref[...]
    lane = lax.broadcasted_iota(jnp.int32, gates.shape, 1).astype(F32)
    member = jnp.where((lane == gates[:, 0:1]) & (lane < N_EXPERT_GROUPS), 1.0, 0.0).astype(BF16)
    grp = lax.broadcasted_iota(jnp.int32, (8, LANES), 0)
    pick = jnp.where(grp == lax.broadcasted_iota(jnp.int32, (8, LANES), 1), 1.0, 0.0).astype(BF16)
    member_t = _dot_nt(pick, member)
    before_t = _dot_nt(member_t.astype(BF16), tri_ref[...])
    gid = lax.broadcasted_iota(jnp.int32, member_t.shape, 0).astype(F32)
    pos_t = jnp.sum(member_t * (gid * MOE_TOK + before_t), axis=0, keepdims=True)
    hi, mid, lo = _split3(gates)

    def compact(row0, nrows):
        row_id = (lax.broadcasted_iota(jnp.int32, (nrows, MOE_TOK), 0) + row0).astype(F32)
        p = jnp.where(row_id == pos_t, 1.0, 0.0).astype(BF16)
        xs_scr[slot, row0:row0 + nrows, :] = _dot(p, h_ref[...]).astype(BF16)
        gs_scr[slot, row0:row0 + nrows, :] = _dot(p, hi) + _dot(p, mid) + _dot(p, lo)

    def all_slots():
        compact(0, N_EXPERT_GROUPS * MOE_TOK)

    def first_halves():
        for g in range(N_EXPERT_GROUPS):
            compact(g * MOE_TOK, MOE_TOK // 2)

    small = cnt_ref[i * N_EXPERT_GROUPS] <= MOE_TOK // 2
    for g in range(1, N_EXPERT_GROUPS):
        small = small & (cnt_ref[i * N_EXPERT_GROUPS + g] <= MOE_TOK // 2)
    lax.cond(small, first_halves, all_slots)

    def for_each_piece(tile, buf, action):
        for g in range(N_EXPERT_GROUPS):
            off = pl.multiple_of(off_ref[tile * N_EXPERT_GROUPS + g], ROW_ALIGN)
            for j in range(MOE_TOK // MOE_PIECE):
                src = pl.ds(g * MOE_TOK + j * MOE_PIECE, MOE_PIECE)
                dst = pl.ds(off + j * MOE_PIECE, MOE_PIECE)

                @pl.when(cnt_ref[tile * N_EXPERT_GROUPS + g] > j * MOE_PIECE)
                def _():
                    action(pltpu.make_async_copy(xs_scr.at[buf, src], xg_ref.at[g, dst], sem.at[buf, 0, g, j]))
                    action(pltpu.make_async_copy(gs_scr.at[buf, src], gg_ref.at[g, dst], sem.at[buf, 1, g, j]))

    @pl.when(i > 0)
    def _():
        for_each_piece(i - 1, 1 - slot, lambda cp: cp.wait())

    for_each_piece(i, slot, lambda cp: cp.start())

    @pl.when(i == pl.num_programs(0) - 1)
    def _():
        for_each_piece(i, slot, lambda cp: cp.wait())
        zx_scr[...] = jnp.zeros_like(zx_scr)
        zg_scr[...] = jnp.zeros_like(zg_scr)

        def tail_copies(g):
            tot = pl.multiple_of(tot_ref[g], ROW_ALIGN)
            return (pltpu.make_async_copy(zx_scr, xg_ref.at[g, pl.ds(tot, MOE_ROWS)], sem.at[0, 0, g, 0]),
                    pltpu.make_async_copy(zg_scr, gg_ref.at[g, pl.ds(tot, MOE_ROWS)], sem.at[0, 1, g, 0]))

        for g in range(N_EXPERT_GROUPS):
            for cp in tail_copies(g):
                cp.start()
        for g in range(N_EXPERT_GROUPS):
            for cp in tail_copies(g):
                cp.wait()


def _moe_expert_kernel(grp_ref, chk_ref, meta_ref, x_ref, g_ref, wg_ref, wu_ref, wd_ref, y_ref):
    k = pl.program_id(0)
    half = MOE_ROWS // 2

    def experts(rows):
        x = x_ref[0:rows, :]
        gates = g_ref[0:rows, :]
        acc = None
        for e in range(EXPERTS_PER_GROUP):
            a = _dot(x, wg_ref[e])
            u = _dot(x, wu_ref[e])
            hid = (_silu(a) * u * gates[:, 1 + e:2 + e]).astype(BF16)
            part = _dot(hid, wd_ref[e])
            acc = part if acc is None else acc + part
        return acc.astype(y_ref.dtype)

    def whole_chunk():
        y_ref[...] = experts(MOE_ROWS)

    def lower_half():
        y_ref[0:half, :] = experts(half)
        y_ref[half:MOE_ROWS, :] = jnp.zeros((MOE_ROWS - half, D_MODEL), y_ref.dtype)

    @pl.when(k < meta_ref[0])
    def _():
        lax.cond(meta_ref[1 + k] > half, whole_chunk, lower_half)


def _moe_unpack_kernel(st_ref, dl_ref, cnt_ref, gate_ref, tri_ref, x1_ref, g2_ref, fn_ref, yg_ref, out_ref,
                       ys_scr, sem, *, final):
    i = pl.program_id(0)
    slot = i % 2

    def for_each_piece(tile, buf, action):
        for g in range(N_EXPERT_GROUPS):
            k = tile * N_EXPERT_GROUPS + g
            start = pl.multiple_of(st_ref[k], ROW_ALIGN)
            for j in range(MOE_TOK // MOE_PIECE):
                lo, hi = j * MOE_PIECE, (j + 1) * MOE_PIECE

                @pl.when((dl_ref[k] < hi) & (dl_ref[k] + cnt_ref[k] > lo))
                def _():
                    action(pltpu.make_async_copy(yg_ref.at[g, pl.ds(start + lo, MOE_PIECE)],
                                                 ys_scr.at[buf, pl.ds(g * MOE_TOK + lo, MOE_PIECE)],
                                                 sem.at[buf, g, j]))

    @pl.when(i == 0)
    def _():
        ys_scr[...] = jnp.zeros_like(ys_scr)
        for_each_piece(0, 0, lambda cp: cp.start())

    @pl.when(i + 1 < pl.num_programs(0))
    def _():
        for_each_piece(i + 1, 1 - slot, lambda cp: cp.start())

    gates = gate_ref[...]
    lane = lax.broadcasted_iota(jnp.int32, (1, LANES), 1)
    delta = jnp.zeros((1, LANES), F32)
    for g in range(N_EXPERT_GROUPS):
        delta = jnp.where(lane == g, dl_ref[i * N_EXPERT_GROUPS + g].astype(F32), delta)
    pt = _tile_positions(gates, tri_ref, delta)
    for_each_piece(i, slot, lambda cp: cp.wait())
    x2 = x1_ref[...] + g2_ref[0] * _dot(pt, ys_scr[slot])
    if final:
        x2 = _rmsnorm_rows(x2, fn_ref[...])
    out_ref[...] = x2


def _moe_schedule(gsel, n_tiles, n_steps):
    ng = N_EXPERT_GROUPS
    member = gsel.reshape(n_tiles, MOE_TOK, 1) == jnp.arange(ng, dtype=jnp.int32)
    cnt = jnp.sum(member.astype(jnp.int32), axis=1)
    padded = (cnt + ROW_ALIGN - 1) // ROW_ALIGN * ROW_ALIGN
    off = jnp.cumsum(padded, axis=0) - padded
    tot = jnp.sum(padded, axis=0)
    n_chunks = jnp.maximum((tot + MOE_ROWS - 1) // MOE_ROWS, 1)
    start = jnp.minimum(off, n_chunks * MOE_ROWS - MOE_TOK)
    ends = jnp.cumsum(n_chunks)
    n_active = ends[-1]
    step = jnp.minimum(jnp.arange(n_steps, dtype=jnp.int32), n_active - 1)
    grp = jnp.sum((step[:, None] >= ends[None, :]).astype(jnp.int32), axis=1)
    chk = step - (ends - n_chunks)[grp]
    i32 = lambda a: a.astype(jnp.int32)
    populated = jnp.clip(tot[grp] - chk * MOE_ROWS, 0, MOE_ROWS)
    meta = jnp.concatenate([n_active.reshape(1), populated])
    return (i32(off).reshape(-1), i32(padded).reshape(-1), i32(tot), i32(start).reshape(-1),
            i32(off - start).reshape(-1), i32(grp), i32(chk), i32(meta))


def _moe(h2, gates, layer, w_gate, w_up, w_down, x1, mod3, row_of_tile, final_g, final):
    t = h2.shape[0]
    ng = N_EXPERT_GROUPS
    n_tiles = t // MOE_TOK
    cap = -(-(t + ROW_ALIGN * n_tiles + MOE_ROWS) // MOE_ROWS) * MOE_ROWS
    n_steps = (t + ROW_ALIGN * n_tiles) // MOE_ROWS + ng
    off, cnt, tot, start, delta, grp, chk, meta = _moe_schedule(gates[:, 0].astype(jnp.int32), n_tiles,
                                                                    n_steps)
    pieces = MOE_TOK // MOE_PIECE
    r = np.arange(MOE_TOK)
    tri = jnp.asarray((r[:, None] > r[None, :]).astype(np.float32), BF16)
    any_spec = pl.BlockSpec(memory_space=pl.ANY)

    xg, gg = pl.pallas_call(
        _moe_pack_kernel,
        grid_spec=pltpu.PrefetchScalarGridSpec(
            num_scalar_prefetch=3,
            grid=(n_tiles,),
            in_specs=[
                pl.BlockSpec((MOE_TOK, D_MODEL), lambda i, *_: (i, 0)),
                pl.BlockSpec((MOE_TOK, LANES), lambda i, *_: (i, 0)),
                pl.BlockSpec((MOE_TOK, MOE_TOK), lambda i, *_: (0, 0)),
            ],
            out_specs=[any_spec, any_spec],
            scratch_shapes=[
                pltpu.VMEM((2, ng * MOE_TOK, D_MODEL), BF16),
                pltpu.VMEM((2, ng * MOE_TOK, LANES), F32),
                pltpu.VMEM((MOE_ROWS, D_MODEL), BF16),
                pltpu.VMEM((MOE_ROWS, LANES), F32),
                pltpu.SemaphoreType.DMA((2, 2, ng, pieces)),
            ],
        ),
        out_shape=[jax.ShapeDtypeStruct((ng, cap, D_MODEL), BF16), jax.ShapeDtypeStruct((ng, cap, LANES), F32)],
        name="moe_pack",
    )(off, cnt, tot, h2, gates, tri)

    group_w = lambda shape: pl.BlockSpec((None, None, EXPERTS_PER_GROUP) + shape,
                                         lambda k, grp, chk, na: (layer, grp[k], 0, 0, 0))
    rows = lambda width: pl.BlockSpec((None, MOE_ROWS, width), lambda k, grp, chk, na: (grp[k], chk[k], 0))
    yg = pl.pallas_call(
        _moe_expert_kernel,
        grid_spec=pltpu.PrefetchScalarGridSpec(
            num_scalar_prefetch=3,
            grid=(n_steps,),
            in_specs=[rows(D_MODEL), rows(LANES), group_w((D_MODEL, D_EXPERT)), group_w((D_MODEL, D_EXPERT)),
                      group_w((D_EXPERT, D_MODEL))],
            out_specs=rows(D_MODEL),
        ),
        out_shape=jax.ShapeDtypeStruct((ng, cap, D_MODEL), BF16),
        name="moe_experts",
    )(grp, chk, meta, xg, gg,
      w_gate.reshape(DEPTH, ng, EXPERTS_PER_GROUP, D_MODEL, D_EXPERT),
      w_up.reshape(DEPTH, ng, EXPERTS_PER_GROUP, D_MODEL, D_EXPERT),
      w_down.reshape(DEPTH, ng, EXPERTS_PER_GROUP, D_EXPERT, D_MODEL))

    return pl.pallas_call(
        functools.partial(_moe_unpack_kernel, final=final),
        grid_spec=pltpu.PrefetchScalarGridSpec(
            num_scalar_prefetch=3,
            grid=(n_tiles,),
            in_specs=[
                pl.BlockSpec((MOE_TOK, LANES), lambda i, *_: (i, 0)),
                pl.BlockSpec((MOE_TOK, MOE_TOK), lambda i, *_: (0, 0)),
                pl.BlockSpec((MOE_TOK, D_MODEL), lambda i, *_: (i, 0)),
                pl.BlockSpec((1, 1, D_MODEL), lambda i, *_: (row_of_tile(i), 0, 5)),
                pl.BlockSpec((1, D_MODEL), lambda i, *_: (0, 0)),
                any_spec,
            ],
            out_specs=pl.BlockSpec((MOE_TOK, D_MODEL), lambda i, *_: (i, 0)),
            scratch_shapes=[pltpu.VMEM((2, ng * MOE_TOK, D_MODEL), BF16),
                            pltpu.SemaphoreType.DMA((2, ng, pieces))],
        ),
        out_shape=jax.ShapeDtypeStruct((t, D_MODEL), F32),
        name="moe_unpack",
    )(start, delta, cnt, gates, tri, x1, mod3, final_g, yg)


def _prep_layer(l, w):
    gw = GROUP_W
    w_in = w['w_in'][l]
    w_g = w_in[:, 9 * gw:9 * gw + N_GATES]
    lb_all = jnp.cumsum(jax.nn.softmax(w['hg_lb'].astype(F32), axis=0), axis=0)
    lb = lb_all[l] - lb_all[0]
    lb_rows = jnp.zeros((8, gw), F32).at[0].set(jnp.log(lb)).at[1].set(jnp.log1p(-lb)).at[2].set(1.0 - lb)
    w_r = jnp.zeros((D_MODEL, LANES), F32)
    w_r = w_r.at[:, 0:N_EXPERT_GROUPS].set(w['moe_wg'][l])
    w_r = w_r.at[:, N_EXPERT_GROUPS:N_EXPERT_GROUPS + N_EXPERTS].set(w['moe_we'][l])
    b_r = jnp.zeros((1, LANES), F32)
    b_r = b_r.at[0, 0:N_EXPERT_GROUPS].set(w['moe_bg'][l])
    b_r = b_r.at[0, N_EXPERT_GROUPS:N_EXPERT_GROUPS + N_EXPERTS].set(w['moe_be'][l])
    gate_bias = jnp.concatenate([w['ml_i_bias'][l], w['ml_f_bias'][l]])
    w1p = jnp.zeros((LANES, HY_HIDDEN), F32).at[0:HY_EMB].set(w['hy_w1'][l])
    return dict(
        norm1=w['norm1'][l][None, :], norm2=w['norm2'][l][None, :],
        w_a=w_in[:, :9 * gw].astype(BF16), w_b=w_in[:, 9 * gw + N_GATES:].astype(BF16),
        w_g=jnp.zeros((D_MODEL, LANES), BF16).at[:, 0:N_GATES].set(w_g.astype(BF16)), w_gt=w_g.T.astype(BF16),
        w_out=w['w_out'][l].astype(BF16),
        lb_rows=lb_rows, hg_norm=w['hg_norm'][l][None, :],
        ml_conv=w['ml_conv'][l], ml_brow=jnp.zeros((1, LANES), F32).at[0, 0:N_GATES].set(gate_bias),
        ml_brows=jnp.repeat(gate_bias.reshape(4, HEADS), HEAD_D, axis=1),
        ml_norm=w['ml_norm'][l][None, :],
        s5=_s5_params(w['s5_a_re'][l], w['s5_a_im'][l], w['s5_log_dt'][l], w['s5_b_re'][l], w['s5_b_im'][l],
                      w['s5_c_re'][l], w['s5_c_im'][l]),
        s5_d=w['s5_d'][l][None, :], s5_glu_w=w['s5_glu_w'][l].astype(BF16), s5_glu_b=w['s5_glu_b'][l][None, :],
        hy_conv=w['hy_conv'][l], hy_bias=w['hy_bias'][l],
        hy_mlp=(w1p, w['hy_b1'][l][None, :], w['hy_w2'][l], w['hy_b2'][l][None, :], w['hy_w3'][l],
                w['hy_freq'][l][None, :], w['hy_decay'][l][None, :]),
        w_r=w_r, b_r=b_r,
        layer=l,
    )


def _trunk_layer(x2d, nb, seq, mod3, cond_row, st, lw, moe_w, filt, conv_width, emit_state, final_g, final):
    if cond_row is None:
        row_of_tile = lambda i: i // (seq // TOK_TILE)
        in_row_of_tile = lambda i: i // (seq // IN_TILE)
    else:
        row_of_tile = in_row_of_tile = lambda i: cond_row
    hg, ml, su, hy, gt, gtt = _inproj(x2d, mod3, in_row_of_tile, lw['norm1'], lw['w_a'], lw['w_b'], lw['w_g'],
                                       lw['w_gt'])
    nch = seq // ML_CHUNK
    a_out, hg_st = _hgrn(hg.reshape(nb, seq, -1), st['hgrn'], lw['lb_rows'], lw['hg_norm'], emit_state)
    gr4 = jnp.transpose(gtt.reshape(4, HEADS, nb, nch, ML_CHUNK), (2, 0, 3, 1, 4)).reshape(nb, 4, nch, GROUP_W)
    b_out, mlc, mln, mlm = _mlstm(ml.reshape(nb, seq, -1), gt.reshape(nb, seq, LANES), gr4, lw['ml_conv'],
                                  lw['ml_brow'], lw['ml_brows'], lw['ml_norm'], st['ml_c'], st['ml_n'],
                                  st['ml_m'], conv_width, emit_state)
    c_out, s5re, s5im = _s5(su.reshape(nb, seq, -1), st['s5_re'], st['s5_im'], lw['s5'], lw['s5_d'],
                            lw['s5_glu_w'], lw['s5_glu_b'])
    d_out = _hyena(hy.reshape(nb, seq, -1), lw['hy_conv'], filt, lw['hy_bias'], conv_width)
    mix = [m.reshape(nb * seq, GROUP_W) for m in (a_out, b_out, c_out, d_out)]
    x1, h2, gates = _outproj(mix, x2d, mod3, row_of_tile, lw['norm2'], lw['w_out'], lw['w_r'], lw['b_r'])
    x2 = _moe(h2, gates, lw['layer'], *moe_w, x1, mod3, row_of_tile, final_g, final)
    new_st = dict(hgrn=hg_st, ml_c=mlc, ml_n=mln, ml_m=mlm, s5_re=s5re, s5_im=s5im)
    return x2, new_st


def kernel(x_prompt, x_sample, c, state_hgrn, state_mlstm_c, state_mlstm_n, state_mlstm_m, state_s5_re, state_s5_im, c_ctx, w_ada, b_ada, norm1, norm2, w_in, w_out, hg_lb, hg_norm, ml_conv, ml_i_bias, ml_f_bias, ml_norm, s5_a_re, s5_a_im, s5_log_dt, s5_b_re, s5_b_im, s5_c_re, s5_c_im, s5_d, s5_glu_w, s5_glu_b, hy_conv, hy_w1, hy_b1, hy_w2, hy_b2, hy_w3, hy_freq, hy_decay, hy_bias, moe_wg, moe_bg, moe_we, moe_be, moe_w_gate, moe_w_up, moe_w_down, final_norm):
    w = dict(w_in=w_in, w_out=w_out, norm1=norm1, norm2=norm2, hg_lb=hg_lb, hg_norm=hg_norm, ml_conv=ml_conv,
             ml_i_bias=ml_i_bias, ml_f_bias=ml_f_bias, ml_norm=ml_norm, s5_a_re=s5_a_re, s5_a_im=s5_a_im,
             s5_log_dt=s5_log_dt, s5_b_re=s5_b_re, s5_b_im=s5_b_im, s5_c_re=s5_c_re, s5_c_im=s5_c_im,
             s5_d=s5_d, s5_glu_w=s5_glu_w, s5_glu_b=s5_glu_b, hy_conv=hy_conv, hy_w1=hy_w1, hy_b1=hy_b1,
             hy_w2=hy_w2, hy_b2=hy_b2, hy_w3=hy_w3, hy_freq=hy_freq, hy_decay=hy_decay, hy_bias=hy_bias,
             moe_wg=moe_wg, moe_bg=moe_bg, moe_we=moe_we, moe_be=moe_be, moe_w_gate=moe_w_gate,
             moe_w_up=moe_w_up, moe_w_down=moe_w_down)
    bp, lp, _ = x_prompt.shape
    bs, ls, _ = x_sample.shape
    ctx_row = bs
    cond = jnp.zeros((MOD_ROWS, D_MODEL), F32).at[0:bs].set(c).at[ctx_row].set(c_ctx)
    mod = _ada_mod(cond, w_ada, b_ada)
    final_g = final_norm[None, :]

    zero_st = dict(
        hgrn=jnp.zeros((bp, 2, HEADS, HEAD_D, HEAD_D), F32),
        ml_c=jnp.zeros((bp, 2, HEADS, HEAD_D, HEAD_D), F32),
        ml_n=jnp.zeros((bp, 2, 1, GROUP_W), F32),
        ml_m=jnp.zeros((bp, 2, 1, GROUP_W), F32),
        s5_re=jnp.zeros((bp, 2, S5_STATE), F32),
        s5_im=jnp.zeros((bp, 2, S5_STATE), F32))

    xp = x_prompt.reshape(bp * lp, D_MODEL)
    xs = x_sample.reshape(bs * ls, D_MODEL)
    moe_w = (moe_w_gate.astype(BF16), moe_w_up.astype(BF16), moe_w_down.astype(BF16))
    per_layer = []
    for l in range(DEPTH):
        lw = _prep_layer(l, w)
        mod3 = mod[l].reshape(MOD_ROWS, 1, 6 * D_MODEL)
        last = l == DEPTH - 1
        filt_p = _hyena_filter(lp, *lw['hy_mlp'])
        filt_s = filt_p if ls == lp else _hyena_filter(ls, *lw['hy_mlp'])
        xp, st = _trunk_layer(xp, bp, lp, mod3, ctx_row, zero_st, lw, moe_w, filt_p, lp, True, final_g, last)
        per_layer.append(st)
        st_in = dict(
            hgrn=state_hgrn[:, l].astype(F32),
            ml_c=state_mlstm_c[:, l].astype(F32),
            ml_n=state_mlstm_n[:, l].astype(F32).reshape(bs, 2, 1, GROUP_W),
            ml_m=jnp.repeat(state_mlstm_m[:, l].astype(F32), HEAD_D, axis=-1).reshape(bs, 2, 1, GROUP_W),
            s5_re=state_s5_re[:, l].astype(F32).reshape(bs, 2, S5_STATE),
            s5_im=state_s5_im[:, l].astype(F32).reshape(bs, 2, S5_STATE))
        xs, _ = _trunk_layer(xs, bs, ls, mod3, None, st_in, lw, moe_w, filt_s, GRID_W, False, final_g, last)

    new_hgrn = jnp.stack([s['hgrn'] for s in per_layer], axis=1)
    new_ml_c = jnp.stack([s['ml_c'] for s in per_layer], axis=1)
    new_ml_n = jnp.stack([s['ml_n'].reshape(bp, 2, HEADS, HEAD_D) for s in per_layer], axis=1)
    new_ml_m = jnp.stack([s['ml_m'][:, :, 0, ::HEAD_D] for s in per_layer], axis=1)
    new_s5_re = jnp.stack([s['s5_re'].reshape(bp, 2, S5_GROUPS, S5_P) for s in per_layer], axis=1)
    new_s5_im = jnp.stack([s['s5_im'].reshape(bp, 2, S5_GROUPS, S5_P) for s in per_layer], axis=1)
    return (xp.reshape(bp, lp, D_MODEL), xs.reshape(bs, ls, D_MODEL),
            new_hgrn, new_ml_c, new_ml_n, new_ml_m, new_s5_re, new_s5_im)
```
